```python
import jax, jax.numpy as jnp
from jax import lax
import numpy as np

D_MODEL = 2048
BATCH = 2
SEQ = 16384
DEPTH = 2

GRID_W = 64
HEAD_DIM = 128
NA_HEADS = D_MODEL // HEAD_DIM
NA_WIN_ROWS = 8
NA_WIN_COLS = 16
DIL_GROUPS = ((128, 1), (512, 4), (2048, 16))
DIL_HEADS = 8
ROPE_THETA = 500000.0
ROPE_DIM = HEAD_DIM // 4
D_FF = 4 * D_MODEL
N_MIXERS = 2
BLOCK_Q = 128
EPS = 1e-6
NEG = -1e30

kernel_name = "hybrid_natten_dilated_encoder"


def rms_norm(x, g):
    xf = x.astype(jnp.float32)
    y = xf * lax.rsqrt(jnp.mean(xf * xf, axis=-1, keepdims=True) + EPS)
    return (y * g.astype(jnp.float32)).astype(x.dtype)


def sq_relu_mlp(h, w1, w2):
    return jnp.square(jax.nn.relu(h @ w1)) @ w2


def neighbourhood_attention(h, w_qkv, rpb, w_o):
    B, T, _ = h.shape
    rows = T // GRID_W
    kh = min(NA_WIN_ROWS, rows)
    kw = NA_WIN_COLS
    qkv = (h @ w_qkv).reshape(B, rows, GRID_W, 3, NA_HEADS, HEAD_DIM)
    q = qkv[:, :, :, 0] * (HEAD_DIM ** -0.5)
    k = qkv[:, :, :, 1]
    v = qkv[:, :, :, 2]
    cols = np.arange(GRID_W)
    col_start = np.clip(cols - kw // 2, 0, GRID_W - kw)
    col_mask = (cols[None, :] >= col_start[:, None]) & (cols[None, :] < col_start[:, None] + kw)
    col_idx = np.clip(cols[None, :] - cols[:, None] + NA_WIN_COLS - 1, 0, 2 * NA_WIN_COLS - 2)
    rpb_cols = rpb[:, :, col_idx].transpose(0, 2, 1, 3)
    mask = jnp.asarray(col_mask)[:, None, :]

    def one_row(r):
        rs = jnp.clip(r - kh // 2, 0, rows - kh)
        q_r = lax.dynamic_index_in_dim(q, r, axis=1, keepdims=False)
        k_r = lax.dynamic_slice_in_dim(k, rs, kh, axis=1)
        v_r = lax.dynamic_slice_in_dim(v, rs, kh, axis=1)
        row_off = rs + jnp.arange(kh) - r
        bias = jnp.take(rpb_cols, row_off + NA_WIN_ROWS - 1, axis=2)
        s = jnp.einsum('bqhd,brwhd->bhqrw', q_r, k_r).astype(jnp.float32)
        s = s + bias.astype(jnp.float32)[None]
        s = jnp.where(mask, s, NEG)
        p = jax.nn.softmax(s, axis=(-2, -1))
        return jnp.einsum('bhqrw,brwhd->bqhd', p.astype(v.dtype), v_r)

    out = lax.map(one_row, jnp.arange(rows))
    out = out.transpose(1, 0, 2, 3, 4).reshape(B, T, NA_HEADS * HEAD_DIM)
    return out @ w_o


def apply_partial_rotary(x, cos, sin):
    half = ROPE_DIM // 2
    x1 = x[..., :half]
    x2 = x[..., half:ROPE_DIM]
    return jnp.concatenate([x1 * cos - x2 * sin, x2 * cos + x1 * sin, x[..., ROPE_DIM:]], axis=-1)


def banded_attention(q, k, v, half):
    L, hd = q.shape[-2], q.shape[-1]
    lead = q.shape[:-2]
    nl = len(lead)
    qb = min(BLOCK_Q, L)
    nb = -(-L // qb)
    pad = nb * qb - L
    qp = jnp.pad(q, [(0, 0)] * nl + [(0, pad), (0, 0)]).reshape(*lead, nb, qb, hd)
    kp = jnp.pad(k, [(0, 0)] * nl + [(half, half + pad), (0, 0)])
    vp = jnp.pad(v, [(0, 0)] * nl + [(half, half + pad), (0, 0)])
    idx = np.arange(nb)[:, None] * qb + np.arange(qb + 2 * half)[None, :]
    kb = jnp.take(kp, idx, axis=-2)
    vb = jnp.take(vp, idx, axis=-2)
    qpos = np.arange(nb)[:, None] * qb + np.arange(qb)[None, :]
    kpos = idx - half
    valid = ((np.abs(kpos[:, None, :] - qpos[:, :, None]) <= half)
             & (kpos[:, None, :] >= 0) & (kpos[:, None, :] < L))
    s = jnp.einsum('...nqd,...nkd->...nqk', qp, kb).astype(jnp.float32)
    s = jnp.where(jnp.asarray(valid), s, NEG)
    lse = jax.nn.logsumexp(s, axis=-1)
    p = jnp.exp(s - lse[..., None])
    o = jnp.einsum('...nqk,...nkd->...nqd', p.astype(v.dtype), vb)
    o = o.reshape(*lead, nb * qb, hd)[..., :L, :]
    lse = lse.reshape(*lead, nb * qb)[..., :L]
    return o, lse


def dilated_attention(h, w_qkv, w_o):
    B, T, _ = h.shape
    G = len(DIL_GROUPS)
    qkv = (h @ w_qkv).reshape(B, T, G, 3, DIL_HEADS, HEAD_DIM)
    pos = jnp.arange(T, dtype=jnp.float32)
    inv_freq = ROPE_THETA ** (-jnp.arange(0, ROPE_DIM, 2, dtype=jnp.float32) / ROPE_DIM)
    ang = pos[:, None] * inv_freq[None, :]
    cos = jnp.cos(ang)[:, None, :].astype(h.dtype)
    sin = jnp.sin(ang)[:, None, :].astype(h.dtype)
    outs, lses = [], []
    for g, (window, dil) in enumerate(DIL_GROUPS):
        L = T // dil
        n_side = (window // 2) // dil

        def split(t):
            return t.reshape(B, L, dil, DIL_HEADS, HEAD_DIM).transpose(0, 3, 2, 1, 4)

        q = apply_partial_rotary(qkv[:, :, g, 0], cos, sin) * (HEAD_DIM ** -0.5)
        k = apply_partial_rotary(qkv[:, :, g, 1], cos, sin)
        v = qkv[:, :, g, 2]
        o, lse = banded_attention(split(q), split(k), split(v), n_side)
        outs.append(o.transpose(0, 3, 2, 1, 4).reshape(B, T, DIL_HEADS, HEAD_DIM))
        lses.append(lse.transpose(0, 3, 2, 1).reshape(B, T, DIL_HEADS))
    wts = jax.nn.softmax(jnp.stack(lses, axis=0), axis=0)
    o = jnp.einsum('gbth,gbthd->bthd', wts.astype(h.dtype), jnp.stack(outs, axis=0))
    return o.reshape(B, T, DIL_HEADS * HEAD_DIM) @ w_o


def setup_inputs(seed: int = 0) -> dict:
    key = jax.random.key(seed)
    ks = jax.random.split(key, 16)
    f32 = jnp.float32
    G = len(DIL_GROUPS)

    def w(k, shape, fan_in):
        return jax.random.normal(k, shape, f32) * (fan_in ** -0.5)

    def gain(k):
        return 1.0 + 0.02 * jax.random.normal(k, (D_MODEL,), f32)

    return {
        "x": jax.random.normal(ks[0], (BATCH, SEQ, D_MODEL), f32),
        "na_norm": gain(ks[1]),
        "na_wqkv": w(ks[2], (D_MODEL, 3 * NA_HEADS * HEAD_DIM), D_MODEL),
        "na_rpb": 0.1 * jax.random.normal(ks[3], (NA_HEADS, 2 * NA_WIN_ROWS - 1, 2 * NA_WIN_COLS - 1), f32),
        "na_wo": w(ks[4], (NA_HEADS * HEAD_DIM, D_MODEL), NA_HEADS * HEAD_DIM),
        "ffn0_norm": gain(ks[5]),
        "ffn0_w1": w(ks[6], (D_MODEL, D_FF), D_MODEL),
        "ffn0_w2": w(ks[7], (D_FF, D_MODEL), D_FF),
        "dil_norm": gain(ks[8]),
        "dil_wqkv": w(ks[9], (D_MODEL, G * 3 * DIL_HEADS * HEAD_DIM), D_MODEL),
        "dil_wo": w(ks[10], (DIL_HEADS * HEAD_DIM, D_MODEL), DIL_HEADS * HEAD_DIM),
        "ffn1_norm": gain(ks[11]),
        "ffn1_w1": w(ks[12], (D_MODEL, D_FF), D_MODEL),
        "ffn1_w2": w(ks[13], (D_FF, D_MODEL), D_FF),
        "final_norm": gain(ks[14]),
    }


def reference(x, na_norm, na_wqkv, na_rpb, na_wo, ffn0_norm, ffn0_w1, ffn0_w2,
              dil_norm, dil_wqkv, dil_wo, ffn1_norm, ffn1_w1, ffn1_w2, final_norm):
    mixer_norms = (na_norm, dil_norm)
    ffns = ((ffn0_norm, ffn0_w1, ffn0_w2), (ffn1_norm, ffn1_w1, ffn1_w2))
    for i in range(DEPTH):
        m = i % N_MIXERS
        hn = rms_norm(x, mixer_norms[m])
        if m == 0:
            x = x + neighbourhood_attention(hn, na_wqkv, na_rpb, na_wo)
        else:
            x = x + dilated_attention(hn, dil_wqkv, dil_wo)
        g, w1, w2 = ffns[i]
        x = x + sq_relu_mlp(rms_norm(x, g), w1, w2)
    return rms_norm(x, final_norm)
```

```python
import functools

import numpy as np
import jax
import jax.numpy as jnp
from jax import lax
from jax.experimental import pallas as pl
from jax.experimental.pallas import tpu as pltpu

GRID_W = 64
HEAD_DIM = 128
NA_WIN_ROWS = 8
NA_WIN_COLS = 16
DIL_GROUPS = ((128, 1), (512, 4), (2048, 16))
ROPE_THETA = 500000.0
ROPE_DIM = HEAD_DIM // 4
BLOCK_Q = 128
EPS = 1e-6
NEG = -1e30

NA_BLOCK_ROWS = 8
NA_KEY_ROWS = 2 * NA_BLOCK_ROWS
NA_KEY_CHUNKS = 4

V7X_VMEM_BYTES = 64 * 1024 * 1024
VMEM_CAP_BYTES = V7X_VMEM_BYTES - 8 * 1024 * 1024

F32 = jnp.float32
BF16 = jnp.bfloat16


def _params(semantics, vmem_bytes):
    limit = min(int(vmem_bytes * 1.25) + (4 << 20), VMEM_CAP_BYTES)
    return pltpu.CompilerParams(dimension_semantics=semantics, vmem_limit_bytes=limit)


def _rms(x, g):
    return x * lax.rsqrt(jnp.mean(x * x, axis=-1, keepdims=True) + EPS) * g


def _norm_matmul_kernel(x_ref, g_ref, w_ref, cs_ref, o_ref, h_ref):
    @pl.when(pl.program_id(1) == 0)
    def _():
        h_ref[...] = _rms(x_ref[...], g_ref[...]).astype(BF16)

    acc = jnp.dot(h_ref[...], w_ref[...], preferred_element_type=F32)
    o_ref[...] = (acc * cs_ref[...]).astype(o_ref.dtype)


def norm_matmul(x, g, w, col_scale, *, tm, tn):
    m, k = x.shape
    n = w.shape[1]
    vmem = 2 * tm * k * 4 + tm * k * 2 + 2 * k * tn * 2 + 2 * tm * tn * 2 + tm * tn * 4
    return pl.pallas_call(
        _norm_matmul_kernel,
        grid=(m // tm, n // tn),
        in_specs=[
            pl.BlockSpec((tm, k), lambda i, j: (i, 0)),
            pl.BlockSpec((1, k), lambda i, j: (0, 0)),
            pl.BlockSpec((k, tn), lambda i, j: (0, j)),
            pl.BlockSpec((1, tn), lambda i, j: (0, j)),
        ],
        out_specs=pl.BlockSpec((tm, tn), lambda i, j: (i, j)),
        out_shape=jax.ShapeDtypeStruct((m, n), BF16),
        scratch_shapes=[pltpu.VMEM((tm, k), BF16)],
        compiler_params=_params(("parallel", "arbitrary"), vmem),
        name="norm_qkv_na",
    )(x, g.reshape(1, k), w, col_scale.reshape(1, n))


def _norm_matmul_rope_kernel(x_ref, g_ref, w_ref, rot_ref, o_ref, h_ref, *, q_scale):
    j = pl.program_id(1)

    @pl.when(j == 0)
    def _():
        h_ref[...] = _rms(x_ref[...], g_ref[...]).astype(BF16)

    acc = jnp.dot(h_ref[...], w_ref[...], preferred_element_type=F32)
    sec = j % 3

    @pl.when(sec == 2)
    def _():
        o_ref[...] = acc.astype(o_ref.dtype)

    @pl.when(sec != 2)
    def _():
        scale = jnp.where(sec == 0, q_scale, 1.0).astype(F32)
        cos = rot_ref[0]
        sin_hi = rot_ref[1]
        sin_lo = rot_ref[2]
        half = ROPE_DIM // 2
        for h in range(acc.shape[1] // HEAD_DIM):
            sl = slice(h * HEAD_DIM, (h + 1) * HEAD_DIM)
            a = acc[:, sl]
            r = (a * cos + pltpu.roll(a, half, 1) * sin_hi
                 + pltpu.roll(a, HEAD_DIM - half, 1) * sin_lo)
            o_ref[:, sl] = (r * scale).astype(o_ref.dtype)


def _rope_tables(t):
    half = ROPE_DIM // 2
    pos = jnp.arange(t, dtype=F32)
    inv_freq = ROPE_THETA ** (-jnp.arange(0, ROPE_DIM, 2, dtype=F32) / ROPE_DIM)
    ang = pos[:, None] * inv_freq[None, :]
    cos = jnp.cos(ang)
    sin = jnp.sin(ang)
    rest = HEAD_DIM - ROPE_DIM
    one = jnp.ones((t, rest), F32)
    zero = jnp.zeros((t, rest), F32)
    zh = jnp.zeros((t, half), F32)
    c = jnp.concatenate([cos, cos, one], axis=-1)
    s_hi = jnp.concatenate([zh, sin, zero], axis=-1)
    s_lo = jnp.concatenate([-sin, zh, zero], axis=-1)
    return jnp.stack([c, s_hi, s_lo], axis=0)


def norm_matmul_rope(x, g, w, rot, *, seq, tm, tn):
    m, k = x.shape
    n = w.shape[1]
    tpb = seq // tm
    vmem = (2 * tm * k * 4 + tm * k * 2 + 2 * k * tn * 2 + 2 * tm * tn * 2 + 2 * tm * tn * 4
            + 2 * 3 * tm * HEAD_DIM * 4)
    return pl.pallas_call(
        functools.partial(_norm_matmul_rope_kernel, q_scale=HEAD_DIM ** -0.5),
        grid=(m // tm, n // tn),
        in_specs=[
            pl.BlockSpec((tm, k), lambda i, j: (i, 0)),
            pl.BlockSpec((1, k), lambda i, j: (0, 0)),
            pl.BlockSpec((k, tn), lambda i, j: (0, j)),
            pl.BlockSpec((3, tm, HEAD_DIM), lambda i, j: (0, i % tpb, 0)),
        ],
        out_specs=pl.BlockSpec((tm, tn), lambda i, j: (i, j)),
        out_shape=jax.ShapeDtypeStruct((m, n), BF16),
        scratch_shapes=[pltpu.VMEM((tm, k), BF16)],
        compiler_params=_params(("parallel", "arbitrary"), vmem),
        name="norm_qkv_dil_rope",
    )(x, g.reshape(1, k), w, rot)


def _na_bias_table(rpb, rows):
    kh = min(NA_WIN_ROWS, rows)
    kw = NA_WIN_COLS
    cols = np.arange(GRID_W)
    col_start = np.clip(cols - kw // 2, 0, GRID_W - kw)
    col_mask = (cols[None, :] >= col_start[:, None]) & (cols[None, :] < col_start[:, None] + kw)
    col_idx = np.clip(cols[None, :] - cols[:, None] + NA_WIN_COLS - 1, 0, 2 * NA_WIN_COLS - 2)
    by_col = jnp.where(jnp.asarray(col_mask)[None, None], rpb[:, :, col_idx], NEG)

    r_blk, k_rows = NA_BLOCK_ROWS, NA_KEY_ROWS
    row_idx = np.zeros((3, r_blk, k_rows), np.int32)
    row_ok = np.zeros((3, r_blk, k_rows), bool)
    for v, r0 in enumerate((0, r_blk, rows - r_blk)):
        k0 = int(np.clip(r0 - r_blk // 2, 0, rows - k_rows))
        for rr in range(r_blk):
            r = r0 + rr
            rs = int(np.clip(r - kh // 2, 0, rows - kh))
            for kk in range(k_rows):
                kr = k0 + kk
                ok = rs <= kr < rs + kh
                row_ok[v, rr, kk] = ok
                row_idx[v, rr, kk] = kr - r + NA_WIN_ROWS - 1 if ok else 0
    tab = by_col[:, row_idx]
    tab = jnp.where(jnp.asarray(row_ok)[None, :, :, :, None, None], tab, NEG)
    tab = tab.transpose(1, 0, 2, 4, 3, 5)
    nh = rpb.shape[0]
    return tab.reshape(3, nh, r_blk * GRID_W, k_rows * GRID_W)


def _na_attn_kernel(q_ref, *refs, heads):
    k_refs = refs[:NA_KEY_CHUNKS]
    v_refs = refs[NA_KEY_CHUNKS:2 * NA_KEY_CHUNKS]
    b_ref = refs[2 * NA_KEY_CHUNKS]
    o_ref = refs[2 * NA_KEY_CHUNKS + 1]
    ck = k_refs[0].shape[0]
    for h in range(heads):
        sl = slice(h * HEAD_DIM, (h + 1) * HEAD_DIM)
        q = q_ref[:, sl]
        s = []
        for c, k_ref in enumerate(k_refs):
            sc = lax.dot_general(q, k_ref[:, sl], (((1,), (1,)), ((), ())), preferred_element_type=F32)
            s.append(sc + b_ref[0, h, :, c * ck:(c + 1) * ck])
        m = functools.reduce(jnp.maximum, [jnp.max(sc, axis=-1, keepdims=True) for sc in s])
        p = [jnp.exp(sc - m) for sc in s]
        l = functools.reduce(jnp.add, [jnp.sum(pc, axis=-1, keepdims=True) for pc in p])
        o = functools.reduce(jnp.add, [
            jnp.dot(pc.astype(BF16), v_ref[:, sl], preferred_element_type=F32)
            for pc, v_ref in zip(p, v_refs)])
        o_ref[:, sl] = (o / l).astype(o_ref.dtype)


def na_attention(qkv, bias, *, batch, seq, heads_per_step):
    m = qkv.shape[0]
    nh = bias.shape[1]
    hg = heads_per_step
    wq = hg * HEAD_DIM
    qb = NA_BLOCK_ROWS * GRID_W
    kb = NA_KEY_ROWS * GRID_W // NA_KEY_CHUNKS
    nblk = seq // qb
    kpb = seq // kb
    ratio = qb // kb
    back = (NA_KEY_ROWS - NA_BLOCK_ROWS) // 2 * GRID_W // kb
    ncol = nh // hg

    def k_map(c, sec):
        def index(g, b, i):
            base = jnp.clip(i * ratio - back, 0, kpb - NA_KEY_CHUNKS)
            return (b * kpb + base + c, sec * ncol + g)
        return index

    def bias_map(g, b, i):
        return (jnp.where(i == 0, 0, jnp.where(i == nblk - 1, 2, 1)), g, 0, 0)

    in_specs = [pl.BlockSpec((qb, wq), lambda g, b, i: (b * nblk + i, g))]
    in_specs += [pl.BlockSpec((kb, wq), k_map(c, 1)) for c in range(NA_KEY_CHUNKS)]
    in_specs += [pl.BlockSpec((kb, wq), k_map(c, 2)) for c in range(NA_KEY_CHUNKS)]
    in_specs += [pl.BlockSpec((1, hg, qb, NA_KEY_CHUNKS * kb), bias_map)]
    vmem = (2 * 2 * qb * wq * 2 + 2 * 2 * NA_KEY_CHUNKS * kb * wq * 2
            + 2 * hg * qb * NA_KEY_CHUNKS * kb * 4 + 4 * qb * NA_KEY_CHUNKS * kb * 4)
    return pl.pallas_call(
        functools.partial(_na_attn_kernel, heads=hg),
        grid=(ncol, batch, nblk),
        in_specs=in_specs,
        out_specs=pl.BlockSpec((qb, wq), lambda g, b, i: (b * nblk + i, g)),
        out_shape=jax.ShapeDtypeStruct((m, nh * HEAD_DIM), BF16),
        compiler_params=_params(("parallel", "parallel", "arbitrary"), vmem),
        name="na_attention",
    )(qkv, *([qkv] * (2 * NA_KEY_CHUNKS)), bias)


def _proj_residual_kernel(a_ref, w_ref, r_ref, o_ref):
    o_ref[...] = r_ref[...] + jnp.dot(a_ref[...], w_ref[...], preferred_element_type=F32)


def proj_residual(a, w, res, *, tm):
    m, k = a.shape
    n = w.shape[1]
    vmem = 2 * tm * k * 2 + 2 * k * n * 2 + 4 * tm * n * 4 + tm * n * 4
    return pl.pallas_call(
        _proj_residual_kernel,
        grid=(m // tm,),
        in_specs=[
            pl.BlockSpec((tm, k), lambda i: (i, 0)),
            pl.BlockSpec((k, n), lambda i: (0, 0)),
            pl.BlockSpec((tm, n), lambda i: (i, 0)),
        ],
        out_specs=pl.BlockSpec((tm, n), lambda i: (i, 0)),
        out_shape=jax.ShapeDtypeStruct((m, n), F32),
        compiler_params=_params(("parallel",), vmem),
        name="na_proj_residual",
    )(a, w, res)


def _ffn_kernel(x_ref, g_ref, w1_ref, w2_ref, gf_ref, o_ref, h_ref, *, final_norm):
    f = pl.program_id(1)

    @pl.when(f == 0)
    def _():
        x = x_ref[...]
        h_ref[...] = _rms(x, g_ref[...]).astype(BF16)
        o_ref[...] = x

    a = jnp.dot(h_ref[...], w1_ref[...], preferred_element_type=F32)
    a = jnp.square(jnp.maximum(a, 0.0)).astype(BF16)
    o_ref[...] += jnp.dot(a, w2_ref[...], preferred_element_type=F32)

    if final_norm:
        @pl.when(f == pl.num_programs(1) - 1)
        def _():
            o_ref[...] = _rms(o_ref[...], gf_ref[...])


def ffn(x, g, w1, w2, g_final, *, tm, tf, final_norm):
    m, d = x.shape
    dff = w1.shape[1]
    vmem = 4 * tm * d * 4 + tm * d * 2 + 2 * 2 * d * tf * 2 + tm * tf * 6 + tm * d * 4
    return pl.pallas_call(
        functools.partial(_ffn_kernel, final_norm=final_norm),
        grid=(m // tm, dff // tf),
        in_specs=[
            pl.BlockSpec((tm, d), lambda i, f: (i, 0)),
            pl.BlockSpec((1, d), lambda i, f: (0, 0)),
            pl.BlockSpec((d, tf), lambda i, f: (0, f)),
            pl.BlockSpec((tf, d), lambda i, f: (f, 0)),
            pl.BlockSpec((1, d), lambda i, f: (0, 0)),
        ],
        out_specs=pl.BlockSpec((tm, d), lambda i, f: (i, 0)),
        out_shape=jax.ShapeDtypeStruct((m, d), F32),
        scratch_shapes=[pltpu.VMEM((tm, d), BF16)],
        compiler_params=_params(("parallel", "arbitrary"), vmem),
        name="ffn_final" if final_norm else "ffn",
    )(x, g.reshape(1, d), w1, w2, g_final.reshape(1, d))


def _dil_attn_kernel(q_ref, kp_ref, kc_ref, kn_ref, vp_ref, vc_ref, vn_ref, o_ref, lse_ref,
                     *, heads, halo, length):
    qb = q_ref.shape[0]
    nk = qb + 2 * halo
    m0 = pl.program_id(2) * qb
    qi = lax.broadcasted_iota(jnp.int32, (qb, nk), 0)
    kj = lax.broadcasted_iota(jnp.int32, (qb, nk), 1) - halo
    kpos = kj + m0
    valid = (jnp.abs(kj - qi) <= halo) & (kpos >= 0) & (kpos < length)
    lane = lax.broadcasted_iota(jnp.int32, (qb, HEAD_DIM), 1)
    lse_tile = jnp.zeros((qb, HEAD_DIM), F32)
    for h in range(heads):
        sl = slice(h * HEAD_DIM, (h + 1) * HEAD_DIM)
        k = jnp.concatenate([kp_ref[:, sl], kc_ref[:, sl], kn_ref[:, sl]], axis=0)
        v = jnp.concatenate([vp_ref[:, sl], vc_ref[:, sl], vn_ref[:, sl]], axis=0)
        s = lax.dot_general(q_ref[:, sl], k, (((1,), (1,)), ((), ())), preferred_element_type=F32)
        s = jnp.where(valid, s, NEG)
        m = jnp.max(s, axis=-1, keepdims=True)
        p = jnp.exp(s - m)
        l = jnp.sum(p, axis=-1, keepdims=True)
        o = jnp.dot(p.astype(BF16), v, preferred_element_type=F32)
        o_ref[:, sl] = (o / l).astype(o_ref.dtype)
        lse_tile = jnp.where(lane == h, m + jnp.log(l), lse_tile)
    lse_ref[...] = lse_tile


def dil_attention(qkv, *, group, n_groups, dil, halo, batch, seq, heads):
    m, ncols = qkv.shape
    length = seq // dil
    wq = heads * HEAD_DIM
    sec_per_tok = ncols // wq
    qb = BLOCK_Q
    assert qb % halo == 0 and length % qb == 0
    view = qkv.reshape(m // dil, dil * ncols)
    nblk = length // qb
    hpb = length // halo
    per = qb // halo

    def col(r, sec):
        return r * sec_per_tok + group * 3 + sec

    def cur_map(sec):
        return lambda b, r, i: (b * nblk + i, col(r, sec))

    def prev_map(sec):
        return lambda b, r, i: (b * hpb + jnp.maximum(i * per - 1, 0), col(r, sec))

    def next_map(sec):
        return lambda b, r, i: (b * hpb + jnp.minimum((i + 1) * per, hpb - 1), col(r, sec))

    in_specs = [pl.BlockSpec((qb, wq), cur_map(0))]
    for sec in (1, 2):
        in_specs += [pl.BlockSpec((halo, wq), prev_map(sec)),
                     pl.BlockSpec((qb, wq), cur_map(sec)),
                     pl.BlockSpec((halo, wq), next_map(sec))]
    vmem = 2 * (2 * qb * wq * 2 + 2 * (qb + 2 * halo) * wq * 2 + qb * HEAD_DIM * 4) + 8 * qb * (qb + 2 * halo) * 4
    o, lse = pl.pallas_call(
        functools.partial(_dil_attn_kernel, heads=heads, halo=halo, length=length),
        grid=(batch, dil, nblk),
        in_specs=in_specs,
        out_specs=[pl.BlockSpec((qb, wq), lambda b, r, i: (b * nblk + i, r)),
                   pl.BlockSpec((qb, HEAD_DIM), lambda b, r, i: (b * nblk + i, r))],
        out_shape=[jax.ShapeDtypeStruct((m // dil, dil * wq), BF16),
                   jax.ShapeDtypeStruct((m // dil, dil * HEAD_DIM), F32)],
        compiler_params=_params(("parallel", "parallel", "arbitrary"), vmem),
        name=f"dil_attention_g{group}",
    )(view, view, view, view, view, view, view)
    return o.reshape(m, wq), lse.reshape(m, HEAD_DIM)


def _merge_proj_kernel(*refs, n_groups, heads):
    o_refs = refs[:n_groups]
    lse_refs = refs[n_groups:2 * n_groups]
    w_ref, r_ref, out_ref, a_ref = refs[2 * n_groups:]
    lses = [ref[...] for ref in lse_refs]
    mx = functools.reduce(jnp.maximum, lses)
    es = [jnp.exp(l - mx) for l in lses]
    den = functools.reduce(jnp.add, es)
    wts = [e / den for e in es]
    for h in range(heads):
        sl = slice(h * HEAD_DIM, (h + 1) * HEAD_DIM)
        acc = None
        for wt, o_ref in zip(wts, o_refs):
            term = wt[:, h:h + 1] * o_ref[:, sl].astype(F32)
            acc = term if acc is None else acc + term
        a_ref[:, sl] = acc.astype(BF16)
    out_ref[...] = r_ref[...] + jnp.dot(a_ref[...], w_ref[...], preferred_element_type=F32)


def merge_proj_residual(outs, lses, w, res, *, tm):
    m, k = outs[0].shape
    n = w.shape[1]
    g = len(outs)
    heads = k // HEAD_DIM
    vmem = (2 * g * tm * k * 2 + 2 * g * tm * HEAD_DIM * 4 + 2 * k * n * 2 + 4 * tm * n * 4
            + tm * k * 2 + tm * n * 4)
    return pl.pallas_call(
        functools.partial(_merge_proj_kernel, n_groups=g, heads=heads),
        grid=(m // tm,),
        in_specs=([pl.BlockSpec((tm, k), lambda i: (i, 0))] * g
                  + [pl.BlockSpec((tm, HEAD_DIM), lambda i: (i, 0))] * g
                  + [pl.BlockSpec((k, n), lambda i: (0, 0)),
                     pl.BlockSpec((tm, n), lambda i: (i, 0))]),
        out_specs=pl.BlockSpec((tm, n), lambda i: (i, 0)),
        out_shape=jax.ShapeDtypeStruct((m, n), F32),
        scratch_shapes=[pltpu.VMEM((tm, k), BF16)],
        compiler_params=_params(("parallel",), vmem),
        name="dil_merge_proj_residual",
    )(*outs, *lses, w, res)


def kernel(x, na_norm, na_wqkv, na_rpb, na_wo, ffn0_norm, ffn0_w1, ffn0_w2, dil_norm, dil_wqkv, dil_wo,
           ffn1_norm, ffn1_w1, ffn1_w2, final_norm):
    batch, seq, d = x.shape
    m = batch * seq
    rows = seq // GRID_W
    na_heads = na_rpb.shape[0]
    n_groups = len(DIL_GROUPS)
    dil_heads = dil_wo.shape[0] // HEAD_DIM
    assert rows >= 2 * NA_KEY_ROWS and rows % NA_BLOCK_ROWS == 0

    xf = x.reshape(m, d)
    bf = lambda w: w.astype(BF16)

    qw = na_heads * HEAD_DIM
    col_scale = jnp.concatenate([jnp.full((qw,), HEAD_DIM ** -0.5, F32), jnp.ones((2 * qw,), F32)])
    qkv = norm_matmul(xf, na_norm, bf(na_wqkv), col_scale, tm=min(1024, m), tn=min(1024, qw))
    bias = _na_bias_table(na_rpb, rows)
    attn = na_attention(qkv, bias, batch=batch, seq=seq, heads_per_step=min(4, na_heads))
    xf = proj_residual(attn, bf(na_wo), xf, tm=min(512, m))
    xf = ffn(xf, ffn0_norm, bf(ffn0_w1), bf(ffn0_w2), final_norm, tm=min(512, m), tf=512, final_norm=False)

    rot = _rope_tables(seq)
    qkv = norm_matmul_rope(xf, dil_norm, bf(dil_wqkv), rot, seq=seq, tm=min(1024, seq),
                           tn=dil_heads * HEAD_DIM)
    outs, lses = [], []
    for gi, (window, dil) in enumerate(DIL_GROUPS):
        o, lse = dil_attention(qkv, group=gi, n_groups=n_groups, dil=dil, halo=(window // 2) // dil,
                               batch=batch, seq=seq, heads=dil_heads)
        outs.append(o)
        lses.append(lse)
    xf = merge_proj_residual(outs, lses, bf(dil_wo), xf, tm=min(512, m))
    xf = ffn(xf, ffn1_norm, bf(ffn1_w1), bf(ffn1_w2), final_norm, tm=min(512, m), tf=512, final_norm=True)
    return xf.reshape(batch, seq, d)
```

```python
import functools

import numpy as np
import jax
import jax.numpy as jnp
from jax import lax
from jax.experimental import pallas as pl
from jax.experimental.pallas import tpu as pltpu

GRID_W = 64
HEAD_DIM = 128
NA_WIN_ROWS = 8
NA_WIN_COLS = 16
DIL_GROUPS = ((128, 1), (512, 4), (2048, 16))
ROPE_THETA = 500000.0
ROPE_DIM = HEAD_DIM // 4
BLOCK_Q = 128
EPS = 1e-6
NEG = -1e30

NA_BLOCK_ROWS = 8
NA_KEY_ROWS = 2 * NA_BLOCK_ROWS
NA_KEY_CHUNKS = 4

ROW_CHUNK = 256
FFN_ROW_CHUNK = 512
BF16_SUBLANES = 16

V7X_VMEM_BYTES = 64 * 1024 * 1024
VMEM_CAP_BYTES = V7X_VMEM_BYTES - 8 * 1024 * 1024

F32 = jnp.float32
BF16 = jnp.bfloat16


def _params(semantics, vmem_bytes):
    limit = min(int(vmem_bytes * 1.25) + (4 << 20), VMEM_CAP_BYTES)
    return pltpu.CompilerParams(dimension_semantics=semantics, vmem_limit_bytes=limit)


def _rms(x, g):
    return x * lax.rsqrt(jnp.mean(x * x, axis=-1, keepdims=True) + EPS) * g


def _norm_matmul_kernel(x_ref, g_ref, w_ref, cs_ref, o_ref, h_ref):
    @pl.when(pl.program_id(1) == 0)
    def _():
        h_ref[...] = _rms(x_ref[...], g_ref[...]).astype(BF16)

    for c in range(x_ref.shape[0] // ROW_CHUNK):
        rows = slice(c * ROW_CHUNK, (c + 1) * ROW_CHUNK)
        acc = jnp.dot(h_ref[rows, :], w_ref[...], preferred_element_type=F32)
        o_ref[rows, :] = (acc * cs_ref[...]).astype(o_ref.dtype)


def norm_matmul(x, g, w, col_scale, *, tm, tn):
    m, k = x.shape
    n = w.shape[1]
    vmem = 2 * tm * k * 4 + tm * k * 2 + 2 * k * tn * 2 + 2 * tm * tn * 2 + tm * tn * 4
    return pl.pallas_call(
        _norm_matmul_kernel,
        grid=(m // tm, n // tn),
        in_specs=[
            pl.BlockSpec((tm, k), lambda i, j: (i, 0)),
            pl.BlockSpec((1, k), lambda i, j: (0, 0)),
            pl.BlockSpec((k, tn), lambda i, j: (0, j)),
            pl.BlockSpec((1, tn), lambda i, j: (0, j)),
        ],
        out_specs=pl.BlockSpec((tm, tn), lambda i, j: (i, j)),
        out_shape=jax.ShapeDtypeStruct((m, n), BF16),
        scratch_shapes=[pltpu.VMEM((tm, k), BF16)],
        compiler_params=_params(("parallel", "arbitrary"), vmem),
        name="norm_qkv_na",
    )(x, g.reshape(1, k), w, col_scale.reshape(1, n))


def _norm_matmul_rope_kernel(x_ref, g_ref, w_ref, rot_ref, *refs, dils):
    o_refs = refs[:len(dils)]
    h_ref, s_ref = refs[len(dils):]
    j = pl.program_id(1)
    tm = x_ref.shape[0]
    heads = s_ref.shape[0]
    half = ROPE_DIM // 2

    @pl.when(j == 0)
    def _():
        h_ref[...] = _rms(x_ref[...], g_ref[...]).astype(BF16)

    def body(o_ref, dil):
        n = ROW_CHUNK // dil
        for c in range(tm // ROW_CHUNK):
            rows = slice(c * ROW_CHUNK, (c + 1) * ROW_CHUNK)
            acc = jnp.dot(h_ref[rows, :], w_ref[...], preferred_element_type=F32)
            cos = rot_ref[0, rows, :]
            sin_hi = rot_ref[1, rows, :]
            sin_lo = rot_ref[2, rows, :]
            for h in range(heads):
                cols = slice(h * HEAD_DIM, (h + 1) * HEAD_DIM)
                a = acc[:, cols]
                rot = (a * cos + pltpu.roll(a, half, 1) * sin_hi
                       + pltpu.roll(a, HEAD_DIM - half, 1) * sin_lo)
                if dil == 1:
                    o_ref[0, rows, cols] = rot.astype(BF16)
                    continue
                s_ref[h, rows, :] = rot
                for r in range(dil):
                    o_ref[r, c * n:(c + 1) * n, cols] = (
                        s_ref[h, pl.ds(c * ROW_CHUNK + r, n, stride=dil), :].astype(BF16))

    for g, dil in enumerate(dils):
        pl.when(j // 3 == g)(functools.partial(body, o_refs[g], dil))


def _rope_tables(t):
    half = ROPE_DIM // 2
    pos = jnp.arange(t, dtype=F32)
    inv_freq = ROPE_THETA ** (-jnp.arange(0, ROPE_DIM, 2, dtype=F32) / ROPE_DIM)
    ang = pos[:, None] * inv_freq[None, :]
    cos = jnp.cos(ang)
    sin = jnp.sin(ang)
    rest = HEAD_DIM - ROPE_DIM
    one = jnp.ones((t, rest), F32)
    zero = jnp.zeros((t, rest), F32)
    zh = jnp.zeros((t, half), F32)
    k_tab = jnp.stack([jnp.concatenate([cos, cos, one], axis=-1),
                       jnp.concatenate([zh, sin, zero], axis=-1),
                       jnp.concatenate([-sin, zh, zero], axis=-1)], axis=0)
    v_tab = jnp.stack([jnp.ones((t, HEAD_DIM), F32), jnp.zeros((t, HEAD_DIM), F32),
                       jnp.zeros((t, HEAD_DIM), F32)], axis=0)
    return jnp.stack([k_tab * (HEAD_DIM ** -0.5), k_tab, v_tab], axis=0)


def norm_matmul_rope(x, g, w, rot, *, batch, seq, tm, dils):
    m, k = x.shape
    n = w.shape[1]
    tn = n // (3 * len(dils))
    heads = tn // HEAD_DIM
    tpb = seq // tm
    assert all(ROW_CHUNK % (BF16_SUBLANES * dil) == 0 for dil in dils) and tm % ROW_CHUNK == 0

    def out_map(g):
        return lambda i, j: (jnp.clip(j - 3 * g, 0, 2), i // tpb, 0, i % tpb, 0)

    vmem = (2 * tm * k * 4 + tm * k * 2 + 2 * k * tn * 2 + 2 * len(dils) * tm * tn * 2 + tm * tn * 4
            + 2 * ROW_CHUNK * tn * 4 + 2 * 3 * tm * HEAD_DIM * 4)
    return pl.pallas_call(
        functools.partial(_norm_matmul_rope_kernel, dils=dils),
        grid=(m // tm, n // tn),
        in_specs=[
            pl.BlockSpec((tm, k), lambda i, j: (i, 0)),
            pl.BlockSpec((1, k), lambda i, j: (0, 0)),
            pl.BlockSpec((k, tn), lambda i, j: (0, j)),
            pl.BlockSpec((None, 3, tm, HEAD_DIM), lambda i, j: (j % 3, 0, i % tpb, 0)),
        ],
        out_specs=[pl.BlockSpec((None, None, dil, tm // dil, tn), out_map(g)) for g, dil in enumerate(dils)],
        out_shape=[jax.ShapeDtypeStruct((3, batch, dil, seq // dil, tn), BF16) for dil in dils],
        scratch_shapes=[pltpu.VMEM((tm, k), BF16), pltpu.VMEM((heads, tm, HEAD_DIM), F32)],
        compiler_params=_params(("arbitrary", "arbitrary"), vmem),
        name="norm_qkv_dil_rope",
    )(x, g.reshape(1, k), w, rot)


def _na_bias_table(rpb, rows):
    kh = min(NA_WIN_ROWS, rows)
    kw = NA_WIN_COLS
    cols = np.arange(GRID_W)
    col_start = np.clip(cols - kw // 2, 0, GRID_W - kw)
    col_mask = (cols[None, :] >= col_start[:, None]) & (cols[None, :] < col_start[:, None] + kw)
    col_idx = np.clip(cols[None, :] - cols[:, None] + NA_WIN_COLS - 1, 0, 2 * NA_WIN_COLS - 2)
    by_col = jnp.where(jnp.asarray(col_mask)[None, None], rpb[:, :, col_idx], NEG)

    r_blk, k_rows = NA_BLOCK_ROWS, NA_KEY_ROWS
    row_idx = np.zeros((3, r_blk, k_rows), np.int32)
    row_ok = np.zeros((3, r_blk, k_rows), bool)
    for v, r0 in enumerate((0, r_blk, rows - r_blk)):
        k0 = int(np.clip(r0 - r_blk // 2, 0, rows - k_rows))
        for rr in range(r_blk):
            r = r0 + rr
            rs = int(np.clip(r - kh // 2, 0, rows - kh))
            for kk in range(k_rows):
                kr = k0 + kk
                ok = rs <= kr < rs + kh
                row_ok[v, rr, kk] = ok
                row_idx[v, rr, kk] = kr - r + NA_WIN_ROWS - 1 if ok else 0
    tab = by_col[:, row_idx]
    tab = jnp.where(jnp.asarray(row_ok)[None, :, :, :, None, None], tab, NEG)
    tab = tab.transpose(1, 0, 2, 4, 3, 5)
    nh = rpb.shape[0]
    return tab.reshape(3, nh, r_blk * GRID_W, k_rows * GRID_W)


def _na_attn_kernel(q_ref, *refs, heads):
    k_refs = refs[:NA_KEY_CHUNKS]
    v_refs = refs[NA_KEY_CHUNKS:2 * NA_KEY_CHUNKS]
    b_ref = refs[2 * NA_KEY_CHUNKS]
    o_ref = refs[2 * NA_KEY_CHUNKS + 1]
    ck = k_refs[0].shape[0]
    for h in range(heads):
        sl = slice(h * HEAD_DIM, (h + 1) * HEAD_DIM)
        q = q_ref[:, sl]
        s = []
        for c, k_ref in enumerate(k_refs):
            sc = lax.dot_general(q, k_ref[:, sl], (((1,), (1,)), ((), ())), preferred_element_type=F32)
            s.append(sc + b_ref[0, h, :, c * ck:(c + 1) * ck])
        m = functools.reduce(jnp.maximum, [jnp.max(sc, axis=-1, keepdims=True) for sc in s])
        p = [jnp.exp(sc - m) for sc in s]
        l = functools.reduce(jnp.add, [jnp.sum(pc, axis=-1, keepdims=True) for pc in p])
        o = functools.reduce(jnp.add, [
            jnp.dot(pc.astype(BF16), v_ref[:, sl], preferred_element_type=F32)
            for pc, v_ref in zip(p, v_refs)])
        o_ref[:, sl] = (o / l).astype(o_ref.dtype)


def na_attention(qkv, bias, *, batch, seq, heads_per_step):
    m = qkv.shape[0]
    nh = bias.shape[1]
    hg = heads_per_step
    wq = hg * HEAD_DIM
    qb = NA_BLOCK_ROWS * GRID_W
    kb = NA_KEY_ROWS * GRID_W // NA_KEY_CHUNKS
    nblk = seq // qb
    kpb = seq // kb
    ratio = qb // kb
    back = (NA_KEY_ROWS - NA_BLOCK_ROWS) // 2 * GRID_W // kb
    ncol = nh // hg

    def k_map(c, sec):
        def index(g, b, i):
            base = jnp.clip(i * ratio - back, 0, kpb - NA_KEY_CHUNKS)
            return (b * kpb + base + c, sec * ncol + g)
        return index

    def bias_map(g, b, i):
        return (jnp.where(i == 0, 0, jnp.where(i == nblk - 1, 2, 1)), g, 0, 0)

    in_specs = [pl.BlockSpec((qb, wq), lambda g, b, i: (b * nblk + i, g))]
    in_specs += [pl.BlockSpec((kb, wq), k_map(c, 1)) for c in range(NA_KEY_CHUNKS)]
    in_specs += [pl.BlockSpec((kb, wq), k_map(c, 2)) for c in range(NA_KEY_CHUNKS)]
    in_specs += [pl.BlockSpec((1, hg, qb, NA_KEY_CHUNKS * kb), bias_map)]
    vmem = (2 * 2 * qb * wq * 2 + 2 * 2 * NA_KEY_CHUNKS * kb * wq * 2
            + 2 * hg * qb * NA_KEY_CHUNKS * kb * 4 + 4 * qb * NA_KEY_CHUNKS * kb * 4)
    return pl.pallas_call(
        functools.partial(_na_attn_kernel, heads=hg),
        grid=(ncol, batch, nblk),
        in_specs=in_specs,
        out_specs=pl.BlockSpec((qb, wq), lambda g, b, i: (b * nblk + i, g)),
        out_shape=jax.ShapeDtypeStruct((m, nh * HEAD_DIM), BF16),
        compiler_params=_params(("parallel", "parallel", "arbitrary"), vmem),
        name="na_attention",
    )(qkv, *([qkv] * (2 * NA_KEY_CHUNKS)), bias)


def _proj_residual_kernel(a_ref, w_ref, r_ref, o_ref):
    o_ref[...] = r_ref[...] + jnp.dot(a_ref[...], w_ref[...], preferred_element_type=F32)


def proj_residual(a, w, res, *, tm):
    m, k = a.shape
    n = w.shape[1]
    vmem = 2 * tm * k * 2 + 2 * k * n * 2 + 4 * tm * n * 4 + tm * n * 4
    return pl.pallas_call(
        _proj_residual_kernel,
        grid=(m // tm,),
        in_specs=[
            pl.BlockSpec((tm, k), lambda i: (i, 0)),
            pl.BlockSpec((k, n), lambda i: (0, 0)),
            pl.BlockSpec((tm, n), lambda i: (i, 0)),
        ],
        out_specs=pl.BlockSpec((tm, n), lambda i: (i, 0)),
        out_shape=jax.ShapeDtypeStruct((m, n), F32),
        compiler_params=_params(("parallel",), vmem),
        name="na_proj_residual",
    )(a, w, res)


def _ffn_kernel(x_ref, g_ref, w1_ref, w2_ref, gf_ref, o_ref, h_ref, *, final_norm):
    f = pl.program_id(1)

    @pl.when(f == 0)
    def _():
        x = x_ref[...]
        h_ref[...] = _rms(x, g_ref[...]).astype(BF16)
        o_ref[...] = x

    for c in range(x_ref.shape[0] // FFN_ROW_CHUNK):
        rows = slice(c * FFN_ROW_CHUNK, (c + 1) * FFN_ROW_CHUNK)
        a = jnp.dot(h_ref[rows, :], w1_ref[...], preferred_element_type=F32)
        a = jnp.square(jnp.maximum(a, 0.0)).astype(BF16)
        o_ref[rows, :] += jnp.dot(a, w2_ref[...], preferred_element_type=F32)

    if final_norm:
        @pl.when(f == pl.num_programs(1) - 1)
        def _():
            o_ref[...] = _rms(o_ref[...], gf_ref[...])


def ffn(x, g, w1, w2, g_final, *, tm, tf, final_norm):
    m, d = x.shape
    dff = w1.shape[1]
    assert tm % FFN_ROW_CHUNK == 0
    vmem = (4 * tm * d * 4 + tm * d * 2 + 2 * 2 * d * tf * 2
            + FFN_ROW_CHUNK * tf * 6 + FFN_ROW_CHUNK * d * 4)
    return pl.pallas_call(
        functools.partial(_ffn_kernel, final_norm=final_norm),
        grid=(m // tm, dff // tf),
        in_specs=[
            pl.BlockSpec((tm, d), lambda i, f: (i, 0)),
            pl.BlockSpec((1, d), lambda i, f: (0, 0)),
            pl.BlockSpec((d, tf), lambda i, f: (0, f)),
            pl.BlockSpec((tf, d), lambda i, f: (f, 0)),
            pl.BlockSpec((1, d), lambda i, f: (0, 0)),
        ],
        out_specs=pl.BlockSpec((tm, d), lambda i, f: (i, 0)),
        out_shape=jax.ShapeDtypeStruct((m, d), F32),
        scratch_shapes=[pltpu.VMEM((tm, d), BF16)],
        compiler_params=_params(("parallel", "arbitrary"), vmem),
        name="ffn_final" if final_norm else "ffn",
    )(x, g.reshape(1, d), w1, w2, g_final.reshape(1, d))


def _dil_attn_kernel(q_ref, kp_ref, kc_ref, kn_ref, vp_ref, vc_ref, vn_ref, o_ref, lse_ref,
                     *, heads, halo, length):
    qb = q_ref.shape[0]
    nk = qb + 2 * halo
    m0 = pl.program_id(2) * qb
    qi = lax.broadcasted_iota(jnp.int32, (qb, nk), 0)
    kj = lax.broadcasted_iota(jnp.int32, (qb, nk), 1) - halo
    kpos = kj + m0
    valid = (jnp.abs(kj - qi) <= halo) & (kpos >= 0) & (kpos < length)
    lane = lax.broadcasted_iota(jnp.int32, (qb, HEAD_DIM), 1)
    lse_tile = jnp.zeros((qb, HEAD_DIM), F32)
    for h in range(heads):
        sl = slice(h * HEAD_DIM, (h + 1) * HEAD_DIM)
        k = jnp.concatenate([kp_ref[:, sl], kc_ref[:, sl], kn_ref[:, sl]], axis=0)
        v = jnp.concatenate([vp_ref[:, sl], vc_ref[:, sl], vn_ref[:, sl]], axis=0)
        s = lax.dot_general(q_ref[:, sl], k, (((1,), (1,)), ((), ())), preferred_element_type=F32)
        s = jnp.where(valid, s, NEG)
        m = jnp.max(s, axis=-1, keepdims=True)
        p = jnp.exp(s - m)
        l = jnp.sum(p, axis=-1, keepdims=True)
        o = jnp.dot(p.astype(BF16), v, preferred_element_type=F32)
        o_ref[:, sl] = (o / l).astype(o_ref.dtype)
        lse_tile = jnp.where(lane == h, m + jnp.log(l), lse_tile)
    lse_ref[...] = lse_tile


def dil_attention(qkv, *, group, halo):
    _, batch, dil, length, wq = qkv.shape
    heads = wq // HEAD_DIM
    qb = BLOCK_Q
    assert qb % halo == 0 and length % qb == 0
    nblk = length // qb
    hpb = length // halo
    per = qb // halo

    def cur_map(sec):
        return lambda b, r, i: (sec, b, r, i, 0)

    def prev_map(sec):
        return lambda b, r, i: (sec, b, r, jnp.maximum(i * per - 1, 0), 0)

    def next_map(sec):
        return lambda b, r, i: (sec, b, r, jnp.minimum((i + 1) * per, hpb - 1), 0)

    def spec(rows, index_map):
        return pl.BlockSpec((None, None, None, rows, wq), index_map)

    in_specs = [spec(qb, cur_map(0))]
    for sec in (1, 2):
        in_specs += [spec(halo, prev_map(sec)), spec(qb, cur_map(sec)), spec(halo, next_map(sec))]
    vmem = 2 * (2 * qb * wq * 2 + 2 * (qb + 2 * halo) * wq * 2 + qb * HEAD_DIM * 4) + 8 * qb * (qb + 2 * halo) * 4
    return pl.pallas_call(
        functools.partial(_dil_attn_kernel, heads=heads, halo=halo, length=length),
        grid=(batch, dil, nblk),
        in_specs=in_specs,
        out_specs=[pl.BlockSpec((None, None, qb, wq), lambda b, r, i: (b, r, i, 0)),
                   pl.BlockSpec((None, None, qb, HEAD_DIM), lambda b, r, i: (b, r, i, 0))],
        out_shape=[jax.ShapeDtypeStruct((batch, dil, length, wq), BF16),
                   jax.ShapeDtypeStruct((batch, dil, length, HEAD_DIM), F32)],
        compiler_params=_params(("parallel", "parallel", "arbitrary"), vmem),
        name=f"dil_attention_g{group}",
    )(*([qkv] * 7))


def _merge_proj_kernel(*refs, dils, heads):
    g_n = len(dils)
    o_refs = refs[:g_n]
    lse_refs = refs[g_n:2 * g_n]
    w_ref, r_ref, out_ref, a_ref, lt_ref, ot_ref = refs[2 * g_n:]
    tm = a_ref.shape[0]

    lses = []
    for g, dil in enumerate(dils):
        if dil == 1:
            lses.append(lse_refs[g][0])
            continue
        n = tm // dil
        for r in range(dil):
            lt_ref[g, pl.ds(r, n, stride=dil), :] = lse_refs[g][r]
            for h in range(heads):
                ot_ref[g * heads + h, pl.ds(r, n, stride=dil), :] = (
                    o_refs[g][r, :, h * HEAD_DIM:(h + 1) * HEAD_DIM].astype(F32))
        lses.append(lt_ref[g])

    mx = functools.reduce(jnp.maximum, lses)
    es = [jnp.exp(l - mx) for l in lses]
    den = functools.reduce(jnp.add, es)
    wts = [e / den for e in es]
    for h in range(heads):
        sl = slice(h * HEAD_DIM, (h + 1) * HEAD_DIM)
        acc = None
        for g, dil in enumerate(dils):
            o = o_refs[g][0, :, sl].astype(F32) if dil == 1 else ot_ref[g * heads + h]
            term = wts[g][:, h:h + 1] * o
            acc = term if acc is None else acc + term
        a_ref[:, sl] = acc.astype(BF16)
    out_ref[...] = r_ref[...] + jnp.dot(a_ref[...], w_ref[...], preferred_element_type=F32)


def merge_proj_residual(outs, lses, w, res, *, seq, tm):
    dils = tuple(o.shape[1] for o in outs)
    k = outs[0].shape[-1]
    m, n = res.shape
    g_n = len(outs)
    heads = k // HEAD_DIM
    tpb = seq // tm
    assert all(tm % (BF16_SUBLANES * dil) == 0 for dil in dils)

    def stream_spec(dil, width):
        return pl.BlockSpec((None, dil, tm // dil, width), lambda i: (i // tpb, 0, i % tpb, 0))

    vmem = (2 * g_n * tm * k * 2 + 2 * g_n * tm * HEAD_DIM * 4 + 2 * k * n * 2 + 4 * tm * n * 4
            + tm * k * 2 + tm * n * 4 + g_n * tm * HEAD_DIM * 4 + g_n * tm * k * 4)
    return pl.pallas_call(
        functools.partial(_merge_proj_kernel, dils=dils, heads=heads),
        grid=(m // tm,),
        in_specs=([stream_spec(dil, k) for dil in dils]
                  + [stream_spec(dil, HEAD_DIM) for dil in dils]
                  + [pl.BlockSpec((k, n), lambda i: (0, 0)),
                     pl.BlockSpec((tm, n), lambda i: (i, 0))]),
        out_specs=pl.BlockSpec((tm, n), lambda i: (i, 0)),
        out_shape=jax.ShapeDtypeStruct((m, n), F32),
        scratch_shapes=[pltpu.VMEM((tm, k), BF16),
                        pltpu.VMEM((g_n, tm, HEAD_DIM), F32),
                        pltpu.VMEM((g_n * heads, tm, HEAD_DIM), F32)],
        compiler_params=_params(("parallel",), vmem),
        name="dil_merge_proj_residual",
    )(*outs, *lses, w, res)


def kernel(x, na_norm, na_wqkv, na_rpb, na_wo, ffn0_norm, ffn0_w1, ffn0_w2, dil_norm, dil_wqkv, dil_wo,
           ffn1_norm, ffn1_w1, ffn1_w2, final_norm):
    batch, seq, d = x.shape
    m = batch * seq
    rows = seq // GRID_W
    na_heads = na_rpb.shape[0]
    assert rows >= 2 * NA_KEY_ROWS and rows % NA_BLOCK_ROWS == 0

    xf = x.reshape(m, d)
    bf = lambda w: w.astype(BF16)

    qw = na_heads * HEAD_DIM
    col_scale = jnp.concatenate([jnp.full((qw,), HEAD_DIM ** -0.5, F32), jnp.ones((2 * qw,), F32)])
    qkv = norm_matmul(xf, na_norm, bf(na_wqkv), col_scale, tm=min(1024, m), tn=min(1024, qw))
    bias = _na_bias_table(na_rpb, rows)
    attn = na_attention(qkv, bias, batch=batch, seq=seq, heads_per_step=min(4, na_heads))
    xf = proj_residual(attn, bf(na_wo), xf, tm=min(512, m))
    xf = ffn(xf, ffn0_norm, bf(ffn0_w1), bf(ffn0_w2), final_norm, tm=min(1024, m), tf=512, final_norm=False)

    rot = _rope_tables(seq)
    dils = tuple(dil for _, dil in DIL_GROUPS)
    qkvs = norm_matmul_rope(xf, dil_norm, bf(dil_wqkv), rot, batch=batch, seq=seq, tm=min(1024, seq), dils=dils)
    outs, lses = [], []
    for gi, (window, dil) in enumerate(DIL_GROUPS):
        o, lse = dil_attention(qkvs[gi], group=gi, halo=(window // 2) // dil)
        outs.append(o)
        lses.append(lse)
    xf = merge_proj_residual(outs, lses, bf(dil_wo), xf, seq=seq, tm=min(512, seq))
    xf = ffn(xf, ffn1_norm, bf(ffn1_w1), bf(ffn1_w2), final_norm, tm=min(1024, m), tf=512, final_norm=True)
    return xf.reshape(batch, seq, d)
```

```python
import functools

import numpy as np
import jax
import jax.numpy as jnp
from jax import lax
from jax.experimental import pallas as pl
from jax.experimental.pallas import tpu as pltpu

GRID_W = 64
HEAD_DIM = 128
NA_WIN_ROWS = 8
NA_WIN_COLS = 16
DIL_GROUPS = ((128, 1), (512, 4), (2048, 16))
ROPE_THETA = 500000.0
ROPE_DIM = HEAD_DIM // 4
BLOCK_Q = 128
EPS = 1e-6
NEG = -1e30

NA_BLOCK_ROWS = 4
NA_KEY_BACK = NA_WIN_ROWS // 2
NA_KEY_ROWS = 12
NA_KEY_CHUNKS = 3

DIL_STEP_ROWS = 256
ROW_CHUNK = 256
FFN_ROW_CHUNK = 512
BF16_SUBLANES = 16

V7X_VMEM_BYTES = 64 * 1024 * 1024
VMEM_CAP_BYTES = V7X_VMEM_BYTES - 8 * 1024 * 1024

F32 = jnp.float32
BF16 = jnp.bfloat16


def _params(semantics, vmem_bytes):
    limit = min(int(vmem_bytes * 1.25) + (4 << 20), VMEM_CAP_BYTES)
    return pltpu.CompilerParams(dimension_semantics=semantics, vmem_limit_bytes=limit)


def _rms(x, g):
    return x * lax.rsqrt(jnp.mean(x * x, axis=-1, keepdims=True) + EPS) * g


def _norm_matmul_kernel(x_ref, g_ref, w_ref, cs_ref, o_ref, h_ref):
    @pl.when(pl.program_id(1) == 0)
    def _():
        h_ref[...] = _rms(x_ref[...], g_ref[...]).astype(BF16)

    for c in range(x_ref.shape[0] // ROW_CHUNK):
        rows = slice(c * ROW_CHUNK, (c + 1) * ROW_CHUNK)
        acc = jnp.dot(h_ref[rows, :], w_ref[...], preferred_element_type=F32)
        o_ref[rows, :] = (acc * cs_ref[...]).astype(o_ref.dtype)


def norm_matmul(x, g, w, col_scale, *, tm, tn):
    m, k = x.shape
    n = w.shape[1]
    vmem = 2 * tm * k * 4 + tm * k * 2 + 2 * k * tn * 2 + 2 * tm * tn * 2 + tm * tn * 4
    return pl.pallas_call(
        _norm_matmul_kernel,
        grid=(m // tm, n // tn),
        in_specs=[
            pl.BlockSpec((tm, k), lambda i, j: (i, 0)),
            pl.BlockSpec((1, k), lambda i, j: (0, 0)),
            pl.BlockSpec((k, tn), lambda i, j: (0, j)),
            pl.BlockSpec((1, tn), lambda i, j: (0, j)),
        ],
        out_specs=pl.BlockSpec((tm, tn), lambda i, j: (i, j)),
        out_shape=jax.ShapeDtypeStruct((m, n), BF16),
        scratch_shapes=[pltpu.VMEM((tm, k), BF16)],
        compiler_params=_params(("parallel", "arbitrary"), vmem),
        name="norm_qkv_na",
    )(x, g.reshape(1, k), w, col_scale.reshape(1, n))


def _norm_matmul_rope_kernel(x_ref, g_ref, w_ref, rot_ref, *refs, dils):
    o_refs = refs[:len(dils)]
    h_ref, s_ref = refs[len(dils):]
    j = pl.program_id(1)
    tm = x_ref.shape[0]
    heads = s_ref.shape[0]
    half = ROPE_DIM // 2

    @pl.when(j == 0)
    def _():
        h_ref[...] = _rms(x_ref[...], g_ref[...]).astype(BF16)

    def body(o_ref, dil):
        n = ROW_CHUNK // dil
        for c in range(tm // ROW_CHUNK):
            rows = slice(c * ROW_CHUNK, (c + 1) * ROW_CHUNK)
            acc = jnp.dot(h_ref[rows, :], w_ref[...], preferred_element_type=F32)
            cos = rot_ref[0, rows, :]
            sin_hi = rot_ref[1, rows, :]
            sin_lo = rot_ref[2, rows, :]
            for h in range(heads):
                cols = slice(h * HEAD_DIM, (h + 1) * HEAD_DIM)
                a = acc[:, cols]
                rot = (a * cos + pltpu.roll(a, half, 1) * sin_hi
                       + pltpu.roll(a, HEAD_DIM - half, 1) * sin_lo)
                if dil == 1:
                    o_ref[0, rows, cols] = rot.astype(BF16)
                    continue
                s_ref[h, rows, :] = rot
                for r in range(dil):
                    o_ref[r, c * n:(c + 1) * n, cols] = (
                        s_ref[h, pl.ds(c * ROW_CHUNK + r, n, stride=dil), :].astype(BF16))

    for g, dil in enumerate(dils):
        pl.when(j // 3 == g)(functools.partial(body, o_refs[g], dil))


def _rope_tables(t):
    half = ROPE_DIM // 2
    pos = jnp.arange(t, dtype=F32)
    inv_freq = ROPE_THETA ** (-jnp.arange(0, ROPE_DIM, 2, dtype=F32) / ROPE_DIM)
    ang = pos[:, None] * inv_freq[None, :]
    cos = jnp.cos(ang)
    sin = jnp.sin(ang)
    rest = HEAD_DIM - ROPE_DIM
    one = jnp.ones((t, rest), F32)
    zero = jnp.zeros((t, rest), F32)
    zh = jnp.zeros((t, half), F32)
    k_tab = jnp.stack([jnp.concatenate([cos, cos, one], axis=-1),
                       jnp.concatenate([zh, sin, zero], axis=-1),
                       jnp.concatenate([-sin, zh, zero], axis=-1)], axis=0)
    v_tab = jnp.stack([jnp.ones((t, HEAD_DIM), F32), jnp.zeros((t, HEAD_DIM), F32),
                       jnp.zeros((t, HEAD_DIM), F32)], axis=0)
    return jnp.stack([k_tab * (HEAD_DIM ** -0.5), k_tab, v_tab], axis=0)


def norm_matmul_rope(x, g, w, rot, *, batch, seq, tm, dils):
    m, k = x.shape
    n = w.shape[1]
    tn = n // (3 * len(dils))
    heads = tn // HEAD_DIM
    tpb = seq // tm
    assert all(ROW_CHUNK % (BF16_SUBLANES * dil) == 0 for dil in dils) and tm % ROW_CHUNK == 0

    def out_map(g):
        return lambda i, j: (jnp.clip(j - 3 * g, 0, 2), i // tpb, 0, i % tpb, 0)

    vmem = (2 * tm * k * 4 + tm * k * 2 + 2 * k * tn * 2 + 2 * len(dils) * tm * tn * 2 + tm * tn * 4
            + 2 * ROW_CHUNK * tn * 4 + 2 * 3 * tm * HEAD_DIM * 4)
    return pl.pallas_call(
        functools.partial(_norm_matmul_rope_kernel, dils=dils),
        grid=(m // tm, n // tn),
        in_specs=[
            pl.BlockSpec((tm, k), lambda i, j: (i, 0)),
            pl.BlockSpec((1, k), lambda i, j: (0, 0)),
            pl.BlockSpec((k, tn), lambda i, j: (0, j)),
            pl.BlockSpec((None, 3, tm, HEAD_DIM), lambda i, j: (j % 3, 0, i % tpb, 0)),
        ],
        out_specs=[pl.BlockSpec((None, None, dil, tm // dil, tn), out_map(g)) for g, dil in enumerate(dils)],
        out_shape=[jax.ShapeDtypeStruct((3, batch, dil, seq // dil, tn), BF16) for dil in dils],
        scratch_shapes=[pltpu.VMEM((tm, k), BF16), pltpu.VMEM((heads, tm, HEAD_DIM), F32)],
        compiler_params=_params(("arbitrary", "arbitrary"), vmem),
        name="norm_qkv_dil_rope",
    )(x, g.reshape(1, k), w, rot)


def _na_bias_table(rpb):
    kw = NA_WIN_COLS
    cols = np.arange(GRID_W)
    col_start = np.clip(cols - kw // 2, 0, GRID_W - kw)
    col_mask = (cols[None, :] >= col_start[:, None]) & (cols[None, :] < col_start[:, None] + kw)
    col_idx = np.clip(cols[None, :] - cols[:, None] + NA_WIN_COLS - 1, 0, 2 * NA_WIN_COLS - 2)
    by_col = jnp.where(jnp.asarray(col_mask)[None, None], rpb[:, :, col_idx], NEG)
    rr = np.arange(NA_BLOCK_ROWS)[:, None]
    kk = np.arange(NA_KEY_ROWS)[None, :]
    row_idx = kk - NA_KEY_BACK - rr + NA_WIN_ROWS - 1
    assert row_idx.min() >= 0 and row_idx.max() <= 2 * NA_WIN_ROWS - 2
    tab = by_col[:, row_idx].transpose(0, 1, 3, 2, 4)
    return tab.reshape(rpb.shape[0], NA_BLOCK_ROWS * GRID_W, NA_KEY_ROWS * GRID_W)


def _na_attn_kernel(q_ref, *refs, heads, rows):
    k_refs = refs[:NA_KEY_CHUNKS]
    v_refs = refs[NA_KEY_CHUNKS:2 * NA_KEY_CHUNKS]
    b_ref, o_ref, bm_ref = refs[2 * NA_KEY_CHUNKS:]
    qb = q_ref.shape[0]
    ck = k_refs[0].shape[0]
    kh = min(NA_WIN_ROWS, rows)
    i = pl.program_id(2)

    def clipped(blk):
        r0 = blk * NA_BLOCK_ROWS
        return (r0 < kh // 2) | (r0 + NA_BLOCK_ROWS - 1 - kh // 2 > rows - kh)

    @pl.when((i == 0) | clipped(i) | clipped(i - 1))
    def _():
        r0 = i * NA_BLOCK_ROWS
        q_row = r0 + lax.broadcasted_iota(jnp.int32, (qb, ck), 0) // GRID_W
        lo = jnp.clip(q_row - kh // 2, 0, rows - kh)
        for c in range(NA_KEY_CHUNKS):
            k_row = (r0 - NA_KEY_BACK + c * (ck // GRID_W)
                     + lax.broadcasted_iota(jnp.int32, (qb, ck), 1) // GRID_W)
            row_mask = jnp.where((k_row >= lo) & (k_row < lo + kh), 0.0, NEG).astype(F32)
            for h in range(heads):
                bm_ref[h, :, c * ck:(c + 1) * ck] = b_ref[h, :, c * ck:(c + 1) * ck] + row_mask

    def scores(h):
        sl = slice(h * HEAD_DIM, (h + 1) * HEAD_DIM)
        q = q_ref[:, sl]
        s = []
        for c, k_ref in enumerate(k_refs):
            sc = lax.dot_general(q, k_ref[:, sl], (((1,), (1,)), ((), ())), preferred_element_type=F32)
            s.append(sc + bm_ref[h, :, c * ck:(c + 1) * ck])
        return s, jnp.max(functools.reduce(jnp.maximum, s), axis=-1, keepdims=True)

    def finish(h, s, m):
        sl = slice(h * HEAD_DIM, (h + 1) * HEAD_DIM)
        p = [jnp.exp(sc - m) for sc in s]
        l = jnp.sum(functools.reduce(jnp.add, p), axis=-1, keepdims=True)
        o = functools.reduce(jnp.add, [
            jnp.dot(pc.astype(BF16), v_ref[:, sl], preferred_element_type=F32)
            for pc, v_ref in zip(p, v_refs)])
        o_ref[:, sl] = (o / l).astype(o_ref.dtype)

    cur = scores(0)
    for h in range(heads):
        nxt = scores(h + 1) if h + 1 < heads else None
        finish(h, *cur)
        cur = nxt


def na_attention(qkv, bias, *, batch, seq, heads_per_step):
    m = qkv.shape[0]
    nh = bias.shape[0]
    hg = heads_per_step
    wq = hg * HEAD_DIM
    qb = NA_BLOCK_ROWS * GRID_W
    kb = NA_KEY_ROWS * GRID_W // NA_KEY_CHUNKS
    assert qb % kb == 0 and (NA_KEY_BACK * GRID_W) % kb == 0
    nblk = seq // qb
    kpb = seq // kb
    ratio = qb // kb
    back = NA_KEY_BACK * GRID_W // kb
    ncol = nh // hg

    def k_map(c, sec):
        def index(g, b, i):
            return (b * kpb + jnp.clip(i * ratio - back + c, 0, kpb - 1), sec * ncol + g)
        return index

    in_specs = [pl.BlockSpec((qb, wq), lambda g, b, i: (b * nblk + i, g))]
    in_specs += [pl.BlockSpec((kb, wq), k_map(c, 1)) for c in range(NA_KEY_CHUNKS)]
    in_specs += [pl.BlockSpec((kb, wq), k_map(c, 2)) for c in range(NA_KEY_CHUNKS)]
    in_specs += [pl.BlockSpec((hg, qb, NA_KEY_CHUNKS * kb), lambda g, b, i: (g, 0, 0))]
    vmem = (2 * 2 * qb * wq * 2 + 2 * 2 * NA_KEY_CHUNKS * kb * wq * 2
            + 3 * hg * qb * NA_KEY_CHUNKS * kb * 4 + 6 * qb * NA_KEY_CHUNKS * kb * 4)
    return pl.pallas_call(
        functools.partial(_na_attn_kernel, heads=hg, rows=seq // GRID_W),
        grid=(ncol, batch, nblk),
        in_specs=in_specs,
        out_specs=pl.BlockSpec((qb, wq), lambda g, b, i: (b * nblk + i, g)),
        out_shape=jax.ShapeDtypeStruct((m, nh * HEAD_DIM), BF16),
        scratch_shapes=[pltpu.VMEM((hg, qb, NA_KEY_CHUNKS * kb), F32)],
        compiler_params=_params(("parallel", "parallel", "arbitrary"), vmem),
        name="na_attention",
    )(qkv, *([qkv] * (2 * NA_KEY_CHUNKS)), bias)


def _proj_residual_kernel(a_ref, w_ref, r_ref, o_ref):
    o_ref[...] = r_ref[...] + jnp.dot(a_ref[...], w_ref[...], preferred_element_type=F32)


def proj_residual(a, w, res, *, tm):
    m, k = a.shape
    n = w.shape[1]
    vmem = 2 * tm * k * 2 + 2 * k * n * 2 + 4 * tm * n * 4 + tm * n * 4
    return pl.pallas_call(
        _proj_residual_kernel,
        grid=(m // tm,),
        in_specs=[
            pl.BlockSpec((tm, k), lambda i: (i, 0)),
            pl.BlockSpec((k, n), lambda i: (0, 0)),
            pl.BlockSpec((tm, n), lambda i: (i, 0)),
        ],
        out_specs=pl.BlockSpec((tm, n), lambda i: (i, 0)),
        out_shape=jax.ShapeDtypeStruct((m, n), F32),
        compiler_params=_params(("parallel",), vmem),
        name="na_proj_residual",
    )(a, w, res)


def _ffn_kernel(x_ref, g_ref, w1_ref, w2_ref, gf_ref, o_ref, h_ref, *, final_norm):
    f = pl.program_id(1)

    @pl.when(f == 0)
    def _():
        x = x_ref[...]
        h_ref[...] = _rms(x, g_ref[...]).astype(BF16)
        o_ref[...] = x

    for c in range(x_ref.shape[0] // FFN_ROW_CHUNK):
        rows = slice(c * FFN_ROW_CHUNK, (c + 1) * FFN_ROW_CHUNK)
        a = jnp.dot(h_ref[rows, :], w1_ref[...], preferred_element_type=F32)
        a = jnp.square(jnp.maximum(a, 0.0)).astype(BF16)
        o_ref[rows, :] += jnp.dot(a, w2_ref[...], preferred_element_type=F32)

    if final_norm:
        @pl.when(f == pl.num_programs(1) - 1)
        def _():
            o_ref[...] = _rms(o_ref[...], gf_ref[...])


def ffn(x, g, w1, w2, g_final, *, tm, tf, final_norm):
    m, d = x.shape
    dff = w1.shape[1]
    assert tm % FFN_ROW_CHUNK == 0
    vmem = (4 * tm * d * 4 + tm * d * 2 + 2 * 2 * d * tf * 2
            + FFN_ROW_CHUNK * tf * 6 + FFN_ROW_CHUNK * d * 4)
    return pl.pallas_call(
        functools.partial(_ffn_kernel, final_norm=final_norm),
        grid=(m // tm, dff // tf),
        in_specs=[
            pl.BlockSpec((tm, d), lambda i, f: (i, 0)),
            pl.BlockSpec((1, d), lambda i, f: (0, 0)),
            pl.BlockSpec((d, tf), lambda i, f: (0, f)),
            pl.BlockSpec((tf, d), lambda i, f: (f, 0)),
            pl.BlockSpec((1, d), lambda i, f: (0, 0)),
        ],
        out_specs=pl.BlockSpec((tm, d), lambda i, f: (i, 0)),
        out_shape=jax.ShapeDtypeStruct((m, d), F32),
        scratch_shapes=[pltpu.VMEM((tm, d), BF16)],
        compiler_params=_params(("parallel", "arbitrary"), vmem),
        name="ffn_final" if final_norm else "ffn",
    )(x, g.reshape(1, d), w1, w2, g_final.reshape(1, d))


def _dil_attn_kernel(q_ref, kp_ref, kc_ref, kn_ref, vp_ref, vc_ref, vn_ref, o_ref, lse_ref,
                     *, heads, halo, length):
    qb = q_ref.shape[0]
    sub = BLOCK_Q
    nk = sub + 2 * halo
    m0 = pl.program_id(2) * qb
    qi = lax.broadcasted_iota(jnp.int32, (sub, nk), 0)
    kj = lax.broadcasted_iota(jnp.int32, (sub, nk), 1) - halo
    band = jnp.abs(kj - qi) <= halo
    lane = lax.broadcasted_iota(jnp.int32, (sub, HEAD_DIM), 1)

    def window(prev_ref, cur_ref, next_ref, lo, hi, cols):
        parts = [prev_ref[:, cols]] if lo < 0 else []
        parts.append(cur_ref[max(lo, 0):min(hi, qb), cols])
        if hi > qb:
            parts.append(next_ref[:, cols])
        return parts[0] if len(parts) == 1 else jnp.concatenate(parts, axis=0)

    def scores(u, h):
        rows = slice(u * sub, (u + 1) * sub)
        cols = slice(h * HEAD_DIM, (h + 1) * HEAD_DIM)
        k = window(kp_ref, kc_ref, kn_ref, u * sub - halo, (u + 1) * sub + halo, cols)
        s = lax.dot_general(q_ref[rows, cols], k, (((1,), (1,)), ((), ())), preferred_element_type=F32)
        s = s + masks[u]
        return s, jnp.max(s, axis=-1, keepdims=True)

    def finish(u, h, s, m):
        rows = slice(u * sub, (u + 1) * sub)
        cols = slice(h * HEAD_DIM, (h + 1) * HEAD_DIM)
        v = window(vp_ref, vc_ref, vn_ref, u * sub - halo, (u + 1) * sub + halo, cols)
        p = jnp.exp(s - m)
        l = jnp.sum(p, axis=-1, keepdims=True)
        o = jnp.dot(p.astype(BF16), v, preferred_element_type=F32)
        o_ref[rows, cols] = (o / l).astype(o_ref.dtype)
        return m + jnp.log(l)

    masks = []
    for u in range(qb // sub):
        kpos = kj + (m0 + u * sub)
        masks.append(jnp.where(band & (kpos >= 0) & (kpos < length), 0.0, NEG).astype(F32))

    units = [(u, h) for u in range(qb // sub) for h in range(heads)]
    cur = scores(*units[0])
    lse_tile = None
    for n, (u, h) in enumerate(units):
        nxt = scores(*units[n + 1]) if n + 1 < len(units) else None
        lse = finish(u, h, *cur)
        lse_tile = jnp.where(lane == h, lse, jnp.zeros((sub, HEAD_DIM), F32) if h == 0 else lse_tile)
        if h == heads - 1:
            lse_ref[u * sub:(u + 1) * sub, :] = lse_tile
        cur = nxt


def dil_attention(qkv, *, group, halo):
    _, batch, dil, length, wq = qkv.shape
    heads = wq // HEAD_DIM
    qb = min(DIL_STEP_ROWS, length)
    assert BLOCK_Q % halo == 0 and qb % BLOCK_Q == 0 and length % qb == 0
    nblk = length // qb
    hpb = length // halo
    per = qb // halo

    def cur_map(sec):
        return lambda b, r, i: (sec, b, r, i, 0)

    def prev_map(sec):
        return lambda b, r, i: (sec, b, r, jnp.maximum(i * per - 1, 0), 0)

    def next_map(sec):
        return lambda b, r, i: (sec, b, r, jnp.minimum((i + 1) * per, hpb - 1), 0)

    def spec(rows, index_map):
        return pl.BlockSpec((None, None, None, rows, wq), index_map)

    in_specs = [spec(qb, cur_map(0))]
    for sec in (1, 2):
        in_specs += [spec(halo, prev_map(sec)), spec(qb, cur_map(sec)), spec(halo, next_map(sec))]
    vmem = (2 * (2 * qb * wq * 2 + 2 * (qb + 2 * halo) * wq * 2 + qb * HEAD_DIM * 4)
            + 8 * BLOCK_Q * (BLOCK_Q + 2 * halo) * 4)
    return pl.pallas_call(
        functools.partial(_dil_attn_kernel, heads=heads, halo=halo, length=length),
        grid=(batch, dil, nblk),
        in_specs=in_specs,
        out_specs=[pl.BlockSpec((None, None, qb, wq), lambda b, r, i: (b, r, i, 0)),
                   pl.BlockSpec((None, None, qb, HEAD_DIM), lambda b, r, i: (b, r, i, 0))],
        out_shape=[jax.ShapeDtypeStruct((batch, dil, length, wq), BF16),
                   jax.ShapeDtypeStruct((batch, dil, length, HEAD_DIM), F32)],
        compiler_params=_params(("parallel", "parallel", "arbitrary"), vmem),
        name=f"dil_attention_g{group}",
    )(*([qkv] * 7))


def _merge_proj_kernel(*refs, dils, heads):
    g_n = len(dils)
    o_refs = refs[:g_n]
    lse_refs = refs[g_n:2 * g_n]
    w_ref, r_ref, out_ref, a_ref, lt_ref, ot_ref = refs[2 * g_n:]
    tm = a_ref.shape[0]

    lses = []
    for g, dil in enumerate(dils):
        if dil == 1:
            lses.append(lse_refs[g][0])
            continue
        n = tm // dil
        for r in range(dil):
            lt_ref[g, pl.ds(r, n, stride=dil), :] = lse_refs[g][r]
            for h in range(heads):
                ot_ref[g * heads + h, pl.ds(r, n, stride=dil), :] = (
                    o_refs[g][r, :, h * HEAD_DIM:(h + 1) * HEAD_DIM].astype(F32))
        lses.append(lt_ref[g])

    mx = functools.reduce(jnp.maximum, lses)
    es = [jnp.exp(l - mx) for l in lses]
    den = functools.reduce(jnp.add, es)
    wts = [e / den for e in es]
    for h in range(heads):
        sl = slice(h * HEAD_DIM, (h + 1) * HEAD_DIM)
        acc = None
        for g, dil in enumerate(dils):
            o = o_refs[g][0, :, sl].astype(F32) if dil == 1 else ot_ref[g * heads + h]
            term = wts[g][:, h:h + 1] * o
            acc = term if acc is None else acc + term
        a_ref[:, sl] = acc.astype(BF16)
    out_ref[...] = r_ref[...] + jnp.dot(a_ref[...], w_ref[...], preferred_element_type=F32)


def merge_proj_residual(outs, lses, w, res, *, seq, tm):
    dils = tuple(o.shape[1] for o in outs)
    k = outs[0].shape[-1]
    m, n = res.shape
    g_n = len(outs)
    heads = k // HEAD_DIM
    tpb = seq // tm
    assert all(tm % (BF16_SUBLANES * dil) == 0 for dil in dils)

    def stream_spec(dil, width):
        return pl.BlockSpec((None, dil, tm // dil, width), lambda i: (i // tpb, 0, i % tpb, 0))

    vmem = (2 * g_n * tm * k * 2 + 2 * g_n * tm * HEAD_DIM * 4 + 2 * k * n * 2 + 4 * tm * n * 4
            + tm * k * 2 + tm * n * 4 + g_n * tm * HEAD_DIM * 4 + g_n * tm * k * 4)
    return pl.pallas_call(
        functools.partial(_merge_proj_kernel, dils=dils, heads=heads),
        grid=(m // tm,),
        in_specs=([stream_spec(dil, k) for dil in dils]
                  + [stream_spec(dil, HEAD_DIM) for dil in dils]
                  + [pl.BlockSpec((k, n), lambda i: (0, 0)),
                     pl.BlockSpec((tm, n), lambda i: (i, 0))]),
        out_specs=pl.BlockSpec((tm, n), lambda i: (i, 0)),
        out_shape=jax.ShapeDtypeStruct((m, n), F32),
        scratch_shapes=[pltpu.VMEM((tm, k), BF16),
                        pltpu.VMEM((g_n, tm, HEAD_DIM), F32),
                        pltpu.VMEM((g_n * heads, tm, HEAD_DIM), F32)],
        compiler_params=_params(("parallel",), vmem),
        name="dil_merge_proj_residual",
    )(*outs, *lses, w, res)


def kernel(x, na_norm, na_wqkv, na_rpb, na_wo, ffn0_norm, ffn0_w1, ffn0_w2, dil_norm, dil_wqkv, dil_wo,
           ffn1_norm, ffn1_w1, ffn1_w2, final_norm):
    batch, seq, d = x.shape
    m = batch * seq
    rows = seq // GRID_W
    na_heads = na_rpb.shape[0]
    assert rows >= NA_KEY_ROWS and rows % NA_BLOCK_ROWS == 0 and rows >= NA_WIN_ROWS
    assert NA_KEY_ROWS >= NA_BLOCK_ROWS + NA_WIN_ROWS - 1

    xf = x.reshape(m, d)
    bf = lambda w: w.astype(BF16)

    qw = na_heads * HEAD_DIM
    col_scale = jnp.concatenate([jnp.full((qw,), HEAD_DIM ** -0.5, F32), jnp.ones((2 * qw,), F32)])
    qkv = norm_matmul(xf, na_norm, bf(na_wqkv), col_scale, tm=min(1024, m), tn=min(1024, qw))
    bias = _na_bias_table(na_rpb)
    attn = na_attention(qkv, bias, batch=batch, seq=seq, heads_per_step=min(8, na_heads))
    xf = proj_residual(attn, bf(na_wo), xf, tm=min(512, m))
    xf = ffn(xf, ffn0_norm, bf(ffn0_w1), bf(ffn0_w2), final_norm, tm=min(1024, m), tf=512, final_norm=False)

    rot = _rope_tables(seq)
    dils = tuple(dil for _, dil in DIL_GROUPS)
    qkvs = norm_matmul_rope(xf, dil_norm, bf(dil_wqkv), rot, batch=batch, seq=seq, tm=min(1024, seq), dils=dils)
    outs, lses = [], []
    for gi, (window, dil) in enumerate(DIL_GROUPS):
        o, lse = dil_attention(qkvs[gi], group=gi, halo=(window // 2) // dil)
        outs.append(o)
        lses.append(lse)
    xf = merge_proj_residual(outs, lses, bf(dil_wo), xf, seq=seq, tm=min(512, seq))
    xf = ffn(xf, ffn1_norm, bf(ffn1_w1), bf(ffn1_w2), final_norm, tm=min(1024, m), tf=512, final_norm=True)
    return xf.reshape(batch, seq, d)
```

```python
import functools
import math

import numpy as np
import jax
import jax.numpy as jnp
from jax import lax
from jax.experimental import pallas as pl
from jax.experimental.pallas import tpu as pltpu

GRID_W = 64
HEAD_DIM = 128
NA_WIN_ROWS = 8
NA_WIN_COLS = 16
DIL_GROUPS = ((128, 1), (512, 4), (2048, 16))
ROPE_THETA = 500000.0
ROPE_DIM = HEAD_DIM // 4
BLOCK_Q = 128
EPS = 1e-6
NEG = -1e30

LOG2E = math.log2(math.e)
LN2 = math.log(2.0)
Q_SCALE = HEAD_DIM ** -0.5 * LOG2E

NA_BLOCK_ROWS = 4
NA_KEY_BACK = NA_WIN_ROWS // 2
NA_KEY_ROWS = 12
NA_KEY_CHUNKS = 3

DIL_STEP_ROWS = 256
ROW_CHUNK = 256
FFN_ROW_CHUNK = 512
BF16_SUBLANES = 16

V7X_VMEM_BYTES = 64 * 1024 * 1024
VMEM_CAP_BYTES = V7X_VMEM_BYTES - 8 * 1024 * 1024

F32 = jnp.float32
BF16 = jnp.bfloat16


def _params(semantics, vmem_bytes):
    limit = min(int(vmem_bytes * 1.25) + (4 << 20), VMEM_CAP_BYTES)
    return pltpu.CompilerParams(dimension_semantics=semantics, vmem_limit_bytes=limit)


def _rms(x, g):
    return x * lax.rsqrt(jnp.mean(x * x, axis=-1, keepdims=True) + EPS) * g


def _norm_matmul_kernel(x_ref, g_ref, w_ref, cs_ref, o_ref, h_ref):
    @pl.when(pl.program_id(1) == 0)
    def _():
        h_ref[...] = _rms(x_ref[...], g_ref[...]).astype(BF16)

    for c in range(x_ref.shape[0] // ROW_CHUNK):
        rows = slice(c * ROW_CHUNK, (c + 1) * ROW_CHUNK)
        acc = jnp.dot(h_ref[rows, :], w_ref[...], preferred_element_type=F32)
        o_ref[rows, :] = (acc * cs_ref[...]).astype(o_ref.dtype)


def norm_matmul(x, g, w, col_scale, *, tm, tn):
    m, k = x.shape
    n = w.shape[1]
    vmem = 2 * tm * k * 4 + tm * k * 2 + 2 * k * tn * 2 + 2 * tm * tn * 2 + tm * tn * 4
    return pl.pallas_call(
        _norm_matmul_kernel,
        grid=(m // tm, n // tn),
        in_specs=[
            pl.BlockSpec((tm, k), lambda i, j: (i, 0)),
            pl.BlockSpec((1, k), lambda i, j: (0, 0)),
            pl.BlockSpec((k, tn), lambda i, j: (0, j)),
            pl.BlockSpec((1, tn), lambda i, j: (0, j)),
        ],
        out_specs=pl.BlockSpec((tm, tn), lambda i, j: (i, j)),
        out_shape=jax.ShapeDtypeStruct((m, n), BF16),
        scratch_shapes=[pltpu.VMEM((tm, k), BF16)],
        compiler_params=_params(("parallel", "arbitrary"), vmem),
        name="norm_qkv_na",
    )(x, g.reshape(1, k), w, col_scale.reshape(1, n))


def _norm_matmul_rope_kernel(x_ref, g_ref, w_ref, rot_ref, *refs, dils):
    o_refs = refs[:len(dils)]
    h_ref, s_ref = refs[len(dils):]
    j = pl.program_id(1)
    tm = x_ref.shape[0]
    heads = s_ref.shape[0]

    @pl.when(j == 0)
    def _():
        h_ref[...] = _rms(x_ref[...], g_ref[...]).astype(BF16)

    def body(o_ref, dil, rotate):
        n = ROW_CHUNK // dil
        scale = jnp.where(j % 3 == 0, Q_SCALE, 1.0).astype(F32)
        for c in range(tm // ROW_CHUNK):
            rows = slice(c * ROW_CHUNK, (c + 1) * ROW_CHUNK)
            acc = jnp.dot(h_ref[rows, :], w_ref[...], preferred_element_type=F32)
            if rotate:
                cos = rot_ref[0, rows, :] * scale
                sin = rot_ref[1, rows, :] * scale
            for h in range(heads):
                cols = slice(h * HEAD_DIM, (h + 1) * HEAD_DIM)
                a = acc[:, cols]
                if rotate:
                    a = a * cos + pltpu.roll(a, HEAD_DIM // 2, 1) * sin
                if dil == 1:
                    o_ref[0, rows, cols] = a.astype(BF16)
                    continue
                s_ref[h, rows, :] = a
                for r in range(dil):
                    o_ref[r, c * n:(c + 1) * n, cols] = (
                        s_ref[h, pl.ds(c * ROW_CHUNK + r, n, stride=dil), :].astype(BF16))

    for g, dil in enumerate(dils):
        pl.when((j // 3 == g) & (j % 3 != 2))(functools.partial(body, o_refs[g], dil, True))
        pl.when((j // 3 == g) & (j % 3 == 2))(functools.partial(body, o_refs[g], dil, False))


def _spread_rotary_columns(w, n_groups):
    d = w.shape[0]
    half = ROPE_DIM // 2
    mid = HEAD_DIM // 2
    w5 = w.reshape(d, n_groups, 3, -1, HEAD_DIM)
    qk = w5[:, :, :2]
    qk = jnp.concatenate([qk[..., :half], qk[..., mid:mid + half], qk[..., ROPE_DIM:mid],
                          qk[..., half:ROPE_DIM], qk[..., mid + half:]], axis=-1)
    return jnp.concatenate([qk, w5[:, :, 2:]], axis=2).reshape(w.shape)


def _rope_tables(t):
    half = ROPE_DIM // 2
    pos = jnp.arange(t, dtype=F32)
    inv_freq = ROPE_THETA ** (-jnp.arange(0, ROPE_DIM, 2, dtype=F32) / ROPE_DIM)
    ang = pos[:, None] * inv_freq[None, :]
    cos = jnp.cos(ang)
    sin = jnp.sin(ang)
    one = jnp.ones((t, HEAD_DIM // 2 - half), F32)
    zero = jnp.zeros((t, HEAD_DIM // 2 - half), F32)
    return jnp.stack([jnp.concatenate([cos, one, cos, one], axis=-1),
                      jnp.concatenate([-sin, zero, sin, zero], axis=-1)], axis=0)


def norm_matmul_rope(x, g, w, rot, *, batch, seq, tm, dils):
    m, k = x.shape
    n = w.shape[1]
    tn = n // (3 * len(dils))
    heads = tn // HEAD_DIM
    tpb = seq // tm
    assert all(ROW_CHUNK % (BF16_SUBLANES * dil) == 0 for dil in dils) and tm % ROW_CHUNK == 0

    def out_map(g):
        return lambda i, j: (jnp.clip(j - 3 * g, 0, 2), i // tpb, 0, i % tpb, 0)

    vmem = (2 * tm * k * 4 + tm * k * 2 + 2 * k * tn * 2 + 2 * len(dils) * tm * tn * 2 + tm * tn * 4
            + 2 * ROW_CHUNK * tn * 4 + 2 * 2 * tm * HEAD_DIM * 4)
    return pl.pallas_call(
        functools.partial(_norm_matmul_rope_kernel, dils=dils),
        grid=(m // tm, n // tn),
        in_specs=[
            pl.BlockSpec((tm, k), lambda i, j: (i, 0)),
            pl.BlockSpec((1, k), lambda i, j: (0, 0)),
            pl.BlockSpec((k, tn), lambda i, j: (0, j)),
            pl.BlockSpec((2, tm, HEAD_DIM), lambda i, j: (0, i % tpb, 0)),
        ],
        out_specs=[pl.BlockSpec((None, None, dil, tm // dil, tn), out_map(g)) for g, dil in enumerate(dils)],
        out_shape=[jax.ShapeDtypeStruct((3, batch, dil, seq // dil, tn), BF16) for dil in dils],
        scratch_shapes=[pltpu.VMEM((tm, k), BF16), pltpu.VMEM((heads, tm, HEAD_DIM), F32)],
        compiler_params=_params(("arbitrary", "arbitrary"), vmem),
        name="norm_qkv_dil_rope",
    )(x, g.reshape(1, k), w, rot)


def _na_bias_table(rpb):
    kw = NA_WIN_COLS
    cols = np.arange(GRID_W)
    col_start = np.clip(cols - kw // 2, 0, GRID_W - kw)
    col_mask = (cols[None, :] >= col_start[:, None]) & (cols[None, :] < col_start[:, None] + kw)
    col_idx = np.clip(cols[None, :] - cols[:, None] + NA_WIN_COLS - 1, 0, 2 * NA_WIN_COLS - 2)
    by_col = jnp.where(jnp.asarray(col_mask)[None, None], rpb[:, :, col_idx] * LOG2E, NEG)
    rr = np.arange(NA_BLOCK_ROWS)[:, None]
    kk = np.arange(NA_KEY_ROWS)[None, :]
    row_idx = kk - NA_KEY_BACK - rr + NA_WIN_ROWS - 1
    assert row_idx.min() >= 0 and row_idx.max() <= 2 * NA_WIN_ROWS - 2
    tab = by_col[:, row_idx].transpose(0, 1, 3, 2, 4)
    return tab.reshape(rpb.shape[0], NA_BLOCK_ROWS * GRID_W, NA_KEY_ROWS * GRID_W)


def _na_attn_kernel(q_ref, *refs, heads, rows):
    k_refs = refs[:NA_KEY_CHUNKS]
    v_refs = refs[NA_KEY_CHUNKS:2 * NA_KEY_CHUNKS]
    b_ref, o_ref, bm_ref = refs[2 * NA_KEY_CHUNKS:]
    qb = q_ref.shape[0]
    ck = k_refs[0].shape[0]
    kh = min(NA_WIN_ROWS, rows)
    i = pl.program_id(2)

    def clipped(blk):
        r0 = blk * NA_BLOCK_ROWS
        return (r0 < kh // 2) | (r0 + NA_BLOCK_ROWS - 1 - kh // 2 > rows - kh)

    @pl.when((i == 0) | clipped(i) | clipped(i - 1))
    def _():
        r0 = i * NA_BLOCK_ROWS
        q_row = r0 + lax.broadcasted_iota(jnp.int32, (qb, ck), 0) // GRID_W
        lo = jnp.clip(q_row - kh // 2, 0, rows - kh)
        for c in range(NA_KEY_CHUNKS):
            k_row = (r0 - NA_KEY_BACK + c * (ck // GRID_W)
                     + lax.broadcasted_iota(jnp.int32, (qb, ck), 1) // GRID_W)
            row_mask = jnp.where((k_row >= lo) & (k_row < lo + kh), 0.0, NEG).astype(F32)
            for h in range(heads):
                bm_ref[h, :, c * ck:(c + 1) * ck] = b_ref[h, :, c * ck:(c + 1) * ck] + row_mask

    def scores(h):
        sl = slice(h * HEAD_DIM, (h + 1) * HEAD_DIM)
        q = q_ref[:, sl]
        s = []
        for c, k_ref in enumerate(k_refs):
            sc = lax.dot_general(q, k_ref[:, sl], (((1,), (1,)), ((), ())), preferred_element_type=F32)
            s.append(sc + bm_ref[h, :, c * ck:(c + 1) * ck])
        return s, jnp.max(functools.reduce(jnp.maximum, s), axis=-1, keepdims=True)

    def finish(h, s, m):
        sl = slice(h * HEAD_DIM, (h + 1) * HEAD_DIM)
        p = [jnp.exp2(sc - m) for sc in s]
        l = jnp.sum(functools.reduce(jnp.add, p), axis=-1, keepdims=True)
        o = functools.reduce(jnp.add, [
            jnp.dot(pc.astype(BF16), v_ref[:, sl], preferred_element_type=F32)
            for pc, v_ref in zip(p, v_refs)])
        o_ref[:, sl] = (o / l).astype(o_ref.dtype)

    cur = scores(0)
    for h in range(heads):
        nxt = scores(h + 1) if h + 1 < heads else None
        finish(h, *cur)
        cur = nxt


def na_attention(qkv, bias, *, batch, seq, heads_per_step):
    m = qkv.shape[0]
    nh = bias.shape[0]
    hg = heads_per_step
    wq = hg * HEAD_DIM
    qb = NA_BLOCK_ROWS * GRID_W
    kb = NA_KEY_ROWS * GRID_W // NA_KEY_CHUNKS
    assert qb % kb == 0 and (NA_KEY_BACK * GRID_W) % kb == 0
    nblk = seq // qb
    kpb = seq // kb
    ratio = qb // kb
    back = NA_KEY_BACK * GRID_W // kb
    ncol = nh // hg

    def k_map(c, sec):
        def index(g, b, i):
            return (b * kpb + jnp.clip(i * ratio - back + c, 0, kpb - 1), sec * ncol + g)
        return index

    in_specs = [pl.BlockSpec((qb, wq), lambda g, b, i: (b * nblk + i, g))]
    in_specs += [pl.BlockSpec((kb, wq), k_map(c, 1)) for c in range(NA_KEY_CHUNKS)]
    in_specs += [pl.BlockSpec((kb, wq), k_map(c, 2)) for c in range(NA_KEY_CHUNKS)]
    in_specs += [pl.BlockSpec((hg, qb, NA_KEY_CHUNKS * kb), lambda g, b, i: (g, 0, 0))]
    vmem = (2 * 2 * qb * wq * 2 + 2 * 2 * NA_KEY_CHUNKS * kb * wq * 2
            + 3 * hg * qb * NA_KEY_CHUNKS * kb * 4 + 6 * qb * NA_KEY_CHUNKS * kb * 4)
    return pl.pallas_call(
        functools.partial(_na_attn_kernel, heads=hg, rows=seq // GRID_W),
        grid=(ncol, batch, nblk),
        in_specs=in_specs,
        out_specs=pl.BlockSpec((qb, wq), lambda g, b, i: (b * nblk + i, g)),
        out_shape=jax.ShapeDtypeStruct((m, nh * HEAD_DIM), BF16),
        scratch_shapes=[pltpu.VMEM((hg, qb, NA_KEY_CHUNKS * kb), F32)],
        compiler_params=_params(("parallel", "parallel", "arbitrary"), vmem),
        name="na_attention",
    )(qkv, *([qkv] * (2 * NA_KEY_CHUNKS)), bias)


def _proj_residual_kernel(a_ref, w_ref, r_ref, o_ref):
    o_ref[...] = r_ref[...] + jnp.dot(a_ref[...], w_ref[...], preferred_element_type=F32)


def proj_residual(a, w, res, *, tm):
    m, k = a.shape
    n = w.shape[1]
    vmem = 2 * tm * k * 2 + 2 * k * n * 2 + 4 * tm * n * 4 + tm * n * 4
    return pl.pallas_call(
        _proj_residual_kernel,
        grid=(m // tm,),
        in_specs=[
            pl.BlockSpec((tm, k), lambda i: (i, 0)),
            pl.BlockSpec((k, n), lambda i: (0, 0)),
            pl.BlockSpec((tm, n), lambda i: (i, 0)),
        ],
        out_specs=pl.BlockSpec((tm, n), lambda i: (i, 0)),
        out_shape=jax.ShapeDtypeStruct((m, n), F32),
        compiler_params=_params(("parallel",), vmem),
        name="na_proj_residual",
    )(a, w, res)


def _ffn_kernel(x_ref, g_ref, w1_ref, w2_ref, gf_ref, o_ref, h_ref, *, final_norm):
    f = pl.program_id(1)

    @pl.when(f == 0)
    def _():
        x = x_ref[...]
        h_ref[...] = _rms(x, g_ref[...]).astype(BF16)
        o_ref[...] = x

    for c in range(x_ref.shape[0] // FFN_ROW_CHUNK):
        rows = slice(c * FFN_ROW_CHUNK, (c + 1) * FFN_ROW_CHUNK)
        a = jnp.dot(h_ref[rows, :], w1_ref[...], preferred_element_type=F32)
        a = jnp.square(jnp.maximum(a, 0.0)).astype(BF16)
        o_ref[rows, :] += jnp.dot(a, w2_ref[...], preferred_element_type=F32)

    if final_norm:
        @pl.when(f == pl.num_programs(1) - 1)
        def _():
            o_ref[...] = _rms(o_ref[...], gf_ref[...])


def ffn(x, g, w1, w2, g_final, *, tm, tf, final_norm):
    m, d = x.shape
    dff = w1.shape[1]
    assert tm % FFN_ROW_CHUNK == 0
    vmem = (4 * tm * d * 4 + tm * d * 2 + 2 * 2 * d * tf * 2
            + FFN_ROW_CHUNK * tf * 6 + FFN_ROW_CHUNK * d * 4)
    return pl.pallas_call(
        functools.partial(_ffn_kernel, final_norm=final_norm),
        grid=(m // tm, dff // tf),
        in_specs=[
            pl.BlockSpec((tm, d), lambda i, f: (i, 0)),
            pl.BlockSpec((1, d), lambda i, f: (0, 0)),
            pl.BlockSpec((d, tf), lambda i, f: (0, f)),
            pl.BlockSpec((tf, d), lambda i, f: (f, 0)),
            pl.BlockSpec((1, d), lambda i, f: (0, 0)),
        ],
        out_specs=pl.BlockSpec((tm, d), lambda i, f: (i, 0)),
        out_shape=jax.ShapeDtypeStruct((m, d), F32),
        scratch_shapes=[pltpu.VMEM((tm, d), BF16)],
        compiler_params=_params(("parallel", "arbitrary"), vmem),
        name="ffn_final" if final_norm else "ffn",
    )(x, g.reshape(1, d), w1, w2, g_final.reshape(1, d))


def _dil_attn_kernel(q_ref, kp_ref, kc_ref, kn_ref, vp_ref, vc_ref, vn_ref, o_ref, lse_ref,
                     *, heads, halo, length):
    qb = q_ref.shape[0]
    sub = BLOCK_Q
    nk = sub + 2 * halo
    m0 = pl.program_id(2) * qb
    qi = lax.broadcasted_iota(jnp.int32, (sub, nk), 0)
    kj = lax.broadcasted_iota(jnp.int32, (sub, nk), 1) - halo
    band = jnp.abs(kj - qi) <= halo
    lane = lax.broadcasted_iota(jnp.int32, (sub, HEAD_DIM), 1)

    def window(prev_ref, cur_ref, next_ref, lo, hi, cols):
        parts = [prev_ref[:, cols]] if lo < 0 else []
        parts.append(cur_ref[max(lo, 0):min(hi, qb), cols])
        if hi > qb:
            parts.append(next_ref[:, cols])
        return parts[0] if len(parts) == 1 else jnp.concatenate(parts, axis=0)

    def scores(u, h):
        rows = slice(u * sub, (u + 1) * sub)
        cols = slice(h * HEAD_DIM, (h + 1) * HEAD_DIM)
        k = window(kp_ref, kc_ref, kn_ref, u * sub - halo, (u + 1) * sub + halo, cols)
        s = lax.dot_general(q_ref[rows, cols], k, (((1,), (1,)), ((), ())), preferred_element_type=F32)
        s = s + masks[u]
        return s, jnp.max(s, axis=-1, keepdims=True)

    def finish(u, h, s, m):
        rows = slice(u * sub, (u + 1) * sub)
        cols = slice(h * HEAD_DIM, (h + 1) * HEAD_DIM)
        v = window(vp_ref, vc_ref, vn_ref, u * sub - halo, (u + 1) * sub + halo, cols)
        p = jnp.exp2(s - m)
        l = jnp.sum(p, axis=-1, keepdims=True)
        o = jnp.dot(p.astype(BF16), v, preferred_element_type=F32)
        o_ref[rows, cols] = (o / l).astype(o_ref.dtype)
        return m * LN2 + jnp.log(l)

    masks = []
    for u in range(qb // sub):
        kpos = kj + (m0 + u * sub)
        masks.append(jnp.where(band & (kpos >= 0) & (kpos < length), 0.0, NEG).astype(F32))

    units = [(u, h) for u in range(qb // sub) for h in range(heads)]
    cur = scores(*units[0])
    lse_tile = None
    for n, (u, h) in enumerate(units):
        nxt = scores(*units[n + 1]) if n + 1 < len(units) else None
        lse = finish(u, h, *cur)
        lse_tile = jnp.where(lane == h, lse, jnp.zeros((sub, HEAD_DIM), F32) if h == 0 else lse_tile)
        if h == heads - 1:
            lse_ref[u * sub:(u + 1) * sub, :] = lse_tile
        cur = nxt


def dil_attention(qkv, *, group, halo):
    _, batch, dil, length, wq = qkv.shape
    heads = wq // HEAD_DIM
    qb = min(DIL_STEP_ROWS, length)
    assert BLOCK_Q % halo == 0 and qb % BLOCK_Q == 0 and length % qb == 0
    nblk = length // qb
    hpb = length // halo
    per = qb // halo

    def cur_map(sec):
        return lambda b, r, i: (sec, b, r, i, 0)

    def prev_map(sec):
        return lambda b, r, i: (sec, b, r, jnp.maximum(i * per - 1, 0), 0)

    def next_map(sec):
        return lambda b, r, i: (sec, b, r, jnp.minimum((i + 1) * per, hpb - 1), 0)

    def spec(rows, index_map):
        return pl.BlockSpec((None, None, None, rows, wq), index_map)

    in_specs = [spec(qb, cur_map(0))]
    for sec in (1, 2):
        in_specs += [spec(halo, prev_map(sec)), spec(qb, cur_map(sec)), spec(halo, next_map(sec))]
    vmem = (2 * (2 * qb * wq * 2 + 2 * (qb + 2 * halo) * wq * 2 + qb * HEAD_DIM * 4)
            + 8 * BLOCK_Q * (BLOCK_Q + 2 * halo) * 4)
    return pl.pallas_call(
        functools.partial(_dil_attn_kernel, heads=heads, halo=halo, length=length),
        grid=(batch, dil, nblk),
        in_specs=in_specs,
        out_specs=[pl.BlockSpec((None, None, qb, wq), lambda b, r, i: (b, r, i, 0)),
                   pl.BlockSpec((None, None, qb, HEAD_DIM), lambda b, r, i: (b, r, i, 0))],
        out_shape=[jax.ShapeDtypeStruct((batch, dil, length, wq), BF16),
                   jax.ShapeDtypeStruct((batch, dil, length, HEAD_DIM), F32)],
        compiler_params=_params(("parallel", "parallel", "arbitrary"), vmem),
        name=f"dil_attention_g{group}",
    )(*([qkv] * 7))


def _merge_proj_kernel(*refs, dils, heads):
    g_n = len(dils)
    o_refs = refs[:g_n]
    lse_refs = refs[g_n:2 * g_n]
    w_ref, r_ref, out_ref, a_ref, lt_ref, ot_ref = refs[2 * g_n:]
    tm = a_ref.shape[0]

    def merged(c):
        rows = slice(c * ROW_CHUNK, (c + 1) * ROW_CHUNK)
        lses = []
        for g, dil in enumerate(dils):
            if dil == 1:
                lses.append(lse_refs[g][0, rows, :])
                continue
            n = ROW_CHUNK // dil
            for r in range(dil):
                dst = pl.ds(c * ROW_CHUNK + r, n, stride=dil)
                lt_ref[g, dst, :] = lse_refs[g][r, c * n:(c + 1) * n, :]
                for h in range(heads):
                    ot_ref[g * heads + h, dst, :] = (
                        o_refs[g][r, c * n:(c + 1) * n, h * HEAD_DIM:(h + 1) * HEAD_DIM].astype(F32))
            lses.append(lt_ref[g, rows, :])

        mx = functools.reduce(jnp.maximum, lses)
        es = [jnp.exp(l - mx) for l in lses]
        den = functools.reduce(jnp.add, es)
        wts = [e / den for e in es]
        for h in range(heads):
            sl = slice(h * HEAD_DIM, (h + 1) * HEAD_DIM)
            acc = None
            for g, dil in enumerate(dils):
                o = o_refs[g][0, rows, sl].astype(F32) if dil == 1 else ot_ref[g * heads + h, rows, :]
                term = wts[g][:, h:h + 1] * o
                acc = term if acc is None else acc + term
            a_ref[rows, sl] = acc.astype(BF16)

    merged(0)
    for c in range(tm // ROW_CHUNK):
        rows = slice(c * ROW_CHUNK, (c + 1) * ROW_CHUNK)
        if c + 1 < tm // ROW_CHUNK:
            merged(c + 1)
        out_ref[rows, :] = r_ref[rows, :] + jnp.dot(a_ref[rows, :], w_ref[...], preferred_element_type=F32)


def merge_proj_residual(outs, lses, w, res, *, seq, tm):
    dils = tuple(o.shape[1] for o in outs)
    k = outs[0].shape[-1]
    m, n = res.shape
    g_n = len(outs)
    heads = k // HEAD_DIM
    tpb = seq // tm
    assert all(ROW_CHUNK % (BF16_SUBLANES * dil) == 0 for dil in dils) and tm % ROW_CHUNK == 0

    def stream_spec(dil, width):
        return pl.BlockSpec((None, dil, tm // dil, width), lambda i: (i // tpb, 0, i % tpb, 0))

    vmem = (2 * g_n * tm * k * 2 + 2 * g_n * tm * HEAD_DIM * 4 + 2 * k * n * 2 + 4 * tm * n * 4
            + tm * k * 2 + tm * n * 4 + g_n * tm * HEAD_DIM * 4 + g_n * tm * k * 4)
    return pl.pallas_call(
        functools.partial(_merge_proj_kernel, dils=dils, heads=heads),
        grid=(m // tm,),
        in_specs=([stream_spec(dil, k) for dil in dils]
                  + [stream_spec(dil, HEAD_DIM) for dil in dils]
                  + [pl.BlockSpec((k, n), lambda i: (0, 0)),
                     pl.BlockSpec((tm, n), lambda i: (i, 0))]),
        out_specs=pl.BlockSpec((tm, n), lambda i: (i, 0)),
        out_shape=jax.ShapeDtypeStruct((m, n), F32),
        scratch_shapes=[pltpu.VMEM((tm, k), BF16),
                        pltpu.VMEM((g_n, tm, HEAD_DIM), F32),
                        pltpu.VMEM((g_n * heads, tm, HEAD_DIM), F32)],
        compiler_params=_params(("parallel",), vmem),
        name="dil_merge_proj_residual",
    )(*outs, *lses, w, res)


def kernel(x, na_norm, na_wqkv, na_rpb, na_wo, ffn0_norm, ffn0_w1, ffn0_w2, dil_norm, dil_wqkv, dil_wo,
           ffn1_norm, ffn1_w1, ffn1_w2, final_norm):
    batch, seq, d = x.shape
    m = batch * seq
    rows = seq // GRID_W
    na_heads = na_rpb.shape[0]
    assert rows >= NA_KEY_ROWS and rows % NA_BLOCK_ROWS == 0 and rows >= NA_WIN_ROWS
    assert NA_KEY_ROWS >= NA_BLOCK_ROWS + NA_WIN_ROWS - 1

    xf = x.reshape(m, d)
    bf = lambda w: w.astype(BF16)

    qw = na_heads * HEAD_DIM
    col_scale = jnp.concatenate([jnp.full((qw,), Q_SCALE, F32), jnp.ones((2 * qw,), F32)])
    qkv = norm_matmul(xf, na_norm, bf(na_wqkv), col_scale, tm=min(1024, m), tn=min(1024, qw))
    bias = _na_bias_table(na_rpb)
    attn = na_attention(qkv, bias, batch=batch, seq=seq, heads_per_step=min(8, na_heads))
    xf = proj_residual(attn, bf(na_wo), xf, tm=min(512, m))
    xf = ffn(xf, ffn0_norm, bf(ffn0_w1), bf(ffn0_w2), final_norm, tm=min(1024, m), tf=512, final_norm=False)

    rot = _rope_tables(seq)
    dils = tuple(dil for _, dil in DIL_GROUPS)
    qkvs = norm_matmul_rope(xf, dil_norm, bf(_spread_rotary_columns(dil_wqkv, len(dils))), rot,
                            batch=batch, seq=seq, tm=min(1024, seq), dils=dils)
    outs, lses = [], []
    for gi, (window, dil) in enumerate(DIL_GROUPS):
        o, lse = dil_attention(qkvs[gi], group=gi, halo=(window // 2) // dil)
        outs.append(o)
        lses.append(lse)
    xf = merge_proj_residual(outs, lses, bf(dil_wo), xf, seq=seq, tm=min(512, seq))
    xf = ffn(xf, ffn1_norm, bf(ffn1_w1), bf(ffn1_w2), final_norm, tm=min(1024, m), tf=512, final_norm=True)
    return xf.reshape(batch, seq, d)
```

```python
import functools
import math

import numpy as np
import jax
import jax.numpy as jnp
from jax import lax
from jax.experimental import pallas as pl
from jax.experimental.pallas import tpu as pltpu

GRID_W = 64
HEAD_DIM = 128
NA_WIN_ROWS = 8
NA_WIN_COLS = 16
DIL_GROUPS = ((128, 1), (512, 4), (2048, 16))
ROPE_THETA = 500000.0
ROPE_DIM = HEAD_DIM // 4
BLOCK_Q = 128
EPS = 1e-6
NEG = -1e30

LOG2E = math.log2(math.e)
LN2 = math.log(2.0)
Q_SCALE = HEAD_DIM ** -0.5 * LOG2E

NA_BLOCK_ROWS = 4
NA_KEY_BACK = NA_WIN_ROWS // 2
NA_KEY_ROWS = 12
NA_KEY_CHUNKS = 3

DIL_STEP_ROWS = 512
ROW_CHUNK = 256
FFN_ROW_CHUNK = 512
BF16_SUBLANES = 16
F32_SUBLANES = 8

V7X_VMEM_BYTES = 64 * 1024 * 1024
VMEM_CAP_BYTES = V7X_VMEM_BYTES - 8 * 1024 * 1024

F32 = jnp.float32
BF16 = jnp.bfloat16


def _params(semantics, vmem_bytes):
    limit = min(int(vmem_bytes * 1.25) + (4 << 20), VMEM_CAP_BYTES)
    return pltpu.CompilerParams(dimension_semantics=semantics, vmem_limit_bytes=limit)


def _rms(x, g):
    return x * lax.rsqrt(jnp.mean(x * x, axis=-1, keepdims=True) + EPS) * g


def _norm_matmul_kernel(x_ref, g_ref, w_ref, cs_ref, o_ref, h_ref):
    @pl.when(pl.program_id(1) == 0)
    def _():
        h_ref[...] = _rms(x_ref[...], g_ref[...]).astype(BF16)

    for c in range(x_ref.shape[0] // ROW_CHUNK):
        rows = slice(c * ROW_CHUNK, (c + 1) * ROW_CHUNK)
        acc = jnp.dot(h_ref[rows, :], w_ref[...], preferred_element_type=F32)
        o_ref[rows, :] = (acc * cs_ref[...]).astype(o_ref.dtype)


def norm_matmul(x, g, w, col_scale, *, tm, tn):
    m, k = x.shape
    n = w.shape[1]
    vmem = 2 * tm * k * 4 + tm * k * 2 + 2 * k * tn * 2 + 2 * tm * tn * 2 + tm * tn * 4
    return pl.pallas_call(
        _norm_matmul_kernel,
        grid=(m // tm, n // tn),
        in_specs=[
            pl.BlockSpec((tm, k), lambda i, j: (i, 0)),
            pl.BlockSpec((1, k), lambda i, j: (0, 0)),
            pl.BlockSpec((k, tn), lambda i, j: (0, j)),
            pl.BlockSpec((1, tn), lambda i, j: (0, j)),
        ],
        out_specs=pl.BlockSpec((tm, tn), lambda i, j: (i, j)),
        out_shape=jax.ShapeDtypeStruct((m, n), BF16),
        scratch_shapes=[pltpu.VMEM((tm, k), BF16)],
        compiler_params=_params(("parallel", "arbitrary"), vmem),
        name="norm_qkv_na",
    )(x, g.reshape(1, k), w, col_scale.reshape(1, n))


def _norm_matmul_rope_kernel(x_ref, g_ref, w_ref, rot_ref, *refs, dils):
    o_refs = refs[:len(dils)]
    h_ref, s_ref, t_ref = refs[len(dils):]
    j = pl.program_id(1)
    tm = x_ref.shape[0]
    heads = s_ref.shape[0]
    half = ROPE_DIM // 2

    @pl.when(j == 0)
    def _():
        h_ref[...] = _rms(x_ref[...], g_ref[...]).astype(BF16)

    def body(o_ref, dil, rotate):
        n = ROW_CHUNK // dil
        d1, d2 = _stride_split(dil)
        n1 = ROW_CHUNK // d1
        scale = jnp.where(j % 3 == 0, Q_SCALE, 1.0).astype(F32)
        for c in range(tm // ROW_CHUNK):
            rows = slice(c * ROW_CHUNK, (c + 1) * ROW_CHUNK)
            acc = jnp.dot(h_ref[rows, :], w_ref[...], preferred_element_type=F32)
            if rotate:
                cos = rot_ref[0, rows, :] * scale
                sin_hi = rot_ref[1, rows, :] * scale
                sin_lo = rot_ref[2, rows, :] * scale
            for h in range(heads):
                cols = slice(h * HEAD_DIM, (h + 1) * HEAD_DIM)
                a = acc[:, cols]
                if rotate:
                    a = (a * cos + pltpu.roll(a, half, 1) * sin_hi
                         + pltpu.roll(a, HEAD_DIM - half, 1) * sin_lo)
                if dil == 1:
                    o_ref[0, rows, cols] = a.astype(BF16)
                    continue
                s_ref[h, rows, :] = a
                if d2 == 1:
                    for r in range(dil):
                        o_ref[r, c * n:(c + 1) * n, cols] = (
                            s_ref[h, pl.ds(c * ROW_CHUNK + r, n, stride=dil), :].astype(BF16))
                    continue
                for r1 in range(d1):
                    t_ref[h, r1 * n1:(r1 + 1) * n1, :] = s_ref[h, pl.ds(c * ROW_CHUNK + r1, n1, stride=d1), :]
                for r1 in range(d1):
                    for r2 in range(d2):
                        o_ref[r1 + d1 * r2, c * n:(c + 1) * n, cols] = (
                            t_ref[h, pl.ds(r1 * n1 + r2, n, stride=d2), :].astype(BF16))

    for g, dil in enumerate(dils):
        pl.when((j // 3 == g) & (j % 3 != 2))(functools.partial(body, o_refs[g], dil, True))
        pl.when((j // 3 == g) & (j % 3 == 2))(functools.partial(body, o_refs[g], dil, False))


def _stride_split(dil):
    if dil <= F32_SUBLANES:
        return dil, 1
    d1 = F32_SUBLANES // 2
    assert dil % d1 == 0 and dil // d1 <= F32_SUBLANES
    return d1, dil // d1


def _rope_tables(t):
    half = ROPE_DIM // 2
    pos = jnp.arange(t, dtype=F32)
    inv_freq = ROPE_THETA ** (-jnp.arange(0, ROPE_DIM, 2, dtype=F32) / ROPE_DIM)
    ang = pos[:, None] * inv_freq[None, :]
    cos = jnp.cos(ang)
    sin = jnp.sin(ang)
    rest = HEAD_DIM - ROPE_DIM
    one = jnp.ones((t, rest), F32)
    zero = jnp.zeros((t, rest), F32)
    zh = jnp.zeros((t, half), F32)
    return jnp.stack([jnp.concatenate([cos, cos, one], axis=-1),
                      jnp.concatenate([zh, sin, zero], axis=-1),
                      jnp.concatenate([-sin, zh, zero], axis=-1)], axis=0)


def norm_matmul_rope(x, g, w, rot, *, batch, seq, tm, dils):
    m, k = x.shape
    n = w.shape[1]
    tn = n // (3 * len(dils))
    heads = tn // HEAD_DIM
    tpb = seq // tm
    assert all(ROW_CHUNK % (BF16_SUBLANES * dil) == 0 for dil in dils) and tm % ROW_CHUNK == 0

    def out_map(g):
        return lambda i, j: (jnp.clip(j - 3 * g, 0, 2), i // tpb, 0, i % tpb, 0)

    vmem = (2 * tm * k * 4 + tm * k * 2 + 2 * k * tn * 2 + 2 * len(dils) * tm * tn * 2 + tm * tn * 4
            + 2 * ROW_CHUNK * tn * 4 + 2 * 3 * tm * HEAD_DIM * 4 + heads * ROW_CHUNK * HEAD_DIM * 4)
    return pl.pallas_call(
        functools.partial(_norm_matmul_rope_kernel, dils=dils),
        grid=(m // tm, n // tn),
        in_specs=[
            pl.BlockSpec((tm, k), lambda i, j: (i, 0)),
            pl.BlockSpec((1, k), lambda i, j: (0, 0)),
            pl.BlockSpec((k, tn), lambda i, j: (0, j)),
            pl.BlockSpec((3, tm, HEAD_DIM), lambda i, j: (0, i % tpb, 0)),
        ],
        out_specs=[pl.BlockSpec((None, None, dil, tm // dil, tn), out_map(g)) for g, dil in enumerate(dils)],
        out_shape=[jax.ShapeDtypeStruct((3, batch, dil, seq // dil, tn), BF16) for dil in dils],
        scratch_shapes=[pltpu.VMEM((tm, k), BF16), pltpu.VMEM((heads, tm, HEAD_DIM), F32),
                        pltpu.VMEM((heads, ROW_CHUNK, HEAD_DIM), F32)],
        compiler_params=_params(("arbitrary", "arbitrary"), vmem),
        name="norm_qkv_dil_rope",
    )(x, g.reshape(1, k), w, rot)


def _na_bias_table(rpb):
    kw = NA_WIN_COLS
    cols = np.arange(GRID_W)
    col_start = np.clip(cols - kw // 2, 0, GRID_W - kw)
    col_mask = (cols[None, :] >= col_start[:, None]) & (cols[None, :] < col_start[:, None] + kw)
    col_idx = np.clip(cols[None, :] - cols[:, None] + NA_WIN_COLS - 1, 0, 2 * NA_WIN_COLS - 2)
    by_col = jnp.where(jnp.asarray(col_mask)[None, None], rpb[:, :, col_idx] * LOG2E, NEG)
    rr = np.arange(NA_BLOCK_ROWS)[:, None]
    kk = np.arange(NA_KEY_ROWS)[None, :]
    row_idx = kk - NA_KEY_BACK - rr + NA_WIN_ROWS - 1
    assert row_idx.min() >= 0 and row_idx.max() <= 2 * NA_WIN_ROWS - 2
    tab = by_col[:, row_idx].transpose(0, 1, 3, 2, 4)
    return tab.reshape(rpb.shape[0], NA_BLOCK_ROWS * GRID_W, NA_KEY_ROWS * GRID_W)


def _na_attn_kernel(q_ref, *refs, heads, rows):
    k_refs = refs[:NA_KEY_CHUNKS]
    v_refs = refs[NA_KEY_CHUNKS:2 * NA_KEY_CHUNKS]
    b_ref, o_ref, bm_ref = refs[2 * NA_KEY_CHUNKS:]
    qb = q_ref.shape[0]
    ck = k_refs[0].shape[0]
    kh = min(NA_WIN_ROWS, rows)
    i = pl.program_id(2)

    def clipped(blk):
        r0 = blk * NA_BLOCK_ROWS
        return (r0 < kh // 2) | (r0 + NA_BLOCK_ROWS - 1 - kh // 2 > rows - kh)

    @pl.when((i == 0) | clipped(i) | clipped(i - 1))
    def _():
        r0 = i * NA_BLOCK_ROWS
        q_row = r0 + lax.broadcasted_iota(jnp.int32, (qb, ck), 0) // GRID_W
        lo = jnp.clip(q_row - kh // 2, 0, rows - kh)
        for c in range(NA_KEY_CHUNKS):
            k_row = (r0 - NA_KEY_BACK + c * (ck // GRID_W)
                     + lax.broadcasted_iota(jnp.int32, (qb, ck), 1) // GRID_W)
            row_mask = jnp.where((k_row >= lo) & (k_row < lo + kh), 0.0, NEG).astype(F32)
            for h in range(heads):
                bm_ref[h, :, c * ck:(c + 1) * ck] = b_ref[h, :, c * ck:(c + 1) * ck] + row_mask

    def scores(h):
        sl = slice(h * HEAD_DIM, (h + 1) * HEAD_DIM)
        q = q_ref[:, sl]
        s = []
        for c, k_ref in enumerate(k_refs):
            sc = lax.dot_general(q, k_ref[:, sl], (((1,), (1,)), ((), ())), preferred_element_type=F32)
            s.append(sc + bm_ref[h, :, c * ck:(c + 1) * ck])
        return s, jnp.max(functools.reduce(jnp.maximum, s), axis=-1, keepdims=True)

    def finish(h, s, m):
        sl = slice(h * HEAD_DIM, (h + 1) * HEAD_DIM)
        p = [jnp.exp2(sc - m) for sc in s]
        l = jnp.sum(functools.reduce(jnp.add, p), axis=-1, keepdims=True)
        o = functools.reduce(jnp.add, [
            jnp.dot(pc.astype(BF16), v_ref[:, sl], preferred_element_type=F32)
            for pc, v_ref in zip(p, v_refs)])
        o_ref[:, sl] = (o / l).astype(o_ref.dtype)

    cur = scores(0)
    for h in range(heads):
        nxt = scores(h + 1) if h + 1 < heads else None
        finish(h, *cur)
        cur = nxt


def na_attention(qkv, bias, *, batch, seq, heads_per_step):
    m = qkv.shape[0]
    nh = bias.shape[0]
    hg = heads_per_step
    wq = hg * HEAD_DIM
    qb = NA_BLOCK_ROWS * GRID_W
    kb = NA_KEY_ROWS * GRID_W // NA_KEY_CHUNKS
    assert qb % kb == 0 and (NA_KEY_BACK * GRID_W) % kb == 0
    nblk = seq // qb
    kpb = seq // kb
    ratio = qb // kb
    back = NA_KEY_BACK * GRID_W // kb
    ncol = nh // hg

    def k_map(c, sec):
        def index(g, b, i):
            return (b * kpb + jnp.clip(i * ratio - back + c, 0, kpb - 1), sec * ncol + g)
        return index

    in_specs = [pl.BlockSpec((qb, wq), lambda g, b, i: (b * nblk + i, g))]
    in_specs += [pl.BlockSpec((kb, wq), k_map(c, 1)) for c in range(NA_KEY_CHUNKS)]
    in_specs += [pl.BlockSpec((kb, wq), k_map(c, 2)) for c in range(NA_KEY_CHUNKS)]
    in_specs += [pl.BlockSpec((hg, qb, NA_KEY_CHUNKS * kb), lambda g, b, i: (g, 0, 0))]
    vmem = (2 * 2 * qb * wq * 2 + 2 * 2 * NA_KEY_CHUNKS * kb * wq * 2
            + 3 * hg * qb * NA_KEY_CHUNKS * kb * 4 + 6 * qb * NA_KEY_CHUNKS * kb * 4)
    return pl.pallas_call(
        functools.partial(_na_attn_kernel, heads=hg, rows=seq // GRID_W),
        grid=(ncol, batch, nblk),
        in_specs=in_specs,
        out_specs=pl.BlockSpec((qb, wq), lambda g, b, i: (b * nblk + i, g)),
        out_shape=jax.ShapeDtypeStruct((m, nh * HEAD_DIM), BF16),
        scratch_shapes=[pltpu.VMEM((hg, qb, NA_KEY_CHUNKS * kb), F32)],
        compiler_params=_params(("parallel", "parallel", "arbitrary"), vmem),
        name="na_attention",
    )(qkv, *([qkv] * (2 * NA_KEY_CHUNKS)), bias)


def _proj_residual_kernel(a_ref, w_ref, r_ref, o_ref):
    o_ref[...] = r_ref[...] + jnp.dot(a_ref[...], w_ref[...], preferred_element_type=F32)


def proj_residual(a, w, res, *, tm):
    m, k = a.shape
    n = w.shape[1]
    vmem = 2 * tm * k * 2 + 2 * k * n * 2 + 4 * tm * n * 4 + tm * n * 4
    return pl.pallas_call(
        _proj_residual_kernel,
        grid=(m // tm,),
        in_specs=[
            pl.BlockSpec((tm, k), lambda i: (i, 0)),
            pl.BlockSpec((k, n), lambda i: (0, 0)),
            pl.BlockSpec((tm, n), lambda i: (i, 0)),
        ],
        out_specs=pl.BlockSpec((tm, n), lambda i: (i, 0)),
        out_shape=jax.ShapeDtypeStruct((m, n), F32),
        compiler_params=_params(("parallel",), vmem),
        name="na_proj_residual",
    )(a, w, res)


def _ffn_kernel(x_ref, g_ref, w1_ref, w2_ref, gf_ref, o_ref, h_ref, *, final_norm):
    f = pl.program_id(1)

    @pl.when(f == 0)
    def _():
        x = x_ref[...]
        h_ref[...] = _rms(x, g_ref[...]).astype(BF16)
        o_ref[...] = x

    for c in range(x_ref.shape[0] // FFN_ROW_CHUNK):
        rows = slice(c * FFN_ROW_CHUNK, (c + 1) * FFN_ROW_CHUNK)
        a = jnp.dot(h_ref[rows, :], w1_ref[...], preferred_element_type=F32)
        a = jnp.square(jnp.maximum(a, 0.0)).astype(BF16)
        o_ref[rows, :] += jnp.dot(a, w2_ref[...], preferred_element_type=F32)

    if final_norm:
        @pl.when(f == pl.num_programs(1) - 1)
        def _():
            o_ref[...] = _rms(o_ref[...], gf_ref[...])


def ffn(x, g, w1, w2, g_final, *, tm, tf, final_norm):
    m, d = x.shape
    dff = w1.shape[1]
    assert tm % FFN_ROW_CHUNK == 0
    vmem = (4 * tm * d * 4 + tm * d * 2 + 2 * 2 * d * tf * 2
            + FFN_ROW_CHUNK * tf * 6 + FFN_ROW_CHUNK * d * 4)
    return pl.pallas_call(
        functools.partial(_ffn_kernel, final_norm=final_norm),
        grid=(m // tm, dff // tf),
        in_specs=[
            pl.BlockSpec((tm, d), lambda i, f: (i, 0)),
            pl.BlockSpec((1, d), lambda i, f: (0, 0)),
            pl.BlockSpec((d, tf), lambda i, f: (0, f)),
            pl.BlockSpec((tf, d), lambda i, f: (f, 0)),
            pl.BlockSpec((1, d), lambda i, f: (0, 0)),
        ],
        out_specs=pl.BlockSpec((tm, d), lambda i, f: (i, 0)),
        out_shape=jax.ShapeDtypeStruct((m, d), F32),
        scratch_shapes=[pltpu.VMEM((tm, d), BF16)],
        compiler_params=_params(("parallel", "arbitrary"), vmem),
        name="ffn_final" if final_norm else "ffn",
    )(x, g.reshape(1, d), w1, w2, g_final.reshape(1, d))


def _dil_attn_kernel(q_ref, kp_ref, kc_ref, kn_ref, vp_ref, vc_ref, vn_ref, o_ref, lse_ref,
                     *, heads, halo, length):
    qb = q_ref.shape[0]
    sub = BLOCK_Q
    nk = sub + 2 * halo
    m0 = pl.program_id(2) * qb
    qi = lax.broadcasted_iota(jnp.int32, (sub, nk), 0)
    kj = lax.broadcasted_iota(jnp.int32, (sub, nk), 1) - halo
    band = jnp.abs(kj - qi) <= halo
    lane = lax.broadcasted_iota(jnp.int32, (sub, HEAD_DIM), 1)

    def window(prev_ref, cur_ref, next_ref, lo, hi, cols):
        parts = [prev_ref[:, cols]] if lo < 0 else []
        parts.append(cur_ref[max(lo, 0):min(hi, qb), cols])
        if hi > qb:
            parts.append(next_ref[:, cols])
        return parts[0] if len(parts) == 1 else jnp.concatenate(parts, axis=0)

    def scores(u, h):
        rows = slice(u * sub, (u + 1) * sub)
        cols = slice(h * HEAD_DIM, (h + 1) * HEAD_DIM)
        k = window(kp_ref, kc_ref, kn_ref, u * sub - halo, (u + 1) * sub + halo, cols)
        s = lax.dot_general(q_ref[rows, cols], k, (((1,), (1,)), ((), ())), preferred_element_type=F32)
        s = s + masks[u]
        return s, jnp.max(s, axis=-1, keepdims=True)

    def finish(u, h, s, m):
        rows = slice(u * sub, (u + 1) * sub)
        cols = slice(h * HEAD_DIM, (h + 1) * HEAD_DIM)
        v = window(vp_ref, vc_ref, vn_ref, u * sub - halo, (u + 1) * sub + halo, cols)
        p = jnp.exp2(s - m)
        l = jnp.sum(p, axis=-1, keepdims=True)
        o = jnp.dot(p.astype(BF16), v, preferred_element_type=F32)
        o_ref[rows, cols] = (o / l).astype(o_ref.dtype)
        return m * LN2 + jnp.log(l)

    masks = []
    for u in range(qb // sub):
        kpos = kj + (m0 + u * sub)
        masks.append(jnp.where(band & (kpos >= 0) & (kpos < length), 0.0, NEG).astype(F32))

    units = [(u, h) for u in range(qb // sub) for h in range(heads)]
    cur = scores(*units[0])
    lse_tile = None
    for n, (u, h) in enumerate(units):
        nxt = scores(*units[n + 1]) if n + 1 < len(units) else None
        lse = finish(u, h, *cur)
        lse_tile = jnp.where(lane == h, lse, jnp.zeros((sub, HEAD_DIM), F32) if h == 0 else lse_tile)
        if h == heads - 1:
            lse_ref[u * sub:(u + 1) * sub, :] = lse_tile
        cur = nxt


def dil_attention(qkv, *, group, halo):
    _, batch, dil, length, wq = qkv.shape
    heads = wq // HEAD_DIM
    qb = min(DIL_STEP_ROWS, length)
    assert BLOCK_Q % halo == 0 and qb % BLOCK_Q == 0 and length % qb == 0
    nblk = length // qb
    hpb = length // halo
    per = qb // halo

    def cur_map(sec):
        return lambda b, r, i: (sec, b, r, i, 0)

    def prev_map(sec):
        return lambda b, r, i: (sec, b, r, jnp.maximum(i * per - 1, 0), 0)

    def next_map(sec):
        return lambda b, r, i: (sec, b, r, jnp.minimum((i + 1) * per, hpb - 1), 0)

    def spec(rows, index_map):
        return pl.BlockSpec((None, None, None, rows, wq), index_map)

    in_specs = [spec(qb, cur_map(0))]
    for sec in (1, 2):
        in_specs += [spec(halo, prev_map(sec)), spec(qb, cur_map(sec)), spec(halo, next_map(sec))]
    vmem = (2 * (2 * qb * wq * 2 + 2 * (qb + 2 * halo) * wq * 2 + qb * HEAD_DIM * 4)
            + 8 * BLOCK_Q * (BLOCK_Q + 2 * halo) * 4)
    return pl.pallas_call(
        functools.partial(_dil_attn_kernel, heads=heads, halo=halo, length=length),
        grid=(batch, dil, nblk),
        in_specs=in_specs,
        out_specs=[pl.BlockSpec((None, None, qb, wq), lambda b, r, i: (b, r, i, 0)),
                   pl.BlockSpec((None, None, qb, HEAD_DIM), lambda b, r, i: (b, r, i, 0))],
        out_shape=[jax.ShapeDtypeStruct((batch, dil, length, wq), BF16),
                   jax.ShapeDtypeStruct((batch, dil, length, HEAD_DIM), F32)],
        compiler_params=_params(("parallel", "parallel", "arbitrary"), vmem),
        name=f"dil_attention_g{group}",
    )(*([qkv] * 7))


def _merge_proj_kernel(*refs, dils, heads):
    g_n = len(dils)
    o_refs = refs[:g_n]
    lse_refs = refs[g_n:2 * g_n]
    w_ref, r_ref, out_ref, a_ref, lt_ref, ot_ref, tt_ref = refs[2 * g_n:]
    tm = a_ref.shape[0]

    def merged(c):
        rows = slice(c * ROW_CHUNK, (c + 1) * ROW_CHUNK)
        lses = []
        for g, dil in enumerate(dils):
            if dil == 1:
                lses.append(lse_refs[g][0, rows, :])
                continue
            n = ROW_CHUNK // dil
            d1, d2 = _stride_split(dil)
            n1 = ROW_CHUNK // d1

            def to_token_order(dst_ref, dst_idx, src):
                if d2 == 1:
                    for r in range(dil):
                        dst_ref[dst_idx, pl.ds(c * ROW_CHUNK + r, n, stride=dil), :] = src(r)
                    return
                for r1 in range(d1):
                    for r2 in range(d2):
                        tt_ref[pl.ds(r1 * n1 + r2, n, stride=d2), :] = src(r1 + d1 * r2)
                for r1 in range(d1):
                    dst_ref[dst_idx, pl.ds(c * ROW_CHUNK + r1, n1, stride=d1), :] = tt_ref[r1 * n1:(r1 + 1) * n1, :]

            to_token_order(lt_ref, g, lambda r: lse_refs[g][r, c * n:(c + 1) * n, :])
            for h in range(heads):
                to_token_order(ot_ref, g * heads + h, lambda r: o_refs[g][
                    r, c * n:(c + 1) * n, h * HEAD_DIM:(h + 1) * HEAD_DIM].astype(F32))
            lses.append(lt_ref[g, rows, :])

        mx = functools.reduce(jnp.maximum, lses)
        es = [jnp.exp(l - mx) for l in lses]
        den = functools.reduce(jnp.add, es)
        wts = [e / den for e in es]
        for h in range(heads):
            sl = slice(h * HEAD_DIM, (h + 1) * HEAD_DIM)
            acc = None
            for g, dil in enumerate(dils):
                o = o_refs[g][0, rows, sl].astype(F32) if dil == 1 else ot_ref[g * heads + h, rows, :]
                term = wts[g][:, h:h + 1] * o
                acc = term if acc is None else acc + term
            a_ref[rows, sl] = acc.astype(BF16)

    merged(0)
    for c in range(tm // ROW_CHUNK):
        rows = slice(c * ROW_CHUNK, (c + 1) * ROW_CHUNK)
        if c + 1 < tm // ROW_CHUNK:
            merged(c + 1)
        out_ref[rows, :] = r_ref[rows, :] + jnp.dot(a_ref[rows, :], w_ref[...], preferred_element_type=F32)


def merge_proj_residual(outs, lses, w, res, *, seq, tm):
    dils = tuple(o.shape[1] for o in outs)
    k = outs[0].shape[-1]
    m, n = res.shape
    g_n = len(outs)
    heads = k // HEAD_DIM
    tpb = seq // tm
    assert all(ROW_CHUNK % (BF16_SUBLANES * dil) == 0 for dil in dils) and tm % ROW_CHUNK == 0

    def stream_spec(dil, width):
        return pl.BlockSpec((None, dil, tm // dil, width), lambda i: (i // tpb, 0, i % tpb, 0))

    vmem = (2 * g_n * tm * k * 2 + 2 * g_n * tm * HEAD_DIM * 4 + 2 * k * n * 2 + 4 * tm * n * 4
            + tm * k * 2 + tm * n * 4 + g_n * tm * HEAD_DIM * 4 + g_n * tm * k * 4)
    return pl.pallas_call(
        functools.partial(_merge_proj_kernel, dils=dils, heads=heads),
        grid=(m // tm,),
        in_specs=([stream_spec(dil, k) for dil in dils]
                  + [stream_spec(dil, HEAD_DIM) for dil in dils]
                  + [pl.BlockSpec((k, n), lambda i: (0, 0)),
                     pl.BlockSpec((tm, n), lambda i: (i, 0))]),
        out_specs=pl.BlockSpec((tm, n), lambda i: (i, 0)),
        out_shape=jax.ShapeDtypeStruct((m, n), F32),
        scratch_shapes=[pltpu.VMEM((tm, k), BF16),
                        pltpu.VMEM((g_n, tm, HEAD_DIM), F32),
                        pltpu.VMEM((g_n * heads, tm, HEAD_DIM), F32),
                        pltpu.VMEM((ROW_CHUNK, HEAD_DIM), F32)],
        compiler_params=_params(("parallel",), vmem),
        name="dil_merge_proj_residual",
    )(*outs, *lses, w, res)


def kernel(x, na_norm, na_wqkv, na_rpb, na_wo, ffn0_norm, ffn0_w1, ffn0_w2, dil_norm, dil_wqkv, dil_wo,
           ffn1_norm, ffn1_w1, ffn1_w2, final_norm):
    batch, seq, d = x.shape
    m = batch * seq
    rows = seq // GRID_W
    na_heads = na_rpb.shape[0]
    assert rows >= NA_KEY_ROWS and rows % NA_BLOCK_ROWS == 0 and rows >= NA_WIN_ROWS
    assert NA_KEY_ROWS >= NA_BLOCK_ROWS + NA_WIN_ROWS - 1

    xf = x.reshape(m, d)
    bf = lambda w: w.astype(BF16)

    qw = na_heads * HEAD_DIM
    col_scale = jnp.concatenate([jnp.full((qw,), Q_SCALE, F32), jnp.ones((2 * qw,), F32)])
    qkv = norm_matmul(xf, na_norm, bf(na_wqkv), col_scale, tm=min(1024, m), tn=min(1024, qw))
    bias = _na_bias_table(na_rpb)
    attn = na_attention(qkv, bias, batch=batch, seq=seq, heads_per_step=min(8, na_heads))
    xf = proj_residual(attn, bf(na_wo), xf, tm=min(512, m))
    xf = ffn(xf, ffn0_norm, bf(ffn0_w1), bf(ffn0_w2), final_norm, tm=min(1024, m), tf=512, final_norm=False)

    rot = _rope_tables(seq)
    dils = tuple(dil for _, dil in DIL_GROUPS)
    qkvs = norm_matmul_rope(xf, dil_norm, bf(dil_wqkv), rot, batch=batch, seq=seq, tm=min(1024, seq), dils=dils)
    outs, lses = [], []
    for gi, (window, dil) in enumerate(DIL_GROUPS):
        o, lse = dil_attention(qkvs[gi], group=gi, halo=(window // 2) // dil)
        outs.append(o)
        lses.append(lse)
    xf = merge_proj_residual(outs, lses, bf(dil_wo), xf, seq=seq, tm=min(512, seq))
    xf = ffn(xf, ffn1_norm, bf(ffn1_w1), bf(ffn1_w2), final_norm, tm=min(1024, m), tf=512, final_norm=True)
    return xf.reshape(batch, seq, d)
```

```python
import functools
import math

import numpy as np
import jax
import jax.numpy as jnp
from jax import lax
from jax.experimental import pallas as pl
from jax.experimental.pallas import tpu as pltpu

GRID_W = 64
HEAD_DIM = 128
NA_WIN_ROWS = 8
NA_WIN_COLS = 16
DIL_GROUPS = ((128, 1), (512, 4), (2048, 16))
ROPE_THETA = 500000.0
ROPE_DIM = HEAD_DIM // 4
BLOCK_Q = 128
EPS = 1e-6
NEG = -1e30

LOG2E = math.log2(math.e)
LN2 = math.log(2.0)
Q_SCALE = HEAD_DIM ** -0.5 * LOG2E

NA_BLOCK_ROWS = 4
NA_KEY_BACK = NA_WIN_ROWS // 2
NA_KEY_ROWS = 12
NA_KEY_CHUNKS = 3

DIL_STEP_ROWS = 512
ROW_CHUNK = 256
FFN_ROW_CHUNK = 512
BF16_SUBLANES = 16
F32_SUBLANES = 8
LANES = 128

V7X_VMEM_BYTES = 64 * 1024 * 1024
VMEM_CAP_BYTES = V7X_VMEM_BYTES - 8 * 1024 * 1024

F32 = jnp.float32
BF16 = jnp.bfloat16


def _params(semantics, vmem_bytes):
    limit = min(int(vmem_bytes * 1.25) + (4 << 20), VMEM_CAP_BYTES)
    return pltpu.CompilerParams(dimension_semantics=semantics, vmem_limit_bytes=limit)


def _rms(x, g):
    return x * lax.rsqrt(jnp.mean(x * x, axis=-1, keepdims=True) + EPS) * g


def _norm_matmul_kernel(x_ref, g_ref, w_ref, cs_ref, o_ref, h_ref):
    @pl.when(pl.program_id(1) == 0)
    def _():
        h_ref[...] = _rms(x_ref[...], g_ref[...]).astype(BF16)

    for c in range(x_ref.shape[0] // ROW_CHUNK):
        rows = slice(c * ROW_CHUNK, (c + 1) * ROW_CHUNK)
        acc = jnp.dot(h_ref[rows, :], w_ref[...], preferred_element_type=F32)
        o_ref[rows, :] = (acc * cs_ref[...]).astype(o_ref.dtype)


def norm_matmul(x, g, w, col_scale, *, tm, tn):
    m, k = x.shape
    n = w.shape[1]
    vmem = 2 * tm * k * 4 + tm * k * 2 + 2 * k * tn * 2 + 2 * tm * tn * 2 + tm * tn * 4
    return pl.pallas_call(
        _norm_matmul_kernel,
        grid=(m // tm, n // tn),
        in_specs=[
            pl.BlockSpec((tm, k), lambda i, j: (i, 0)),
            pl.BlockSpec((1, k), lambda i, j: (0, 0)),
            pl.BlockSpec((k, tn), lambda i, j: (0, j)),
            pl.BlockSpec((1, tn), lambda i, j: (0, j)),
        ],
        out_specs=pl.BlockSpec((tm, tn), lambda i, j: (i, j)),
        out_shape=jax.ShapeDtypeStruct((m, n), BF16),
        scratch_shapes=[pltpu.VMEM((tm, k), BF16)],
        compiler_params=_params(("parallel", "arbitrary"), vmem),
        name="norm_qkv_na",
    )(x, g.reshape(1, k), w, col_scale.reshape(1, n))


def _norm_matmul_rope_kernel(x_ref, g_ref, w_ref, rot_ref, *refs, dils):
    o_refs = refs[:len(dils)]
    h_ref, s_ref, t_ref = refs[len(dils):]
    j = pl.program_id(1)
    tm = x_ref.shape[0]
    heads = s_ref.shape[0]
    half = ROPE_DIM // 2

    @pl.when(j == 0)
    def _():
        h_ref[...] = _rms(x_ref[...], g_ref[...]).astype(BF16)

    def body(o_ref, dil, rotate):
        n = ROW_CHUNK // dil
        d1, d2 = _stride_split(dil)
        n1 = ROW_CHUNK // d1
        scale = jnp.where(j % 3 == 0, Q_SCALE, 1.0).astype(F32)
        for c in range(tm // ROW_CHUNK):
            rows = slice(c * ROW_CHUNK, (c + 1) * ROW_CHUNK)
            acc = jnp.dot(h_ref[rows, :], w_ref[...], preferred_element_type=F32)
            if rotate:
                cos = rot_ref[0, rows, :] * scale
                sin_hi = rot_ref[1, rows, :] * scale
                sin_lo = rot_ref[2, rows, :] * scale
            for h in range(heads):
                cols = slice(h * HEAD_DIM, (h + 1) * HEAD_DIM)
                a = acc[:, cols]
                if rotate:
                    a = (a * cos + pltpu.roll(a, half, 1) * sin_hi
                         + pltpu.roll(a, HEAD_DIM - half, 1) * sin_lo)
                if dil == 1:
                    o_ref[0, rows, cols] = a.astype(BF16)
                    continue
                s_ref[h, rows, :] = a
                if d2 == 1:
                    for r in range(dil):
                        o_ref[r, c * n:(c + 1) * n, cols] = (
                            s_ref[h, pl.ds(c * ROW_CHUNK + r, n, stride=dil), :].astype(BF16))
                    continue
                for r1 in range(d1):
                    t_ref[h, r1 * n1:(r1 + 1) * n1, :] = s_ref[h, pl.ds(c * ROW_CHUNK + r1, n1, stride=d1), :]
                for r1 in range(d1):
                    for r2 in range(d2):
                        o_ref[r1 + d1 * r2, c * n:(c + 1) * n, cols] = (
                            t_ref[h, pl.ds(r1 * n1 + r2, n, stride=d2), :].astype(BF16))

    for g, dil in enumerate(dils):
        pl.when((j // 3 == g) & (j % 3 != 2))(functools.partial(body, o_refs[g], dil, True))
        pl.when((j // 3 == g) & (j % 3 == 2))(functools.partial(body, o_refs[g], dil, False))


def _stride_split(dil):
    if dil <= F32_SUBLANES:
        return dil, 1
    d1 = F32_SUBLANES // 2
    assert dil % d1 == 0 and dil // d1 <= F32_SUBLANES
    return d1, dil // d1


def _rope_tables(t):
    half = ROPE_DIM // 2
    pos = jnp.arange(t, dtype=F32)
    inv_freq = ROPE_THETA ** (-jnp.arange(0, ROPE_DIM, 2, dtype=F32) / ROPE_DIM)
    ang = pos[:, None] * inv_freq[None, :]
    cos = jnp.cos(ang)
    sin = jnp.sin(ang)
    rest = HEAD_DIM - ROPE_DIM
    one = jnp.ones((t, rest), F32)
    zero = jnp.zeros((t, rest), F32)
    zh = jnp.zeros((t, half), F32)
    return jnp.stack([jnp.concatenate([cos, cos, one], axis=-1),
                      jnp.concatenate([zh, sin, zero], axis=-1),
                      jnp.concatenate([-sin, zh, zero], axis=-1)], axis=0)


def norm_matmul_rope(x, g, w, rot, *, batch, seq, tm, dils):
    m, k = x.shape
    n = w.shape[1]
    tn = n // (3 * len(dils))
    heads = tn // HEAD_DIM
    tpb = seq // tm
    assert all(ROW_CHUNK % (BF16_SUBLANES * dil) == 0 for dil in dils) and tm % ROW_CHUNK == 0

    def out_map(g):
        return lambda i, j: (jnp.clip(j - 3 * g, 0, 2), i // tpb, 0, i % tpb, 0)

    vmem = (2 * tm * k * 4 + tm * k * 2 + 2 * k * tn * 2 + 2 * len(dils) * tm * tn * 2 + tm * tn * 4
            + 2 * ROW_CHUNK * tn * 4 + 2 * 3 * tm * HEAD_DIM * 4 + heads * ROW_CHUNK * HEAD_DIM * 4)
    return pl.pallas_call(
        functools.partial(_norm_matmul_rope_kernel, dils=dils),
        grid=(m // tm, n // tn),
        in_specs=[
            pl.BlockSpec((tm, k), lambda i, j: (i, 0)),
            pl.BlockSpec((1, k), lambda i, j: (0, 0)),
            pl.BlockSpec((k, tn), lambda i, j: (0, j)),
            pl.BlockSpec((3, tm, HEAD_DIM), lambda i, j: (0, i % tpb, 0)),
        ],
        out_specs=[pl.BlockSpec((None, None, dil, tm // dil, tn), out_map(g)) for g, dil in enumerate(dils)],
        out_shape=[jax.ShapeDtypeStruct((3, batch, dil, seq // dil, tn), BF16) for dil in dils],
        scratch_shapes=[pltpu.VMEM((tm, k), BF16), pltpu.VMEM((heads, tm, HEAD_DIM), F32),
                        pltpu.VMEM((heads, ROW_CHUNK, HEAD_DIM), F32)],
        compiler_params=_params(("arbitrary", "arbitrary"), vmem),
        name="norm_qkv_dil_rope",
    )(x, g.reshape(1, k), w, rot)


def _na_bias_table(rpb):
    nh, n_row_off, n_col_off = rpb.shape
    assert 2 * GRID_W == LANES and NA_KEY_ROWS % 2 == 0 and n_col_off <= LANES
    rpb_lanes = jnp.pad(rpb, ((0, 0), (0, 0), (0, LANES - n_col_off)))
    return pl.pallas_call(
        _na_bias_kernel,
        grid=(nh,),
        in_specs=[pl.BlockSpec((None, n_row_off, LANES), lambda h: (h, 0, 0))],
        out_specs=pl.BlockSpec((None, NA_BLOCK_ROWS * GRID_W, NA_KEY_ROWS * GRID_W), lambda h: (h, 0, 0)),
        out_shape=jax.ShapeDtypeStruct((nh, NA_BLOCK_ROWS * GRID_W, NA_KEY_ROWS * GRID_W), F32),
        compiler_params=_params(("parallel",), 4 * NA_BLOCK_ROWS * GRID_W * NA_KEY_ROWS * GRID_W * 4),
        name="na_bias_table",
    )(rpb_lanes)


def _na_bias_kernel(rpb_ref, o_ref):
    w = GRID_W
    qc = lax.broadcasted_iota(jnp.int32, (w, LANES), 0)
    kc = lax.broadcasted_iota(jnp.int32, (w, LANES), 1)
    start = jnp.clip(qc - NA_WIN_COLS // 2, 0, w - NA_WIN_COLS)
    in_window = (kc >= start) & (kc < start + NA_WIN_COLS)
    by_col = []
    for a in range(rpb_ref.shape[0]):
        row = jnp.broadcast_to(rpb_ref[a:a + 1, :], (w, LANES))
        shifted = pltpu.roll(row, LANES - (NA_WIN_COLS - 1), 1, stride=1, stride_axis=0)
        by_col.append(jnp.where(in_window, shifted * LOG2E, NEG))
    for rr in range(NA_BLOCK_ROWS):
        for kk in range(0, NA_KEY_ROWS, 2):
            a = kk - NA_KEY_BACK - rr + NA_WIN_ROWS - 1
            assert 0 <= a and a + 1 < rpb_ref.shape[0]
            pair = jnp.where(kc < w, by_col[a], pltpu.roll(by_col[a + 1], w, 1))
            o_ref[rr * w:(rr + 1) * w, kk * w:(kk + 2) * w] = pair


def _na_attn_kernel(q_ref, *refs, heads, rows):
    k_refs = refs[:NA_KEY_CHUNKS]
    v_refs = refs[NA_KEY_CHUNKS:2 * NA_KEY_CHUNKS]
    b_ref, o_ref, bm_ref = refs[2 * NA_KEY_CHUNKS:]
    qb = q_ref.shape[0]
    ck = k_refs[0].shape[0]
    kh = min(NA_WIN_ROWS, rows)
    i = pl.program_id(2)

    def clipped(blk):
        r0 = blk * NA_BLOCK_ROWS
        return (r0 < kh // 2) | (r0 + NA_BLOCK_ROWS - 1 - kh // 2 > rows - kh)

    @pl.when((i == 0) | clipped(i) | clipped(i - 1))
    def _():
        r0 = i * NA_BLOCK_ROWS
        q_row = r0 + lax.broadcasted_iota(jnp.int32, (qb, ck), 0) // GRID_W
        lo = jnp.clip(q_row - kh // 2, 0, rows - kh)
        for c in range(NA_KEY_CHUNKS):
            k_row = (r0 - NA_KEY_BACK + c * (ck // GRID_W)
                     + lax.broadcasted_iota(jnp.int32, (qb, ck), 1) // GRID_W)
            row_mask = jnp.where((k_row >= lo) & (k_row < lo + kh), 0.0, NEG).astype(F32)
            for h in range(heads):
                bm_ref[h, :, c * ck:(c + 1) * ck] = b_ref[h, :, c * ck:(c + 1) * ck] + row_mask

    def scores(h):
        sl = slice(h * HEAD_DIM, (h + 1) * HEAD_DIM)
        q = q_ref[:, sl]
        s = []
        for c, k_ref in enumerate(k_refs):
            sc = lax.dot_general(q, k_ref[:, sl], (((1,), (1,)), ((), ())), preferred_element_type=F32)
            s.append(sc + bm_ref[h, :, c * ck:(c + 1) * ck])
        return s, jnp.max(functools.reduce(jnp.maximum, s), axis=-1, keepdims=True)

    def finish(h, s, m):
        sl = slice(h * HEAD_DIM, (h + 1) * HEAD_DIM)
        p = [jnp.exp2(sc - m) for sc in s]
        l = jnp.sum(functools.reduce(jnp.add, p), axis=-1, keepdims=True)
        o = functools.reduce(jnp.add, [
            jnp.dot(pc.astype(BF16), v_ref[:, sl], preferred_element_type=F32)
            for pc, v_ref in zip(p, v_refs)])
        o_ref[:, sl] = (o / l).astype(o_ref.dtype)

    cur = scores(0)
    for h in range(heads):
        nxt = scores(h + 1) if h + 1 < heads else None
        finish(h, *cur)
        cur = nxt


def na_attention(qkv, bias, *, batch, seq, heads_per_step):
    m = qkv.shape[0]
    nh = bias.shape[0]
    hg = heads_per_step
    wq = hg * HEAD_DIM
    qb = NA_BLOCK_ROWS * GRID_W
    kb = NA_KEY_ROWS * GRID_W // NA_KEY_CHUNKS
    assert qb % kb == 0 and (NA_KEY_BACK * GRID_W) % kb == 0
    nblk = seq // qb
    kpb = seq // kb
    ratio = qb // kb
    back = NA_KEY_BACK * GRID_W // kb
    ncol = nh // hg

    def k_map(c, sec):
        def index(g, b, i):
            return (b * kpb + jnp.clip(i * ratio - back + c, 0, kpb - 1), sec * ncol + g)
        return index

    in_specs = [pl.BlockSpec((qb, wq), lambda g, b, i: (b * nblk + i, g))]
    in_specs += [pl.BlockSpec((kb, wq), k_map(c, 1)) for c in range(NA_KEY_CHUNKS)]
    in_specs += [pl.BlockSpec((kb, wq), k_map(c, 2)) for c in range(NA_KEY_CHUNKS)]
    in_specs += [pl.BlockSpec((hg, qb, NA_KEY_CHUNKS * kb), lambda g, b, i: (g, 0, 0))]
    vmem = (2 * 2 * qb * wq * 2 + 2 * 2 * NA_KEY_CHUNKS * kb * wq * 2
            + 3 * hg * qb * NA_KEY_CHUNKS * kb * 4 + 6 * qb * NA_KEY_CHUNKS * kb * 4)
    return pl.pallas_call(
        functools.partial(_na_attn_kernel, heads=hg, rows=seq // GRID_W),
        grid=(ncol, batch, nblk),
        in_specs=in_specs,
        out_specs=pl.BlockSpec((qb, wq), lambda g, b, i: (b * nblk + i, g)),
        out_shape=jax.ShapeDtypeStruct((m, nh * HEAD_DIM), BF16),
        scratch_shapes=[pltpu.VMEM((hg, qb, NA_KEY_CHUNKS * kb), F32)],
        compiler_params=_params(("parallel", "parallel", "arbitrary"), vmem),
        name="na_attention",
    )(qkv, *([qkv] * (2 * NA_KEY_CHUNKS)), bias)


def _proj_residual_kernel(a_ref, w_ref, r_ref, o_ref):
    o_ref[...] = r_ref[...] + jnp.dot(a_ref[...], w_ref[...], preferred_element_type=F32)


def proj_residual(a, w, res, *, tm):
    m, k = a.shape
    n = w.shape[1]
    vmem = 2 * tm * k * 2 + 2 * k * n * 2 + 4 * tm * n * 4 + tm * n * 4
    return pl.pallas_call(
        _proj_residual_kernel,
        grid=(m // tm,),
        in_specs=[
            pl.BlockSpec((tm, k), lambda i: (i, 0)),
            pl.BlockSpec((k, n), lambda i: (0, 0)),
            pl.BlockSpec((tm, n), lambda i: (i, 0)),
        ],
        out_specs=pl.BlockSpec((tm, n), lambda i: (i, 0)),
        out_shape=jax.ShapeDtypeStruct((m, n), F32),
        compiler_params=_params(("parallel",), vmem),
        name="na_proj_residual",
    )(a, w, res)


def _ffn_kernel(x_ref, g_ref, w1_ref, w2_ref, gf_ref, o_ref, h_ref, *, final_norm):
    f = pl.program_id(1)

    @pl.when(f == 0)
    def _():
        x = x_ref[...]
        h_ref[...] = _rms(x, g_ref[...]).astype(BF16)
        o_ref[...] = x

    for c in range(x_ref.shape[0] // FFN_ROW_CHUNK):
        rows = slice(c * FFN_ROW_CHUNK, (c + 1) * FFN_ROW_CHUNK)
        a = jnp.dot(h_ref[rows, :], w1_ref[...], preferred_element_type=F32)
        a = jnp.square(jnp.maximum(a, 0.0)).astype(BF16)
        o_ref[rows, :] += jnp.dot(a, w2_ref[...], preferred_element_type=F32)

    if final_norm:
        @pl.when(f == pl.num_programs(1) - 1)
        def _():
            o_ref[...] = _rms(o_ref[...], gf_ref[...])


def ffn(x, g, w1, w2, g_final, *, tm, tf, final_norm):
    m, d = x.shape
    dff = w1.shape[1]
    assert tm % FFN_ROW_CHUNK == 0
    vmem = (4 * tm * d * 4 + tm * d * 2 + 2 * 2 * d * tf * 2
            + FFN_ROW_CHUNK * tf * 6 + FFN_ROW_CHUNK * d * 4)
    return pl.pallas_call(
        functools.partial(_ffn_kernel, final_norm=final_norm),
        grid=(m // tm, dff // tf),
        in_specs=[
            pl.BlockSpec((tm, d), lambda i, f: (i, 0)),
            pl.BlockSpec((1, d), lambda i, f: (0, 0)),
            pl.BlockSpec((d, tf), lambda i, f: (0, f)),
            pl.BlockSpec((tf, d), lambda i, f: (f, 0)),
            pl.BlockSpec((1, d), lambda i, f: (0, 0)),
        ],
        out_specs=pl.BlockSpec((tm, d), lambda i, f: (i, 0)),
        out_shape=jax.ShapeDtypeStruct((m, d), F32),
        scratch_shapes=[pltpu.VMEM((tm, d), BF16)],
        compiler_params=_params(("parallel", "arbitrary"), vmem),
        name="ffn_final" if final_norm else "ffn",
    )(x, g.reshape(1, d), w1, w2, g_final.reshape(1, d))


def _dil_attn_kernel(q_ref, kp_ref, kc_ref, kn_ref, vp_ref, vc_ref, vn_ref, o_ref, lse_ref,
                     *, heads, halo, length):
    qb = q_ref.shape[0]
    sub = BLOCK_Q
    nk = sub + 2 * halo
    m0 = pl.program_id(2) * qb
    qi = lax.broadcasted_iota(jnp.int32, (sub, nk), 0)
    kj = lax.broadcasted_iota(jnp.int32, (sub, nk), 1) - halo
    band = jnp.abs(kj - qi) <= halo
    lane = lax.broadcasted_iota(jnp.int32, (sub, HEAD_DIM), 1)

    def window(prev_ref, cur_ref, next_ref, lo, hi, cols):
        parts = [prev_ref[:, cols]] if lo < 0 else []
        parts.append(cur_ref[max(lo, 0):min(hi, qb), cols])
        if hi > qb:
            parts.append(next_ref[:, cols])
        return parts[0] if len(parts) == 1 else jnp.concatenate(parts, axis=0)

    def scores(u, h):
        rows = slice(u * sub, (u + 1) * sub)
        cols = slice(h * HEAD_DIM, (h + 1) * HEAD_DIM)
        k = window(kp_ref, kc_ref, kn_ref, u * sub - halo, (u + 1) * sub + halo, cols)
        s = lax.dot_general(q_ref[rows, cols], k, (((1,), (1,)), ((), ())), preferred_element_type=F32)
        s = s + masks[u]
        return s, jnp.max(s, axis=-1, keepdims=True)

    def finish(u, h, s, m):
        rows = slice(u * sub, (u + 1) * sub)
        cols = slice(h * HEAD_DIM, (h + 1) * HEAD_DIM)
        v = window(vp_ref, vc_ref, vn_ref, u * sub - halo, (u + 1) * sub + halo, cols)
        p = jnp.exp2(s - m)
        l = jnp.sum(p, axis=-1, keepdims=True)
        o = jnp.dot(p.astype(BF16), v, preferred_element_type=F32)
        o_ref[rows, cols] = (o / l).astype(o_ref.dtype)
        return m * LN2 + jnp.log(l)

    masks = []
    for u in range(qb // sub):
        kpos = kj + (m0 + u * sub)
        masks.append(jnp.where(band & (kpos >= 0) & (kpos < length), 0.0, NEG).astype(F32))

    units = [(u, h) for u in range(qb // sub) for h in range(heads)]
    cur = scores(*units[0])
    lse_tile = None
    for n, (u, h) in enumerate(units):
        nxt = scores(*units[n + 1]) if n + 1 < len(units) else None
        lse = finish(u, h, *cur)
        lse_tile = jnp.where(lane == h, lse, jnp.zeros((sub, HEAD_DIM), F32) if h == 0 else lse_tile)
        if h == heads - 1:
            lse_ref[u * sub:(u + 1) * sub, :] = lse_tile
        cur = nxt


def dil_attention(qkv, *, group, halo):
    _, batch, dil, length, wq = qkv.shape
    heads = wq // HEAD_DIM
    qb = min(DIL_STEP_ROWS, length)
    assert BLOCK_Q % halo == 0 and qb % BLOCK_Q == 0 and length % qb == 0
    nblk = length // qb
    hpb = length // halo
    per = qb // halo

    def cur_map(sec):
        return lambda b, r, i: (sec, b, r, i, 0)

    def prev_map(sec):
        return lambda b, r, i: (sec, b, r, jnp.maximum(i * per - 1, 0), 0)

    def next_map(sec):
        return lambda b, r, i: (sec, b, r, jnp.minimum((i + 1) * per, hpb - 1), 0)

    def spec(rows, index_map):
        return pl.BlockSpec((None, None, None, rows, wq), index_map)

    in_specs = [spec(qb, cur_map(0))]
    for sec in (1, 2):
        in_specs += [spec(halo, prev_map(sec)), spec(qb, cur_map(sec)), spec(halo, next_map(sec))]
    vmem = (2 * (2 * qb * wq * 2 + 2 * (qb + 2 * halo) * wq * 2 + qb * HEAD_DIM * 4)
            + 8 * BLOCK_Q * (BLOCK_Q + 2 * halo) * 4)
    return pl.pallas_call(
        functools.partial(_dil_attn_kernel, heads=heads, halo=halo, length=length),
        grid=(batch, dil, nblk),
        in_specs=in_specs,
        out_specs=[pl.BlockSpec((None, None, qb, wq), lambda b, r, i: (b, r, i, 0)),
                   pl.BlockSpec((None, None, qb, HEAD_DIM), lambda b, r, i: (b, r, i, 0))],
        out_shape=[jax.ShapeDtypeStruct((batch, dil, length, wq), BF16),
                   jax.ShapeDtypeStruct((batch, dil, length, HEAD_DIM), F32)],
        compiler_params=_params(("parallel", "parallel", "arbitrary"), vmem),
        name=f"dil_attention_g{group}",
    )(*([qkv] * 7))


def _merge_proj_kernel(*refs, dils, heads):
    g_n = len(dils)
    o_refs = refs[:g_n]
    lse_refs = refs[g_n:2 * g_n]
    w_ref, r_ref, out_ref, a_ref, lt_ref, ot_ref, tt_ref = refs[2 * g_n:]
    tm = a_ref.shape[0]

    def merged(c):
        rows = slice(c * ROW_CHUNK, (c + 1) * ROW_CHUNK)
        lses = []
        for g, dil in enumerate(dils):
            if dil == 1:
                lses.append(lse_refs[g][0, rows, :])
                continue
            n = ROW_CHUNK // dil
            d1, d2 = _stride_split(dil)
            n1 = ROW_CHUNK // d1

            def to_token_order(dst_ref, dst_idx, src):
                if d2 == 1:
                    for r in range(dil):
                        dst_ref[dst_idx, pl.ds(c * ROW_CHUNK + r, n, stride=dil), :] = src(r)
                    return
                for r1 in range(d1):
                    for r2 in range(d2):
                        tt_ref[pl.ds(r1 * n1 + r2, n, stride=d2), :] = src(r1 + d1 * r2)
                for r1 in range(d1):
                    dst_ref[dst_idx, pl.ds(c * ROW_CHUNK + r1, n1, stride=d1), :] = tt_ref[r1 * n1:(r1 + 1) * n1, :]

            to_token_order(lt_ref, g, lambda r: lse_refs[g][r, c * n:(c + 1) * n, :])
            for h in range(heads):
                to_token_order(ot_ref, g * heads + h, lambda r: o_refs[g][
                    r, c * n:(c + 1) * n, h * HEAD_DIM:(h + 1) * HEAD_DIM].astype(F32))
            lses.append(lt_ref[g, rows, :])

        mx = functools.reduce(jnp.maximum, lses)
        es = [jnp.exp(l - mx) for l in lses]
        den = functools.reduce(jnp.add, es)
        wts = [e / den for e in es]
        for h in range(heads):
            sl = slice(h * HEAD_DIM, (h + 1) * HEAD_DIM)
            acc = None
            for g, dil in enumerate(dils):
                o = o_refs[g][0, rows, sl].astype(F32) if dil == 1 else ot_ref[g * heads + h, rows, :]
                term = wts[g][:, h:h + 1] * o
                acc = term if acc is None else acc + term
            a_ref[rows, sl] = acc.astype(BF16)

    merged(0)
    for c in range(tm // ROW_CHUNK):
        rows = slice(c * ROW_CHUNK, (c + 1) * ROW_CHUNK)
        if c + 1 < tm // ROW_CHUNK:
            merged(c + 1)
        out_ref[rows, :] = r_ref[rows, :] + jnp.dot(a_ref[rows, :], w_ref[...], preferred_element_type=F32)


def merge_proj_residual(outs, lses, w, res, *, seq, tm):
    dils = tuple(o.shape[1] for o in outs)
    k = outs[0].shape[-1]
    m, n = res.shape
    g_n = len(outs)
    heads = k // HEAD_DIM
    tpb = seq // tm
    assert all(ROW_CHUNK % (BF16_SUBLANES * dil) == 0 for dil in dils) and tm % ROW_CHUNK == 0

    def stream_spec(dil, width):
        return pl.BlockSpec((None, dil, tm // dil, width), lambda i: (i // tpb, 0, i % tpb, 0))

    vmem = (2 * g_n * tm * k * 2 + 2 * g_n * tm * HEAD_DIM * 4 + 2 * k * n * 2 + 4 * tm * n * 4
            + tm * k * 2 + tm * n * 4 + g_n * tm * HEAD_DIM * 4 + g_n * tm * k * 4)
    return pl.pallas_call(
        functools.partial(_merge_proj_kernel, dils=dils, heads=heads),
        grid=(m // tm,),
        in_specs=([stream_spec(dil, k) for dil in dils]
                  + [stream_spec(dil, HEAD_DIM) for dil in dils]
                  + [pl.BlockSpec((k, n), lambda i: (0, 0)),
                     pl.BlockSpec((tm, n), lambda i: (i, 0))]),
        out_specs=pl.BlockSpec((tm, n), lambda i: (i, 0)),
        out_shape=jax.ShapeDtypeStruct((m, n), F32),
        scratch_shapes=[pltpu.VMEM((tm, k), BF16),
                        pltpu.VMEM((g_n, tm, HEAD_DIM), F32),
                        pltpu.VMEM((g_n * heads, tm, HEAD_DIM), F32),
                        pltpu.VMEM((ROW_CHUNK, HEAD_DIM), F32)],
        compiler_params=_params(("parallel",), vmem),
        name="dil_merge_proj_residual",
    )(*outs, *lses, w, res)


def kernel(x, na_norm, na_wqkv, na_rpb, na_wo, ffn0_norm, ffn0_w1, ffn0_w2, dil_norm, dil_wqkv, dil_wo,
           ffn1_norm, ffn1_w1, ffn1_w2, final_norm):
    batch, seq, d = x.shape
    m = batch * seq
    rows = seq // GRID_W
    na_heads = na_rpb.shape[0]
    assert rows >= NA_KEY_ROWS and rows % NA_BLOCK_ROWS == 0 and rows >= NA_WIN_ROWS
    assert NA_KEY_ROWS >= NA_BLOCK_ROWS + NA_WIN_ROWS - 1

    xf = x.reshape(m, d)
    bf = lambda w: w.astype(BF16)

    qw = na_heads * HEAD_DIM
    col_scale = jnp.concatenate([jnp.full((qw,), Q_SCALE, F32), jnp.ones((2 * qw,), F32)])
    qkv = norm_matmul(xf, na_norm, bf(na_wqkv), col_scale, tm=min(1024, m), tn=min(1024, qw))
    bias = _na_bias_table(na_rpb)
    attn = na_attention(qkv, bias, batch=batch, seq=seq, heads_per_step=min(8, na_heads))
    xf = proj_residual(attn, bf(na_wo), xf, tm=min(512, m))
    xf = ffn(xf, ffn0_norm, bf(ffn0_w1), bf(ffn0_w2), final_norm, tm=min(1024, m), tf=512, final_norm=False)

    rot = _rope_tables(seq)
    dils = tuple(dil for _, dil in DIL_GROUPS)
    qkvs = norm_matmul_rope(xf, dil_norm, bf(dil_wqkv), rot, batch=batch, seq=seq, tm=min(1024, seq), dils=dils)
    outs, lses = [], []
    for gi, (window, dil) in enumerate(DIL_GROUPS):
        o, lse = dil_attention(qkvs[gi], group=gi, halo=(window // 2) // dil)
        outs.append(o)
        lses.append(lse)
    xf = merge_proj_residual(outs, lses, bf(dil_wo), xf, seq=seq, tm=min(512, seq))
    xf = ffn(xf, ffn1_norm, bf(ffn1_w1), bf(ffn1_w2), final_norm, tm=min(1024, m), tf=512, final_norm=True)
    return xf.reshape(batch, seq, d)
```

```python
import functools
import math

import numpy as np
import jax
import jax.numpy as jnp
from jax import lax
from jax.experimental import pallas as pl
from jax.experimental.pallas import tpu as pltpu

GRID_W = 64
HEAD_DIM = 128
NA_WIN_ROWS = 8
NA_WIN_COLS = 16
DIL_GROUPS = ((128, 1), (512, 4), (2048, 16))
ROPE_THETA = 500000.0
ROPE_DIM = HEAD_DIM // 4
BLOCK_Q = 128
EPS = 1e-6
NEG = -1e30

LOG2E = math.log2(math.e)
LN2 = math.log(2.0)
Q_SCALE = HEAD_DIM ** -0.5 * LOG2E

NA_BLOCK_ROWS = 4
NA_KEY_BACK = NA_WIN_ROWS // 2
NA_KEY_ROWS = 12
NA_KEY_CHUNKS = 3

DIL_STEP_ROWS = 512
ROW_CHUNK = 256
FFN_ROW_CHUNK = 512
BF16_SUBLANES = 16
F32_SUBLANES = 8
LANES = 128

V7X_VMEM_BYTES = 64 * 1024 * 1024
VMEM_CAP_BYTES = V7X_VMEM_BYTES - 8 * 1024 * 1024

F32 = jnp.float32
BF16 = jnp.bfloat16


def _params(semantics, vmem_bytes):
    limit = min(int(vmem_bytes * 1.25) + (4 << 20), VMEM_CAP_BYTES)
    return pltpu.CompilerParams(dimension_semantics=semantics, vmem_limit_bytes=limit)


def _rms(x, g):
    return x * lax.rsqrt(jnp.mean(x * x, axis=-1, keepdims=True) + EPS) * g


def _tile_then_next(n_tiles):
    return lambda i, j: (jnp.minimum(i + jnp.minimum(j, 1), n_tiles - 1), 0)


def _norm_matmul_kernel(x_ref, g_ref, w_ref, cs_ref, o_ref, h_ref):
    def body(with_norm):
        for c in range(x_ref.shape[0] // ROW_CHUNK):
            rows = slice(c * ROW_CHUNK, (c + 1) * ROW_CHUNK)
            if with_norm:
                h_ref[rows, :] = _rms(x_ref[rows, :], g_ref[...]).astype(BF16)
            acc = jnp.dot(h_ref[rows, :], w_ref[...], preferred_element_type=F32)
            o_ref[rows, :] = (acc * cs_ref[...]).astype(o_ref.dtype)

    pl.when(pl.program_id(1) == 0)(functools.partial(body, True))
    pl.when(pl.program_id(1) != 0)(functools.partial(body, False))


def norm_matmul(x, g, w, col_scale, *, tm, tn):
    m, k = x.shape
    n = w.shape[1]
    vmem = 2 * tm * k * 4 + tm * k * 2 + 2 * k * tn * 2 + 2 * tm * tn * 2 + tm * tn * 4
    return pl.pallas_call(
        _norm_matmul_kernel,
        grid=(m // tm, n // tn),
        in_specs=[
            pl.BlockSpec((tm, k), _tile_then_next(m // tm)),
            pl.BlockSpec((1, k), lambda i, j: (0, 0)),
            pl.BlockSpec((k, tn), lambda i, j: (0, j)),
            pl.BlockSpec((1, tn), lambda i, j: (0, j)),
        ],
        out_specs=pl.BlockSpec((tm, tn), lambda i, j: (i, j)),
        out_shape=jax.ShapeDtypeStruct((m, n), BF16),
        scratch_shapes=[pltpu.VMEM((tm, k), BF16)],
        compiler_params=_params(("parallel", "arbitrary"), vmem),
        name="norm_qkv_na",
    )(x, g.reshape(1, k), w, col_scale.reshape(1, n))


def _norm_matmul_rope_kernel(x_ref, g_ref, w_ref, rot_ref, *refs, dils):
    o_refs = refs[:len(dils)]
    h_ref, s_ref, t_ref = refs[len(dils):]
    j = pl.program_id(1)
    tm = x_ref.shape[0]
    heads = s_ref.shape[0]
    half = ROPE_DIM // 2

    def body(o_ref, dil, rotate, with_norm=False):
        n = ROW_CHUNK // dil
        d1, d2 = _stride_split(dil)
        n1 = ROW_CHUNK // d1
        scale = jnp.where(j % 3 == 0, Q_SCALE, 1.0).astype(F32)
        for c in range(tm // ROW_CHUNK):
            rows = slice(c * ROW_CHUNK, (c + 1) * ROW_CHUNK)
            if with_norm:
                h_ref[rows, :] = _rms(x_ref[rows, :], g_ref[...]).astype(BF16)
            acc = jnp.dot(h_ref[rows, :], w_ref[...], preferred_element_type=F32)
            if rotate:
                cos = rot_ref[0, rows, :] * scale
                sin_hi = rot_ref[1, rows, :] * scale
                sin_lo = rot_ref[2, rows, :] * scale
            for h in range(heads):
                cols = slice(h * HEAD_DIM, (h + 1) * HEAD_DIM)
                a = acc[:, cols]
                if rotate:
                    a = (a * cos + pltpu.roll(a, half, 1) * sin_hi
                         + pltpu.roll(a, HEAD_DIM - half, 1) * sin_lo)
                if dil == 1:
                    o_ref[0, rows, cols] = a.astype(BF16)
                    continue
                s_ref[h, rows, :] = a
                if d2 == 1:
                    for r in range(dil):
                        o_ref[r, c * n:(c + 1) * n, cols] = (
                            s_ref[h, pl.ds(c * ROW_CHUNK + r, n, stride=dil), :].astype(BF16))
                    continue
                for r1 in range(d1):
                    t_ref[h, r1 * n1:(r1 + 1) * n1, :] = s_ref[h, pl.ds(c * ROW_CHUNK + r1, n1, stride=d1), :]
                for r1 in range(d1):
                    for r2 in range(d2):
                        o_ref[r1 + d1 * r2, c * n:(c + 1) * n, cols] = (
                            t_ref[h, pl.ds(r1 * n1 + r2, n, stride=d2), :].astype(BF16))

    pl.when(j == 0)(functools.partial(body, o_refs[0], dils[0], True, with_norm=True))
    for g, dil in enumerate(dils):
        pl.when((j // 3 == g) & (j % 3 != 2) & (j != 0))(functools.partial(body, o_refs[g], dil, True))
        pl.when((j // 3 == g) & (j % 3 == 2))(functools.partial(body, o_refs[g], dil, False))


def _stride_split(dil):
    if dil <= F32_SUBLANES:
        return dil, 1
    d1 = F32_SUBLANES // 2
    assert dil % d1 == 0 and dil // d1 <= F32_SUBLANES
    return d1, dil // d1


def _rope_tables(t):
    half = ROPE_DIM // 2
    pos = jnp.arange(t, dtype=F32)
    inv_freq = ROPE_THETA ** (-jnp.arange(0, ROPE_DIM, 2, dtype=F32) / ROPE_DIM)
    ang = pos[:, None] * inv_freq[None, :]
    cos = jnp.cos(ang)
    sin = jnp.sin(ang)
    rest = HEAD_DIM - ROPE_DIM
    one = jnp.ones((t, rest), F32)
    zero = jnp.zeros((t, rest), F32)
    zh = jnp.zeros((t, half), F32)
    return jnp.stack([jnp.concatenate([cos, cos, one], axis=-1),
                      jnp.concatenate([zh, sin, zero], axis=-1),
                      jnp.concatenate([-sin, zh, zero], axis=-1)], axis=0)


def norm_matmul_rope(x, g, w, rot, *, batch, seq, tm, dils):
    m, k = x.shape
    n = w.shape[1]
    tn = n // (3 * len(dils))
    heads = tn // HEAD_DIM
    tpb = seq // tm
    assert all(ROW_CHUNK % (BF16_SUBLANES * dil) == 0 for dil in dils) and tm % ROW_CHUNK == 0

    def out_map(g):
        return lambda i, j: (jnp.clip(j - 3 * g, 0, 2), i // tpb, 0, i % tpb, 0)

    vmem = (2 * tm * k * 4 + tm * k * 2 + 2 * k * tn * 2 + 2 * len(dils) * tm * tn * 2 + tm * tn * 4
            + 2 * ROW_CHUNK * tn * 4 + 2 * 3 * tm * HEAD_DIM * 4 + heads * ROW_CHUNK * HEAD_DIM * 4)
    return pl.pallas_call(
        functools.partial(_norm_matmul_rope_kernel, dils=dils),
        grid=(m // tm, n // tn),
        in_specs=[
            pl.BlockSpec((tm, k), _tile_then_next(m // tm)),
            pl.BlockSpec((1, k), lambda i, j: (0, 0)),
            pl.BlockSpec((k, tn), lambda i, j: (0, j)),
            pl.BlockSpec((3, tm, HEAD_DIM), lambda i, j: (0, i % tpb, 0)),
        ],
        out_specs=[pl.BlockSpec((None, None, dil, tm // dil, tn), out_map(g)) for g, dil in enumerate(dils)],
        out_shape=[jax.ShapeDtypeStruct((3, batch, dil, seq // dil, tn), BF16) for dil in dils],
        scratch_shapes=[pltpu.VMEM((tm, k), BF16), pltpu.VMEM((heads, tm, HEAD_DIM), F32),
                        pltpu.VMEM((heads, ROW_CHUNK, HEAD_DIM), F32)],
        compiler_params=_params(("arbitrary", "arbitrary"), vmem),
        name="norm_qkv_dil_rope",
    )(x, g.reshape(1, k), w, rot)


def _na_bias_table(rpb):
    nh, n_row_off, n_col_off = rpb.shape
    assert 2 * GRID_W == LANES and NA_KEY_ROWS % 2 == 0 and n_col_off <= LANES
    rpb_lanes = jnp.pad(rpb, ((0, 0), (0, 0), (0, LANES - n_col_off)))
    return pl.pallas_call(
        _na_bias_kernel,
        grid=(nh,),
        in_specs=[pl.BlockSpec((None, n_row_off, LANES), lambda h: (h, 0, 0))],
        out_specs=pl.BlockSpec((None, NA_BLOCK_ROWS * GRID_W, NA_KEY_ROWS * GRID_W), lambda h: (h, 0, 0)),
        out_shape=jax.ShapeDtypeStruct((nh, NA_BLOCK_ROWS * GRID_W, NA_KEY_ROWS * GRID_W), F32),
        compiler_params=_params(("parallel",), 4 * NA_BLOCK_ROWS * GRID_W * NA_KEY_ROWS * GRID_W * 4),
        name="na_bias_table",
    )(rpb_lanes)


def _na_bias_kernel(rpb_ref, o_ref):
    w = GRID_W
    qc = lax.broadcasted_iota(jnp.int32, (w, LANES), 0)
    kc = lax.broadcasted_iota(jnp.int32, (w, LANES), 1)
    start = jnp.clip(qc - NA_WIN_COLS // 2, 0, w - NA_WIN_COLS)
    in_window = (kc >= start) & (kc < start + NA_WIN_COLS)
    by_col = []
    for a in range(rpb_ref.shape[0]):
        row = jnp.broadcast_to(rpb_ref[a:a + 1, :], (w, LANES))
        shifted = pltpu.roll(row, LANES - (NA_WIN_COLS - 1), 1, stride=1, stride_axis=0)
        by_col.append(jnp.where(in_window, shifted * LOG2E, NEG))
    for rr in range(NA_BLOCK_ROWS):
        for kk in range(0, NA_KEY_ROWS, 2):
            a = kk - NA_KEY_BACK - rr + NA_WIN_ROWS - 1
            assert 0 <= a and a + 1 < rpb_ref.shape[0]
            pair = jnp.where(kc < w, by_col[a], pltpu.roll(by_col[a + 1], w, 1))
            o_ref[rr * w:(rr + 1) * w, kk * w:(kk + 2) * w] = pair


def _na_attn_kernel(q_ref, *refs, heads, rows):
    k_refs = refs[:NA_KEY_CHUNKS]
    v_refs = refs[NA_KEY_CHUNKS:2 * NA_KEY_CHUNKS]
    b_ref, o_ref, bm_ref = refs[2 * NA_KEY_CHUNKS:]
    qb = q_ref.shape[0]
    ck = k_refs[0].shape[0]
    kh = min(NA_WIN_ROWS, rows)
    i = pl.program_id(2)

    def clipped(blk):
        r0 = blk * NA_BLOCK_ROWS
        return (r0 < kh // 2) | (r0 + NA_BLOCK_ROWS - 1 - kh // 2 > rows - kh)

    @pl.when((i == 0) | clipped(i) | clipped(i - 1))
    def _():
        r0 = i * NA_BLOCK_ROWS
        q_row = r0 + lax.broadcasted_iota(jnp.int32, (qb, ck), 0) // GRID_W
        lo = jnp.clip(q_row - kh // 2, 0, rows - kh)
        for c in range(NA_KEY_CHUNKS):
            k_row = (r0 - NA_KEY_BACK + c * (ck // GRID_W)
                     + lax.broadcasted_iota(jnp.int32, (qb, ck), 1) // GRID_W)
            row_mask = jnp.where((k_row >= lo) & (k_row < lo + kh), 0.0, NEG).astype(F32)
            for h in range(heads):
                bm_ref[h, :, c * ck:(c + 1) * ck] = b_ref[h, :, c * ck:(c + 1) * ck] + row_mask

    def scores(h):
        sl = slice(h * HEAD_DIM, (h + 1) * HEAD_DIM)
        q = q_ref[:, sl]
        s = []
        for c, k_ref in enumerate(k_refs):
            sc = lax.dot_general(q, k_ref[:, sl], (((1,), (1,)), ((), ())), preferred_element_type=F32)
            s.append(sc + bm_ref[h, :, c * ck:(c + 1) * ck])
        return s, jnp.max(functools.reduce(jnp.maximum, s), axis=-1, keepdims=True)

    def finish(h, s, m):
        sl = slice(h * HEAD_DIM, (h + 1) * HEAD_DIM)
        p = [jnp.exp2(sc - m) for sc in s]
        l = jnp.sum(functools.reduce(jnp.add, p), axis=-1, keepdims=True)
        o = functools.reduce(jnp.add, [
            jnp.dot(pc.astype(BF16), v_ref[:, sl], preferred_element_type=F32)
            for pc, v_ref in zip(p, v_refs)])
        o_ref[:, sl] = (o / l).astype(o_ref.dtype)

    cur = scores(0)
    for h in range(heads):
        nxt = scores(h + 1) if h + 1 < heads else None
        finish(h, *cur)
        cur = nxt


def na_attention(qkv, bias, *, batch, seq, heads_per_step):
    m = qkv.shape[0]
    nh = bias.shape[0]
    hg = heads_per_step
    wq = hg * HEAD_DIM
    qb = NA_BLOCK_ROWS * GRID_W
    kb = NA_KEY_ROWS * GRID_W // NA_KEY_CHUNKS
    assert qb % kb == 0 and (NA_KEY_BACK * GRID_W) % kb == 0
    nblk = seq // qb
    kpb = seq // kb
    ratio = qb // kb
    back = NA_KEY_BACK * GRID_W // kb
    ncol = nh // hg

    def k_map(c, sec):
        def index(g, b, i):
            return (b * kpb + jnp.clip(i * ratio - back + c, 0, kpb - 1), sec * ncol + g)
        return index

    in_specs = [pl.BlockSpec((qb, wq), lambda g, b, i: (b * nblk + i, g))]
    in_specs += [pl.BlockSpec((kb, wq), k_map(c, 1)) for c in range(NA_KEY_CHUNKS)]
    in_specs += [pl.BlockSpec((kb, wq), k_map(c, 2)) for c in range(NA_KEY_CHUNKS)]
    in_specs += [pl.BlockSpec((hg, qb, NA_KEY_CHUNKS * kb), lambda g, b, i: (g, 0, 0))]
    vmem = (2 * 2 * qb * wq * 2 + 2 * 2 * NA_KEY_CHUNKS * kb * wq * 2
            + 3 * hg * qb * NA_KEY_CHUNKS * kb * 4 + 6 * qb * NA_KEY_CHUNKS * kb * 4)
    return pl.pallas_call(
        functools.partial(_na_attn_kernel, heads=hg, rows=seq // GRID_W),
        grid=(ncol, batch, nblk),
        in_specs=in_specs,
        out_specs=pl.BlockSpec((qb, wq), lambda g, b, i: (b * nblk + i, g)),
        out_shape=jax.ShapeDtypeStruct((m, nh * HEAD_DIM), BF16),
        scratch_shapes=[pltpu.VMEM((hg, qb, NA_KEY_CHUNKS * kb), F32)],
        compiler_params=_params(("parallel", "parallel", "arbitrary"), vmem),
        name="na_attention",
    )(qkv, *([qkv] * (2 * NA_KEY_CHUNKS)), bias)


def _proj_residual_kernel(a_ref, w_ref, r_ref, o_ref):
    o_ref[...] = r_ref[...] + jnp.dot(a_ref[...], w_ref[...], preferred_element_type=F32)


def proj_residual(a, w, res, *, tm):
    m, k = a.shape
    n = w.shape[1]
    vmem = 2 * tm * k * 2 + 2 * k * n * 2 + 4 * tm * n * 4 + tm * n * 4
    return pl.pallas_call(
        _proj_residual_kernel,
        grid=(m // tm,),
        in_specs=[
            pl.BlockSpec((tm, k), lambda i: (i, 0)),
            pl.BlockSpec((k, n), lambda i: (0, 0)),
            pl.BlockSpec((tm, n), lambda i: (i, 0)),
        ],
        out_specs=pl.BlockSpec((tm, n), lambda i: (i, 0)),
        out_shape=jax.ShapeDtypeStruct((m, n), F32),
        compiler_params=_params(("parallel",), vmem),
        name="na_proj_residual",
    )(a, w, res)


def _ffn_kernel(x_ref, g_ref, w1_ref, w2_ref, gf_ref, o_ref, h_ref, *, final_norm):
    f = pl.program_id(1)

    def body(first):
        for c in range(x_ref.shape[0] // FFN_ROW_CHUNK):
            rows = slice(c * FFN_ROW_CHUNK, (c + 1) * FFN_ROW_CHUNK)
            if first:
                h_ref[rows, :] = _rms(x_ref[rows, :], g_ref[...]).astype(BF16)
            a = jnp.dot(h_ref[rows, :], w1_ref[...], preferred_element_type=F32)
            a = jnp.square(jnp.maximum(a, 0.0)).astype(BF16)
            y = jnp.dot(a, w2_ref[...], preferred_element_type=F32)
            if first:
                o_ref[rows, :] = x_ref[rows, :] + y
            else:
                o_ref[rows, :] += y

    pl.when(f == 0)(functools.partial(body, True))
    pl.when(f != 0)(functools.partial(body, False))

    if final_norm:
        @pl.when(f == pl.num_programs(1) - 1)
        def _():
            o_ref[...] = _rms(o_ref[...], gf_ref[...])


def ffn(x, g, w1, w2, g_final, *, tm, tf, final_norm):
    m, d = x.shape
    dff = w1.shape[1]
    assert tm % FFN_ROW_CHUNK == 0
    vmem = (4 * tm * d * 4 + tm * d * 2 + 2 * 2 * d * tf * 2
            + FFN_ROW_CHUNK * tf * 6 + FFN_ROW_CHUNK * d * 4)
    return pl.pallas_call(
        functools.partial(_ffn_kernel, final_norm=final_norm),
        grid=(m // tm, dff // tf),
        in_specs=[
            pl.BlockSpec((tm, d), _tile_then_next(m // tm)),
            pl.BlockSpec((1, d), lambda i, f: (0, 0)),
            pl.BlockSpec((d, tf), lambda i, f: (0, f)),
            pl.BlockSpec((tf, d), lambda i, f: (f, 0)),
            pl.BlockSpec((1, d), lambda i, f: (0, 0)),
        ],
        out_specs=pl.BlockSpec((tm, d), lambda i, f: (i, 0)),
        out_shape=jax.ShapeDtypeStruct((m, d), F32),
        scratch_shapes=[pltpu.VMEM((tm, d), BF16)],
        compiler_params=_params(("parallel", "arbitrary"), vmem),
        name="ffn_final" if final_norm else "ffn",
    )(x, g.reshape(1, d), w1, w2, g_final.reshape(1, d))


def _dil_attn_kernel(q_ref, kp_ref, kc_ref, kn_ref, vp_ref, vc_ref, vn_ref, o_ref, lse_ref,
                     *, heads, halo, length):
    qb = q_ref.shape[0]
    sub = BLOCK_Q
    nk = sub + 2 * halo
    m0 = pl.program_id(2) * qb
    qi = lax.broadcasted_iota(jnp.int32, (sub, nk), 0)
    kj = lax.broadcasted_iota(jnp.int32, (sub, nk), 1) - halo
    band = jnp.abs(kj - qi) <= halo
    lane = lax.broadcasted_iota(jnp.int32, (sub, HEAD_DIM), 1)

    def window(prev_ref, cur_ref, next_ref, lo, hi, cols):
        parts = [prev_ref[:, cols]] if lo < 0 else []
        parts.append(cur_ref[max(lo, 0):min(hi, qb), cols])
        if hi > qb:
            parts.append(next_ref[:, cols])
        return parts[0] if len(parts) == 1 else jnp.concatenate(parts, axis=0)

    def scores(u, h):
        rows = slice(u * sub, (u + 1) * sub)
        cols = slice(h * HEAD_DIM, (h + 1) * HEAD_DIM)
        k = window(kp_ref, kc_ref, kn_ref, u * sub - halo, (u + 1) * sub + halo, cols)
        s = lax.dot_general(q_ref[rows, cols], k, (((1,), (1,)), ((), ())), preferred_element_type=F32)
        s = s + masks[u]
        return s, jnp.max(s, axis=-1, keepdims=True)

    def finish(u, h, s, m):
        rows = slice(u * sub, (u + 1) * sub)
        cols = slice(h * HEAD_DIM, (h + 1) * HEAD_DIM)
        v = window(vp_ref, vc_ref, vn_ref, u * sub - halo, (u + 1) * sub + halo, cols)
        p = jnp.exp2(s - m)
        l = jnp.sum(p, axis=-1, keepdims=True)
        o = jnp.dot(p.astype(BF16), v, preferred_element_type=F32)
        o_ref[rows, cols] = (o / l).astype(o_ref.dtype)
        return m * LN2 + jnp.log(l)

    masks = []
    for u in range(qb // sub):
        kpos = kj + (m0 + u * sub)
        masks.append(jnp.where(band & (kpos >= 0) & (kpos < length), 0.0, NEG).astype(F32))

    units = [(u, h) for u in range(qb // sub) for h in range(heads)]
    cur = scores(*units[0])
    lse_tile = None
    for n, (u, h) in enumerate(units):
        nxt = scores(*units[n + 1]) if n + 1 < len(units) else None
        lse = finish(u, h, *cur)
        lse_tile = jnp.where(lane == h, lse, jnp.zeros((sub, HEAD_DIM), F32) if h == 0 else lse_tile)
        if h == heads - 1:
            lse_ref[u * sub:(u + 1) * sub, :] = lse_tile
        cur = nxt


def dil_attention(qkv, *, group, halo):
    _, batch, dil, length, wq = qkv.shape
    heads = wq // HEAD_DIM
    qb = min(DIL_STEP_ROWS, length)
    assert BLOCK_Q % halo == 0 and qb % BLOCK_Q == 0 and length % qb == 0
    nblk = length // qb
    hpb = length // halo
    per = qb // halo

    def cur_map(sec):
        return lambda b, r, i: (sec, b, r, i, 0)

    def prev_map(sec):
        return lambda b, r, i: (sec, b, r, jnp.maximum(i * per - 1, 0), 0)

    def next_map(sec):
        return lambda b, r, i: (sec, b, r, jnp.minimum((i + 1) * per, hpb - 1), 0)

    def spec(rows, index_map):
        return pl.BlockSpec((None, None, None, rows, wq), index_map)

    in_specs = [spec(qb, cur_map(0))]
    for sec in (1, 2):
        in_specs += [spec(halo, prev_map(sec)), spec(qb, cur_map(sec)), spec(halo, next_map(sec))]
    vmem = (2 * (2 * qb * wq * 2 + 2 * (qb + 2 * halo) * wq * 2 + qb * HEAD_DIM * 4)
            + 8 * BLOCK_Q * (BLOCK_Q + 2 * halo) * 4)
    return pl.pallas_call(
        functools.partial(_dil_attn_kernel, heads=heads, halo=halo, length=length),
        grid=(batch, dil, nblk),
        in_specs=in_specs,
        out_specs=[pl.BlockSpec((None, None, qb, wq), lambda b, r, i: (b, r, i, 0)),
                   pl.BlockSpec((None, None, qb, HEAD_DIM), lambda b, r, i: (b, r, i, 0))],
        out_shape=[jax.ShapeDtypeStruct((batch, dil, length, wq), BF16),
                   jax.ShapeDtypeStruct((batch, dil, length, HEAD_DIM), F32)],
        compiler_params=_params(("parallel", "parallel", "arbitrary"), vmem),
        name=f"dil_attention_g{group}",
    )(*([qkv] * 7))


def _merge_proj_kernel(*refs, dils, heads):
    g_n = len(dils)
    o_refs = refs[:g_n]
    lse_refs = refs[g_n:2 * g_n]
    w_ref, r_ref, out_ref, a_ref, lt_ref, ot_ref, tt_ref = refs[2 * g_n:]
    tm = a_ref.shape[0]

    def merged(c):
        rows = slice(c * ROW_CHUNK, (c + 1) * ROW_CHUNK)
        lses = []
        for g, dil in enumerate(dils):
            if dil == 1:
                lses.append(lse_refs[g][0, rows, :])
                continue
            n = ROW_CHUNK // dil
            d1, d2 = _stride_split(dil)
            n1 = ROW_CHUNK // d1

            def to_token_order(dst_ref, dst_idx, src):
                if d2 == 1:
                    for r in range(dil):
                        dst_ref[dst_idx, pl.ds(c * ROW_CHUNK + r, n, stride=dil), :] = src(r)
                    return
                for r1 in range(d1):
                    for r2 in range(d2):
                        tt_ref[pl.ds(r1 * n1 + r2, n, stride=d2), :] = src(r1 + d1 * r2)
                for r1 in range(d1):
                    dst_ref[dst_idx, pl.ds(c * ROW_CHUNK + r1, n1, stride=d1), :] = tt_ref[r1 * n1:(r1 + 1) * n1, :]

            to_token_order(lt_ref, g, lambda r: lse_refs[g][r, c * n:(c + 1) * n, :])
            for h in range(heads):
                to_token_order(ot_ref, g * heads + h, lambda r: o_refs[g][
                    r, c * n:(c + 1) * n, h * HEAD_DIM:(h + 1) * HEAD_DIM].astype(F32))
            lses.append(lt_ref[g, rows, :])

        mx = functools.reduce(jnp.maximum, lses)
        es = [jnp.exp(l - mx) for l in lses]
        den = functools.reduce(jnp.add, es)
        wts = [e / den for e in es]
        for h in range(heads):
            sl = slice(h * HEAD_DIM, (h + 1) * HEAD_DIM)
            acc = None
            for g, dil in enumerate(dils):
                o = o_refs[g][0, rows, sl].astype(F32) if dil == 1 else ot_ref[g * heads + h, rows, :]
                term = wts[g][:, h:h + 1] * o
                acc = term if acc is None else acc + term
            a_ref[rows, sl] = acc.astype(BF16)

    merged(0)
    for c in range(tm // ROW_CHUNK):
        rows = slice(c * ROW_CHUNK, (c + 1) * ROW_CHUNK)
        if c + 1 < tm // ROW_CHUNK:
            merged(c + 1)
        out_ref[rows, :] = r_ref[rows, :] + jnp.dot(a_ref[rows, :], w_ref[...], preferred_element_type=F32)


def merge_proj_residual(outs, lses, w, res, *, seq, tm):
    dils = tuple(o.shape[1] for o in outs)
    k = outs[0].shape[-1]
    m, n = res.shape
    g_n = len(outs)
    heads = k // HEAD_DIM
    tpb = seq // tm
    assert all(ROW_CHUNK % (BF16_SUBLANES * dil) == 0 for dil in dils) and tm % ROW_CHUNK == 0

    def stream_spec(dil, width):
        return pl.BlockSpec((None, dil, tm // dil, width), lambda i: (i // tpb, 0, i % tpb, 0))

    vmem = (2 * g_n * tm * k * 2 + 2 * g_n * tm * HEAD_DIM * 4 + 2 * k * n * 2 + 4 * tm * n * 4
            + tm * k * 2 + tm * n * 4 + g_n * tm * HEAD_DIM * 4 + g_n * tm * k * 4)
    return pl.pallas_call(
        functools.partial(_merge_proj_kernel, dils=dils, heads=heads),
        grid=(m // tm,),
        in_specs=([stream_spec(dil, k) for dil in dils]
                  + [stream_spec(dil, HEAD_DIM) for dil in dils]
                  + [pl.BlockSpec((k, n), lambda i: (0, 0)),
                     pl.BlockSpec((tm, n), lambda i: (i, 0))]),
        out_specs=pl.BlockSpec((tm, n), lambda i: (i, 0)),
        out_shape=jax.ShapeDtypeStruct((m, n), F32),
        scratch_shapes=[pltpu.VMEM((tm, k), BF16),
                        pltpu.VMEM((g_n, tm, HEAD_DIM), F32),
                        pltpu.VMEM((g_n * heads, tm, HEAD_DIM), F32),
                        pltpu.VMEM((ROW_CHUNK, HEAD_DIM), F32)],
        compiler_params=_params(("parallel",), vmem),
        name="dil_merge_proj_residual",
    )(*outs, *lses, w, res)


def kernel(x, na_norm, na_wqkv, na_rpb, na_wo, ffn0_norm, ffn0_w1, ffn0_w2, dil_norm, dil_wqkv, dil_wo,
           ffn1_norm, ffn1_w1, ffn1_w2, final_norm):
    batch, seq, d = x.shape
    m = batch * seq
    rows = seq // GRID_W
    na_heads = na_rpb.shape[0]
    assert rows >= NA_KEY_ROWS and rows % NA_BLOCK_ROWS == 0 and rows >= NA_WIN_ROWS
    assert NA_KEY_ROWS >= NA_BLOCK_ROWS + NA_WIN_ROWS - 1

    xf = x.reshape(m, d)
    bf = lambda w: w.astype(BF16)

    qw = na_heads * HEAD_DIM
    col_scale = jnp.concatenate([jnp.full((qw,), Q_SCALE, F32), jnp.ones((2 * qw,), F32)])
    qkv = norm_matmul(xf, na_norm, bf(na_wqkv), col_scale, tm=min(1024, m), tn=min(1024, qw))
    bias = _na_bias_table(na_rpb)
    attn = na_attention(qkv, bias, batch=batch, seq=seq, heads_per_step=min(8, na_heads))
    xf = proj_residual(attn, bf(na_wo), xf, tm=min(512, m))
    xf = ffn(xf, ffn0_norm, bf(ffn0_w1), bf(ffn0_w2), final_norm, tm=min(1024, m), tf=512, final_norm=False)

    rot = _rope_tables(seq)
    dils = tuple(dil for _, dil in DIL_GROUPS)
    qkvs = norm_matmul_rope(xf, dil_norm, bf(dil_wqkv), rot, batch=batch, seq=seq, tm=min(1024, seq), dils=dils)
    outs, lses = [], []
    for gi, (window, dil) in enumerate(DIL_GROUPS):
        o, lse = dil_attention(qkvs[gi], group=gi, halo=(window // 2) // dil)
        outs.append(o)
        lses.append(lse)
    xf = merge_proj_residual(outs, lses, bf(dil_wo), xf, seq=seq, tm=min(512, seq))
    xf = ffn(xf, ffn1_norm, bf(ffn1_w1), bf(ffn1_w2), final_norm, tm=min(1024, m), tf=512, final_norm=True)
    return xf.reshape(batch, seq, d)
```

```python
import functools
import math

import numpy as np
import jax
import jax.numpy as jnp
from jax import lax
from jax.experimental import pallas as pl
from jax.experimental.pallas import tpu as pltpu

GRID_W = 64
HEAD_DIM = 128
NA_WIN_ROWS = 8
NA_WIN_COLS = 16
DIL_GROUPS = ((128, 1), (512, 4), (2048, 16))
ROPE_THETA = 500000.0
ROPE_DIM = HEAD_DIM // 4
BLOCK_Q = 128
EPS = 1e-6
NEG = -1e30

LOG2E = math.log2(math.e)
LN2 = math.log(2.0)
Q_SCALE = HEAD_DIM ** -0.5 * LOG2E

NA_BLOCK_ROWS = 4
NA_KEY_BACK = NA_WIN_ROWS // 2
NA_KEY_ROWS = 12
NA_KEY_CHUNKS = 3

DIL_STEP_ROWS = 512
ROW_CHUNK = 256
FFN_ROW_CHUNK = 512
BF16_SUBLANES = 16
F32_SUBLANES = 8
LANES = 128

V7X_VMEM_BYTES = 64 * 1024 * 1024
VMEM_CAP_BYTES = V7X_VMEM_BYTES - 8 * 1024 * 1024

F32 = jnp.float32
BF16 = jnp.bfloat16


def _params(semantics, vmem_bytes):
    limit = min(int(vmem_bytes * 1.25) + (4 << 20), VMEM_CAP_BYTES)
    return pltpu.CompilerParams(dimension_semantics=semantics, vmem_limit_bytes=limit)


def _rms(x, g):
    return x * lax.rsqrt(jnp.mean(x * x, axis=-1, keepdims=True) + EPS) * g


def _tile_then_next(n_tiles, n_inner):
    switch = max(n_inner // 2, 1)
    return lambda i, j: (jnp.minimum(i + jnp.where(j >= switch, 1, 0), n_tiles - 1), 0)


def _norm_matmul_kernel(x_ref, g_ref, w_ref, cs_ref, o_ref, h_ref):
    def body(with_norm):
        for c in range(x_ref.shape[0] // ROW_CHUNK):
            rows = slice(c * ROW_CHUNK, (c + 1) * ROW_CHUNK)
            if with_norm:
                h_ref[rows, :] = _rms(x_ref[rows, :], g_ref[...]).astype(BF16)
            acc = jnp.dot(h_ref[rows, :], w_ref[...], preferred_element_type=F32)
            o_ref[rows, :] = (acc * cs_ref[...]).astype(o_ref.dtype)

    pl.when(pl.program_id(1) == 0)(functools.partial(body, True))
    pl.when(pl.program_id(1) != 0)(functools.partial(body, False))


def norm_matmul(x, g, w, col_scale, *, tm, tn):
    m, k = x.shape
    n = w.shape[1]
    vmem = 2 * tm * k * 4 + tm * k * 2 + 2 * k * tn * 2 + 2 * tm * tn * 2 + tm * tn * 4
    return pl.pallas_call(
        _norm_matmul_kernel,
        grid=(m // tm, n // tn),
        in_specs=[
            pl.BlockSpec((tm, k), _tile_then_next(m // tm, n // tn)),
            pl.BlockSpec((1, k), lambda i, j: (0, 0)),
            pl.BlockSpec((k, tn), lambda i, j: (0, j)),
            pl.BlockSpec((1, tn), lambda i, j: (0, j)),
        ],
        out_specs=pl.BlockSpec((tm, tn), lambda i, j: (i, j)),
        out_shape=jax.ShapeDtypeStruct((m, n), BF16),
        scratch_shapes=[pltpu.VMEM((tm, k), BF16)],
        compiler_params=_params(("parallel", "arbitrary"), vmem),
        name="norm_qkv_na",
    )(x, g.reshape(1, k), w, col_scale.reshape(1, n))


def _norm_matmul_rope_kernel(x_ref, g_ref, w_ref, rot_ref, *refs, dils):
    o_refs = refs[:len(dils)]
    h_ref, s_ref, t_ref = refs[len(dils):]
    j = pl.program_id(1)
    tm = x_ref.shape[0]
    heads = s_ref.shape[0]
    half = ROPE_DIM // 2

    def body(o_ref, dil, rotate, with_norm=False):
        n = ROW_CHUNK // dil
        d1, d2 = _stride_split(dil)
        n1 = ROW_CHUNK // d1
        scale = jnp.where(j % 3 == 0, Q_SCALE, 1.0).astype(F32)
        for c in range(tm // ROW_CHUNK):
            rows = slice(c * ROW_CHUNK, (c + 1) * ROW_CHUNK)
            if with_norm:
                h_ref[rows, :] = _rms(x_ref[rows, :], g_ref[...]).astype(BF16)
            acc = jnp.dot(h_ref[rows, :], w_ref[...], preferred_element_type=F32)
            if rotate:
                cos = rot_ref[0, rows, :] * scale
                sin_hi = rot_ref[1, rows, :] * scale
                sin_lo = rot_ref[2, rows, :] * scale
            for h in range(heads):
                cols = slice(h * HEAD_DIM, (h + 1) * HEAD_DIM)
                a = acc[:, cols]
                if rotate:
                    a = (a * cos + pltpu.roll(a, half, 1) * sin_hi
                         + pltpu.roll(a, HEAD_DIM - half, 1) * sin_lo)
                if dil == 1:
                    o_ref[0, rows, cols] = a.astype(BF16)
                    continue
                s_ref[h, rows, :] = a
                if d2 == 1:
                    for r in range(dil):
                        o_ref[r, c * n:(c + 1) * n, cols] = (
                            s_ref[h, pl.ds(c * ROW_CHUNK + r, n, stride=dil), :].astype(BF16))
                    continue
                for r1 in range(d1):
                    t_ref[h, r1 * n1:(r1 + 1) * n1, :] = s_ref[h, pl.ds(c * ROW_CHUNK + r1, n1, stride=d1), :]
                for r1 in range(d1):
                    for r2 in range(d2):
                        o_ref[r1 + d1 * r2, c * n:(c + 1) * n, cols] = (
                            t_ref[h, pl.ds(r1 * n1 + r2, n, stride=d2), :].astype(BF16))

    pl.when(j == 0)(functools.partial(body, o_refs[0], dils[0], True, with_norm=True))
    for g, dil in enumerate(dils):
        pl.when((j // 3 == g) & (j % 3 != 2) & (j != 0))(functools.partial(body, o_refs[g], dil, True))
        pl.when((j // 3 == g) & (j % 3 == 2))(functools.partial(body, o_refs[g], dil, False))


def _stride_split(dil):
    if dil <= F32_SUBLANES:
        return dil, 1
    d1 = F32_SUBLANES // 2
    assert dil % d1 == 0 and dil // d1 <= F32_SUBLANES
    return d1, dil // d1


def _rope_tables(t):
    half = ROPE_DIM // 2
    pos = jnp.arange(t, dtype=F32)
    inv_freq = ROPE_THETA ** (-jnp.arange(0, ROPE_DIM, 2, dtype=F32) / ROPE_DIM)
    ang = pos[:, None] * inv_freq[None, :]
    cos = jnp.cos(ang)
    sin = jnp.sin(ang)
    rest = HEAD_DIM - ROPE_DIM
    one = jnp.ones((t, rest), F32)
    zero = jnp.zeros((t, rest), F32)
    zh = jnp.zeros((t, half), F32)
    return jnp.stack([jnp.concatenate([cos, cos, one], axis=-1),
                      jnp.concatenate([zh, sin, zero], axis=-1),
                      jnp.concatenate([-sin, zh, zero], axis=-1)], axis=0)


def norm_matmul_rope(x, g, w, rot, *, batch, seq, tm, dils):
    m, k = x.shape
    n = w.shape[1]
    tn = n // (3 * len(dils))
    heads = tn // HEAD_DIM
    tpb = seq // tm
    assert all(ROW_CHUNK % (BF16_SUBLANES * dil) == 0 for dil in dils) and tm % ROW_CHUNK == 0

    def out_map(g):
        return lambda i, j: (jnp.clip(j - 3 * g, 0, 2), i // tpb, 0, i % tpb, 0)

    vmem = (2 * tm * k * 4 + tm * k * 2 + 2 * k * tn * 2 + 2 * len(dils) * tm * tn * 2 + tm * tn * 4
            + 2 * ROW_CHUNK * tn * 4 + 2 * 3 * tm * HEAD_DIM * 4 + heads * ROW_CHUNK * HEAD_DIM * 4)
    return pl.pallas_call(
        functools.partial(_norm_matmul_rope_kernel, dils=dils),
        grid=(m // tm, n // tn),
        in_specs=[
            pl.BlockSpec((tm, k), _tile_then_next(m // tm, n // tn)),
            pl.BlockSpec((1, k), lambda i, j: (0, 0)),
            pl.BlockSpec((k, tn), lambda i, j: (0, j)),
            pl.BlockSpec((3, tm, HEAD_DIM), lambda i, j: (0, i % tpb, 0)),
        ],
        out_specs=[pl.BlockSpec((None, None, dil, tm // dil, tn), out_map(g)) for g, dil in enumerate(dils)],
        out_shape=[jax.ShapeDtypeStruct((3, batch, dil, seq // dil, tn), BF16) for dil in dils],
        scratch_shapes=[pltpu.VMEM((tm, k), BF16), pltpu.VMEM((heads, tm, HEAD_DIM), F32),
                        pltpu.VMEM((heads, ROW_CHUNK, HEAD_DIM), F32)],
        compiler_params=_params(("arbitrary", "arbitrary"), vmem),
        name="norm_qkv_dil_rope",
    )(x, g.reshape(1, k), w, rot)


def _na_bias_table(rpb):
    nh, n_row_off, n_col_off = rpb.shape
    assert 2 * GRID_W == LANES and NA_KEY_ROWS % 2 == 0 and n_col_off <= LANES
    rpb_lanes = jnp.pad(rpb, ((0, 0), (0, 0), (0, LANES - n_col_off)))
    return pl.pallas_call(
        _na_bias_kernel,
        grid=(nh,),
        in_specs=[pl.BlockSpec((None, n_row_off, LANES), lambda h: (h, 0, 0))],
        out_specs=pl.BlockSpec((None, NA_BLOCK_ROWS * GRID_W, NA_KEY_ROWS * GRID_W), lambda h: (h, 0, 0)),
        out_shape=jax.ShapeDtypeStruct((nh, NA_BLOCK_ROWS * GRID_W, NA_KEY_ROWS * GRID_W), F32),
        compiler_params=_params(("parallel",), 4 * NA_BLOCK_ROWS * GRID_W * NA_KEY_ROWS * GRID_W * 4),
        name="na_bias_table",
    )(rpb_lanes)


def _na_bias_kernel(rpb_ref, o_ref):
    w = GRID_W
    qc = lax.broadcasted_iota(jnp.int32, (w, LANES), 0)
    kc = lax.broadcasted_iota(jnp.int32, (w, LANES), 1)
    start = jnp.clip(qc - NA_WIN_COLS // 2, 0, w - NA_WIN_COLS)
    in_window = (kc >= start) & (kc < start + NA_WIN_COLS)
    by_col = []
    for a in range(rpb_ref.shape[0]):
        row = jnp.broadcast_to(rpb_ref[a:a + 1, :], (w, LANES))
        shifted = pltpu.roll(row, LANES - (NA_WIN_COLS - 1), 1, stride=1, stride_axis=0)
        by_col.append(jnp.where(in_window, shifted * LOG2E, NEG))
    for rr in range(NA_BLOCK_ROWS):
        for kk in range(0, NA_KEY_ROWS, 2):
            a = kk - NA_KEY_BACK - rr + NA_WIN_ROWS - 1
            assert 0 <= a and a + 1 < rpb_ref.shape[0]
            pair = jnp.where(kc < w, by_col[a], pltpu.roll(by_col[a + 1], w, 1))
            o_ref[rr * w:(rr + 1) * w, kk * w:(kk + 2) * w] = pair


def _na_attn_kernel(q_ref, *refs, heads, rows):
    k_refs = refs[:NA_KEY_CHUNKS]
    v_refs = refs[NA_KEY_CHUNKS:2 * NA_KEY_CHUNKS]
    b_ref, o_ref, bm_ref = refs[2 * NA_KEY_CHUNKS:]
    qb = q_ref.shape[0]
    ck = k_refs[0].shape[0]
    kh = min(NA_WIN_ROWS, rows)
    i = pl.program_id(2)

    def clipped(blk):
        r0 = blk * NA_BLOCK_ROWS
        return (r0 < kh // 2) | (r0 + NA_BLOCK_ROWS - 1 - kh // 2 > rows - kh)

    @pl.when((i == 0) | clipped(i) | clipped(i - 1))
    def _():
        r0 = i * NA_BLOCK_ROWS
        q_row = r0 + lax.broadcasted_iota(jnp.int32, (qb, ck), 0) // GRID_W
        lo = jnp.clip(q_row - kh // 2, 0, rows - kh)
        for c in range(NA_KEY_CHUNKS):
            k_row = (r0 - NA_KEY_BACK + c * (ck // GRID_W)
                     + lax.broadcasted_iota(jnp.int32, (qb, ck), 1) // GRID_W)
            row_mask = jnp.where((k_row >= lo) & (k_row < lo + kh), 0.0, NEG).astype(F32)
            for h in range(heads):
                bm_ref[h, :, c * ck:(c + 1) * ck] = b_ref[h, :, c * ck:(c + 1) * ck] + row_mask

    def scores(h):
        sl = slice(h * HEAD_DIM, (h + 1) * HEAD_DIM)
        q = q_ref[:, sl]
        s = []
        for c, k_ref in enumerate(k_refs):
            sc = lax.dot_general(q, k_ref[:, sl], (((1,), (1,)), ((), ())), preferred_element_type=F32)
            s.append(sc + bm_ref[h, :, c * ck:(c + 1) * ck])
        return s, jnp.max(functools.reduce(jnp.maximum, s), axis=-1, keepdims=True)

    def finish(h, s, m):
        sl = slice(h * HEAD_DIM, (h + 1) * HEAD_DIM)
        p = [jnp.exp2(sc - m) for sc in s]
        l = jnp.sum(functools.reduce(jnp.add, p), axis=-1, keepdims=True)
        o = functools.reduce(jnp.add, [
            jnp.dot(pc.astype(BF16), v_ref[:, sl], preferred_element_type=F32)
            for pc, v_ref in zip(p, v_refs)])
        o_ref[:, sl] = (o / l).astype(o_ref.dtype)

    cur = scores(0)
    for h in range(heads):
        nxt = scores(h + 1) if h + 1 < heads else None
        finish(h, *cur)
        cur = nxt


def na_attention(qkv, bias, *, batch, seq, heads_per_step):
    m = qkv.shape[0]
    nh = bias.shape[0]
    hg = heads_per_step
    wq = hg * HEAD_DIM
    qb = NA_BLOCK_ROWS * GRID_W
    kb = NA_KEY_ROWS * GRID_W // NA_KEY_CHUNKS
    assert qb % kb == 0 and (NA_KEY_BACK * GRID_W) % kb == 0
    nblk = seq // qb
    kpb = seq // kb
    ratio = qb // kb
    back = NA_KEY_BACK * GRID_W // kb
    ncol = nh // hg

    def k_map(c, sec):
        def index(g, b, i):
            return (b * kpb + jnp.clip(i * ratio - back + c, 0, kpb - 1), sec * ncol + g)
        return index

    in_specs = [pl.BlockSpec((qb, wq), lambda g, b, i: (b * nblk + i, g))]
    in_specs += [pl.BlockSpec((kb, wq), k_map(c, 1)) for c in range(NA_KEY_CHUNKS)]
    in_specs += [pl.BlockSpec((kb, wq), k_map(c, 2)) for c in range(NA_KEY_CHUNKS)]
    in_specs += [pl.BlockSpec((hg, qb, NA_KEY_CHUNKS * kb), lambda g, b, i: (g, 0, 0))]
    vmem = (2 * 2 * qb * wq * 2 + 2 * 2 * NA_KEY_CHUNKS * kb * wq * 2
            + 3 * hg * qb * NA_KEY_CHUNKS * kb * 4 + 6 * qb * NA_KEY_CHUNKS * kb * 4)
    return pl.pallas_call(
        functools.partial(_na_attn_kernel, heads=hg, rows=seq // GRID_W),
        grid=(ncol, batch, nblk),
        in_specs=in_specs,
        out_specs=pl.BlockSpec((qb, wq), lambda g, b, i: (b * nblk + i, g)),
        out_shape=jax.ShapeDtypeStruct((m, nh * HEAD_DIM), BF16),
        scratch_shapes=[pltpu.VMEM((hg, qb, NA_KEY_CHUNKS * kb), F32)],
        compiler_params=_params(("parallel", "parallel", "arbitrary"), vmem),
        name="na_attention",
    )(qkv, *([qkv] * (2 * NA_KEY_CHUNKS)), bias)


def _proj_residual_kernel(a_ref, w_ref, r_ref, o_ref):
    o_ref[...] = r_ref[...] + jnp.dot(a_ref[...], w_ref[...], preferred_element_type=F32)


def proj_residual(a, w, res, *, tm):
    m, k = a.shape
    n = w.shape[1]
    vmem = 2 * tm * k * 2 + 2 * k * n * 2 + 4 * tm * n * 4 + tm * n * 4
    return pl.pallas_call(
        _proj_residual_kernel,
        grid=(m // tm,),
        in_specs=[
            pl.BlockSpec((tm, k), lambda i: (i, 0)),
            pl.BlockSpec((k, n), lambda i: (0, 0)),
            pl.BlockSpec((tm, n), lambda i: (i, 0)),
        ],
        out_specs=pl.BlockSpec((tm, n), lambda i: (i, 0)),
        out_shape=jax.ShapeDtypeStruct((m, n), F32),
        compiler_params=_params(("parallel",), vmem),
        name="na_proj_residual",
    )(a, w, res)


def _ffn_kernel(x_ref, g_ref, w1_ref, w2_ref, gf_ref, o_ref, h_ref, *, final_norm):
    f = pl.program_id(1)

    def body(first):
        for c in range(x_ref.shape[0] // FFN_ROW_CHUNK):
            rows = slice(c * FFN_ROW_CHUNK, (c + 1) * FFN_ROW_CHUNK)
            if first:
                h_ref[rows, :] = _rms(x_ref[rows, :], g_ref[...]).astype(BF16)
            a = jnp.dot(h_ref[rows, :], w1_ref[...], preferred_element_type=F32)
            a = jnp.square(jnp.maximum(a, 0.0)).astype(BF16)
            y = jnp.dot(a, w2_ref[...], preferred_element_type=F32)
            if first:
                o_ref[rows, :] = x_ref[rows, :] + y
            else:
                o_ref[rows, :] += y

    pl.when(f == 0)(functools.partial(body, True))
    pl.when(f != 0)(functools.partial(body, False))

    if final_norm:
        @pl.when(f == pl.num_programs(1) - 1)
        def _():
            o_ref[...] = _rms(o_ref[...], gf_ref[...])


def ffn(x, g, w1, w2, g_final, *, tm, tf, final_norm):
    m, d = x.shape
    dff = w1.shape[1]
    assert tm % FFN_ROW_CHUNK == 0
    vmem = (4 * tm * d * 4 + tm * d * 2 + 2 * 2 * d * tf * 2
            + FFN_ROW_CHUNK * tf * 6 + FFN_ROW_CHUNK * d * 4)
    return pl.pallas_call(
        functools.partial(_ffn_kernel, final_norm=final_norm),
        grid=(m // tm, dff // tf),
        in_specs=[
            pl.BlockSpec((tm, d), _tile_then_next(m // tm, dff // tf)),
            pl.BlockSpec((1, d), lambda i, f: (0, 0)),
            pl.BlockSpec((d, tf), lambda i, f: (0, f)),
            pl.BlockSpec((tf, d), lambda i, f: (f, 0)),
            pl.BlockSpec((1, d), lambda i, f: (0, 0)),
        ],
        out_specs=pl.BlockSpec((tm, d), lambda i, f: (i, 0)),
        out_shape=jax.ShapeDtypeStruct((m, d), F32),
        scratch_shapes=[pltpu.VMEM((tm, d), BF16)],
        compiler_params=_params(("parallel", "arbitrary"), vmem),
        name="ffn_final" if final_norm else "ffn",
    )(x, g.reshape(1, d), w1, w2, g_final.reshape(1, d))


def _dil_attn_kernel(q_ref, kp_ref, kc_ref, kn_ref, vp_ref, vc_ref, vn_ref, o_ref, lse_ref,
                     *, heads, halo, length):
    qb = q_ref.shape[0]
    sub = BLOCK_Q
    nk = sub + 2 * halo
    m0 = pl.program_id(2) * qb
    qi = lax.broadcasted_iota(jnp.int32, (sub, nk), 0)
    kj = lax.broadcasted_iota(jnp.int32, (sub, nk), 1) - halo
    band = jnp.abs(kj - qi) <= halo
    lane = lax.broadcasted_iota(jnp.int32, (sub, HEAD_DIM), 1)

    def window(prev_ref, cur_ref, next_ref, lo, hi, cols):
        parts = [prev_ref[:, cols]] if lo < 0 else []
        parts.append(cur_ref[max(lo, 0):min(hi, qb), cols])
        if hi > qb:
            parts.append(next_ref[:, cols])
        return parts[0] if len(parts) == 1 else jnp.concatenate(parts, axis=0)

    def scores(u, h):
        rows = slice(u * sub, (u + 1) * sub)
        cols = slice(h * HEAD_DIM, (h + 1) * HEAD_DIM)
        k = window(kp_ref, kc_ref, kn_ref, u * sub - halo, (u + 1) * sub + halo, cols)
        s = lax.dot_general(q_ref[rows, cols], k, (((1,), (1,)), ((), ())), preferred_element_type=F32)
        s = s + masks[u]
        return s, jnp.max(s, axis=-1, keepdims=True)

    def finish(u, h, s, m):
        rows = slice(u * sub, (u + 1) * sub)
        cols = slice(h * HEAD_DIM, (h + 1) * HEAD_DIM)
        v = window(vp_ref, vc_ref, vn_ref, u * sub - halo, (u + 1) * sub + halo, cols)
        p = jnp.exp2(s - m)
        l = jnp.sum(p, axis=-1, keepdims=True)
        o = jnp.dot(p.astype(BF16), v, preferred_element_type=F32)
        o_ref[rows, cols] = (o / l).astype(o_ref.dtype)
        return m * LN2 + jnp.log(l)

    masks = []
    for u in range(qb // sub):
        kpos = kj + (m0 + u * sub)
        masks.append(jnp.where(band & (kpos >= 0) & (kpos < length), 0.0, NEG).astype(F32))

    units = [(u, h) for u in range(qb // sub) for h in range(heads)]
    cur = scores(*units[0])
    lse_tile = None
    for n, (u, h) in enumerate(units):
        nxt = scores(*units[n + 1]) if n + 1 < len(units) else None
        lse = finish(u, h, *cur)
        lse_tile = jnp.where(lane == h, lse, jnp.zeros((sub, HEAD_DIM), F32) if h == 0 else lse_tile)
        if h == heads - 1:
            lse_ref[u * sub:(u + 1) * sub, :] = lse_tile
        cur = nxt


def dil_attention(qkv, *, group, halo):
    _, batch, dil, length, wq = qkv.shape
    heads = wq // HEAD_DIM
    qb = min(DIL_STEP_ROWS, length)
    assert BLOCK_Q % halo == 0 and qb % BLOCK_Q == 0 and length % qb == 0
    nblk = length // qb
    hpb = length // halo
    per = qb // halo

    def cur_map(sec):
        return lambda b, r, i: (sec, b, r, i, 0)

    def prev_map(sec):
        return lambda b, r, i: (sec, b, r, jnp.maximum(i * per - 1, 0), 0)

    def next_map(sec):
        return lambda b, r, i: (sec, b, r, jnp.minimum((i + 1) * per, hpb - 1), 0)

    def spec(rows, index_map):
        return pl.BlockSpec((None, None, None, rows, wq), index_map)

    in_specs = [spec(qb, cur_map(0))]
    for sec in (1, 2):
        in_specs += [spec(halo, prev_map(sec)), spec(qb, cur_map(sec)), spec(halo, next_map(sec))]
    vmem = (2 * (2 * qb * wq * 2 + 2 * (qb + 2 * halo) * wq * 2 + qb * HEAD_DIM * 4)
            + 8 * BLOCK_Q * (BLOCK_Q + 2 * halo) * 4)
    return pl.pallas_call(
        functools.partial(_dil_attn_kernel, heads=heads, halo=halo, length=length),
        grid=(batch, dil, nblk),
        in_specs=in_specs,
        out_specs=[pl.BlockSpec((None, None, qb, wq), lambda b, r, i: (b, r, i, 0)),
                   pl.BlockSpec((None, None, qb, HEAD_DIM), lambda b, r, i: (b, r, i, 0))],
        out_shape=[jax.ShapeDtypeStruct((batch, dil, length, wq), BF16),
                   jax.ShapeDtypeStruct((batch, dil, length, HEAD_DIM), F32)],
        compiler_params=_params(("parallel", "parallel", "arbitrary"), vmem),
        name=f"dil_attention_g{group}",
    )(*([qkv] * 7))


def _merge_proj_kernel(*refs, dils, heads):
    g_n = len(dils)
    o_refs = refs[:g_n]
    lse_refs = refs[g_n:2 * g_n]
    w_ref, r_ref, out_ref, a_ref, lt_ref, ot_ref, tt_ref = refs[2 * g_n:]
    tm = a_ref.shape[0]

    def merged(c):
        rows = slice(c * ROW_CHUNK, (c + 1) * ROW_CHUNK)
        lses = []
        for g, dil in enumerate(dils):
            if dil == 1:
                lses.append(lse_refs[g][0, rows, :])
                continue
            n = ROW_CHUNK // dil
            d1, d2 = _stride_split(dil)
            n1 = ROW_CHUNK // d1

            def to_token_order(dst_ref, dst_idx, src):
                if d2 == 1:
                    for r in range(dil):
                        dst_ref[dst_idx, pl.ds(c * ROW_CHUNK + r, n, stride=dil), :] = src(r)
                    return
                for r1 in range(d1):
                    for r2 in range(d2):
                        tt_ref[pl.ds(r1 * n1 + r2, n, stride=d2), :] = src(r1 + d1 * r2)
                for r1 in range(d1):
                    dst_ref[dst_idx, pl.ds(c * ROW_CHUNK + r1, n1, stride=d1), :] = tt_ref[r1 * n1:(r1 + 1) * n1, :]

            to_token_order(lt_ref, g, lambda r: lse_refs[g][r, c * n:(c + 1) * n, :])
            for h in range(heads):
                to_token_order(ot_ref, g * heads + h, lambda r: o_refs[g][
                    r, c * n:(c + 1) * n, h * HEAD_DIM:(h + 1) * HEAD_DIM].astype(F32))
            lses.append(lt_ref[g, rows, :])

        mx = functools.reduce(jnp.maximum, lses)
        es = [jnp.exp(l - mx) for l in lses]
        den = functools.reduce(jnp.add, es)
        wts = [e / den for e in es]
        for h in range(heads):
            sl = slice(h * HEAD_DIM, (h + 1) * HEAD_DIM)
            acc = None
            for g, dil in enumerate(dils):
                o = o_refs[g][0, rows, sl].astype(F32) if dil == 1 else ot_ref[g * heads + h, rows, :]
                term = wts[g][:, h:h + 1] * o
                acc = term if acc is None else acc + term
            a_ref[rows, sl] = acc.astype(BF16)

    merged(0)
    for c in range(tm // ROW_CHUNK):
        rows = slice(c * ROW_CHUNK, (c + 1) * ROW_CHUNK)
        if c + 1 < tm // ROW_CHUNK:
            merged(c + 1)
        out_ref[rows, :] = r_ref[rows, :] + jnp.dot(a_ref[rows, :], w_ref[...], preferred_element_type=F32)


def merge_proj_residual(outs, lses, w, res, *, seq, tm):
    dils = tuple(o.shape[1] for o in outs)
    k = outs[0].shape[-1]
    m, n = res.shape
    g_n = len(outs)
    heads = k // HEAD_DIM
    tpb = seq // tm
    assert all(ROW_CHUNK % (BF16_SUBLANES * dil) == 0 for dil in dils) and tm % ROW_CHUNK == 0

    def stream_spec(dil, width):
        return pl.BlockSpec((None, dil, tm // dil, width), lambda i: (i // tpb, 0, i % tpb, 0))

    vmem = (2 * g_n * tm * k * 2 + 2 * g_n * tm * HEAD_DIM * 4 + 2 * k * n * 2 + 4 * tm * n * 4
            + tm * k * 2 + tm * n * 4 + g_n * tm * HEAD_DIM * 4 + g_n * tm * k * 4)
    return pl.pallas_call(
        functools.partial(_merge_proj_kernel, dils=dils, heads=heads),
        grid=(m // tm,),
        in_specs=([stream_spec(dil, k) for dil in dils]
                  + [stream_spec(dil, HEAD_DIM) for dil in dils]
                  + [pl.BlockSpec((k, n), lambda i: (0, 0)),
                     pl.BlockSpec((tm, n), lambda i: (i, 0))]),
        out_specs=pl.BlockSpec((tm, n), lambda i: (i, 0)),
        out_shape=jax.ShapeDtypeStruct((m, n), F32),
        scratch_shapes=[pltpu.VMEM((tm, k), BF16),
                        pltpu.VMEM((g_n, tm, HEAD_DIM), F32),
                        pltpu.VMEM((g_n * heads, tm, HEAD_DIM), F32),
                        pltpu.VMEM((ROW_CHUNK, HEAD_DIM), F32)],
        compiler_params=_params(("parallel",), vmem),
        name="dil_merge_proj_residual",
    )(*outs, *lses, w, res)


def kernel(x, na_norm, na_wqkv, na_rpb, na_wo, ffn0_norm, ffn0_w1, ffn0_w2, dil_norm, dil_wqkv, dil_wo,
           ffn1_norm, ffn1_w1, ffn1_w2, final_norm):
    batch, seq, d = x.shape
    m = batch * seq
    rows = seq // GRID_W
    na_heads = na_rpb.shape[0]
    assert rows >= NA_KEY_ROWS and rows % NA_BLOCK_ROWS == 0 and rows >= NA_WIN_ROWS
    assert NA_KEY_ROWS >= NA_BLOCK_ROWS + NA_WIN_ROWS - 1

    xf = x.reshape(m, d)
    bf = lambda w: w.astype(BF16)

    qw = na_heads * HEAD_DIM
    col_scale = jnp.concatenate([jnp.full((qw,), Q_SCALE, F32), jnp.ones((2 * qw,), F32)])
    qkv = norm_matmul(xf, na_norm, bf(na_wqkv), col_scale, tm=min(1024, m), tn=min(1024, qw))
    bias = _na_bias_table(na_rpb)
    attn = na_attention(qkv, bias, batch=batch, seq=seq, heads_per_step=min(8, na_heads))
    xf = proj_residual(attn, bf(na_wo), xf, tm=min(512, m))
    xf = ffn(xf, ffn0_norm, bf(ffn0_w1), bf(ffn0_w2), final_norm, tm=min(1024, m), tf=512, final_norm=False)

    rot = _rope_tables(seq)
    dils = tuple(dil for _, dil in DIL_GROUPS)
    qkvs = norm_matmul_rope(xf, dil_norm, bf(dil_wqkv), rot, batch=batch, seq=seq, tm=min(1024, seq), dils=dils)
    outs, lses = [], []
    for gi, (window, dil) in enumerate(DIL_GROUPS):
        o, lse = dil_attention(qkvs[gi], group=gi, halo=(window // 2) // dil)
        outs.append(o)
        lses.append(lse)
    xf = merge_proj_residual(outs, lses, bf(dil_wo), xf, seq=seq, tm=min(512, seq))
    xf = ffn(xf, ffn1_norm, bf(ffn1_w1), bf(ffn1_w2), final_norm, tm=min(1024, m), tf=512, final_norm=True)
    return xf.reshape(batch, seq, d)
```

```python
import functools
import math

import numpy as np
import jax
import jax.numpy as jnp
from jax import lax
from jax.experimental import pallas as pl
from jax.experimental.pallas import tpu as pltpu

GRID_W = 64
HEAD_DIM = 128
NA_WIN_ROWS = 8
NA_WIN_COLS = 16
DIL_GROUPS = ((128, 1), (512, 4), (2048, 16))
ROPE_THETA = 500000.0
ROPE_DIM = HEAD_DIM // 4
BLOCK_Q = 128
EPS = 1e-6
NEG = -1e30

LOG2E = math.log2(math.e)
LN2 = math.log(2.0)
Q_SCALE = HEAD_DIM ** -0.5 * LOG2E

NA_BLOCK_ROWS = 4
NA_KEY_BACK = NA_WIN_ROWS // 2
NA_KEY_ROWS = 12
NA_KEY_CHUNKS = 3

DIL_STEP_ROWS = 512
ROW_CHUNK = 256
FFN_ROW_CHUNK = 512
FFN_HIDDEN_TILE = 512
BF16_SUBLANES = 16
F32_SUBLANES = 8
LANES = 128

V7X_VMEM_BYTES = 64 * 1024 * 1024
VMEM_CAP_BYTES = V7X_VMEM_BYTES - 8 * 1024 * 1024

F32 = jnp.float32
BF16 = jnp.bfloat16


def _params(semantics, vmem_bytes):
    limit = min(int(vmem_bytes * 1.25) + (4 << 20), VMEM_CAP_BYTES)
    return pltpu.CompilerParams(dimension_semantics=semantics, vmem_limit_bytes=limit)


def _rms(x, g):
    return x * lax.rsqrt(jnp.mean(x * x, axis=-1, keepdims=True) + EPS) * g


def _tile_then_next(n_tiles, n_inner):
    switch = max(n_inner // 2, 1)
    return lambda i, j: (jnp.minimum(i + jnp.where(j >= switch, 1, 0), n_tiles - 1), 0)


def _norm_matmul_kernel(x_ref, g_ref, w_ref, cs_ref, o_ref, h_ref):
    def body(with_norm):
        for c in range(x_ref.shape[0] // ROW_CHUNK):
            rows = slice(c * ROW_CHUNK, (c + 1) * ROW_CHUNK)
            if with_norm:
                h_ref[rows, :] = _rms(x_ref[rows, :], g_ref[...]).astype(BF16)
            acc = jnp.dot(h_ref[rows, :], w_ref[...], preferred_element_type=F32)
            o_ref[rows, :] = (acc * cs_ref[...]).astype(o_ref.dtype)

    pl.when(pl.program_id(1) == 0)(functools.partial(body, True))
    pl.when(pl.program_id(1) != 0)(functools.partial(body, False))


def _column_tiles(w, tn):
    k, n = w.shape
    return w.reshape(k, n // tn, tn).transpose(1, 0, 2).astype(BF16)


def norm_matmul(x, g, w, col_scale, *, tm):
    m, k = x.shape
    tn = w.shape[2]
    n = w.shape[0] * tn
    vmem = 2 * tm * k * 4 + tm * k * 2 + 2 * k * tn * 2 + 2 * tm * tn * 2 + tm * tn * 4
    return pl.pallas_call(
        _norm_matmul_kernel,
        grid=(m // tm, n // tn),
        in_specs=[
            pl.BlockSpec((tm, k), _tile_then_next(m // tm, n // tn)),
            pl.BlockSpec((1, k), lambda i, j: (0, 0)),
            pl.BlockSpec((None, k, tn), lambda i, j: (j, 0, 0)),
            pl.BlockSpec((1, tn), lambda i, j: (0, j)),
        ],
        out_specs=pl.BlockSpec((tm, tn), lambda i, j: (i, j)),
        out_shape=jax.ShapeDtypeStruct((m, n), BF16),
        scratch_shapes=[pltpu.VMEM((tm, k), BF16)],
        compiler_params=_params(("parallel", "arbitrary"), vmem),
        name="norm_qkv_na",
    )(x, g.reshape(1, k), w, col_scale.reshape(1, n))


def _norm_matmul_rope_kernel(x_ref, g_ref, w_ref, rot_ref, *refs, dils):
    o_refs = refs[:len(dils)]
    h_ref, s_ref, t_ref = refs[len(dils):]
    j = pl.program_id(1)
    tm = x_ref.shape[0]
    heads = s_ref.shape[0]
    half = ROPE_DIM // 2

    def body(o_ref, dil, rotate, with_norm=False):
        n = ROW_CHUNK // dil
        d1, d2 = _stride_split(dil)
        n1 = ROW_CHUNK // d1
        scale = jnp.where(j % 3 == 0, Q_SCALE, 1.0).astype(F32)
        for c in range(tm // ROW_CHUNK):
            rows = slice(c * ROW_CHUNK, (c + 1) * ROW_CHUNK)
            if with_norm:
                h_ref[rows, :] = _rms(x_ref[rows, :], g_ref[...]).astype(BF16)
            acc = jnp.dot(h_ref[rows, :], w_ref[...], preferred_element_type=F32)
            if rotate:
                cos = rot_ref[0, rows, :] * scale
                sin_hi = rot_ref[1, rows, :] * scale
                sin_lo = rot_ref[2, rows, :] * scale
            for h in range(heads):
                cols = slice(h * HEAD_DIM, (h + 1) * HEAD_DIM)
                a = acc[:, cols]
                if rotate:
                    a = (a * cos + pltpu.roll(a, half, 1) * sin_hi
                         + pltpu.roll(a, HEAD_DIM - half, 1) * sin_lo)
                if dil == 1:
                    o_ref[0, rows, cols] = a.astype(BF16)
                    continue
                s_ref[h, rows, :] = a
                if d2 == 1:
                    for r in range(dil):
                        o_ref[r, c * n:(c + 1) * n, cols] = (
                            s_ref[h, pl.ds(c * ROW_CHUNK + r, n, stride=dil), :].astype(BF16))
                    continue
                for r1 in range(d1):
                    t_ref[h, r1 * n1:(r1 + 1) * n1, :] = s_ref[h, pl.ds(c * ROW_CHUNK + r1, n1, stride=d1), :]
                for r1 in range(d1):
                    for r2 in range(d2):
                        o_ref[r1 + d1 * r2, c * n:(c + 1) * n, cols] = (
                            t_ref[h, pl.ds(r1 * n1 + r2, n, stride=d2), :].astype(BF16))

    pl.when(j == 0)(functools.partial(body, o_refs[0], dils[0], True, with_norm=True))
    for g, dil in enumerate(dils):
        pl.when((j // 3 == g) & (j % 3 != 2) & (j != 0))(functools.partial(body, o_refs[g], dil, True))
        pl.when((j // 3 == g) & (j % 3 == 2))(functools.partial(body, o_refs[g], dil, False))


def _stride_split(dil):
    if dil <= F32_SUBLANES:
        return dil, 1
    d1 = F32_SUBLANES // 2
    assert dil % d1 == 0 and dil // d1 <= F32_SUBLANES
    return d1, dil // d1


def _rope_tables(t):
    half = ROPE_DIM // 2
    pos = jnp.arange(t, dtype=F32)
    inv_freq = ROPE_THETA ** (-jnp.arange(0, ROPE_DIM, 2, dtype=F32) / ROPE_DIM)
    ang = pos[:, None] * inv_freq[None, :]
    cos = jnp.cos(ang)
    sin = jnp.sin(ang)
    rest = HEAD_DIM - ROPE_DIM
    one = jnp.ones((t, rest), F32)
    zero = jnp.zeros((t, rest), F32)
    zh = jnp.zeros((t, half), F32)
    return jnp.stack([jnp.concatenate([cos, cos, one], axis=-1),
                      jnp.concatenate([zh, sin, zero], axis=-1),
                      jnp.concatenate([-sin, zh, zero], axis=-1)], axis=0)


def norm_matmul_rope(x, g, w, rot, *, batch, seq, tm, dils):
    m, k = x.shape
    tn = w.shape[2]
    n = w.shape[0] * tn
    assert w.shape[0] == 3 * len(dils)
    heads = tn // HEAD_DIM
    tpb = seq // tm
    assert all(ROW_CHUNK % (BF16_SUBLANES * dil) == 0 for dil in dils) and tm % ROW_CHUNK == 0

    def out_map(g):
        return lambda i, j: (jnp.clip(j - 3 * g, 0, 2), i // tpb, 0, i % tpb, 0)

    vmem = (2 * tm * k * 4 + tm * k * 2 + 2 * k * tn * 2 + 2 * len(dils) * tm * tn * 2 + tm * tn * 4
            + 2 * ROW_CHUNK * tn * 4 + 2 * 3 * tm * HEAD_DIM * 4 + heads * ROW_CHUNK * HEAD_DIM * 4)
    return pl.pallas_call(
        functools.partial(_norm_matmul_rope_kernel, dils=dils),
        grid=(m // tm, n // tn),
        in_specs=[
            pl.BlockSpec((tm, k), _tile_then_next(m // tm, n // tn)),
            pl.BlockSpec((1, k), lambda i, j: (0, 0)),
            pl.BlockSpec((None, k, tn), lambda i, j: (j, 0, 0)),
            pl.BlockSpec((3, tm, HEAD_DIM), lambda i, j: (0, i % tpb, 0)),
        ],
        out_specs=[pl.BlockSpec((None, None, dil, tm // dil, tn), out_map(g)) for g, dil in enumerate(dils)],
        out_shape=[jax.ShapeDtypeStruct((3, batch, dil, seq // dil, tn), BF16) for dil in dils],
        scratch_shapes=[pltpu.VMEM((tm, k), BF16), pltpu.VMEM((heads, tm, HEAD_DIM), F32),
                        pltpu.VMEM((heads, ROW_CHUNK, HEAD_DIM), F32)],
        compiler_params=_params(("arbitrary", "arbitrary"), vmem),
        name="norm_qkv_dil_rope",
    )(x, g.reshape(1, k), w, rot)


def _na_bias_table(rpb):
    nh, n_row_off, n_col_off = rpb.shape
    assert 2 * GRID_W == LANES and NA_KEY_ROWS % 2 == 0 and n_col_off <= LANES
    rpb_lanes = jnp.pad(rpb, ((0, 0), (0, 0), (0, LANES - n_col_off)))
    return pl.pallas_call(
        _na_bias_kernel,
        grid=(nh,),
        in_specs=[pl.BlockSpec((None, n_row_off, LANES), lambda h: (h, 0, 0))],
        out_specs=pl.BlockSpec((None, NA_BLOCK_ROWS * GRID_W, NA_KEY_ROWS * GRID_W), lambda h: (h, 0, 0)),
        out_shape=jax.ShapeDtypeStruct((nh, NA_BLOCK_ROWS * GRID_W, NA_KEY_ROWS * GRID_W), F32),
        compiler_params=_params(("parallel",), 4 * NA_BLOCK_ROWS * GRID_W * NA_KEY_ROWS * GRID_W * 4),
        name="na_bias_table",
    )(rpb_lanes)


def _na_bias_kernel(rpb_ref, o_ref):
    w = GRID_W
    qc = lax.broadcasted_iota(jnp.int32, (w, LANES), 0)
    kc = lax.broadcasted_iota(jnp.int32, (w, LANES), 1)
    start = jnp.clip(qc - NA_WIN_COLS // 2, 0, w - NA_WIN_COLS)
    in_window = (kc >= start) & (kc < start + NA_WIN_COLS)
    by_col = []
    for a in range(rpb_ref.shape[0]):
        row = jnp.broadcast_to(rpb_ref[a:a + 1, :], (w, LANES))
        shifted = pltpu.roll(row, LANES - (NA_WIN_COLS - 1), 1, stride=1, stride_axis=0)
        by_col.append(jnp.where(in_window, shifted * LOG2E, NEG))
    for rr in range(NA_BLOCK_ROWS):
        for kk in range(0, NA_KEY_ROWS, 2):
            a = kk - NA_KEY_BACK - rr + NA_WIN_ROWS - 1
            assert 0 <= a and a + 1 < rpb_ref.shape[0]
            pair = jnp.where(kc < w, by_col[a], pltpu.roll(by_col[a + 1], w, 1))
            o_ref[rr * w:(rr + 1) * w, kk * w:(kk + 2) * w] = pair


def _na_attn_kernel(q_ref, *refs, heads, rows):
    k_refs = refs[:NA_KEY_CHUNKS]
    v_refs = refs[NA_KEY_CHUNKS:2 * NA_KEY_CHUNKS]
    b_ref, o_ref, bm_ref = refs[2 * NA_KEY_CHUNKS:]
    qb = q_ref.shape[0]
    ck = k_refs[0].shape[0]
    kh = min(NA_WIN_ROWS, rows)
    i = pl.program_id(2)

    def clipped(blk):
        r0 = blk * NA_BLOCK_ROWS
        return (r0 < kh // 2) | (r0 + NA_BLOCK_ROWS - 1 - kh // 2 > rows - kh)

    @pl.when((i == 0) | clipped(i) | clipped(i - 1))
    def _():
        r0 = i * NA_BLOCK_ROWS
        q_row = r0 + lax.broadcasted_iota(jnp.int32, (qb, ck), 0) // GRID_W
        lo = jnp.clip(q_row - kh // 2, 0, rows - kh)
        for c in range(NA_KEY_CHUNKS):
            k_row = (r0 - NA_KEY_BACK + c * (ck // GRID_W)
                     + lax.broadcasted_iota(jnp.int32, (qb, ck), 1) // GRID_W)
            row_mask = jnp.where((k_row >= lo) & (k_row < lo + kh), 0.0, NEG).astype(F32)
            for h in range(heads):
                bm_ref[h, :, c * ck:(c + 1) * ck] = b_ref[h, :, c * ck:(c + 1) * ck] + row_mask

    def scores(h):
        sl = slice(h * HEAD_DIM, (h + 1) * HEAD_DIM)
        q = q_ref[:, sl]
        s = []
        for c, k_ref in enumerate(k_refs):
            sc = lax.dot_general(q, k_ref[:, sl], (((1,), (1,)), ((), ())), preferred_element_type=F32)
            s.append(sc + bm_ref[h, :, c * ck:(c + 1) * ck])
        return s, jnp.max(functools.reduce(jnp.maximum, s), axis=-1, keepdims=True)

    def finish(h, s, m):
        sl = slice(h * HEAD_DIM, (h + 1) * HEAD_DIM)
        p = [jnp.exp2(sc - m) for sc in s]
        l = jnp.sum(functools.reduce(jnp.add, p), axis=-1, keepdims=True)
        o = functools.reduce(jnp.add, [
            jnp.dot(pc.astype(BF16), v_ref[:, sl], preferred_element_type=F32)
            for pc, v_ref in zip(p, v_refs)])
        o_ref[:, sl] = (o / l).astype(o_ref.dtype)

    cur = scores(0)
    for h in range(heads):
        nxt = scores(h + 1) if h + 1 < heads else None
        finish(h, *cur)
        cur = nxt


def na_attention(qkv, bias, *, batch, seq, heads_per_step):
    m = qkv.shape[0]
    nh = bias.shape[0]
    hg = heads_per_step
    wq = hg * HEAD_DIM
    qb = NA_BLOCK_ROWS * GRID_W
    kb = NA_KEY_ROWS * GRID_W // NA_KEY_CHUNKS
    assert qb % kb == 0 and (NA_KEY_BACK * GRID_W) % kb == 0
    nblk = seq // qb
    kpb = seq // kb
    ratio = qb // kb
    back = NA_KEY_BACK * GRID_W // kb
    ncol = nh // hg

    def k_map(c, sec):
        def index(g, b, i):
            return (b * kpb + jnp.clip(i * ratio - back + c, 0, kpb - 1), sec * ncol + g)
        return index

    in_specs = [pl.BlockSpec((qb, wq), lambda g, b, i: (b * nblk + i, g))]
    in_specs += [pl.BlockSpec((kb, wq), k_map(c, 1)) for c in range(NA_KEY_CHUNKS)]
    in_specs += [pl.BlockSpec((kb, wq), k_map(c, 2)) for c in range(NA_KEY_CHUNKS)]
    in_specs += [pl.BlockSpec((hg, qb, NA_KEY_CHUNKS * kb), lambda g, b, i: (g, 0, 0))]
    vmem = (2 * 2 * qb * wq * 2 + 2 * 2 * NA_KEY_CHUNKS * kb * wq * 2
            + 3 * hg * qb * NA_KEY_CHUNKS * kb * 4 + 6 * qb * NA_KEY_CHUNKS * kb * 4)
    return pl.pallas_call(
        functools.partial(_na_attn_kernel, heads=hg, rows=seq // GRID_W),
        grid=(ncol, batch, nblk),
        in_specs=in_specs,
        out_specs=pl.BlockSpec((qb, wq), lambda g, b, i: (b * nblk + i, g)),
        out_shape=jax.ShapeDtypeStruct((m, nh * HEAD_DIM), BF16),
        scratch_shapes=[pltpu.VMEM((hg, qb, NA_KEY_CHUNKS * kb), F32)],
        compiler_params=_params(("parallel", "parallel", "arbitrary"), vmem),
        name="na_attention",
    )(qkv, *([qkv] * (2 * NA_KEY_CHUNKS)), bias)


def _proj_residual_kernel(a_ref, w_ref, r_ref, o_ref):
    o_ref[...] = r_ref[...] + jnp.dot(a_ref[...], w_ref[...], preferred_element_type=F32)


def proj_residual(a, w, res, *, tm):
    m, k = a.shape
    n = w.shape[1]
    vmem = 2 * tm * k * 2 + 2 * k * n * 2 + 4 * tm * n * 4 + tm * n * 4
    return pl.pallas_call(
        _proj_residual_kernel,
        grid=(m // tm,),
        in_specs=[
            pl.BlockSpec((tm, k), lambda i: (i, 0)),
            pl.BlockSpec((k, n), lambda i: (0, 0)),
            pl.BlockSpec((tm, n), lambda i: (i, 0)),
        ],
        out_specs=pl.BlockSpec((tm, n), lambda i: (i, 0)),
        out_shape=jax.ShapeDtypeStruct((m, n), F32),
        compiler_params=_params(("parallel",), vmem),
        name="na_proj_residual",
    )(a, w, res)


def _ffn_kernel(x_ref, g_ref, w1_ref, w2_ref, gf_ref, o_ref, h_ref, *, final_norm):
    f = pl.program_id(1)

    def body(first):
        for c in range(x_ref.shape[0] // FFN_ROW_CHUNK):
            rows = slice(c * FFN_ROW_CHUNK, (c + 1) * FFN_ROW_CHUNK)
            if first:
                h_ref[rows, :] = _rms(x_ref[rows, :], g_ref[...]).astype(BF16)
            a = jnp.dot(h_ref[rows, :], w1_ref[...], preferred_element_type=F32)
            a = jnp.square(jnp.maximum(a, 0.0)).astype(BF16)
            y = jnp.dot(a, w2_ref[...], preferred_element_type=F32)
            if first:
                o_ref[rows, :] = x_ref[rows, :] + y
            else:
                o_ref[rows, :] += y

    pl.when(f == 0)(functools.partial(body, True))
    pl.when(f != 0)(functools.partial(body, False))

    if final_norm:
        @pl.when(f == pl.num_programs(1) - 1)
        def _():
            o_ref[...] = _rms(o_ref[...], gf_ref[...])


def ffn(x, g, w1, w2, g_final, *, tm, final_norm):
    m, d = x.shape
    tf = w1.shape[2]
    dff = w1.shape[0] * tf
    assert tm % FFN_ROW_CHUNK == 0
    vmem = (4 * tm * d * 4 + tm * d * 2 + 2 * 2 * d * tf * 2
            + FFN_ROW_CHUNK * tf * 6 + FFN_ROW_CHUNK * d * 4)
    return pl.pallas_call(
        functools.partial(_ffn_kernel, final_norm=final_norm),
        grid=(m // tm, dff // tf),
        in_specs=[
            pl.BlockSpec((tm, d), _tile_then_next(m // tm, dff // tf)),
            pl.BlockSpec((1, d), lambda i, f: (0, 0)),
            pl.BlockSpec((None, d, tf), lambda i, f: (f, 0, 0)),
            pl.BlockSpec((tf, d), lambda i, f: (f, 0)),
            pl.BlockSpec((1, d), lambda i, f: (0, 0)),
        ],
        out_specs=pl.BlockSpec((tm, d), lambda i, f: (i, 0)),
        out_shape=jax.ShapeDtypeStruct((m, d), F32),
        scratch_shapes=[pltpu.VMEM((tm, d), BF16)],
        compiler_params=_params(("parallel", "arbitrary"), vmem),
        name="ffn_final" if final_norm else "ffn",
    )(x, g.reshape(1, d), w1, w2, g_final.reshape(1, d))


def _dil_attn_kernel(q_ref, kp_ref, kc_ref, kn_ref, vp_ref, vc_ref, vn_ref, o_ref, lse_ref,
                     *, heads, halo, length):
    qb = q_ref.shape[0]
    sub = BLOCK_Q
    nk = sub + 2 * halo
    m0 = pl.program_id(2) * qb
    qi = lax.broadcasted_iota(jnp.int32, (sub, nk), 0)
    kj = lax.broadcasted_iota(jnp.int32, (sub, nk), 1) - halo
    band = jnp.abs(kj - qi) <= halo
    lane = lax.broadcasted_iota(jnp.int32, (sub, HEAD_DIM), 1)

    def window(prev_ref, cur_ref, next_ref, lo, hi, cols):
        parts = [prev_ref[:, cols]] if lo < 0 else []
        parts.append(cur_ref[max(lo, 0):min(hi, qb), cols])
        if hi > qb:
            parts.append(next_ref[:, cols])
        return parts[0] if len(parts) == 1 else jnp.concatenate(parts, axis=0)

    def scores(u, h):
        rows = slice(u * sub, (u + 1) * sub)
        cols = slice(h * HEAD_DIM, (h + 1) * HEAD_DIM)
        k = window(kp_ref, kc_ref, kn_ref, u * sub - halo, (u + 1) * sub + halo, cols)
        s = lax.dot_general(q_ref[rows, cols], k, (((1,), (1,)), ((), ())), preferred_element_type=F32)
        s = s + masks[u]
        return s, jnp.max(s, axis=-1, keepdims=True)

    def finish(u, h, s, m):
        rows = slice(u * sub, (u + 1) * sub)
        cols = slice(h * HEAD_DIM, (h + 1) * HEAD_DIM)
        v = window(vp_ref, vc_ref, vn_ref, u * sub - halo, (u + 1) * sub + halo, cols)
        p = jnp.exp2(s - m)
        l = jnp.sum(p, axis=-1, keepdims=True)
        o = jnp.dot(p.astype(BF16), v, preferred_element_type=F32)
        o_ref[rows, cols] = (o / l).astype(o_ref.dtype)
        return m * LN2 + jnp.log(l)

    masks = []
    for u in range(qb // sub):
        kpos = kj + (m0 + u * sub)
        masks.append(jnp.where(band & (kpos >= 0) & (kpos < length), 0.0, NEG).astype(F32))

    units = [(u, h) for u in range(qb // sub) for h in range(heads)]
    cur = scores(*units[0])
    lse_tile = None
    for n, (u, h) in enumerate(units):
        nxt = scores(*units[n + 1]) if n + 1 < len(units) else None
        lse = finish(u, h, *cur)
        lse_tile = jnp.where(lane == h, lse, jnp.zeros((sub, HEAD_DIM), F32) if h == 0 else lse_tile)
        if h == heads - 1:
            lse_ref[u * sub:(u + 1) * sub, :] = lse_tile
        cur = nxt


def dil_attention(qkv, *, group, halo):
    _, batch, dil, length, wq = qkv.shape
    heads = wq // HEAD_DIM
    qb = min(DIL_STEP_ROWS, length)
    assert BLOCK_Q % halo == 0 and qb % BLOCK_Q == 0 and length % qb == 0
    nblk = length // qb
    hpb = length // halo
    per = qb // halo

    def cur_map(sec):
        return lambda b, r, i: (sec, b, r, i, 0)

    def prev_map(sec):
        return lambda b, r, i: (sec, b, r, jnp.maximum(i * per - 1, 0), 0)

    def next_map(sec):
        return lambda b, r, i: (sec, b, r, jnp.minimum((i + 1) * per, hpb - 1), 0)

    def spec(rows, index_map):
        return pl.BlockSpec((None, None, None, rows, wq), index_map)

    in_specs = [spec(qb, cur_map(0))]
    for sec in (1, 2):
        in_specs += [spec(halo, prev_map(sec)), spec(qb, cur_map(sec)), spec(halo, next_map(sec))]
    vmem = (2 * (2 * qb * wq * 2 + 2 * (qb + 2 * halo) * wq * 2 + qb * HEAD_DIM * 4)
            + 8 * BLOCK_Q * (BLOCK_Q + 2 * halo) * 4)
    return pl.pallas_call(
        functools.partial(_dil_attn_kernel, heads=heads, halo=halo, length=length),
        grid=(batch, dil, nblk),
        in_specs=in_specs,
        out_specs=[pl.BlockSpec((None, None, qb, wq), lambda b, r, i: (b, r, i, 0)),
                   pl.BlockSpec((None, None, qb, HEAD_DIM), lambda b, r, i: (b, r, i, 0))],
        out_shape=[jax.ShapeDtypeStruct((batch, dil, length, wq), BF16),
                   jax.ShapeDtypeStruct((batch, dil, length, HEAD_DIM), F32)],
        compiler_params=_params(("parallel", "parallel", "arbitrary"), vmem),
        name=f"dil_attention_g{group}",
    )(*([qkv] * 7))


def _merge_proj_kernel(*refs, dils, heads):
    g_n = len(dils)
    o_refs = refs[:g_n]
    lse_refs = refs[g_n:2 * g_n]
    w_ref, r_ref, out_ref, a_ref, lt_ref, ot_ref, tt_ref = refs[2 * g_n:]
    tm = a_ref.shape[0]

    def merged(c):
        rows = slice(c * ROW_CHUNK, (c + 1) * ROW_CHUNK)
        lses = []
        for g, dil in enumerate(dils):
            if dil == 1:
                lses.append(lse_refs[g][0, rows, :])
                continue
            n = ROW_CHUNK // dil
            d1, d2 = _stride_split(dil)
            n1 = ROW_CHUNK // d1

            def to_token_order(dst_ref, dst_idx, src):
                if d2 == 1:
                    for r in range(dil):
                        dst_ref[dst_idx, pl.ds(c * ROW_CHUNK + r, n, stride=dil), :] = src(r)
                    return
                for r1 in range(d1):
                    for r2 in range(d2):
                        tt_ref[pl.ds(r1 * n1 + r2, n, stride=d2), :] = src(r1 + d1 * r2)
                for r1 in range(d1):
                    dst_ref[dst_idx, pl.ds(c * ROW_CHUNK + r1, n1, stride=d1), :] = tt_ref[r1 * n1:(r1 + 1) * n1, :]

            to_token_order(lt_ref, g, lambda r: lse_refs[g][r, c * n:(c + 1) * n, :])
            for h in range(heads):
                to_token_order(ot_ref, g * heads + h, lambda r: o_refs[g][
                    r, c * n:(c + 1) * n, h * HEAD_DIM:(h + 1) * HEAD_DIM].astype(F32))
            lses.append(lt_ref[g, rows, :])

        mx = functools.reduce(jnp.maximum, lses)
        es = [jnp.exp(l - mx) for l in lses]
        den = functools.reduce(jnp.add, es)
        wts = [e / den for e in es]
        for h in range(heads):
            sl = slice(h * HEAD_DIM, (h + 1) * HEAD_DIM)
            acc = None
            for g, dil in enumerate(dils):
                o = o_refs[g][0, rows, sl].astype(F32) if dil == 1 else ot_ref[g * heads + h, rows, :]
                term = wts[g][:, h:h + 1] * o
                acc = term if acc is None else acc + term
            a_ref[rows, sl] = acc.astype(BF16)

    merged(0)
    for c in range(tm // ROW_CHUNK):
        rows = slice(c * ROW_CHUNK, (c + 1) * ROW_CHUNK)
        if c + 1 < tm // ROW_CHUNK:
            merged(c + 1)
        out_ref[rows, :] = r_ref[rows, :] + jnp.dot(a_ref[rows, :], w_ref[...], preferred_element_type=F32)


def merge_proj_residual(outs, lses, w, res, *, seq, tm):
    dils = tuple(o.shape[1] for o in outs)
    k = outs[0].shape[-1]
    m, n = res.shape
    g_n = len(outs)
    heads = k // HEAD_DIM
    tpb = seq // tm
    assert all(ROW_CHUNK % (BF16_SUBLANES * dil) == 0 for dil in dils) and tm % ROW_CHUNK == 0

    def stream_spec(dil, width):
        return pl.BlockSpec((None, dil, tm // dil, width), lambda i: (i // tpb, 0, i % tpb, 0))

    vmem = (2 * g_n * tm * k * 2 + 2 * g_n * tm * HEAD_DIM * 4 + 2 * k * n * 2 + 4 * tm * n * 4
            + tm * k * 2 + tm * n * 4 + g_n * tm * HEAD_DIM * 4 + g_n * tm * k * 4)
    return pl.pallas_call(
        functools.partial(_merge_proj_kernel, dils=dils, heads=heads),
        grid=(m // tm,),
        in_specs=([stream_spec(dil, k) for dil in dils]
                  + [stream_spec(dil, HEAD_DIM) for dil in dils]
                  + [pl.BlockSpec((k, n), lambda i: (0, 0)),
                     pl.BlockSpec((tm, n), lambda i: (i, 0))]),
        out_specs=pl.BlockSpec((tm, n), lambda i: (i, 0)),
        out_shape=jax.ShapeDtypeStruct((m, n), F32),
        scratch_shapes=[pltpu.VMEM((tm, k), BF16),
                        pltpu.VMEM((g_n, tm, HEAD_DIM), F32),
                        pltpu.VMEM((g_n * heads, tm, HEAD_DIM), F32),
                        pltpu.VMEM((ROW_CHUNK, HEAD_DIM), F32)],
        compiler_params=_params(("parallel",), vmem),
        name="dil_merge_proj_residual",
    )(*outs, *lses, w, res)


def kernel(x, na_norm, na_wqkv, na_rpb, na_wo, ffn0_norm, ffn0_w1, ffn0_w2, dil_norm, dil_wqkv, dil_wo,
           ffn1_norm, ffn1_w1, ffn1_w2, final_norm):
    batch, seq, d = x.shape
    m = batch * seq
    rows = seq // GRID_W
    na_heads = na_rpb.shape[0]
    assert rows >= NA_KEY_ROWS and rows % NA_BLOCK_ROWS == 0 and rows >= NA_WIN_ROWS
    assert NA_KEY_ROWS >= NA_BLOCK_ROWS + NA_WIN_ROWS - 1

    xf = x.reshape(m, d)
    bf = lambda w: w.astype(BF16)

    qw = na_heads * HEAD_DIM
    col_scale = jnp.concatenate([jnp.full((qw,), Q_SCALE, F32), jnp.ones((2 * qw,), F32)])
    qkv = norm_matmul(xf, na_norm, _column_tiles(na_wqkv, min(1024, qw)), col_scale, tm=min(1024, m))
    bias = _na_bias_table(na_rpb)
    attn = na_attention(qkv, bias, batch=batch, seq=seq, heads_per_step=min(8, na_heads))
    xf = proj_residual(attn, bf(na_wo), xf, tm=min(512, m))
    xf = ffn(xf, ffn0_norm, _column_tiles(ffn0_w1, FFN_HIDDEN_TILE), bf(ffn0_w2), final_norm,
             tm=min(1024, m), final_norm=False)

    rot = _rope_tables(seq)
    dils = tuple(dil for _, dil in DIL_GROUPS)
    dil_w = _column_tiles(dil_wqkv, dil_wqkv.shape[1] // (3 * len(dils)))
    qkvs = norm_matmul_rope(xf, dil_norm, dil_w, rot, batch=batch, seq=seq, tm=min(1024, seq), dils=dils)
    outs, lses = [], []
    for gi, (window, dil) in enumerate(DIL_GROUPS):
        o, lse = dil_attention(qkvs[gi], group=gi, halo=(window // 2) // dil)
        outs.append(o)
        lses.append(lse)
    xf = merge_proj_residual(outs, lses, bf(dil_wo), xf, seq=seq, tm=min(512, seq))
    xf = ffn(xf, ffn1_norm, _column_tiles(ffn1_w1, FFN_HIDDEN_TILE), bf(ffn1_w2), final_norm,
             tm=min(1024, m), final_norm=True)
    return xf.reshape(batch, seq, d)
```

```python
import functools
import math

import numpy as np
import jax
import jax.numpy as jnp
from jax import lax
from jax.experimental import pallas as pl
from jax.experimental.pallas import tpu as pltpu

GRID_W = 64
HEAD_DIM = 128
NA_WIN_ROWS = 8
NA_WIN_COLS = 16
DIL_GROUPS = ((128, 1), (512, 4), (2048, 16))
ROPE_THETA = 500000.0
ROPE_DIM = HEAD_DIM // 4
BLOCK_Q = 128
EPS = 1e-6
NEG = -1e30

LOG2E = math.log2(math.e)
LN2 = math.log(2.0)
Q_SCALE = HEAD_DIM ** -0.5 * LOG2E

NA_BLOCK_ROWS = 4
NA_KEY_BACK = NA_WIN_ROWS // 2
NA_KEY_ROWS = 12
NA_KEY_CHUNKS = 3

DIL_STEP_ROWS = 512
ROW_CHUNK = 256
FFN_ROW_CHUNK = 512
BF16_SUBLANES = 16
F32_SUBLANES = 8
LANES = 128

V7X_VMEM_BYTES = 64 * 1024 * 1024
VMEM_CAP_BYTES = V7X_VMEM_BYTES - 8 * 1024 * 1024

F32 = jnp.float32
BF16 = jnp.bfloat16


def _params(semantics, vmem_bytes):
    limit = min(int(vmem_bytes * 1.25) + (4 << 20), VMEM_CAP_BYTES)
    return pltpu.CompilerParams(dimension_semantics=semantics, vmem_limit_bytes=limit)


def _rms(x, g):
    return x * lax.rsqrt(jnp.mean(x * x, axis=-1, keepdims=True) + EPS) * g


def _tile_then_next(n_tiles, n_inner):
    switch = max(n_inner // 2, 1)
    return lambda i, j: (jnp.minimum(i + jnp.where(j >= switch, 1, 0), n_tiles - 1), 0)


def _cast_rows(n_rows, n_steps):
    for rows in range(BF16_SUBLANES, n_rows + 1, BF16_SUBLANES):
        if n_rows % rows == 0 and n_rows // rows <= n_steps:
            return rows
    raise ValueError((n_rows, n_steps))


def _cast_specs(weights, n_steps, n_inner):
    specs = []
    for w in weights:
        rows = _cast_rows(w.shape[0], n_steps)
        last = w.shape[0] // rows - 1
        specs.append(pl.BlockSpec((rows, w.shape[1]), lambda i, j, last=last: (jnp.minimum(i * n_inner + j, last), 0)))
    return specs


def _cast_vmem(weights, n_steps):
    return sum(2 * _cast_rows(w.shape[0], n_steps) * w.shape[1] * 6 for w in weights)


def _cast_slabs(src_refs, dst_refs):
    for src, dst in zip(src_refs, dst_refs):
        dst[...] = src[...].astype(BF16)


def _norm_matmul_kernel(x_ref, g_ref, w_ref, cs_ref, *refs, n_cast):
    cast_src = refs[:n_cast]
    o_ref = refs[n_cast]
    cast_dst = refs[n_cast + 1:2 * n_cast + 1]
    h_ref = refs[2 * n_cast + 1]

    def body(with_norm):
        _cast_slabs(cast_src, cast_dst)
        for c in range(x_ref.shape[0] // ROW_CHUNK):
            rows = slice(c * ROW_CHUNK, (c + 1) * ROW_CHUNK)
            if with_norm:
                h_ref[rows, :] = _rms(x_ref[rows, :], g_ref[...]).astype(BF16)
            acc = jnp.dot(h_ref[rows, :], w_ref[...], preferred_element_type=F32)
            o_ref[rows, :] = (acc * cs_ref[...]).astype(o_ref.dtype)

    pl.when(pl.program_id(1) == 0)(functools.partial(body, True))
    pl.when(pl.program_id(1) != 0)(functools.partial(body, False))


def norm_matmul(x, g, w, col_scale, *, tm, tn, cast=()):
    m, k = x.shape
    n = w.shape[1]
    n_steps = (m // tm) * (n // tn)
    vmem = (2 * tm * k * 4 + tm * k * 2 + 2 * k * tn * 2 + 2 * tm * tn * 2 + tm * tn * 4
            + _cast_vmem(cast, n_steps))
    cast_specs = _cast_specs(cast, n_steps, n // tn)
    return pl.pallas_call(
        functools.partial(_norm_matmul_kernel, n_cast=len(cast)),
        grid=(m // tm, n // tn),
        in_specs=[
            pl.BlockSpec((tm, k), _tile_then_next(m // tm, n // tn)),
            pl.BlockSpec((1, k), lambda i, j: (0, 0)),
            pl.BlockSpec((k, tn), lambda i, j: (0, j)),
            pl.BlockSpec((1, tn), lambda i, j: (0, j)),
        ] + cast_specs,
        out_specs=[pl.BlockSpec((tm, tn), lambda i, j: (i, j))] + cast_specs,
        out_shape=[jax.ShapeDtypeStruct((m, n), BF16)] + [jax.ShapeDtypeStruct(c.shape, BF16) for c in cast],
        scratch_shapes=[pltpu.VMEM((tm, k), BF16)],
        compiler_params=_params(("arbitrary", "arbitrary"), vmem),
        name="norm_qkv_na",
    )(x, g.reshape(1, k), w, col_scale.reshape(1, n), *cast)


def _norm_matmul_rope_kernel(x_ref, g_ref, w_ref, rot_ref, *refs, dils):
    o_refs = refs[:len(dils)]
    h_ref, s_ref, t_ref = refs[len(dils):]
    j = pl.program_id(1)
    tm = x_ref.shape[0]
    heads = s_ref.shape[0]
    half = ROPE_DIM // 2

    def body(o_ref, dil, rotate, with_norm=False):
        n = ROW_CHUNK // dil
        d1, d2 = _stride_split(dil)
        n1 = ROW_CHUNK // d1
        scale = jnp.where(j % 3 == 0, Q_SCALE, 1.0).astype(F32)
        for c in range(tm // ROW_CHUNK):
            rows = slice(c * ROW_CHUNK, (c + 1) * ROW_CHUNK)
            if with_norm:
                h_ref[rows, :] = _rms(x_ref[rows, :], g_ref[...]).astype(BF16)
            acc = jnp.dot(h_ref[rows, :], w_ref[...], preferred_element_type=F32)
            if rotate:
                cos = rot_ref[0, rows, :] * scale
                sin_hi = rot_ref[1, rows, :] * scale
                sin_lo = rot_ref[2, rows, :] * scale
            for h in range(heads):
                cols = slice(h * HEAD_DIM, (h + 1) * HEAD_DIM)
                a = acc[:, cols]
                if rotate:
                    a = (a * cos + pltpu.roll(a, half, 1) * sin_hi
                         + pltpu.roll(a, HEAD_DIM - half, 1) * sin_lo)
                if dil == 1:
                    o_ref[0, rows, cols] = a.astype(BF16)
                    continue
                s_ref[h, rows, :] = a
                if d2 == 1:
                    for r in range(dil):
                        o_ref[r, c * n:(c + 1) * n, cols] = (
                            s_ref[h, pl.ds(c * ROW_CHUNK + r, n, stride=dil), :].astype(BF16))
                    continue
                for r1 in range(d1):
                    t_ref[h, r1 * n1:(r1 + 1) * n1, :] = s_ref[h, pl.ds(c * ROW_CHUNK + r1, n1, stride=d1), :]
                for r1 in range(d1):
                    for r2 in range(d2):
                        o_ref[r1 + d1 * r2, c * n:(c + 1) * n, cols] = (
                            t_ref[h, pl.ds(r1 * n1 + r2, n, stride=d2), :].astype(BF16))

    pl.when(j == 0)(functools.partial(body, o_refs[0], dils[0], True, with_norm=True))
    for g, dil in enumerate(dils):
        pl.when((j // 3 == g) & (j % 3 != 2) & (j != 0))(functools.partial(body, o_refs[g], dil, True))
        pl.when((j // 3 == g) & (j % 3 == 2))(functools.partial(body, o_refs[g], dil, False))


def _stride_split(dil):
    if dil <= F32_SUBLANES:
        return dil, 1
    d1 = F32_SUBLANES // 2
    assert dil % d1 == 0 and dil // d1 <= F32_SUBLANES
    return d1, dil // d1


def _rope_tables(t):
    half = ROPE_DIM // 2
    pos = jnp.arange(t, dtype=F32)
    inv_freq = ROPE_THETA ** (-jnp.arange(0, ROPE_DIM, 2, dtype=F32) / ROPE_DIM)
    ang = pos[:, None] * inv_freq[None, :]
    cos = jnp.cos(ang)
    sin = jnp.sin(ang)
    rest = HEAD_DIM - ROPE_DIM
    one = jnp.ones((t, rest), F32)
    zero = jnp.zeros((t, rest), F32)
    zh = jnp.zeros((t, half), F32)
    return jnp.stack([jnp.concatenate([cos, cos, one], axis=-1),
                      jnp.concatenate([zh, sin, zero], axis=-1),
                      jnp.concatenate([-sin, zh, zero], axis=-1)], axis=0)


def norm_matmul_rope(x, g, w, rot, *, batch, seq, tm, dils):
    m, k = x.shape
    n = w.shape[1]
    tn = n // (3 * len(dils))
    heads = tn // HEAD_DIM
    tpb = seq // tm
    assert all(ROW_CHUNK % (BF16_SUBLANES * dil) == 0 for dil in dils) and tm % ROW_CHUNK == 0

    def out_map(g):
        return lambda i, j: (jnp.clip(j - 3 * g, 0, 2), i // tpb, 0, i % tpb, 0)

    vmem = (2 * tm * k * 4 + tm * k * 2 + 2 * k * tn * 2 + 2 * len(dils) * tm * tn * 2 + tm * tn * 4
            + 2 * ROW_CHUNK * tn * 4 + 2 * 3 * tm * HEAD_DIM * 4 + heads * ROW_CHUNK * HEAD_DIM * 4)
    return pl.pallas_call(
        functools.partial(_norm_matmul_rope_kernel, dils=dils),
        grid=(m // tm, n // tn),
        in_specs=[
            pl.BlockSpec((tm, k), _tile_then_next(m // tm, n // tn)),
            pl.BlockSpec((1, k), lambda i, j: (0, 0)),
            pl.BlockSpec((k, tn), lambda i, j: (0, j)),
            pl.BlockSpec((3, tm, HEAD_DIM), lambda i, j: (0, i % tpb, 0)),
        ],
        out_specs=[pl.BlockSpec((None, None, dil, tm // dil, tn), out_map(g)) for g, dil in enumerate(dils)],
        out_shape=[jax.ShapeDtypeStruct((3, batch, dil, seq // dil, tn), BF16) for dil in dils],
        scratch_shapes=[pltpu.VMEM((tm, k), BF16), pltpu.VMEM((heads, tm, HEAD_DIM), F32),
                        pltpu.VMEM((heads, ROW_CHUNK, HEAD_DIM), F32)],
        compiler_params=_params(("arbitrary", "arbitrary"), vmem),
        name="norm_qkv_dil_rope",
    )(x, g.reshape(1, k), w, rot)


def _na_bias_table(rpb):
    nh, n_row_off, n_col_off = rpb.shape
    assert 2 * GRID_W == LANES and NA_KEY_ROWS % 2 == 0 and n_col_off <= LANES
    rpb_lanes = jnp.pad(rpb, ((0, 0), (0, 0), (0, LANES - n_col_off)))
    return pl.pallas_call(
        _na_bias_kernel,
        grid=(nh,),
        in_specs=[pl.BlockSpec((None, n_row_off, LANES), lambda h: (h, 0, 0))],
        out_specs=pl.BlockSpec((None, NA_BLOCK_ROWS * GRID_W, NA_KEY_ROWS * GRID_W), lambda h: (h, 0, 0)),
        out_shape=jax.ShapeDtypeStruct((nh, NA_BLOCK_ROWS * GRID_W, NA_KEY_ROWS * GRID_W), F32),
        compiler_params=_params(("parallel",), 4 * NA_BLOCK_ROWS * GRID_W * NA_KEY_ROWS * GRID_W * 4),
        name="na_bias_table",
    )(rpb_lanes)


def _na_bias_kernel(rpb_ref, o_ref):
    w = GRID_W
    qc = lax.broadcasted_iota(jnp.int32, (w, LANES), 0)
    kc = lax.broadcasted_iota(jnp.int32, (w, LANES), 1)
    start = jnp.clip(qc - NA_WIN_COLS // 2, 0, w - NA_WIN_COLS)
    in_window = (kc >= start) & (kc < start + NA_WIN_COLS)
    by_col = []
    for a in range(rpb_ref.shape[0]):
        row = jnp.broadcast_to(rpb_ref[a:a + 1, :], (w, LANES))
        shifted = pltpu.roll(row, LANES - (NA_WIN_COLS - 1), 1, stride=1, stride_axis=0)
        by_col.append(jnp.where(in_window, shifted * LOG2E, NEG))
    for rr in range(NA_BLOCK_ROWS):
        for kk in range(0, NA_KEY_ROWS, 2):
            a = kk - NA_KEY_BACK - rr + NA_WIN_ROWS - 1
            assert 0 <= a and a + 1 < rpb_ref.shape[0]
            pair = jnp.where(kc < w, by_col[a], pltpu.roll(by_col[a + 1], w, 1))
            o_ref[rr * w:(rr + 1) * w, kk * w:(kk + 2) * w] = pair


def _na_attn_kernel(q_ref, *refs, heads, rows):
    k_refs = refs[:NA_KEY_CHUNKS]
    v_refs = refs[NA_KEY_CHUNKS:2 * NA_KEY_CHUNKS]
    b_ref, o_ref, bm_ref = refs[2 * NA_KEY_CHUNKS:]
    qb = q_ref.shape[0]
    ck = k_refs[0].shape[0]
    kh = min(NA_WIN_ROWS, rows)
    i = pl.program_id(2)

    def clipped(blk):
        r0 = blk * NA_BLOCK_ROWS
        return (r0 < kh // 2) | (r0 + NA_BLOCK_ROWS - 1 - kh // 2 > rows - kh)

    @pl.when((i == 0) | clipped(i) | clipped(i - 1))
    def _():
        r0 = i * NA_BLOCK_ROWS
        q_row = r0 + lax.broadcasted_iota(jnp.int32, (qb, ck), 0) // GRID_W
        lo = jnp.clip(q_row - kh // 2, 0, rows - kh)
        for c in range(NA_KEY_CHUNKS):
            k_row = (r0 - NA_KEY_BACK + c * (ck // GRID_W)
                     + lax.broadcasted_iota(jnp.int32, (qb, ck), 1) // GRID_W)
            row_mask = jnp.where((k_row >= lo) & (k_row < lo + kh), 0.0, NEG).astype(F32)
            for h in range(heads):
                bm_ref[h, :, c * ck:(c + 1) * ck] = b_ref[h, :, c * ck:(c + 1) * ck] + row_mask

    def scores(h):
        sl = slice(h * HEAD_DIM, (h + 1) * HEAD_DIM)
        q = q_ref[:, sl]
        s = []
        for c, k_ref in enumerate(k_refs):
            sc = lax.dot_general(q, k_ref[:, sl], (((1,), (1,)), ((), ())), preferred_element_type=F32)
            s.append(sc + bm_ref[h, :, c * ck:(c + 1) * ck])
        return s, jnp.max(functools.reduce(jnp.maximum, s), axis=-1, keepdims=True)

    def finish(h, s, m):
        sl = slice(h * HEAD_DIM, (h + 1) * HEAD_DIM)
        p = [jnp.exp2(sc - m) for sc in s]
        l = jnp.sum(functools.reduce(jnp.add, p), axis=-1, keepdims=True)
        o = functools.reduce(jnp.add, [
            jnp.dot(pc.astype(BF16), v_ref[:, sl], preferred_element_type=F32)
            for pc, v_ref in zip(p, v_refs)])
        o_ref[:, sl] = (o / l).astype(o_ref.dtype)

    cur = scores(0)
    for h in range(heads):
        nxt = scores(h + 1) if h + 1 < heads else None
        finish(h, *cur)
        cur = nxt


def na_attention(qkv, bias, *, batch, seq, heads_per_step):
    m = qkv.shape[0]
    nh = bias.shape[0]
    hg = heads_per_step
    wq = hg * HEAD_DIM
    qb = NA_BLOCK_ROWS * GRID_W
    kb = NA_KEY_ROWS * GRID_W // NA_KEY_CHUNKS
    assert qb % kb == 0 and (NA_KEY_BACK * GRID_W) % kb == 0
    nblk = seq // qb
    kpb = seq // kb
    ratio = qb // kb
    back = NA_KEY_BACK * GRID_W // kb
    ncol = nh // hg

    def k_map(c, sec):
        def index(g, b, i):
            return (b * kpb + jnp.clip(i * ratio - back + c, 0, kpb - 1), sec * ncol + g)
        return index

    in_specs = [pl.BlockSpec((qb, wq), lambda g, b, i: (b * nblk + i, g))]
    in_specs += [pl.BlockSpec((kb, wq), k_map(c, 1)) for c in range(NA_KEY_CHUNKS)]
    in_specs += [pl.BlockSpec((kb, wq), k_map(c, 2)) for c in range(NA_KEY_CHUNKS)]
    in_specs += [pl.BlockSpec((hg, qb, NA_KEY_CHUNKS * kb), lambda g, b, i: (g, 0, 0))]
    vmem = (2 * 2 * qb * wq * 2 + 2 * 2 * NA_KEY_CHUNKS * kb * wq * 2
            + 3 * hg * qb * NA_KEY_CHUNKS * kb * 4 + 6 * qb * NA_KEY_CHUNKS * kb * 4)
    return pl.pallas_call(
        functools.partial(_na_attn_kernel, heads=hg, rows=seq // GRID_W),
        grid=(ncol, batch, nblk),
        in_specs=in_specs,
        out_specs=pl.BlockSpec((qb, wq), lambda g, b, i: (b * nblk + i, g)),
        out_shape=jax.ShapeDtypeStruct((m, nh * HEAD_DIM), BF16),
        scratch_shapes=[pltpu.VMEM((hg, qb, NA_KEY_CHUNKS * kb), F32)],
        compiler_params=_params(("parallel", "parallel", "arbitrary"), vmem),
        name="na_attention",
    )(qkv, *([qkv] * (2 * NA_KEY_CHUNKS)), bias)


def _proj_residual_kernel(a_ref, w_ref, r_ref, o_ref):
    o_ref[...] = r_ref[...] + jnp.dot(a_ref[...], w_ref[...], preferred_element_type=F32)


def proj_residual(a, w, res, *, tm):
    m, k = a.shape
    n = w.shape[1]
    vmem = 2 * tm * k * 2 + 2 * k * n * 2 + 4 * tm * n * 4 + tm * n * 4
    return pl.pallas_call(
        _proj_residual_kernel,
        grid=(m // tm,),
        in_specs=[
            pl.BlockSpec((tm, k), lambda i: (i, 0)),
            pl.BlockSpec((k, n), lambda i: (0, 0)),
            pl.BlockSpec((tm, n), lambda i: (i, 0)),
        ],
        out_specs=pl.BlockSpec((tm, n), lambda i: (i, 0)),
        out_shape=jax.ShapeDtypeStruct((m, n), F32),
        compiler_params=_params(("parallel",), vmem),
        name="na_proj_residual",
    )(a, w, res)


def _ffn_kernel(x_ref, g_ref, w1_ref, w2_ref, gf_ref, *refs, final_norm, n_cast):
    cast_src = refs[:n_cast]
    o_ref = refs[n_cast]
    cast_dst = refs[n_cast + 1:2 * n_cast + 1]
    h_ref = refs[2 * n_cast + 1]
    f = pl.program_id(1)

    def body(first):
        _cast_slabs(cast_src, cast_dst)
        for c in range(x_ref.shape[0] // FFN_ROW_CHUNK):
            rows = slice(c * FFN_ROW_CHUNK, (c + 1) * FFN_ROW_CHUNK)
            if first:
                h_ref[rows, :] = _rms(x_ref[rows, :], g_ref[...]).astype(BF16)
            a = jnp.dot(h_ref[rows, :], w1_ref[...], preferred_element_type=F32)
            a = jnp.square(jnp.maximum(a, 0.0)).astype(BF16)
            y = jnp.dot(a, w2_ref[...], preferred_element_type=F32)
            if first:
                o_ref[rows, :] = x_ref[rows, :] + y
            else:
                o_ref[rows, :] += y

    pl.when(f == 0)(functools.partial(body, True))
    pl.when(f != 0)(functools.partial(body, False))

    if final_norm:
        @pl.when(f == pl.num_programs(1) - 1)
        def _():
            o_ref[...] = _rms(o_ref[...], gf_ref[...])


def ffn(x, g, w1, w2, g_final, *, tm, tf, final_norm, cast=()):
    m, d = x.shape
    dff = w1.shape[1]
    assert tm % FFN_ROW_CHUNK == 0
    n_steps = (m // tm) * (dff // tf)
    vmem = (4 * tm * d * 4 + tm * d * 2 + 2 * 2 * d * tf * 2
            + FFN_ROW_CHUNK * tf * 6 + FFN_ROW_CHUNK * d * 4 + _cast_vmem(cast, n_steps))
    cast_specs = _cast_specs(cast, n_steps, dff // tf)
    return pl.pallas_call(
        functools.partial(_ffn_kernel, final_norm=final_norm, n_cast=len(cast)),
        grid=(m // tm, dff // tf),
        in_specs=[
            pl.BlockSpec((tm, d), _tile_then_next(m // tm, dff // tf)),
            pl.BlockSpec((1, d), lambda i, f: (0, 0)),
            pl.BlockSpec((d, tf), lambda i, f: (0, f)),
            pl.BlockSpec((tf, d), lambda i, f: (f, 0)),
            pl.BlockSpec((1, d), lambda i, f: (0, 0)),
        ] + cast_specs,
        out_specs=[pl.BlockSpec((tm, d), lambda i, f: (i, 0))] + cast_specs,
        out_shape=[jax.ShapeDtypeStruct((m, d), F32)] + [jax.ShapeDtypeStruct(c.shape, BF16) for c in cast],
        scratch_shapes=[pltpu.VMEM((tm, d), BF16)],
        compiler_params=_params(("arbitrary", "arbitrary"), vmem),
        name="ffn_final" if final_norm else "ffn",
    )(x, g.reshape(1, d), w1, w2, g_final.reshape(1, d), *cast)


def _dil_attn_kernel(q_ref, kp_ref, kc_ref, kn_ref, vp_ref, vc_ref, vn_ref, o_ref, lse_ref,
                     *, heads, halo, length):
    qb = q_ref.shape[0]
    sub = BLOCK_Q
    nk = sub + 2 * halo
    m0 = pl.program_id(2) * qb
    qi = lax.broadcasted_iota(jnp.int32, (sub, nk), 0)
    kj = lax.broadcasted_iota(jnp.int32, (sub, nk), 1) - halo
    band = jnp.abs(kj - qi) <= halo
    lane = lax.broadcasted_iota(jnp.int32, (sub, HEAD_DIM), 1)

    def window(prev_ref, cur_ref, next_ref, lo, hi, cols):
        parts = [prev_ref[:, cols]] if lo < 0 else []
        parts.append(cur_ref[max(lo, 0):min(hi, qb), cols])
        if hi > qb:
            parts.append(next_ref[:, cols])
        return parts[0] if len(parts) == 1 else jnp.concatenate(parts, axis=0)

    def scores(u, h):
        rows = slice(u * sub, (u + 1) * sub)
        cols = slice(h * HEAD_DIM, (h + 1) * HEAD_DIM)
        k = window(kp_ref, kc_ref, kn_ref, u * sub - halo, (u + 1) * sub + halo, cols)
        s = lax.dot_general(q_ref[rows, cols], k, (((1,), (1,)), ((), ())), preferred_element_type=F32)
        s = s + masks[u]
        return s, jnp.max(s, axis=-1, keepdims=True)

    def finish(u, h, s, m):
        rows = slice(u * sub, (u + 1) * sub)
        cols = slice(h * HEAD_DIM, (h + 1) * HEAD_DIM)
        v = window(vp_ref, vc_ref, vn_ref, u * sub - halo, (u + 1) * sub + halo, cols)
        p = jnp.exp2(s - m)
        l = jnp.sum(p, axis=-1, keepdims=True)
        o = jnp.dot(p.astype(BF16), v, preferred_element_type=F32)
        o_ref[rows, cols] = (o / l).astype(o_ref.dtype)
        return m * LN2 + jnp.log(l)

    masks = []
    for u in range(qb // sub):
        kpos = kj + (m0 + u * sub)
        masks.append(jnp.where(band & (kpos >= 0) & (kpos < length), 0.0, NEG).astype(F32))

    units = [(u, h) for u in range(qb // sub) for h in range(heads)]
    cur = scores(*units[0])
    lse_tile = None
    for n, (u, h) in enumerate(units):
        nxt = scores(*units[n + 1]) if n + 1 < len(units) else None
        lse = finish(u, h, *cur)
        lse_tile = jnp.where(lane == h, lse, jnp.zeros((sub, HEAD_DIM), F32) if h == 0 else lse_tile)
        if h == heads - 1:
            lse_ref[u * sub:(u + 1) * sub, :] = lse_tile
        cur = nxt


def dil_attention(qkv, *, group, halo):
    _, batch, dil, length, wq = qkv.shape
    heads = wq // HEAD_DIM
    qb = min(DIL_STEP_ROWS, length)
    assert BLOCK_Q % halo == 0 and qb % BLOCK_Q == 0 and length % qb == 0
    nblk = length // qb
    hpb = length // halo
    per = qb // halo

    def cur_map(sec):
        return lambda b, r, i: (sec, b, r, i, 0)

    def prev_map(sec):
        return lambda b, r, i: (sec, b, r, jnp.maximum(i * per - 1, 0), 0)

    def next_map(sec):
        return lambda b, r, i: (sec, b, r, jnp.minimum((i + 1) * per, hpb - 1), 0)

    def spec(rows, index_map):
        return pl.BlockSpec((None, None, None, rows, wq), index_map)

    in_specs = [spec(qb, cur_map(0))]
    for sec in (1, 2):
        in_specs += [spec(halo, prev_map(sec)), spec(qb, cur_map(sec)), spec(halo, next_map(sec))]
    vmem = (2 * (2 * qb * wq * 2 + 2 * (qb + 2 * halo) * wq * 2 + qb * HEAD_DIM * 4)
            + 8 * BLOCK_Q * (BLOCK_Q + 2 * halo) * 4)
    return pl.pallas_call(
        functools.partial(_dil_attn_kernel, heads=heads, halo=halo, length=length),
        grid=(batch, dil, nblk),
        in_specs=in_specs,
        out_specs=[pl.BlockSpec((None, None, qb, wq), lambda b, r, i: (b, r, i, 0)),
                   pl.BlockSpec((None, None, qb, HEAD_DIM), lambda b, r, i: (b, r, i, 0))],
        out_shape=[jax.ShapeDtypeStruct((batch, dil, length, wq), BF16),
                   jax.ShapeDtypeStruct((batch, dil, length, HEAD_DIM), F32)],
        compiler_params=_params(("parallel", "parallel", "arbitrary"), vmem),
        name=f"dil_attention_g{group}",
    )(*([qkv] * 7))


def _merge_proj_kernel(*refs, dils, heads):
    g_n = len(dils)
    o_refs = refs[:g_n]
    lse_refs = refs[g_n:2 * g_n]
    w_ref, r_ref, out_ref, a_ref, lt_ref, ot_ref, tt_ref = refs[2 * g_n:]
    tm = a_ref.shape[0]

    def merged(c):
        rows = slice(c * ROW_CHUNK, (c + 1) * ROW_CHUNK)
        lses = []
        for g, dil in enumerate(dils):
            if dil == 1:
                lses.append(lse_refs[g][0, rows, :])
                continue
            n = ROW_CHUNK // dil
            d1, d2 = _stride_split(dil)
            n1 = ROW_CHUNK // d1

            def to_token_order(dst_ref, dst_idx, src):
                if d2 == 1:
                    for r in range(dil):
                        dst_ref[dst_idx, pl.ds(c * ROW_CHUNK + r, n, stride=dil), :] = src(r)
                    return
                for r1 in range(d1):
                    for r2 in range(d2):
                        tt_ref[pl.ds(r1 * n1 + r2, n, stride=d2), :] = src(r1 + d1 * r2)
                for r1 in range(d1):
                    dst_ref[dst_idx, pl.ds(c * ROW_CHUNK + r1, n1, stride=d1), :] = tt_ref[r1 * n1:(r1 + 1) * n1, :]

            to_token_order(lt_ref, g, lambda r: lse_refs[g][r, c * n:(c + 1) * n, :])
            for h in range(heads):
                to_token_order(ot_ref, g * heads + h, lambda r: o_refs[g][
                    r, c * n:(c + 1) * n, h * HEAD_DIM:(h + 1) * HEAD_DIM].astype(F32))
            lses.append(lt_ref[g, rows, :])

        mx = functools.reduce(jnp.maximum, lses)
        es = [jnp.exp(l - mx) for l in lses]
        den = functools.reduce(jnp.add, es)
        wts = [e / den for e in es]
        for h in range(heads):
            sl = slice(h * HEAD_DIM, (h + 1) * HEAD_DIM)
            acc = None
            for g, dil in enumerate(dils):
                o = o_refs[g][0, rows, sl].astype(F32) if dil == 1 else ot_ref[g * heads + h, rows, :]
                term = wts[g][:, h:h + 1] * o
                acc = term if acc is None else acc + term
            a_ref[rows, sl] = acc.astype(BF16)

    merged(0)
    for c in range(tm // ROW_CHUNK):
        rows = slice(c * ROW_CHUNK, (c + 1) * ROW_CHUNK)
        if c + 1 < tm // ROW_CHUNK:
            merged(c + 1)
        out_ref[rows, :] = r_ref[rows, :] + jnp.dot(a_ref[rows, :], w_ref[...], preferred_element_type=F32)


def merge_proj_residual(outs, lses, w, res, *, seq, tm):
    dils = tuple(o.shape[1] for o in outs)
    k = outs[0].shape[-1]
    m, n = res.shape
    g_n = len(outs)
    heads = k // HEAD_DIM
    tpb = seq // tm
    assert all(ROW_CHUNK % (BF16_SUBLANES * dil) == 0 for dil in dils) and tm % ROW_CHUNK == 0

    def stream_spec(dil, width):
        return pl.BlockSpec((None, dil, tm // dil, width), lambda i: (i // tpb, 0, i % tpb, 0))

    vmem = (2 * g_n * tm * k * 2 + 2 * g_n * tm * HEAD_DIM * 4 + 2 * k * n * 2 + 4 * tm * n * 4
            + tm * k * 2 + tm * n * 4 + g_n * tm * HEAD_DIM * 4 + g_n * tm * k * 4)
    return pl.pallas_call(
        functools.partial(_merge_proj_kernel, dils=dils, heads=heads),
        grid=(m // tm,),
        in_specs=([stream_spec(dil, k) for dil in dils]
                  + [stream_spec(dil, HEAD_DIM) for dil in dils]
                  + [pl.BlockSpec((k, n), lambda i: (0, 0)),
                     pl.BlockSpec((tm, n), lambda i: (i, 0))]),
        out_specs=pl.BlockSpec((tm, n), lambda i: (i, 0)),
        out_shape=jax.ShapeDtypeStruct((m, n), F32),
        scratch_shapes=[pltpu.VMEM((tm, k), BF16),
                        pltpu.VMEM((g_n, tm, HEAD_DIM), F32),
                        pltpu.VMEM((g_n * heads, tm, HEAD_DIM), F32),
                        pltpu.VMEM((ROW_CHUNK, HEAD_DIM), F32)],
        compiler_params=_params(("parallel",), vmem),
        name="dil_merge_proj_residual",
    )(*outs, *lses, w, res)


def kernel(x, na_norm, na_wqkv, na_rpb, na_wo, ffn0_norm, ffn0_w1, ffn0_w2, dil_norm, dil_wqkv, dil_wo,
           ffn1_norm, ffn1_w1, ffn1_w2, final_norm):
    batch, seq, d = x.shape
    m = batch * seq
    rows = seq // GRID_W
    na_heads = na_rpb.shape[0]
    assert rows >= NA_KEY_ROWS and rows % NA_BLOCK_ROWS == 0 and rows >= NA_WIN_ROWS
    assert NA_KEY_ROWS >= NA_BLOCK_ROWS + NA_WIN_ROWS - 1

    xf = x.reshape(m, d)
    bf = lambda w: w.astype(BF16)

    qw = na_heads * HEAD_DIM
    col_scale = jnp.concatenate([jnp.full((qw,), Q_SCALE, F32), jnp.ones((2 * qw,), F32)])
    qkv, na_wo_b, ffn0_w1_b, ffn0_w2_b = norm_matmul(
        xf, na_norm, bf(na_wqkv), col_scale, tm=min(1024, m), tn=min(1024, qw), cast=(na_wo, ffn0_w1, ffn0_w2))
    bias = _na_bias_table(na_rpb)
    attn = na_attention(qkv, bias, batch=batch, seq=seq, heads_per_step=min(8, na_heads))
    xf = proj_residual(attn, na_wo_b, xf, tm=min(512, m))
    xf, dil_wqkv_b, dil_wo_b, ffn1_w1_b, ffn1_w2_b = ffn(
        xf, ffn0_norm, ffn0_w1_b, ffn0_w2_b, final_norm, tm=min(1024, m), tf=512, final_norm=False,
        cast=(dil_wqkv, dil_wo, ffn1_w1, ffn1_w2))

    rot = _rope_tables(seq)
    dils = tuple(dil for _, dil in DIL_GROUPS)
    qkvs = norm_matmul_rope(xf, dil_norm, dil_wqkv_b, rot, batch=batch, seq=seq, tm=min(1024, seq), dils=dils)
    outs, lses = [], []
    for gi, (window, dil) in enumerate(DIL_GROUPS):
        o, lse = dil_attention(qkvs[gi], group=gi, halo=(window // 2) // dil)
        outs.append(o)
        lses.append(lse)
    xf = merge_proj_residual(outs, lses, dil_wo_b, xf, seq=seq, tm=min(512, seq))
    xf, = ffn(xf, ffn1_norm, ffn1_w1_b, ffn1_w2_b, final_norm, tm=min(1024, m), tf=512, final_norm=True)
    return xf.reshape(batch, seq, d)
```

```python
import functools
import math

import numpy as np
import jax
import jax.numpy as jnp
from jax import lax
from jax.experimental import pallas as pl
from jax.experimental.pallas import tpu as pltpu

GRID_W = 64
HEAD_DIM = 128
NA_WIN_ROWS = 8
NA_WIN_COLS = 16
DIL_GROUPS = ((128, 1), (512, 4), (2048, 16))
ROPE_THETA = 500000.0
ROPE_DIM = HEAD_DIM // 4
BLOCK_Q = 128
EPS = 1e-6
NEG = -1e30

LOG2E = math.log2(math.e)
LN2 = math.log(2.0)
Q_SCALE = HEAD_DIM ** -0.5 * LOG2E

NA_BLOCK_ROWS = 4
NA_KEY_BACK = NA_WIN_ROWS // 2
NA_KEY_ROWS = 12
NA_KEY_CHUNKS = 3

DIL_STEP_ROWS = 512
ROW_CHUNK = 256
FFN_ROW_CHUNK = 512
BF16_SUBLANES = 16
F32_SUBLANES = 8
LANES = 128

V7X_VMEM_BYTES = 64 * 1024 * 1024
VMEM_CAP_BYTES = V7X_VMEM_BYTES - 8 * 1024 * 1024

F32 = jnp.float32
BF16 = jnp.bfloat16


def _params(semantics, vmem_bytes):
    limit = min(int(vmem_bytes * 1.25) + (4 << 20), VMEM_CAP_BYTES)
    return pltpu.CompilerParams(dimension_semantics=semantics, vmem_limit_bytes=limit)


def _rms(x, g):
    return x * lax.rsqrt(jnp.mean(x * x, axis=-1, keepdims=True) + EPS) * g


def _tile_then_next(n_tiles, n_inner):
    switch = max(n_inner // 2, 1)
    return lambda i, j: (jnp.minimum(i + jnp.where(j >= switch, 1, 0), n_tiles - 1), 0)


def _cast_rows(n_rows, n_steps):
    for rows in range(BF16_SUBLANES, n_rows + 1, BF16_SUBLANES):
        if n_rows % rows == 0 and n_rows // rows <= n_steps:
            return rows
    raise ValueError((n_rows, n_steps))


def _cast_specs(weights, n_steps, n_inner=None):
    specs = []
    for w in weights:
        rows = _cast_rows(w.shape[0], n_steps)
        last = w.shape[0] // rows - 1
        if n_inner is None:
            index = lambda i, last=last: (jnp.minimum(i, last), 0)
        else:
            index = lambda i, j, last=last: (jnp.minimum(i * n_inner + j, last), 0)
        specs.append(pl.BlockSpec((rows, w.shape[1]), index))
    return specs


def _cast_vmem(weights, n_steps):
    return sum(2 * _cast_rows(w.shape[0], n_steps) * w.shape[1] * 6 for w in weights)


def _cast_slabs(src_refs, dst_refs):
    for src, dst in zip(src_refs, dst_refs):
        dst[...] = src[...].astype(BF16)


def _norm_matmul_kernel(x_ref, g_ref, w_ref, cs_ref, *refs, n_cast):
    cast_src = refs[:n_cast]
    o_ref = refs[n_cast]
    cast_dst = refs[n_cast + 1:2 * n_cast + 1]
    h_ref = refs[2 * n_cast + 1]

    def body(with_norm):
        _cast_slabs(cast_src, cast_dst)
        for c in range(x_ref.shape[0] // ROW_CHUNK):
            rows = slice(c * ROW_CHUNK, (c + 1) * ROW_CHUNK)
            if with_norm:
                h_ref[rows, :] = _rms(x_ref[rows, :], g_ref[...]).astype(BF16)
            acc = jnp.dot(h_ref[rows, :], w_ref[...], preferred_element_type=F32)
            o_ref[rows, :] = (acc * cs_ref[...]).astype(o_ref.dtype)

    pl.when(pl.program_id(1) == 0)(functools.partial(body, True))
    pl.when(pl.program_id(1) != 0)(functools.partial(body, False))


def norm_matmul(x, g, w, col_scale, *, tm, tn, cast=()):
    m, k = x.shape
    n = w.shape[1]
    n_steps = (m // tm) * (n // tn)
    vmem = (2 * tm * k * 4 + tm * k * 2 + 2 * k * tn * 2 + 2 * tm * tn * 2 + tm * tn * 4
            + _cast_vmem(cast, n_steps))
    cast_specs = _cast_specs(cast, n_steps, n // tn)
    return pl.pallas_call(
        functools.partial(_norm_matmul_kernel, n_cast=len(cast)),
        grid=(m // tm, n // tn),
        in_specs=[
            pl.BlockSpec((tm, k), _tile_then_next(m // tm, n // tn)),
            pl.BlockSpec((1, k), lambda i, j: (0, 0)),
            pl.BlockSpec((k, tn), lambda i, j: (0, j)),
            pl.BlockSpec((1, tn), lambda i, j: (0, j)),
        ] + cast_specs,
        out_specs=[pl.BlockSpec((tm, tn), lambda i, j: (i, j))] + cast_specs,
        out_shape=[jax.ShapeDtypeStruct((m, n), BF16)] + [jax.ShapeDtypeStruct(c.shape, BF16) for c in cast],
        scratch_shapes=[pltpu.VMEM((tm, k), BF16)],
        compiler_params=_params(("arbitrary", "arbitrary"), vmem),
        name="norm_qkv_na",
    )(x, g.reshape(1, k), w, col_scale.reshape(1, n), *cast)


def _norm_matmul_rope_kernel(x_ref, g_ref, w_ref, rot_ref, *refs, dils):
    o_refs = refs[:len(dils)]
    h_ref, s_ref, t_ref = refs[len(dils):]
    j = pl.program_id(1)
    tm = x_ref.shape[0]
    heads = s_ref.shape[0]
    half = ROPE_DIM // 2

    def body(o_ref, dil, rotate, with_norm=False):
        n = ROW_CHUNK // dil
        d1, d2 = _stride_split(dil)
        n1 = ROW_CHUNK // d1
        scale = jnp.where(j % 3 == 0, Q_SCALE, 1.0).astype(F32)
        for c in range(tm // ROW_CHUNK):
            rows = slice(c * ROW_CHUNK, (c + 1) * ROW_CHUNK)
            if with_norm:
                h_ref[rows, :] = _rms(x_ref[rows, :], g_ref[...]).astype(BF16)
            acc = jnp.dot(h_ref[rows, :], w_ref[...], preferred_element_type=F32)
            if rotate:
                cos = rot_ref[0, rows, :] * scale
                sin_hi = rot_ref[1, rows, :] * scale
                sin_lo = rot_ref[2, rows, :] * scale
            for h in range(heads):
                cols = slice(h * HEAD_DIM, (h + 1) * HEAD_DIM)
                a = acc[:, cols]
                if rotate:
                    a = (a * cos + pltpu.roll(a, half, 1) * sin_hi
                         + pltpu.roll(a, HEAD_DIM - half, 1) * sin_lo)
                if dil == 1:
                    o_ref[0, rows, cols] = a.astype(BF16)
                    continue
                s_ref[h, rows, :] = a
                if d2 == 1:
                    for r in range(dil):
                        o_ref[r, c * n:(c + 1) * n, cols] = (
                            s_ref[h, pl.ds(c * ROW_CHUNK + r, n, stride=dil), :].astype(BF16))
                    continue
                for r1 in range(d1):
                    t_ref[h, r1 * n1:(r1 + 1) * n1, :] = s_ref[h, pl.ds(c * ROW_CHUNK + r1, n1, stride=d1), :]
                for r1 in range(d1):
                    for r2 in range(d2):
                        o_ref[r1 + d1 * r2, c * n:(c + 1) * n, cols] = (
                            t_ref[h, pl.ds(r1 * n1 + r2, n, stride=d2), :].astype(BF16))

    pl.when(j == 0)(functools.partial(body, o_refs[0], dils[0], True, with_norm=True))
    for g, dil in enumerate(dils):
        pl.when((j // 3 == g) & (j % 3 != 2) & (j != 0))(functools.partial(body, o_refs[g], dil, True))
        pl.when((j // 3 == g) & (j % 3 == 2))(functools.partial(body, o_refs[g], dil, False))


def _stride_split(dil):
    if dil <= F32_SUBLANES:
        return dil, 1
    d1 = F32_SUBLANES // 2
    assert dil % d1 == 0 and dil // d1 <= F32_SUBLANES
    return d1, dil // d1


def _rope_tables(t):
    half = ROPE_DIM // 2
    pos = jnp.arange(t, dtype=F32)
    inv_freq = ROPE_THETA ** (-jnp.arange(0, ROPE_DIM, 2, dtype=F32) / ROPE_DIM)
    ang = pos[:, None] * inv_freq[None, :]
    cos = jnp.cos(ang)
    sin = jnp.sin(ang)
    rest = HEAD_DIM - ROPE_DIM
    one = jnp.ones((t, rest), F32)
    zero = jnp.zeros((t, rest), F32)
    zh = jnp.zeros((t, half), F32)
    return jnp.stack([jnp.concatenate([cos, cos, one], axis=-1),
                      jnp.concatenate([zh, sin, zero], axis=-1),
                      jnp.concatenate([-sin, zh, zero], axis=-1)], axis=0)


def norm_matmul_rope(x, g, w, rot, *, batch, seq, tm, dils):
    m, k = x.shape
    n = w.shape[1]
    tn = n // (3 * len(dils))
    heads = tn // HEAD_DIM
    tpb = seq // tm
    assert all(ROW_CHUNK % (BF16_SUBLANES * dil) == 0 for dil in dils) and tm % ROW_CHUNK == 0

    def out_map(g):
        return lambda i, j: (jnp.clip(j - 3 * g, 0, 2), i // tpb, 0, i % tpb, 0)

    vmem = (2 * tm * k * 4 + tm * k * 2 + 2 * k * tn * 2 + 2 * len(dils) * tm * tn * 2 + tm * tn * 4
            + 2 * ROW_CHUNK * tn * 4 + 2 * 3 * tm * HEAD_DIM * 4 + heads * ROW_CHUNK * HEAD_DIM * 4)
    return pl.pallas_call(
        functools.partial(_norm_matmul_rope_kernel, dils=dils),
        grid=(m // tm, n // tn),
        in_specs=[
            pl.BlockSpec((tm, k), _tile_then_next(m // tm, n // tn)),
            pl.BlockSpec((1, k), lambda i, j: (0, 0)),
            pl.BlockSpec((k, tn), lambda i, j: (0, j)),
            pl.BlockSpec((3, tm, HEAD_DIM), lambda i, j: (0, i % tpb, 0)),
        ],
        out_specs=[pl.BlockSpec((None, None, dil, tm // dil, tn), out_map(g)) for g, dil in enumerate(dils)],
        out_shape=[jax.ShapeDtypeStruct((3, batch, dil, seq // dil, tn), BF16) for dil in dils],
        scratch_shapes=[pltpu.VMEM((tm, k), BF16), pltpu.VMEM((heads, tm, HEAD_DIM), F32),
                        pltpu.VMEM((heads, ROW_CHUNK, HEAD_DIM), F32)],
        compiler_params=_params(("arbitrary", "arbitrary"), vmem),
        name="norm_qkv_dil_rope",
    )(x, g.reshape(1, k), w, rot)


def _na_bias_table(rpb):
    nh, n_row_off, n_col_off = rpb.shape
    assert 2 * GRID_W == LANES and NA_KEY_ROWS % 2 == 0 and n_col_off <= LANES
    rpb_lanes = jnp.pad(rpb, ((0, 0), (0, 0), (0, LANES - n_col_off)))
    return pl.pallas_call(
        _na_bias_kernel,
        grid=(nh,),
        in_specs=[pl.BlockSpec((None, n_row_off, LANES), lambda h: (h, 0, 0))],
        out_specs=pl.BlockSpec((None, NA_BLOCK_ROWS * GRID_W, NA_KEY_ROWS * GRID_W), lambda h: (h, 0, 0)),
        out_shape=jax.ShapeDtypeStruct((nh, NA_BLOCK_ROWS * GRID_W, NA_KEY_ROWS * GRID_W), F32),
        compiler_params=_params(("parallel",), 4 * NA_BLOCK_ROWS * GRID_W * NA_KEY_ROWS * GRID_W * 4),
        name="na_bias_table",
    )(rpb_lanes)


def _na_bias_kernel(rpb_ref, o_ref):
    w = GRID_W
    qc = lax.broadcasted_iota(jnp.int32, (w, LANES), 0)
    kc = lax.broadcasted_iota(jnp.int32, (w, LANES), 1)
    start = jnp.clip(qc - NA_WIN_COLS // 2, 0, w - NA_WIN_COLS)
    in_window = (kc >= start) & (kc < start + NA_WIN_COLS)
    by_col = []
    for a in range(rpb_ref.shape[0]):
        row = jnp.broadcast_to(rpb_ref[a:a + 1, :], (w, LANES))
        shifted = pltpu.roll(row, LANES - (NA_WIN_COLS - 1), 1, stride=1, stride_axis=0)
        by_col.append(jnp.where(in_window, shifted * LOG2E, NEG))
    for rr in range(NA_BLOCK_ROWS):
        for kk in range(0, NA_KEY_ROWS, 2):
            a = kk - NA_KEY_BACK - rr + NA_WIN_ROWS - 1
            assert 0 <= a and a + 1 < rpb_ref.shape[0]
            pair = jnp.where(kc < w, by_col[a], pltpu.roll(by_col[a + 1], w, 1))
            o_ref[rr * w:(rr + 1) * w, kk * w:(kk + 2) * w] = pair


def _na_attn_kernel(q_ref, *refs, heads, rows):
    k_refs = refs[:NA_KEY_CHUNKS]
    v_refs = refs[NA_KEY_CHUNKS:2 * NA_KEY_CHUNKS]
    b_ref, o_ref, bm_ref = refs[2 * NA_KEY_CHUNKS:]
    qb = q_ref.shape[0]
    ck = k_refs[0].shape[0]
    kh = min(NA_WIN_ROWS, rows)
    i = pl.program_id(2)

    def clipped(blk):
        r0 = blk * NA_BLOCK_ROWS
        return (r0 < kh // 2) | (r0 + NA_BLOCK_ROWS - 1 - kh // 2 > rows - kh)

    @pl.when((i == 0) | clipped(i) | clipped(i - 1))
    def _():
        r0 = i * NA_BLOCK_ROWS
        q_row = r0 + lax.broadcasted_iota(jnp.int32, (qb, ck), 0) // GRID_W
        lo = jnp.clip(q_row - kh // 2, 0, rows - kh)
        for c in range(NA_KEY_CHUNKS):
            k_row = (r0 - NA_KEY_BACK + c * (ck // GRID_W)
                     + lax.broadcasted_iota(jnp.int32, (qb, ck), 1) // GRID_W)
            row_mask = jnp.where((k_row >= lo) & (k_row < lo + kh), 0.0, NEG).astype(F32)
            for h in range(heads):
                bm_ref[h, :, c * ck:(c + 1) * ck] = b_ref[h, :, c * ck:(c + 1) * ck] + row_mask

    def scores(h):
        sl = slice(h * HEAD_DIM, (h + 1) * HEAD_DIM)
        q = q_ref[:, sl]
        s = []
        for c, k_ref in enumerate(k_refs):
            sc = lax.dot_general(q, k_ref[:, sl], (((1,), (1,)), ((), ())), preferred_element_type=F32)
            s.append(sc + bm_ref[h, :, c * ck:(c + 1) * ck])
        return s, jnp.max(functools.reduce(jnp.maximum, s), axis=-1, keepdims=True)

    def finish(h, s, m):
        sl = slice(h * HEAD_DIM, (h + 1) * HEAD_DIM)
        p = [jnp.exp2(sc - m) for sc in s]
        l = jnp.sum(functools.reduce(jnp.add, p), axis=-1, keepdims=True)
        o = functools.reduce(jnp.add, [
            jnp.dot(pc.astype(BF16), v_ref[:, sl], preferred_element_type=F32)
            for pc, v_ref in zip(p, v_refs)])
        o_ref[:, sl] = (o / l).astype(o_ref.dtype)

    cur = scores(0)
    for h in range(heads):
        nxt = scores(h + 1) if h + 1 < heads else None
        finish(h, *cur)
        cur = nxt


def na_attention(qkv, bias, *, batch, seq, heads_per_step):
    m = qkv.shape[0]
    nh = bias.shape[0]
    hg = heads_per_step
    wq = hg * HEAD_DIM
    qb = NA_BLOCK_ROWS * GRID_W
    kb = NA_KEY_ROWS * GRID_W // NA_KEY_CHUNKS
    assert qb % kb == 0 and (NA_KEY_BACK * GRID_W) % kb == 0
    nblk = seq // qb
    kpb = seq // kb
    ratio = qb // kb
    back = NA_KEY_BACK * GRID_W // kb
    ncol = nh // hg

    def k_map(c, sec):
        def index(g, b, i):
            return (b * kpb + jnp.clip(i * ratio - back + c, 0, kpb - 1), sec * ncol + g)
        return index

    in_specs = [pl.BlockSpec((qb, wq), lambda g, b, i: (b * nblk + i, g))]
    in_specs += [pl.BlockSpec((kb, wq), k_map(c, 1)) for c in range(NA_KEY_CHUNKS)]
    in_specs += [pl.BlockSpec((kb, wq), k_map(c, 2)) for c in range(NA_KEY_CHUNKS)]
    in_specs += [pl.BlockSpec((hg, qb, NA_KEY_CHUNKS * kb), lambda g, b, i: (g, 0, 0))]
    vmem = (2 * 2 * qb * wq * 2 + 2 * 2 * NA_KEY_CHUNKS * kb * wq * 2
            + 3 * hg * qb * NA_KEY_CHUNKS * kb * 4 + 6 * qb * NA_KEY_CHUNKS * kb * 4)
    return pl.pallas_call(
        functools.partial(_na_attn_kernel, heads=hg, rows=seq // GRID_W),
        grid=(ncol, batch, nblk),
        in_specs=in_specs,
        out_specs=pl.BlockSpec((qb, wq), lambda g, b, i: (b * nblk + i, g)),
        out_shape=jax.ShapeDtypeStruct((m, nh * HEAD_DIM), BF16),
        scratch_shapes=[pltpu.VMEM((hg, qb, NA_KEY_CHUNKS * kb), F32)],
        compiler_params=_params(("parallel", "parallel", "arbitrary"), vmem),
        name="na_attention",
    )(qkv, *([qkv] * (2 * NA_KEY_CHUNKS)), bias)


def _proj_residual_kernel(a_ref, w_ref, r_ref, *refs, n_cast):
    cast_src = refs[:n_cast]
    o_ref = refs[n_cast]
    cast_dst = refs[n_cast + 1:]
    _cast_slabs(cast_src, cast_dst)
    o_ref[...] = r_ref[...] + jnp.dot(a_ref[...], w_ref[...], preferred_element_type=F32)


def proj_residual(a, w, res, *, tm, cast=()):
    m, k = a.shape
    n = w.shape[1]
    vmem = 2 * tm * k * 2 + 2 * k * n * 2 + 4 * tm * n * 4 + tm * n * 4 + _cast_vmem(cast, m // tm)
    cast_specs = _cast_specs(cast, m // tm)
    return pl.pallas_call(
        functools.partial(_proj_residual_kernel, n_cast=len(cast)),
        grid=(m // tm,),
        in_specs=[
            pl.BlockSpec((tm, k), lambda i: (i, 0)),
            pl.BlockSpec((k, n), lambda i: (0, 0)),
            pl.BlockSpec((tm, n), lambda i: (i, 0)),
        ] + cast_specs,
        out_specs=[pl.BlockSpec((tm, n), lambda i: (i, 0))] + cast_specs,
        out_shape=[jax.ShapeDtypeStruct((m, n), F32)] + [jax.ShapeDtypeStruct(c.shape, BF16) for c in cast],
        compiler_params=_params(("arbitrary",), vmem),
        name="na_proj_residual",
    )(a, w, res, *cast)


def _ffn_kernel(x_ref, g_ref, w1_ref, w2_ref, gf_ref, *refs, final_norm, n_cast):
    cast_src = refs[:n_cast]
    o_ref = refs[n_cast]
    cast_dst = refs[n_cast + 1:2 * n_cast + 1]
    h_ref = refs[2 * n_cast + 1]
    f = pl.program_id(1)

    def body(first):
        _cast_slabs(cast_src, cast_dst)
        for c in range(x_ref.shape[0] // FFN_ROW_CHUNK):
            rows = slice(c * FFN_ROW_CHUNK, (c + 1) * FFN_ROW_CHUNK)
            if first:
                h_ref[rows, :] = _rms(x_ref[rows, :], g_ref[...]).astype(BF16)
            a = jnp.dot(h_ref[rows, :], w1_ref[...], preferred_element_type=F32)
            a = jnp.square(jnp.maximum(a, 0.0)).astype(BF16)
            y = jnp.dot(a, w2_ref[...], preferred_element_type=F32)
            if first:
                o_ref[rows, :] = x_ref[rows, :] + y
            else:
                o_ref[rows, :] += y

    pl.when(f == 0)(functools.partial(body, True))
    pl.when(f != 0)(functools.partial(body, False))

    if final_norm:
        @pl.when(f == pl.num_programs(1) - 1)
        def _():
            o_ref[...] = _rms(o_ref[...], gf_ref[...])


def ffn(x, g, w1, w2, g_final, *, tm, tf, final_norm, cast=()):
    m, d = x.shape
    dff = w1.shape[1]
    assert tm % FFN_ROW_CHUNK == 0
    n_steps = (m // tm) * (dff // tf)
    vmem = (4 * tm * d * 4 + tm * d * 2 + 2 * 2 * d * tf * 2
            + FFN_ROW_CHUNK * tf * 6 + FFN_ROW_CHUNK * d * 4 + _cast_vmem(cast, n_steps))
    cast_specs = _cast_specs(cast, n_steps, dff // tf)
    return pl.pallas_call(
        functools.partial(_ffn_kernel, final_norm=final_norm, n_cast=len(cast)),
        grid=(m // tm, dff // tf),
        in_specs=[
            pl.BlockSpec((tm, d), _tile_then_next(m // tm, dff // tf)),
            pl.BlockSpec((1, d), lambda i, f: (0, 0)),
            pl.BlockSpec((d, tf), lambda i, f: (0, f)),
            pl.BlockSpec((tf, d), lambda i, f: (f, 0)),
            pl.BlockSpec((1, d), lambda i, f: (0, 0)),
        ] + cast_specs,
        out_specs=[pl.BlockSpec((tm, d), lambda i, f: (i, 0))] + cast_specs,
        out_shape=[jax.ShapeDtypeStruct((m, d), F32)] + [jax.ShapeDtypeStruct(c.shape, BF16) for c in cast],
        scratch_shapes=[pltpu.VMEM((tm, d), BF16)],
        compiler_params=_params(("arbitrary", "arbitrary"), vmem),
        name="ffn_final" if final_norm else "ffn",
    )(x, g.reshape(1, d), w1, w2, g_final.reshape(1, d), *cast)


def _dil_attn_kernel(q_ref, kp_ref, kc_ref, kn_ref, vp_ref, vc_ref, vn_ref, o_ref, lse_ref,
                     *, heads, halo, length):
    qb = q_ref.shape[0]
    sub = BLOCK_Q
    nk = sub + 2 * halo
    m0 = pl.program_id(2) * qb
    qi = lax.broadcasted_iota(jnp.int32, (sub, nk), 0)
    kj = lax.broadcasted_iota(jnp.int32, (sub, nk), 1) - halo
    band = jnp.abs(kj - qi) <= halo
    lane = lax.broadcasted_iota(jnp.int32, (sub, HEAD_DIM), 1)

    def window(prev_ref, cur_ref, next_ref, lo, hi, cols):
        parts = [prev_ref[:, cols]] if lo < 0 else []
        parts.append(cur_ref[max(lo, 0):min(hi, qb), cols])
        if hi > qb:
            parts.append(next_ref[:, cols])
        return parts[0] if len(parts) == 1 else jnp.concatenate(parts, axis=0)

    def scores(u, h):
        rows = slice(u * sub, (u + 1) * sub)
        cols = slice(h * HEAD_DIM, (h + 1) * HEAD_DIM)
        k = window(kp_ref, kc_ref, kn_ref, u * sub - halo, (u + 1) * sub + halo, cols)
        s = lax.dot_general(q_ref[rows, cols], k, (((1,), (1,)), ((), ())), preferred_element_type=F32)
        s = s + masks[u]
        return s, jnp.max(s, axis=-1, keepdims=True)

    def finish(u, h, s, m):
        rows = slice(u * sub, (u + 1) * sub)
        cols = slice(h * HEAD_DIM, (h + 1) * HEAD_DIM)
        v = window(vp_ref, vc_ref, vn_ref, u * sub - halo, (u + 1) * sub + halo, cols)
        p = jnp.exp2(s - m)
        l = jnp.sum(p, axis=-1, keepdims=True)
        o = jnp.dot(p.astype(BF16), v, preferred_element_type=F32)
        o_ref[rows, cols] = (o / l).astype(o_ref.dtype)
        return m * LN2 + jnp.log(l)

    masks = []
    for u in range(qb // sub):
        kpos = kj + (m0 + u * sub)
        masks.append(jnp.where(band & (kpos >= 0) & (kpos < length), 0.0, NEG).astype(F32))

    units = [(u, h) for u in range(qb // sub) for h in range(heads)]
    cur = scores(*units[0])
    lse_tile = None
    for n, (u, h) in enumerate(units):
        nxt = scores(*units[n + 1]) if n + 1 < len(units) else None
        lse = finish(u, h, *cur)
        lse_tile = jnp.where(lane == h, lse, jnp.zeros((sub, HEAD_DIM), F32) if h == 0 else lse_tile)
        if h == heads - 1:
            lse_ref[u * sub:(u + 1) * sub, :] = lse_tile
        cur = nxt


def dil_attention(qkv, *, group, halo):
    _, batch, dil, length, wq = qkv.shape
    heads = wq // HEAD_DIM
    qb = min(DIL_STEP_ROWS, length)
    assert BLOCK_Q % halo == 0 and qb % BLOCK_Q == 0 and length % qb == 0
    nblk = length // qb
    hpb = length // halo
    per = qb // halo

    def cur_map(sec):
        return lambda b, r, i: (sec, b, r, i, 0)

    def prev_map(sec):
        return lambda b, r, i: (sec, b, r, jnp.maximum(i * per - 1, 0), 0)

    def next_map(sec):
        return lambda b, r, i: (sec, b, r, jnp.minimum((i + 1) * per, hpb - 1), 0)

    def spec(rows, index_map):
        return pl.BlockSpec((None, None, None, rows, wq), index_map)

    in_specs = [spec(qb, cur_map(0))]
    for sec in (1, 2):
        in_specs += [spec(halo, prev_map(sec)), spec(qb, cur_map(sec)), spec(halo, next_map(sec))]
    vmem = (2 * (2 * qb * wq * 2 + 2 * (qb + 2 * halo) * wq * 2 + qb * HEAD_DIM * 4)
            + 8 * BLOCK_Q * (BLOCK_Q + 2 * halo) * 4)
    return pl.pallas_call(
        functools.partial(_dil_attn_kernel, heads=heads, halo=halo, length=length),
        grid=(batch, dil, nblk),
        in_specs=in_specs,
        out_specs=[pl.BlockSpec((None, None, qb, wq), lambda b, r, i: (b, r, i, 0)),
                   pl.BlockSpec((None, None, qb, HEAD_DIM), lambda b, r, i: (b, r, i, 0))],
        out_shape=[jax.ShapeDtypeStruct((batch, dil, length, wq), BF16),
                   jax.ShapeDtypeStruct((batch, dil, length, HEAD_DIM), F32)],
        compiler_params=_params(("parallel", "parallel", "arbitrary"), vmem),
        name=f"dil_attention_g{group}",
    )(*([qkv] * 7))


def _merge_proj_kernel(*refs, dils, heads):
    g_n = len(dils)
    o_refs = refs[:g_n]
    lse_refs = refs[g_n:2 * g_n]
    w_ref, r_ref, out_ref, a_ref, lt_ref, ot_ref, tt_ref = refs[2 * g_n:]
    tm = a_ref.shape[0]

    def merged(c):
        rows = slice(c * ROW_CHUNK, (c + 1) * ROW_CHUNK)
        lses = []
        for g, dil in enumerate(dils):
            if dil == 1:
                lses.append(lse_refs[g][0, rows, :])
                continue
            n = ROW_CHUNK // dil
            d1, d2 = _stride_split(dil)
            n1 = ROW_CHUNK // d1

            def to_token_order(dst_ref, dst_idx, src):
                if d2 == 1:
                    for r in range(dil):
                        dst_ref[dst_idx, pl.ds(c * ROW_CHUNK + r, n, stride=dil), :] = src(r)
                    return
                for r1 in range(d1):
                    for r2 in range(d2):
                        tt_ref[pl.ds(r1 * n1 + r2, n, stride=d2), :] = src(r1 + d1 * r2)
                for r1 in range(d1):
                    dst_ref[dst_idx, pl.ds(c * ROW_CHUNK + r1, n1, stride=d1), :] = tt_ref[r1 * n1:(r1 + 1) * n1, :]

            to_token_order(lt_ref, g, lambda r: lse_refs[g][r, c * n:(c + 1) * n, :])
            for h in range(heads):
                to_token_order(ot_ref, g * heads + h, lambda r: o_refs[g][
                    r, c * n:(c + 1) * n, h * HEAD_DIM:(h + 1) * HEAD_DIM].astype(F32))
            lses.append(lt_ref[g, rows, :])

        mx = functools.reduce(jnp.maximum, lses)
        es = [jnp.exp(l - mx) for l in lses]
        den = functools.reduce(jnp.add, es)
        wts = [e / den for e in es]
        for h in range(heads):
            sl = slice(h * HEAD_DIM, (h + 1) * HEAD_DIM)
            acc = None
            for g, dil in enumerate(dils):
                o = o_refs[g][0, rows, sl].astype(F32) if dil == 1 else ot_ref[g * heads + h, rows, :]
                term = wts[g][:, h:h + 1] * o
                acc = term if acc is None else acc + term
            a_ref[rows, sl] = acc.astype(BF16)

    merged(0)
    for c in range(tm // ROW_CHUNK):
        rows = slice(c * ROW_CHUNK, (c + 1) * ROW_CHUNK)
        if c + 1 < tm // ROW_CHUNK:
            merged(c + 1)
        out_ref[rows, :] = r_ref[rows, :] + jnp.dot(a_ref[rows, :], w_ref[...], preferred_element_type=F32)


def merge_proj_residual(outs, lses, w, res, *, seq, tm):
    dils = tuple(o.shape[1] for o in outs)
    k = outs[0].shape[-1]
    m, n = res.shape
    g_n = len(outs)
    heads = k // HEAD_DIM
    tpb = seq // tm
    assert all(ROW_CHUNK % (BF16_SUBLANES * dil) == 0 for dil in dils) and tm % ROW_CHUNK == 0

    def stream_spec(dil, width):
        return pl.BlockSpec((None, dil, tm // dil, width), lambda i: (i // tpb, 0, i % tpb, 0))

    vmem = (2 * g_n * tm * k * 2 + 2 * g_n * tm * HEAD_DIM * 4 + 2 * k * n * 2 + 4 * tm * n * 4
            + tm * k * 2 + tm * n * 4 + g_n * tm * HEAD_DIM * 4 + g_n * tm * k * 4)
    return pl.pallas_call(
        functools.partial(_merge_proj_kernel, dils=dils, heads=heads),
        grid=(m // tm,),
        in_specs=([stream_spec(dil, k) for dil in dils]
                  + [stream_spec(dil, HEAD_DIM) for dil in dils]
                  + [pl.BlockSpec((k, n), lambda i: (0, 0)),
                     pl.BlockSpec((tm, n), lambda i: (i, 0))]),
        out_specs=pl.BlockSpec((tm, n), lambda i: (i, 0)),
        out_shape=jax.ShapeDtypeStruct((m, n), F32),
        scratch_shapes=[pltpu.VMEM((tm, k), BF16),
                        pltpu.VMEM((g_n, tm, HEAD_DIM), F32),
                        pltpu.VMEM((g_n * heads, tm, HEAD_DIM), F32),
                        pltpu.VMEM((ROW_CHUNK, HEAD_DIM), F32)],
        compiler_params=_params(("parallel",), vmem),
        name="dil_merge_proj_residual",
    )(*outs, *lses, w, res)


def kernel(x, na_norm, na_wqkv, na_rpb, na_wo, ffn0_norm, ffn0_w1, ffn0_w2, dil_norm, dil_wqkv, dil_wo,
           ffn1_norm, ffn1_w1, ffn1_w2, final_norm):
    batch, seq, d = x.shape
    m = batch * seq
    rows = seq // GRID_W
    na_heads = na_rpb.shape[0]
    assert rows >= NA_KEY_ROWS and rows % NA_BLOCK_ROWS == 0 and rows >= NA_WIN_ROWS
    assert NA_KEY_ROWS >= NA_BLOCK_ROWS + NA_WIN_ROWS - 1

    xf = x.reshape(m, d)
    bf = lambda w: w.astype(BF16)

    qw = na_heads * HEAD_DIM
    col_scale = jnp.concatenate([jnp.full((qw,), Q_SCALE, F32), jnp.ones((2 * qw,), F32)])
    qkv, na_wo_b, ffn0_w1_b, ffn0_w2_b, dil_wqkv_b, dil_wo_b = norm_matmul(
        xf, na_norm, bf(na_wqkv), col_scale, tm=min(1024, m), tn=min(1024, qw),
        cast=(na_wo, ffn0_w1, ffn0_w2, dil_wqkv, dil_wo))
    bias = _na_bias_table(na_rpb)
    attn = na_attention(qkv, bias, batch=batch, seq=seq, heads_per_step=min(8, na_heads))
    xf, ffn1_w1_b, ffn1_w2_b = proj_residual(attn, na_wo_b, xf, tm=min(512, m), cast=(ffn1_w1, ffn1_w2))
    xf, = ffn(xf, ffn0_norm, ffn0_w1_b, ffn0_w2_b, final_norm, tm=min(1024, m), tf=512, final_norm=False)

    rot = _rope_tables(seq)
    dils = tuple(dil for _, dil in DIL_GROUPS)
    qkvs = norm_matmul_rope(xf, dil_norm, dil_wqkv_b, rot, batch=batch, seq=seq, tm=min(1024, seq), dils=dils)
    outs, lses = [], []
    for gi, (window, dil) in enumerate(DIL_GROUPS):
        o, lse = dil_attention(qkvs[gi], group=gi, halo=(window // 2) // dil)
        outs.append(o)
        lses.append(lse)
    xf = merge_proj_residual(outs, lses, dil_wo_b, xf, seq=seq, tm=min(512, seq))
    xf, = ffn(xf, ffn1_norm, ffn1_w1_b, ffn1_w2_b, final_norm, tm=min(1024, m), tf=512, final_norm=True)
    return xf.reshape(batch, seq, d)
```

```python
import functools
import math

import numpy as np
import jax
import jax.numpy as jnp
from jax import lax
from jax.experimental import pallas as pl
from jax.experimental.pallas import tpu as pltpu

GRID_W = 64
HEAD_DIM = 128
NA_WIN_ROWS = 8
NA_WIN_COLS = 16
DIL_GROUPS = ((128, 1), (512, 4), (2048, 16))
ROPE_THETA = 500000.0
ROPE_DIM = HEAD_DIM // 4
BLOCK_Q = 128
EPS = 1e-6
NEG = -1e30

LOG2E = math.log2(math.e)
LN2 = math.log(2.0)
Q_SCALE = HEAD_DIM ** -0.5 * LOG2E

NA_BLOCK_ROWS = 4
NA_KEY_BACK = NA_WIN_ROWS // 2
NA_KEY_ROWS = 12
NA_KEY_CHUNKS = 3

DIL_STEP_ROWS = 512
ROW_CHUNK = 256
FFN_ROW_CHUNK = 512
BF16_SUBLANES = 16
F32_SUBLANES = 8
LANES = 128

V7X_VMEM_BYTES = 64 * 1024 * 1024
VMEM_CAP_BYTES = V7X_VMEM_BYTES - 8 * 1024 * 1024

F32 = jnp.float32
BF16 = jnp.bfloat16


def _params(semantics, vmem_bytes):
    limit = min(int(vmem_bytes * 1.25) + (4 << 20), VMEM_CAP_BYTES)
    return pltpu.CompilerParams(dimension_semantics=semantics, vmem_limit_bytes=limit)


def _rms(x, g):
    return x * lax.rsqrt(jnp.mean(x * x, axis=-1, keepdims=True) + EPS) * g


def _tile_then_next(n_tiles, n_inner):
    switch = max(n_inner // 2, 1)
    return lambda i, j: (jnp.minimum(i + jnp.where(j >= switch, 1, 0), n_tiles - 1), 0)


def _cast_rows(n_rows, n_steps):
    for rows in range(BF16_SUBLANES, n_rows + 1, BF16_SUBLANES):
        if n_rows % rows == 0 and n_rows // rows <= n_steps:
            return rows
    raise ValueError((n_rows, n_steps))


def _cast_specs(weights, n_steps, n_inner):
    specs = []
    for w in weights:
        rows = _cast_rows(w.shape[0], n_steps)
        last = w.shape[0] // rows - 1
        specs.append(pl.BlockSpec((rows, w.shape[1]), lambda i, j, last=last: (jnp.minimum(i * n_inner + j, last), 0)))
    return specs


def _cast_vmem(weights, n_steps):
    return sum(2 * _cast_rows(w.shape[0], n_steps) * w.shape[1] * 6 for w in weights)


def _cast_slabs(src_refs, dst_refs):
    for src, dst in zip(src_refs, dst_refs):
        dst[...] = src[...].astype(BF16)


def _rope_slab(freq_ref, rot_ref, slab):
    _, rows, _ = rot_ref.shape
    pos = (slab * rows + lax.broadcasted_iota(jnp.int32, (rows, LANES), 0)).astype(F32)
    ang = pos * freq_ref[...]
    lane = lax.broadcasted_iota(jnp.int32, (rows, LANES), 1)
    half = ROPE_DIM // 2
    sin = jnp.sin(ang)
    rot_ref[0] = jnp.cos(ang)
    rot_ref[1] = jnp.where((lane >= half) & (lane < ROPE_DIM), sin, 0.0)
    rot_ref[2] = jnp.where(lane < half, -sin, 0.0)


def _norm_matmul_kernel(x_ref, g_ref, w_ref, cs_ref, *refs, n_cast, rope_slabs):
    cast_src = refs[:n_cast]
    refs = refs[n_cast:]
    if rope_slabs:
        freq_ref, o_ref = refs[:2]
        cast_dst = refs[2:2 + n_cast]
        rot_ref, h_ref = refs[2 + n_cast:]
    else:
        o_ref = refs[0]
        cast_dst = refs[1:1 + n_cast]
        h_ref = refs[1 + n_cast]

    def body(with_norm):
        _cast_slabs(cast_src, cast_dst)
        if rope_slabs:
            step = pl.program_id(0) * pl.num_programs(1) + pl.program_id(1)
            _rope_slab(freq_ref, rot_ref, jnp.minimum(step, rope_slabs - 1))
        for c in range(x_ref.shape[0] // ROW_CHUNK):
            rows = slice(c * ROW_CHUNK, (c + 1) * ROW_CHUNK)
            if with_norm:
                h_ref[rows, :] = _rms(x_ref[rows, :], g_ref[...]).astype(BF16)
            acc = jnp.dot(h_ref[rows, :], w_ref[...], preferred_element_type=F32)
            o_ref[rows, :] = (acc * cs_ref[...]).astype(o_ref.dtype)

    pl.when(pl.program_id(1) == 0)(functools.partial(body, True))
    pl.when(pl.program_id(1) != 0)(functools.partial(body, False))


def norm_matmul(x, g, w, col_scale, *, tm, tn, cast=(), rope_seq=0):
    m, k = x.shape
    n = w.shape[1]
    n_inner = n // tn
    n_steps = (m // tm) * n_inner
    vmem = (2 * tm * k * 4 + tm * k * 2 + 2 * k * tn * 2 + 2 * tm * tn * 2 + tm * tn * 4
            + _cast_vmem(cast, n_steps))
    cast_specs = _cast_specs(cast, n_steps, n_inner)
    in_specs, out_specs, out_shape, args, rope_slabs = [], [], [], [], 0
    if rope_seq:
        rope_rows = _cast_rows(rope_seq, n_steps)
        rope_slabs = rope_seq // rope_rows
        inv_freq = ROPE_THETA ** (-jnp.arange(0, ROPE_DIM, 2, dtype=F32) / ROPE_DIM)
        freq = jnp.concatenate([inv_freq, inv_freq, jnp.zeros((HEAD_DIM - ROPE_DIM,), F32)]).reshape(1, HEAD_DIM)
        in_specs = [pl.BlockSpec((1, HEAD_DIM), lambda i, j: (0, 0))]
        out_specs = [pl.BlockSpec((3, rope_rows, HEAD_DIM),
                                  lambda i, j: (0, jnp.minimum(i * n_inner + j, rope_slabs - 1), 0))]
        out_shape = [jax.ShapeDtypeStruct((3, rope_seq, HEAD_DIM), F32)]
        args = [freq]
        vmem += 2 * 3 * rope_rows * HEAD_DIM * 4 * 3
    return pl.pallas_call(
        functools.partial(_norm_matmul_kernel, n_cast=len(cast), rope_slabs=rope_slabs),
        grid=(m // tm, n_inner),
        in_specs=[
            pl.BlockSpec((tm, k), _tile_then_next(m // tm, n_inner)),
            pl.BlockSpec((1, k), lambda i, j: (0, 0)),
            pl.BlockSpec((k, tn), lambda i, j: (0, j)),
            pl.BlockSpec((1, tn), lambda i, j: (0, j)),
        ] + cast_specs + in_specs,
        out_specs=[pl.BlockSpec((tm, tn), lambda i, j: (i, j))] + cast_specs + out_specs,
        out_shape=([jax.ShapeDtypeStruct((m, n), BF16)] + [jax.ShapeDtypeStruct(c.shape, BF16) for c in cast]
                   + out_shape),
        scratch_shapes=[pltpu.VMEM((tm, k), BF16)],
        compiler_params=_params(("arbitrary", "arbitrary"), vmem),
        name="norm_qkv_na",
    )(x, g.reshape(1, k), w, col_scale.reshape(1, n), *cast, *args)


def _norm_matmul_rope_kernel(x_ref, g_ref, w_ref, rot_ref, *refs, dils):
    o_refs = refs[:len(dils)]
    h_ref, s_ref, t_ref = refs[len(dils):]
    j = pl.program_id(1)
    tm = x_ref.shape[0]
    heads = s_ref.shape[0]
    half = ROPE_DIM // 2

    def body(o_ref, dil, rotate, with_norm=False):
        n = ROW_CHUNK // dil
        d1, d2 = _stride_split(dil)
        n1 = ROW_CHUNK // d1
        scale = jnp.where(j % 3 == 0, Q_SCALE, 1.0).astype(F32)
        for c in range(tm // ROW_CHUNK):
            rows = slice(c * ROW_CHUNK, (c + 1) * ROW_CHUNK)
            if with_norm:
                h_ref[rows, :] = _rms(x_ref[rows, :], g_ref[...]).astype(BF16)
            acc = jnp.dot(h_ref[rows, :], w_ref[...], preferred_element_type=F32)
            if rotate:
                cos = rot_ref[0, rows, :] * scale
                sin_hi = rot_ref[1, rows, :] * scale
                sin_lo = rot_ref[2, rows, :] * scale
            for h in range(heads):
                cols = slice(h * HEAD_DIM, (h + 1) * HEAD_DIM)
                a = acc[:, cols]
                if rotate:
                    a = (a * cos + pltpu.roll(a, half, 1) * sin_hi
                         + pltpu.roll(a, HEAD_DIM - half, 1) * sin_lo)
                if dil == 1:
                    o_ref[0, rows, cols] = a.astype(BF16)
                    continue
                s_ref[h, rows, :] = a
                if d2 == 1:
                    for r in range(dil):
                        o_ref[r, c * n:(c + 1) * n, cols] = (
                            s_ref[h, pl.ds(c * ROW_CHUNK + r, n, stride=dil), :].astype(BF16))
                    continue
                for r1 in range(d1):
                    t_ref[h, r1 * n1:(r1 + 1) * n1, :] = s_ref[h, pl.ds(c * ROW_CHUNK + r1, n1, stride=d1), :]
                for r1 in range(d1):
                    for r2 in range(d2):
                        o_ref[r1 + d1 * r2, c * n:(c + 1) * n, cols] = (
                            t_ref[h, pl.ds(r1 * n1 + r2, n, stride=d2), :].astype(BF16))

    pl.when(j == 0)(functools.partial(body, o_refs[0], dils[0], True, with_norm=True))
    for g, dil in enumerate(dils):
        pl.when((j // 3 == g) & (j % 3 != 2) & (j != 0))(functools.partial(body, o_refs[g], dil, True))
        pl.when((j // 3 == g) & (j % 3 == 2))(functools.partial(body, o_refs[g], dil, False))


def _stride_split(dil):
    if dil <= F32_SUBLANES:
        return dil, 1
    d1 = F32_SUBLANES // 2
    assert dil % d1 == 0 and dil // d1 <= F32_SUBLANES
    return d1, dil // d1


def norm_matmul_rope(x, g, w, rot, *, batch, seq, tm, dils):
    m, k = x.shape
    n = w.shape[1]
    tn = n // (3 * len(dils))
    heads = tn // HEAD_DIM
    tpb = seq // tm
    assert all(ROW_CHUNK % (BF16_SUBLANES * dil) == 0 for dil in dils) and tm % ROW_CHUNK == 0

    def out_map(g):
        return lambda i, j: (jnp.clip(j - 3 * g, 0, 2), i // tpb, 0, i % tpb, 0)

    vmem = (2 * tm * k * 4 + tm * k * 2 + 2 * k * tn * 2 + 2 * len(dils) * tm * tn * 2 + tm * tn * 4
            + 2 * ROW_CHUNK * tn * 4 + 2 * 3 * tm * HEAD_DIM * 4 + heads * ROW_CHUNK * HEAD_DIM * 4)
    return pl.pallas_call(
        functools.partial(_norm_matmul_rope_kernel, dils=dils),
        grid=(m // tm, n // tn),
        in_specs=[
            pl.BlockSpec((tm, k), _tile_then_next(m // tm, n // tn)),
            pl.BlockSpec((1, k), lambda i, j: (0, 0)),
            pl.BlockSpec((k, tn), lambda i, j: (0, j)),
            pl.BlockSpec((3, tm, HEAD_DIM), lambda i, j: (0, i % tpb, 0)),
        ],
        out_specs=[pl.BlockSpec((None, None, dil, tm // dil, tn), out_map(g)) for g, dil in enumerate(dils)],
        out_shape=[jax.ShapeDtypeStruct((3, batch, dil, seq // dil, tn), BF16) for dil in dils],
        scratch_shapes=[pltpu.VMEM((tm, k), BF16), pltpu.VMEM((heads, tm, HEAD_DIM), F32),
                        pltpu.VMEM((heads, ROW_CHUNK, HEAD_DIM), F32)],
        compiler_params=_params(("arbitrary", "arbitrary"), vmem),
        name="norm_qkv_dil_rope",
    )(x, g.reshape(1, k), w, rot)


def _na_bias_table(rpb):
    nh, n_row_off, n_col_off = rpb.shape
    assert 2 * GRID_W == LANES and NA_KEY_ROWS % 2 == 0 and n_col_off <= LANES
    rpb_lanes = jnp.pad(rpb, ((0, 0), (0, 0), (0, LANES - n_col_off)))
    return pl.pallas_call(
        _na_bias_kernel,
        grid=(nh,),
        in_specs=[pl.BlockSpec((None, n_row_off, LANES), lambda h: (h, 0, 0))],
        out_specs=pl.BlockSpec((None, NA_BLOCK_ROWS * GRID_W, NA_KEY_ROWS * GRID_W), lambda h: (h, 0, 0)),
        out_shape=jax.ShapeDtypeStruct((nh, NA_BLOCK_ROWS * GRID_W, NA_KEY_ROWS * GRID_W), F32),
        compiler_params=_params(("parallel",), 4 * NA_BLOCK_ROWS * GRID_W * NA_KEY_ROWS * GRID_W * 4),
        name="na_bias_table",
    )(rpb_lanes)


def _na_bias_kernel(rpb_ref, o_ref):
    w = GRID_W
    qc = lax.broadcasted_iota(jnp.int32, (w, LANES), 0)
    kc = lax.broadcasted_iota(jnp.int32, (w, LANES), 1)
    start = jnp.clip(qc - NA_WIN_COLS // 2, 0, w - NA_WIN_COLS)
    in_window = (kc >= start) & (kc < start + NA_WIN_COLS)
    by_col = []
    for a in range(rpb_ref.shape[0]):
        row = jnp.broadcast_to(rpb_ref[a:a + 1, :], (w, LANES))
        shifted = pltpu.roll(row, LANES - (NA_WIN_COLS - 1), 1, stride=1, stride_axis=0)
        by_col.append(jnp.where(in_window, shifted * LOG2E, NEG))
    for rr in range(NA_BLOCK_ROWS):
        for kk in range(0, NA_KEY_ROWS, 2):
            a = kk - NA_KEY_BACK - rr + NA_WIN_ROWS - 1
            assert 0 <= a and a + 1 < rpb_ref.shape[0]
            pair = jnp.where(kc < w, by_col[a], pltpu.roll(by_col[a + 1], w, 1))
            o_ref[rr * w:(rr + 1) * w, kk * w:(kk + 2) * w] = pair


def _na_attn_kernel(q_ref, *refs, heads, rows):
    k_refs = refs[:NA_KEY_CHUNKS]
    v_refs = refs[NA_KEY_CHUNKS:2 * NA_KEY_CHUNKS]
    b_ref, o_ref, bm_ref = refs[2 * NA_KEY_CHUNKS:]
    qb = q_ref.shape[0]
    ck = k_refs[0].shape[0]
    kh = min(NA_WIN_ROWS, rows)
    i = pl.program_id(2)

    def clipped(blk):
        r0 = blk * NA_BLOCK_ROWS
        return (r0 < kh // 2) | (r0 + NA_BLOCK_ROWS - 1 - kh // 2 > rows - kh)

    @pl.when((i == 0) | clipped(i) | clipped(i - 1))
    def _():
        r0 = i * NA_BLOCK_ROWS
        q_row = r0 + lax.broadcasted_iota(jnp.int32, (qb, ck), 0) // GRID_W
        lo = jnp.clip(q_row - kh // 2, 0, rows - kh)
        for c in range(NA_KEY_CHUNKS):
            k_row = (r0 - NA_KEY_BACK + c * (ck // GRID_W)
                     + lax.broadcasted_iota(jnp.int32, (qb, ck), 1) // GRID_W)
            row_mask = jnp.where((k_row >= lo) & (k_row < lo + kh), 0.0, NEG).astype(F32)
            for h in range(heads):
                bm_ref[h, :, c * ck:(c + 1) * ck] = b_ref[h, :, c * ck:(c + 1) * ck] + row_mask

    def scores(h):
        sl = slice(h * HEAD_DIM, (h + 1) * HEAD_DIM)
        q = q_ref[:, sl]
        s = []
        for c, k_ref in enumerate(k_refs):
            sc = lax.dot_general(q, k_ref[:, sl], (((1,), (1,)), ((), ())), preferred_element_type=F32)
            s.append(sc + bm_ref[h, :, c * ck:(c + 1) * ck])
        return s, jnp.max(functools.reduce(jnp.maximum, s), axis=-1, keepdims=True)

    def finish(h, s, m):
        sl = slice(h * HEAD_DIM, (h + 1) * HEAD_DIM)
        p = [jnp.exp2(sc - m) for sc in s]
        l = jnp.sum(functools.reduce(jnp.add, p), axis=-1, keepdims=True)
        o = functools.reduce(jnp.add, [
            jnp.dot(pc.astype(BF16), v_ref[:, sl], preferred_element_type=F32)
            for pc, v_ref in zip(p, v_refs)])
        o_ref[:, sl] = (o / l).astype(o_ref.dtype)

    cur = scores(0)
    for h in range(heads):
        nxt = scores(h + 1) if h + 1 < heads else None
        finish(h, *cur)
        cur = nxt


def na_attention(qkv, bias, *, batch, seq, heads_per_step):
    m = qkv.shape[0]
    nh = bias.shape[0]
    hg = heads_per_step
    wq = hg * HEAD_DIM
    qb = NA_BLOCK_ROWS * GRID_W
    kb = NA_KEY_ROWS * GRID_W // NA_KEY_CHUNKS
    assert qb % kb == 0 and (NA_KEY_BACK * GRID_W) % kb == 0
    nblk = seq // qb
    kpb = seq // kb
    ratio = qb // kb
    back = NA_KEY_BACK * GRID_W // kb
    ncol = nh // hg

    def k_map(c, sec):
        def index(g, b, i):
            return (b * kpb + jnp.clip(i * ratio - back + c, 0, kpb - 1), sec * ncol + g)
        return index

    in_specs = [pl.BlockSpec((qb, wq), lambda g, b, i: (b * nblk + i, g))]
    in_specs += [pl.BlockSpec((kb, wq), k_map(c, 1)) for c in range(NA_KEY_CHUNKS)]
    in_specs += [pl.BlockSpec((kb, wq), k_map(c, 2)) for c in range(NA_KEY_CHUNKS)]
    in_specs += [pl.BlockSpec((hg, qb, NA_KEY_CHUNKS * kb), lambda g, b, i: (g, 0, 0))]
    vmem = (2 * 2 * qb * wq * 2 + 2 * 2 * NA_KEY_CHUNKS * kb * wq * 2
            + 3 * hg * qb * NA_KEY_CHUNKS * kb * 4 + 6 * qb * NA_KEY_CHUNKS * kb * 4)
    return pl.pallas_call(
        functools.partial(_na_attn_kernel, heads=hg, rows=seq // GRID_W),
        grid=(ncol, batch, nblk),
        in_specs=in_specs,
        out_specs=pl.BlockSpec((qb, wq), lambda g, b, i: (b * nblk + i, g)),
        out_shape=jax.ShapeDtypeStruct((m, nh * HEAD_DIM), BF16),
        scratch_shapes=[pltpu.VMEM((hg, qb, NA_KEY_CHUNKS * kb), F32)],
        compiler_params=_params(("parallel", "parallel", "arbitrary"), vmem),
        name="na_attention",
    )(qkv, *([qkv] * (2 * NA_KEY_CHUNKS)), bias)


def _proj_residual_kernel(a_ref, w_ref, r_ref, o_ref):
    o_ref[...] = r_ref[...] + jnp.dot(a_ref[...], w_ref[...], preferred_element_type=F32)


def proj_residual(a, w, res, *, tm):
    m, k = a.shape
    n = w.shape[1]
    vmem = 2 * tm * k * 2 + 2 * k * n * 2 + 4 * tm * n * 4 + tm * n * 4
    return pl.pallas_call(
        _proj_residual_kernel,
        grid=(m // tm,),
        in_specs=[
            pl.BlockSpec((tm, k), lambda i: (i, 0)),
            pl.BlockSpec((k, n), lambda i: (0, 0)),
            pl.BlockSpec((tm, n), lambda i: (i, 0)),
        ],
        out_specs=pl.BlockSpec((tm, n), lambda i: (i, 0)),
        out_shape=jax.ShapeDtypeStruct((m, n), F32),
        compiler_params=_params(("parallel",), vmem),
        name="na_proj_residual",
    )(a, w, res)


def _ffn_kernel(x_ref, g_ref, w1_ref, w2_ref, gf_ref, *refs, final_norm, n_cast):
    cast_src = refs[:n_cast]
    o_ref = refs[n_cast]
    cast_dst = refs[n_cast + 1:2 * n_cast + 1]
    h_ref = refs[2 * n_cast + 1]
    f = pl.program_id(1)

    def body(first):
        _cast_slabs(cast_src, cast_dst)
        for c in range(x_ref.shape[0] // FFN_ROW_CHUNK):
            rows = slice(c * FFN_ROW_CHUNK, (c + 1) * FFN_ROW_CHUNK)
            if first:
                h_ref[rows, :] = _rms(x_ref[rows, :], g_ref[...]).astype(BF16)
            a = jnp.dot(h_ref[rows, :], w1_ref[...], preferred_element_type=F32)
            a = jnp.square(jnp.maximum(a, 0.0)).astype(BF16)
            y = jnp.dot(a, w2_ref[...], preferred_element_type=F32)
            if first:
                o_ref[rows, :] = x_ref[rows, :] + y
            else:
                o_ref[rows, :] += y

    pl.when(f == 0)(functools.partial(body, True))
    pl.when(f != 0)(functools.partial(body, False))

    if final_norm:
        @pl.when(f == pl.num_programs(1) - 1)
        def _():
            o_ref[...] = _rms(o_ref[...], gf_ref[...])


def ffn(x, g, w1, w2, g_final, *, tm, tf, final_norm, cast=()):
    m, d = x.shape
    dff = w1.shape[1]
    assert tm % FFN_ROW_CHUNK == 0
    n_steps = (m // tm) * (dff // tf)
    vmem = (4 * tm * d * 4 + tm * d * 2 + 2 * 2 * d * tf * 2
            + FFN_ROW_CHUNK * tf * 6 + FFN_ROW_CHUNK * d * 4 + _cast_vmem(cast, n_steps))
    cast_specs = _cast_specs(cast, n_steps, dff // tf)
    return pl.pallas_call(
        functools.partial(_ffn_kernel, final_norm=final_norm, n_cast=len(cast)),
        grid=(m // tm, dff // tf),
        in_specs=[
            pl.BlockSpec((tm, d), _tile_then_next(m // tm, dff // tf)),
            pl.BlockSpec((1, d), lambda i, f: (0, 0)),
            pl.BlockSpec((d, tf), lambda i, f: (0, f)),
            pl.BlockSpec((tf, d), lambda i, f: (f, 0)),
            pl.BlockSpec((1, d), lambda i, f: (0, 0)),
        ] + cast_specs,
        out_specs=[pl.BlockSpec((tm, d), lambda i, f: (i, 0))] + cast_specs,
        out_shape=[jax.ShapeDtypeStruct((m, d), F32)] + [jax.ShapeDtypeStruct(c.shape, BF16) for c in cast],
        scratch_shapes=[pltpu.VMEM((tm, d), BF16)],
        compiler_params=_params(("arbitrary", "arbitrary"), vmem),
        name="ffn_final" if final_norm else "ffn",
    )(x, g.reshape(1, d), w1, w2, g_final.reshape(1, d), *cast)


def _dil_attn_kernel(q_ref, kp_ref, kc_ref, kn_ref, vp_ref, vc_ref, vn_ref, o_ref, lse_ref,
                     *, heads, halo, length):
    qb = q_ref.shape[0]
    sub = BLOCK_Q
    nk = sub + 2 * halo
    m0 = pl.program_id(2) * qb
    qi = lax.broadcasted_iota(jnp.int32, (sub, nk), 0)
    kj = lax.broadcasted_iota(jnp.int32, (sub, nk), 1) - halo
    band = jnp.abs(kj - qi) <= halo
    lane = lax.broadcasted_iota(jnp.int32, (sub, HEAD_DIM), 1)

    def window(prev_ref, cur_ref, next_ref, lo, hi, cols):
        parts = [prev_ref[:, cols]] if lo < 0 else []
        parts.append(cur_ref[max(lo, 0):min(hi, qb), cols])
        if hi > qb:
            parts.append(next_ref[:, cols])
        return parts[0] if len(parts) == 1 else jnp.concatenate(parts, axis=0)

    def scores(u, h):
        rows = slice(u * sub, (u + 1) * sub)
        cols = slice(h * HEAD_DIM, (h + 1) * HEAD_DIM)
        k = window(kp_ref, kc_ref, kn_ref, u * sub - halo, (u + 1) * sub + halo, cols)
        s = lax.dot_general(q_ref[rows, cols], k, (((1,), (1,)), ((), ())), preferred_element_type=F32)
        s = s + masks[u]
        return s, jnp.max(s, axis=-1, keepdims=True)

    def finish(u, h, s, m):
        rows = slice(u * sub, (u + 1) * sub)
        cols = slice(h * HEAD_DIM, (h + 1) * HEAD_DIM)
        v = window(vp_ref, vc_ref, vn_ref, u * sub - halo, (u + 1) * sub + halo, cols)
        p = jnp.exp2(s - m)
        l = jnp.sum(p, axis=-1, keepdims=True)
        o = jnp.dot(p.astype(BF16), v, preferred_element_type=F32)
        o_ref[rows, cols] = (o / l).astype(o_ref.dtype)
        return m * LN2 + jnp.log(l)

    masks = []
    for u in range(qb // sub):
        kpos = kj + (m0 + u * sub)
        masks.append(jnp.where(band & (kpos >= 0) & (kpos < length), 0.0, NEG).astype(F32))

    units = [(u, h) for u in range(qb // sub) for h in range(heads)]
    cur = scores(*units[0])
    lse_tile = None
    for n, (u, h) in enumerate(units):
        nxt = scores(*units[n + 1]) if n + 1 < len(units) else None
        lse = finish(u, h, *cur)
        lse_tile = jnp.where(lane == h, lse, jnp.zeros((sub, HEAD_DIM), F32) if h == 0 else lse_tile)
        if h == heads - 1:
            lse_ref[u * sub:(u + 1) * sub, :] = lse_tile
        cur = nxt


def dil_attention(qkv, *, group, halo):
    _, batch, dil, length, wq = qkv.shape
    heads = wq // HEAD_DIM
    qb = min(DIL_STEP_ROWS, length)
    assert BLOCK_Q % halo == 0 and qb % BLOCK_Q == 0 and length % qb == 0
    nblk = length // qb
    hpb = length // halo
    per = qb // halo

    def cur_map(sec):
        return lambda b, r, i: (sec, b, r, i, 0)

    def prev_map(sec):
        return lambda b, r, i: (sec, b, r, jnp.maximum(i * per - 1, 0), 0)

    def next_map(sec):
        return lambda b, r, i: (sec, b, r, jnp.minimum((i + 1) * per, hpb - 1), 0)

    def spec(rows, index_map):
        return pl.BlockSpec((None, None, None, rows, wq), index_map)

    in_specs = [spec(qb, cur_map(0))]
    for sec in (1, 2):
        in_specs += [spec(halo, prev_map(sec)), spec(qb, cur_map(sec)), spec(halo, next_map(sec))]
    vmem = (2 * (2 * qb * wq * 2 + 2 * (qb + 2 * halo) * wq * 2 + qb * HEAD_DIM * 4)
            + 8 * BLOCK_Q * (BLOCK_Q + 2 * halo) * 4)
    return pl.pallas_call(
        functools.partial(_dil_attn_kernel, heads=heads, halo=halo, length=length),
        grid=(batch, dil, nblk),
        in_specs=in_specs,
        out_specs=[pl.BlockSpec((None, None, qb, wq), lambda b, r, i: (b, r, i, 0)),
                   pl.BlockSpec((None, None, qb, HEAD_DIM), lambda b, r, i: (b, r, i, 0))],
        out_shape=[jax.ShapeDtypeStruct((batch, dil, length, wq), BF16),
                   jax.ShapeDtypeStruct((batch, dil, length, HEAD_DIM), F32)],
        compiler_params=_params(("parallel", "parallel", "arbitrary"), vmem),
        name=f"dil_attention_g{group}",
    )(*([qkv] * 7))


def _merge_proj_kernel(*refs, dils, heads):
    g_n = len(dils)
    o_refs = refs[:g_n]
    lse_refs = refs[g_n:2 * g_n]
    w_ref, r_ref, out_ref, a_ref, lt_ref, ot_ref, tt_ref = refs[2 * g_n:]
    tm = a_ref.shape[0]

    def merged(c):
        rows = slice(c * ROW_CHUNK, (c + 1) * ROW_CHUNK)
        lses = []
        for g, dil in enumerate(dils):
            if dil == 1:
                lses.append(lse_refs[g][0, rows, :])
                continue
            n = ROW_CHUNK // dil
            d1, d2 = _stride_split(dil)
            n1 = ROW_CHUNK // d1

            def to_token_order(dst_ref, dst_idx, src):
                if d2 == 1:
                    for r in range(dil):
                        dst_ref[dst_idx, pl.ds(c * ROW_CHUNK + r, n, stride=dil), :] = src(r)
                    return
                for r1 in range(d1):
                    for r2 in range(d2):
                        tt_ref[pl.ds(r1 * n1 + r2, n, stride=d2), :] = src(r1 + d1 * r2)
                for r1 in range(d1):
                    dst_ref[dst_idx, pl.ds(c * ROW_CHUNK + r1, n1, stride=d1), :] = tt_ref[r1 * n1:(r1 + 1) * n1, :]

            to_token_order(lt_ref, g, lambda r: lse_refs[g][r, c * n:(c + 1) * n, :])
            for h in range(heads):
                to_token_order(ot_ref, g * heads + h, lambda r: o_refs[g][
                    r, c * n:(c + 1) * n, h * HEAD_DIM:(h + 1) * HEAD_DIM].astype(F32))
            lses.append(lt_ref[g, rows, :])

        mx = functools.reduce(jnp.maximum, lses)
        es = [jnp.exp(l - mx) for l in lses]
        den = functools.reduce(jnp.add, es)
        wts = [e / den for e in es]
        for h in range(heads):
            sl = slice(h * HEAD_DIM, (h + 1) * HEAD_DIM)
            acc = None
            for g, dil in enumerate(dils):
                o = o_refs[g][0, rows, sl].astype(F32) if dil == 1 else ot_ref[g * heads + h, rows, :]
                term = wts[g][:, h:h + 1] * o
                acc = term if acc is None else acc + term
            a_ref[rows, sl] = acc.astype(BF16)

    merged(0)
    for c in range(tm // ROW_CHUNK):
        rows = slice(c * ROW_CHUNK, (c + 1) * ROW_CHUNK)
        if c + 1 < tm // ROW_CHUNK:
            merged(c + 1)
        out_ref[rows, :] = r_ref[rows, :] + jnp.dot(a_ref[rows, :], w_ref[...], preferred_element_type=F32)


def merge_proj_residual(outs, lses, w, res, *, seq, tm):
    dils = tuple(o.shape[1] for o in outs)
    k = outs[0].shape[-1]
    m, n = res.shape
    g_n = len(outs)
    heads = k // HEAD_DIM
    tpb = seq // tm
    assert all(ROW_CHUNK % (BF16_SUBLANES * dil) == 0 for dil in dils) and tm % ROW_CHUNK == 0

    def stream_spec(dil, width):
        return pl.BlockSpec((None, dil, tm // dil, width), lambda i: (i // tpb, 0, i % tpb, 0))

    vmem = (2 * g_n * tm * k * 2 + 2 * g_n * tm * HEAD_DIM * 4 + 2 * k * n * 2 + 4 * tm * n * 4
            + tm * k * 2 + tm * n * 4 + g_n * tm * HEAD_DIM * 4 + g_n * tm * k * 4)
    return pl.pallas_call(
        functools.partial(_merge_proj_kernel, dils=dils, heads=heads),
        grid=(m // tm,),
        in_specs=([stream_spec(dil, k) for dil in dils]
                  + [stream_spec(dil, HEAD_DIM) for dil in dils]
                  + [pl.BlockSpec((k, n), lambda i: (0, 0)),
                     pl.BlockSpec((tm, n), lambda i: (i, 0))]),
        out_specs=pl.BlockSpec((tm, n), lambda i: (i, 0)),
        out_shape=jax.ShapeDtypeStruct((m, n), F32),
        scratch_shapes=[pltpu.VMEM((tm, k), BF16),
                        pltpu.VMEM((g_n, tm, HEAD_DIM), F32),
                        pltpu.VMEM((g_n * heads, tm, HEAD_DIM), F32),
                        pltpu.VMEM((ROW_CHUNK, HEAD_DIM), F32)],
        compiler_params=_params(("parallel",), vmem),
        name="dil_merge_proj_residual",
    )(*outs, *lses, w, res)


def kernel(x, na_norm, na_wqkv, na_rpb, na_wo, ffn0_norm, ffn0_w1, ffn0_w2, dil_norm, dil_wqkv, dil_wo,
           ffn1_norm, ffn1_w1, ffn1_w2, final_norm):
    batch, seq, d = x.shape
    m = batch * seq
    rows = seq // GRID_W
    na_heads = na_rpb.shape[0]
    assert rows >= NA_KEY_ROWS and rows % NA_BLOCK_ROWS == 0 and rows >= NA_WIN_ROWS
    assert NA_KEY_ROWS >= NA_BLOCK_ROWS + NA_WIN_ROWS - 1

    xf = x.reshape(m, d)
    bf = lambda w: w.astype(BF16)

    qw = na_heads * HEAD_DIM
    col_scale = jnp.concatenate([jnp.full((qw,), Q_SCALE, F32), jnp.ones((2 * qw,), F32)])
    qkv, na_wo_b, ffn0_w1_b, ffn0_w2_b, rot = norm_matmul(
        xf, na_norm, bf(na_wqkv), col_scale, tm=min(1024, m), tn=min(1024, qw), cast=(na_wo, ffn0_w1, ffn0_w2),
        rope_seq=seq)
    bias = _na_bias_table(na_rpb)
    attn = na_attention(qkv, bias, batch=batch, seq=seq, heads_per_step=min(8, na_heads))
    xf = proj_residual(attn, na_wo_b, xf, tm=min(512, m))
    xf, dil_wqkv_b, dil_wo_b, ffn1_w1_b, ffn1_w2_b = ffn(
        xf, ffn0_norm, ffn0_w1_b, ffn0_w2_b, final_norm, tm=min(1024, m), tf=512, final_norm=False,
        cast=(dil_wqkv, dil_wo, ffn1_w1, ffn1_w2))

    dils = tuple(dil for _, dil in DIL_GROUPS)
    qkvs = norm_matmul_rope(xf, dil_norm, dil_wqkv_b, rot, batch=batch, seq=seq, tm=min(1024, seq), dils=dils)
    outs, lses = [], []
    for gi, (window, dil) in enumerate(DIL_GROUPS):
        o, lse = dil_attention(qkvs[gi], group=gi, halo=(window // 2) // dil)
        outs.append(o)
        lses.append(lse)
    xf = merge_proj_residual(outs, lses, dil_wo_b, xf, seq=seq, tm=min(512, seq))
    xf, = ffn(xf, ffn1_norm, ffn1_w1_b, ffn1_w2_b, final_norm, tm=min(1024, m), tf=512, final_norm=True)
    return xf.reshape(batch, seq, d)
```

```python
import functools
import math

import numpy as np
import jax
import jax.numpy as jnp
from jax import lax
from jax.experimental import pallas as pl
from jax.experimental.pallas import tpu as pltpu

GRID_W = 64
HEAD_DIM = 128
NA_WIN_ROWS = 8
NA_WIN_COLS = 16
DIL_GROUPS = ((128, 1), (512, 4), (2048, 16))
ROPE_THETA = 500000.0
ROPE_DIM = HEAD_DIM // 4
BLOCK_Q = 128
EPS = 1e-6
NEG = -1e30

LOG2E = math.log2(math.e)
LN2 = math.log(2.0)
Q_SCALE = HEAD_DIM ** -0.5 * LOG2E

NA_BLOCK_ROWS = 4
NA_KEY_BACK = NA_WIN_ROWS // 2
NA_KEY_ROWS = 12
NA_KEY_CHUNKS = 3

DIL_STEP_ROWS = 512
ROW_CHUNK = 256
FFN_ROW_CHUNK = 512
BF16_SUBLANES = 16
F32_SUBLANES = 8
LANES = 128

V7X_VMEM_BYTES = 64 * 1024 * 1024
VMEM_CAP_BYTES = V7X_VMEM_BYTES - 8 * 1024 * 1024

F32 = jnp.float32
BF16 = jnp.bfloat16


def _params(semantics, vmem_bytes):
    limit = min(int(vmem_bytes * 1.25) + (4 << 20), VMEM_CAP_BYTES)
    return pltpu.CompilerParams(dimension_semantics=semantics, vmem_limit_bytes=limit)


def _rms(x, g):
    return x * lax.rsqrt(jnp.mean(x * x, axis=-1, keepdims=True) + EPS) * g


def _tile_then_next(n_tiles, n_inner):
    switch = max(n_inner // 2, 1)
    return lambda i, j: (jnp.minimum(i + jnp.where(j >= switch, 1, 0), n_tiles - 1), 0)


def _cast_rows(n_rows, n_steps):
    for rows in range(BF16_SUBLANES, n_rows + 1, BF16_SUBLANES):
        if n_rows % rows == 0 and n_rows // rows <= n_steps:
            return rows
    raise ValueError((n_rows, n_steps))


def _cast_specs(weights, n_steps, n_inner):
    specs = []
    for w in weights:
        rows = _cast_rows(w.shape[0], n_steps)
        last = w.shape[0] // rows - 1
        specs.append(pl.BlockSpec((rows, w.shape[1]), lambda i, j, last=last: (jnp.minimum(i * n_inner + j, last), 0)))
    return specs


def _cast_vmem(weights, n_steps):
    return sum(2 * _cast_rows(w.shape[0], n_steps) * w.shape[1] * 6 for w in weights)


def _cast_slabs(src_refs, dst_refs):
    for src, dst in zip(src_refs, dst_refs):
        dst[...] = src[...].astype(BF16)


def _norm_matmul_kernel(x_ref, g_ref, w_ref, cs_ref, *refs, n_cast):
    cast_src = refs[:n_cast]
    o_ref = refs[n_cast]
    cast_dst = refs[n_cast + 1:2 * n_cast + 1]
    h_ref = refs[2 * n_cast + 1]

    def body(with_norm):
        _cast_slabs(cast_src, cast_dst)
        for c in range(x_ref.shape[0] // ROW_CHUNK):
            rows = slice(c * ROW_CHUNK, (c + 1) * ROW_CHUNK)
            if with_norm:
                h_ref[rows, :] = _rms(x_ref[rows, :], g_ref[...]).astype(BF16)
            acc = jnp.dot(h_ref[rows, :], w_ref[...], preferred_element_type=F32)
            o_ref[rows, :] = (acc * cs_ref[...]).astype(o_ref.dtype)

    pl.when(pl.program_id(1) == 0)(functools.partial(body, True))
    pl.when(pl.program_id(1) != 0)(functools.partial(body, False))


def norm_matmul(x, g, w, col_scale, *, tm, tn, cast=()):
    m, k = x.shape
    n = w.shape[1]
    n_steps = (m // tm) * (n // tn)
    vmem = (2 * tm * k * 4 + tm * k * 2 + 2 * k * tn * 2 + 2 * tm * tn * 2 + tm * tn * 4
            + _cast_vmem(cast, n_steps))
    cast_specs = _cast_specs(cast, n_steps, n // tn)
    return pl.pallas_call(
        functools.partial(_norm_matmul_kernel, n_cast=len(cast)),
        grid=(m // tm, n // tn),
        in_specs=[
            pl.BlockSpec((tm, k), _tile_then_next(m // tm, n // tn)),
            pl.BlockSpec((1, k), lambda i, j: (0, 0)),
            pl.BlockSpec((k, tn), lambda i, j: (0, j)),
            pl.BlockSpec((1, tn), lambda i, j: (0, j)),
        ] + cast_specs,
        out_specs=[pl.BlockSpec((tm, tn), lambda i, j: (i, j))] + cast_specs,
        out_shape=[jax.ShapeDtypeStruct((m, n), BF16)] + [jax.ShapeDtypeStruct(c.shape, BF16) for c in cast],
        scratch_shapes=[pltpu.VMEM((tm, k), BF16)],
        compiler_params=_params(("arbitrary", "arbitrary"), vmem),
        name="norm_qkv_na",
    )(x, g.reshape(1, k), w, col_scale.reshape(1, n), *cast)


def _norm_matmul_rope_kernel(x_ref, g_ref, w_ref, rot_ref, *refs, dils):
    o_refs = refs[:len(dils)]
    h_ref, s_ref, t_ref = refs[len(dils):]
    j = pl.program_id(1)
    tm = x_ref.shape[0]
    heads = s_ref.shape[0]
    half = ROPE_DIM // 2

    def body(o_ref, dil, rotate, with_norm=False):
        n = ROW_CHUNK // dil
        d1, d2 = _stride_split(dil)
        n1 = ROW_CHUNK // d1
        scale = jnp.where(j % 3 == 0, Q_SCALE, 1.0).astype(F32)
        for c in range(tm // ROW_CHUNK):
            rows = slice(c * ROW_CHUNK, (c + 1) * ROW_CHUNK)
            if with_norm:
                h_ref[rows, :] = _rms(x_ref[rows, :], g_ref[...]).astype(BF16)
            acc = jnp.dot(h_ref[rows, :], w_ref[...], preferred_element_type=F32)
            if rotate:
                cos = rot_ref[0, rows, :] * scale
                sin_hi = rot_ref[1, rows, :] * scale
                sin_lo = rot_ref[2, rows, :] * scale
            for h in range(heads):
                cols = slice(h * HEAD_DIM, (h + 1) * HEAD_DIM)
                a = acc[:, cols]
                if rotate:
                    a = (a * cos + pltpu.roll(a, half, 1) * sin_hi
                         + pltpu.roll(a, HEAD_DIM - half, 1) * sin_lo)
                if dil == 1:
                    o_ref[0, rows, cols] = a.astype(BF16)
                    continue
                s_ref[h, rows, :] = a
                if d2 == 1:
                    for r in range(dil):
                        o_ref[r, c * n:(c + 1) * n, cols] = (
                            s_ref[h, pl.ds(c * ROW_CHUNK + r, n, stride=dil), :].astype(BF16))
                    continue
                for r1 in range(d1):
                    t_ref[h, r1 * n1:(r1 + 1) * n1, :] = s_ref[h, pl.ds(c * ROW_CHUNK + r1, n1, stride=d1), :]
                for r1 in range(d1):
                    for r2 in range(d2):
                        o_ref[r1 + d1 * r2, c * n:(c + 1) * n, cols] = (
                            t_ref[h, pl.ds(r1 * n1 + r2, n, stride=d2), :].astype(BF16))

    pl.when(j == 0)(functools.partial(body, o_refs[0], dils[0], True, with_norm=True))
    for g, dil in enumerate(dils):
        pl.when((j // 3 == g) & (j % 3 != 2) & (j != 0))(functools.partial(body, o_refs[g], dil, True))
        pl.when((j // 3 == g) & (j % 3 == 2))(functools.partial(body, o_refs[g], dil, False))


def _stride_split(dil):
    if dil <= F32_SUBLANES:
        return dil, 1
    d1 = F32_SUBLANES // 2
    assert dil % d1 == 0 and dil // d1 <= F32_SUBLANES
    return d1, dil // d1


def _rope_tables(t):
    half = ROPE_DIM // 2
    pos = jnp.arange(t, dtype=F32)
    inv_freq = ROPE_THETA ** (-jnp.arange(0, ROPE_DIM, 2, dtype=F32) / ROPE_DIM)
    ang = pos[:, None] * inv_freq[None, :]
    cos = jnp.cos(ang)
    sin = jnp.sin(ang)
    rest = HEAD_DIM - ROPE_DIM
    one = jnp.ones((t, rest), F32)
    zero = jnp.zeros((t, rest), F32)
    zh = jnp.zeros((t, half), F32)
    return jnp.stack([jnp.concatenate([cos, cos, one], axis=-1),
                      jnp.concatenate([zh, sin, zero], axis=-1),
                      jnp.concatenate([-sin, zh, zero], axis=-1)], axis=0)


def norm_matmul_rope(x, g, w, rot, *, batch, seq, tm, dils):
    m, k = x.shape
    n = w.shape[1]
    tn = n // (3 * len(dils))
    heads = tn // HEAD_DIM
    tpb = seq // tm
    assert all(ROW_CHUNK % (BF16_SUBLANES * dil) == 0 for dil in dils) and tm % ROW_CHUNK == 0

    def out_map(g):
        return lambda i, j: (jnp.clip(j - 3 * g, 0, 2), i // tpb, 0, i % tpb, 0)

    vmem = (2 * tm * k * 4 + tm * k * 2 + 2 * k * tn * 2 + 2 * len(dils) * tm * tn * 2 + tm * tn * 4
            + 2 * ROW_CHUNK * tn * 4 + 2 * 3 * tm * HEAD_DIM * 4 + heads * ROW_CHUNK * HEAD_DIM * 4)
    return pl.pallas_call(
        functools.partial(_norm_matmul_rope_kernel, dils=dils),
        grid=(m // tm, n // tn),
        in_specs=[
            pl.BlockSpec((tm, k), _tile_then_next(m // tm, n // tn)),
            pl.BlockSpec((1, k), lambda i, j: (0, 0)),
            pl.BlockSpec((k, tn), lambda i, j: (0, j)),
            pl.BlockSpec((3, tm, HEAD_DIM), lambda i, j: (0, i % tpb, 0)),
        ],
        out_specs=[pl.BlockSpec((None, None, dil, tm // dil, tn), out_map(g)) for g, dil in enumerate(dils)],
        out_shape=[jax.ShapeDtypeStruct((3, batch, dil, seq // dil, tn), BF16) for dil in dils],
        scratch_shapes=[pltpu.VMEM((tm, k), BF16), pltpu.VMEM((heads, tm, HEAD_DIM), F32),
                        pltpu.VMEM((heads, ROW_CHUNK, HEAD_DIM), F32)],
        compiler_params=_params(("arbitrary", "arbitrary"), vmem),
        name="norm_qkv_dil_rope",
    )(x, g.reshape(1, k), w, rot)


def _na_bias_table(rpb):
    nh, n_row_off, n_col_off = rpb.shape
    assert 2 * GRID_W == LANES and NA_KEY_ROWS % 2 == 0 and n_col_off <= LANES
    rpb_lanes = jnp.pad(rpb, ((0, 0), (0, 0), (0, LANES - n_col_off)))
    return pl.pallas_call(
        _na_bias_kernel,
        grid=(nh,),
        in_specs=[pl.BlockSpec((None, n_row_off, LANES), lambda h: (h, 0, 0))],
        out_specs=pl.BlockSpec((None, NA_BLOCK_ROWS * GRID_W, NA_KEY_ROWS * GRID_W), lambda h: (h, 0, 0)),
        out_shape=jax.ShapeDtypeStruct((nh, NA_BLOCK_ROWS * GRID_W, NA_KEY_ROWS * GRID_W), F32),
        compiler_params=_params(("parallel",), 4 * NA_BLOCK_ROWS * GRID_W * NA_KEY_ROWS * GRID_W * 4),
        name="na_bias_table",
    )(rpb_lanes)


def _na_bias_kernel(rpb_ref, o_ref):
    w = GRID_W
    qc = lax.broadcasted_iota(jnp.int32, (w, LANES), 0)
    kc = lax.broadcasted_iota(jnp.int32, (w, LANES), 1)
    start = jnp.clip(qc - NA_WIN_COLS // 2, 0, w - NA_WIN_COLS)
    in_window = (kc >= start) & (kc < start + NA_WIN_COLS)
    by_col = []
    for a in range(rpb_ref.shape[0]):
        row = jnp.broadcast_to(rpb_ref[a:a + 1, :], (w, LANES))
        shifted = pltpu.roll(row, LANES - (NA_WIN_COLS - 1), 1, stride=1, stride_axis=0)
        by_col.append(jnp.where(in_window, shifted * LOG2E, NEG))
    for rr in range(NA_BLOCK_ROWS):
        for kk in range(0, NA_KEY_ROWS, 2):
            a = kk - NA_KEY_BACK - rr + NA_WIN_ROWS - 1
            assert 0 <= a and a + 1 < rpb_ref.shape[0]
            pair = jnp.where(kc < w, by_col[a], pltpu.roll(by_col[a + 1], w, 1))
            o_ref[rr * w:(rr + 1) * w, kk * w:(kk + 2) * w] = pair


def _na_attn_kernel(q_ref, *refs, heads, rows):
    k_refs = refs[:NA_KEY_CHUNKS]
    v_refs = refs[NA_KEY_CHUNKS:2 * NA_KEY_CHUNKS]
    b_ref, o_ref, bm_ref = refs[2 * NA_KEY_CHUNKS:]
    qb = q_ref.shape[0]
    ck = k_refs[0].shape[0]
    kh = min(NA_WIN_ROWS, rows)
    i = pl.program_id(2)

    def clipped(blk):
        r0 = blk * NA_BLOCK_ROWS
        return (r0 < kh // 2) | (r0 + NA_BLOCK_ROWS - 1 - kh // 2 > rows - kh)

    @pl.when((i == 0) | clipped(i) | clipped(i - 1))
    def _():
        r0 = i * NA_BLOCK_ROWS
        q_row = r0 + lax.broadcasted_iota(jnp.int32, (qb, ck), 0) // GRID_W
        lo = jnp.clip(q_row - kh // 2, 0, rows - kh)
        for c in range(NA_KEY_CHUNKS):
            k_row = (r0 - NA_KEY_BACK + c * (ck // GRID_W)
                     + lax.broadcasted_iota(jnp.int32, (qb, ck), 1) // GRID_W)
            row_mask = jnp.where((k_row >= lo) & (k_row < lo + kh), 0.0, NEG).astype(F32)
            for h in range(heads):
                bm_ref[h, :, c * ck:(c + 1) * ck] = b_ref[h, :, c * ck:(c + 1) * ck] + row_mask

    def scores(h):
        sl = slice(h * HEAD_DIM, (h + 1) * HEAD_DIM)
        q = q_ref[:, sl]
        s = []
        for c, k_ref in enumerate(k_refs):
            sc = lax.dot_general(q, k_ref[:, sl], (((1,), (1,)), ((), ())), preferred_element_type=F32)
            s.append(sc + bm_ref[h, :, c * ck:(c + 1) * ck])
        return s, jnp.max(functools.reduce(jnp.maximum, s), axis=-1, keepdims=True)

    def finish(h, s, m):
        sl = slice(h * HEAD_DIM, (h + 1) * HEAD_DIM)
        p = [jnp.exp2(sc - m) for sc in s]
        l = jnp.sum(functools.reduce(jnp.add, p), axis=-1, keepdims=True)
        o = functools.reduce(jnp.add, [
            jnp.dot(pc.astype(BF16), v_ref[:, sl], preferred_element_type=F32)
            for pc, v_ref in zip(p, v_refs)])
        o_ref[:, sl] = (o / l).astype(o_ref.dtype)

    cur = scores(0)
    for h in range(heads):
        nxt = scores(h + 1) if h + 1 < heads else None
        finish(h, *cur)
        cur = nxt


def na_attention(qkv, bias, *, batch, seq, heads_per_step):
    m = qkv.shape[0]
    nh = bias.shape[0]
    hg = heads_per_step
    wq = hg * HEAD_DIM
    qb = NA_BLOCK_ROWS * GRID_W
    kb = NA_KEY_ROWS * GRID_W // NA_KEY_CHUNKS
    assert qb % kb == 0 and (NA_KEY_BACK * GRID_W) % kb == 0
    nblk = seq // qb
    kpb = seq // kb
    ratio = qb // kb
    back = NA_KEY_BACK * GRID_W // kb
    ncol = nh // hg

    def k_map(c, sec):
        def index(g, b, i):
            return (b * kpb + jnp.clip(i * ratio - back + c, 0, kpb - 1), sec * ncol + g)
        return index

    in_specs = [pl.BlockSpec((qb, wq), lambda g, b, i: (b * nblk + i, g))]
    in_specs += [pl.BlockSpec((kb, wq), k_map(c, 1)) for c in range(NA_KEY_CHUNKS)]
    in_specs += [pl.BlockSpec((kb, wq), k_map(c, 2)) for c in range(NA_KEY_CHUNKS)]
    in_specs += [pl.BlockSpec((hg, qb, NA_KEY_CHUNKS * kb), lambda g, b, i: (g, 0, 0),
                              pipeline_mode=pl.Buffered(1))]
    vmem = (2 * 2 * qb * wq * 2 + 2 * 2 * NA_KEY_CHUNKS * kb * wq * 2
            + 2 * hg * qb * NA_KEY_CHUNKS * kb * 4 + 6 * qb * NA_KEY_CHUNKS * kb * 4)
    return pl.pallas_call(
        functools.partial(_na_attn_kernel, heads=hg, rows=seq // GRID_W),
        grid=(ncol, batch, nblk),
        in_specs=in_specs,
        out_specs=pl.BlockSpec((qb, wq), lambda g, b, i: (b * nblk + i, g)),
        out_shape=jax.ShapeDtypeStruct((m, nh * HEAD_DIM), BF16),
        scratch_shapes=[pltpu.VMEM((hg, qb, NA_KEY_CHUNKS * kb), F32)],
        compiler_params=_params(("parallel", "parallel", "arbitrary"), vmem),
        name="na_attention",
    )(qkv, *([qkv] * (2 * NA_KEY_CHUNKS)), bias)


def _proj_residual_kernel(a_ref, w_ref, r_ref, o_ref):
    o_ref[...] = r_ref[...] + jnp.dot(a_ref[...], w_ref[...], preferred_element_type=F32)


def proj_residual(a, w, res, *, tm):
    m, k = a.shape
    n = w.shape[1]
    vmem = 2 * tm * k * 2 + 2 * k * n * 2 + 4 * tm * n * 4 + tm * n * 4
    return pl.pallas_call(
        _proj_residual_kernel,
        grid=(m // tm,),
        in_specs=[
            pl.BlockSpec((tm, k), lambda i: (i, 0)),
            pl.BlockSpec((k, n), lambda i: (0, 0)),
            pl.BlockSpec((tm, n), lambda i: (i, 0)),
        ],
        out_specs=pl.BlockSpec((tm, n), lambda i: (i, 0)),
        out_shape=jax.ShapeDtypeStruct((m, n), F32),
        compiler_params=_params(("parallel",), vmem),
        name="na_proj_residual",
    )(a, w, res)


def _ffn_kernel(x_ref, g_ref, w1_ref, w2_ref, gf_ref, *refs, final_norm, n_cast):
    cast_src = refs[:n_cast]
    o_ref = refs[n_cast]
    cast_dst = refs[n_cast + 1:2 * n_cast + 1]
    h_ref = refs[2 * n_cast + 1]
    f = pl.program_id(1)

    def body(first):
        _cast_slabs(cast_src, cast_dst)
        for c in range(x_ref.shape[0] // FFN_ROW_CHUNK):
            rows = slice(c * FFN_ROW_CHUNK, (c + 1) * FFN_ROW_CHUNK)
            if first:
                h_ref[rows, :] = _rms(x_ref[rows, :], g_ref[...]).astype(BF16)
            a = jnp.dot(h_ref[rows, :], w1_ref[...], preferred_element_type=F32)
            a = jnp.square(jnp.maximum(a, 0.0)).astype(BF16)
            y = jnp.dot(a, w2_ref[...], preferred_element_type=F32)
            if first:
                o_ref[rows, :] = x_ref[rows, :] + y
            else:
                o_ref[rows, :] += y

    pl.when(f == 0)(functools.partial(body, True))
    pl.when(f != 0)(functools.partial(body, False))

    if final_norm:
        @pl.when(f == pl.num_programs(1) - 1)
        def _():
            o_ref[...] = _rms(o_ref[...], gf_ref[...])


def ffn(x, g, w1, w2, g_final, *, tm, tf, final_norm, cast=()):
    m, d = x.shape
    dff = w1.shape[1]
    assert tm % FFN_ROW_CHUNK == 0
    n_steps = (m // tm) * (dff // tf)
    vmem = (4 * tm * d * 4 + tm * d * 2 + 2 * 2 * d * tf * 2
            + FFN_ROW_CHUNK * tf * 6 + FFN_ROW_CHUNK * d * 4 + _cast_vmem(cast, n_steps))
    cast_specs = _cast_specs(cast, n_steps, dff // tf)
    return pl.pallas_call(
        functools.partial(_ffn_kernel, final_norm=final_norm, n_cast=len(cast)),
        grid=(m // tm, dff // tf),
        in_specs=[
            pl.BlockSpec((tm, d), _tile_then_next(m // tm, dff // tf)),
            pl.BlockSpec((1, d), lambda i, f: (0, 0)),
            pl.BlockSpec((d, tf), lambda i, f: (0, f)),
            pl.BlockSpec((tf, d), lambda i, f: (f, 0)),
            pl.BlockSpec((1, d), lambda i, f: (0, 0)),
        ] + cast_specs,
        out_specs=[pl.BlockSpec((tm, d), lambda i, f: (i, 0))] + cast_specs,
        out_shape=[jax.ShapeDtypeStruct((m, d), F32)] + [jax.ShapeDtypeStruct(c.shape, BF16) for c in cast],
        scratch_shapes=[pltpu.VMEM((tm, d), BF16)],
        compiler_params=_params(("arbitrary", "arbitrary"), vmem),
        name="ffn_final" if final_norm else "ffn",
    )(x, g.reshape(1, d), w1, w2, g_final.reshape(1, d), *cast)


def _dil_attn_kernel(q_ref, kp_ref, kc_ref, kn_ref, vp_ref, vc_ref, vn_ref, o_ref, lse_ref,
                     *, heads, halo, length):
    qb = q_ref.shape[0]
    sub = BLOCK_Q
    nk = sub + 2 * halo
    m0 = pl.program_id(2) * qb
    qi = lax.broadcasted_iota(jnp.int32, (sub, nk), 0)
    kj = lax.broadcasted_iota(jnp.int32, (sub, nk), 1) - halo
    band = jnp.abs(kj - qi) <= halo
    lane = lax.broadcasted_iota(jnp.int32, (sub, HEAD_DIM), 1)

    def window(prev_ref, cur_ref, next_ref, lo, hi, cols):
        parts = [prev_ref[:, cols]] if lo < 0 else []
        parts.append(cur_ref[max(lo, 0):min(hi, qb), cols])
        if hi > qb:
            parts.append(next_ref[:, cols])
        return parts[0] if len(parts) == 1 else jnp.concatenate(parts, axis=0)

    def scores(u, h):
        rows = slice(u * sub, (u + 1) * sub)
        cols = slice(h * HEAD_DIM, (h + 1) * HEAD_DIM)
        k = window(kp_ref, kc_ref, kn_ref, u * sub - halo, (u + 1) * sub + halo, cols)
        s = lax.dot_general(q_ref[rows, cols], k, (((1,), (1,)), ((), ())), preferred_element_type=F32)
        s = s + masks[u]
        return s, jnp.max(s, axis=-1, keepdims=True)

    def finish(u, h, s, m):
        rows = slice(u * sub, (u + 1) * sub)
        cols = slice(h * HEAD_DIM, (h + 1) * HEAD_DIM)
        v = window(vp_ref, vc_ref, vn_ref, u * sub - halo, (u + 1) * sub + halo, cols)
        p = jnp.exp2(s - m)
        l = jnp.sum(p, axis=-1, keepdims=True)
        o = jnp.dot(p.astype(BF16), v, preferred_element_type=F32)
        o_ref[rows, cols] = (o / l).astype(o_ref.dtype)
        return m * LN2 + jnp.log(l)

    masks = []
    for u in range(qb // sub):
        kpos = kj + (m0 + u * sub)
        masks.append(jnp.where(band & (kpos >= 0) & (kpos < length), 0.0, NEG).astype(F32))

    units = [(u, h) for u in range(qb // sub) for h in range(heads)]
    cur = scores(*units[0])
    lse_tile = None
    for n, (u, h) in enumerate(units):
        nxt = scores(*units[n + 1]) if n + 1 < len(units) else None
        lse = finish(u, h, *cur)
        lse_tile = jnp.where(lane == h, lse, jnp.zeros((sub, HEAD_DIM), F32) if h == 0 else lse_tile)
        if h == heads - 1:
            lse_ref[u * sub:(u + 1) * sub, :] = lse_tile
        cur = nxt


def dil_attention(qkv, *, group, halo):
    _, batch, dil, length, wq = qkv.shape
    heads = wq // HEAD_DIM
    qb = min(DIL_STEP_ROWS, length)
    assert BLOCK_Q % halo == 0 and qb % BLOCK_Q == 0 and length % qb == 0
    nblk = length // qb
    hpb = length // halo
    per = qb // halo

    def cur_map(sec):
        return lambda b, r, i: (sec, b, r, i, 0)

    def prev_map(sec):
        return lambda b, r, i: (sec, b, r, jnp.maximum(i * per - 1, 0), 0)

    def next_map(sec):
        return lambda b, r, i: (sec, b, r, jnp.minimum((i + 1) * per, hpb - 1), 0)

    def spec(rows, index_map):
        return pl.BlockSpec((None, None, None, rows, wq), index_map)

    in_specs = [spec(qb, cur_map(0))]
    for sec in (1, 2):
        in_specs += [spec(halo, prev_map(sec)), spec(qb, cur_map(sec)), spec(halo, next_map(sec))]
    vmem = (2 * (2 * qb * wq * 2 + 2 * (qb + 2 * halo) * wq * 2 + qb * HEAD_DIM * 4)
            + 8 * BLOCK_Q * (BLOCK_Q + 2 * halo) * 4)
    return pl.pallas_call(
        functools.partial(_dil_attn_kernel, heads=heads, halo=halo, length=length),
        grid=(batch, dil, nblk),
        in_specs=in_specs,
        out_specs=[pl.BlockSpec((None, None, qb, wq), lambda b, r, i: (b, r, i, 0)),
                   pl.BlockSpec((None, None, qb, HEAD_DIM), lambda b, r, i: (b, r, i, 0))],
        out_shape=[jax.ShapeDtypeStruct((batch, dil, length, wq), BF16),
                   jax.ShapeDtypeStruct((batch, dil, length, HEAD_DIM), F32)],
        compiler_params=_params(("parallel", "parallel", "arbitrary"), vmem),
        name=f"dil_attention_g{group}",
    )(*([qkv] * 7))


def _merge_proj_kernel(*refs, dils, heads):
    g_n = len(dils)
    o_refs = refs[:g_n]
    lse_refs = refs[g_n:2 * g_n]
    w_ref, r_ref, out_ref, a_ref, lt_ref, ot_ref, tt_ref = refs[2 * g_n:]
    tm = a_ref.shape[0]

    def merged(c):
        rows = slice(c * ROW_CHUNK, (c + 1) * ROW_CHUNK)
        lses = []
        for g, dil in enumerate(dils):
            if dil == 1:
                lses.append(lse_refs[g][0, rows, :])
                continue
            n = ROW_CHUNK // dil
            d1, d2 = _stride_split(dil)
            n1 = ROW_CHUNK // d1

            def to_token_order(dst_ref, dst_idx, src):
                if d2 == 1:
                    for r in range(dil):
                        dst_ref[dst_idx, pl.ds(c * ROW_CHUNK + r, n, stride=dil), :] = src(r)
                    return
                for r1 in range(d1):
                    for r2 in range(d2):
                        tt_ref[pl.ds(r1 * n1 + r2, n, stride=d2), :] = src(r1 + d1 * r2)
                for r1 in range(d1):
                    dst_ref[dst_idx, pl.ds(c * ROW_CHUNK + r1, n1, stride=d1), :] = tt_ref[r1 * n1:(r1 + 1) * n1, :]

            to_token_order(lt_ref, g, lambda r: lse_refs[g][r, c * n:(c + 1) * n, :])
            for h in range(heads):
                to_token_order(ot_ref, g * heads + h, lambda r: o_refs[g][
                    r, c * n:(c + 1) * n, h * HEAD_DIM:(h + 1) * HEAD_DIM].astype(F32))
            lses.append(lt_ref[g, rows, :])

        mx = functools.reduce(jnp.maximum, lses)
        es = [jnp.exp(l - mx) for l in lses]
        den = functools.reduce(jnp.add, es)
        wts = [e / den for e in es]
        for h in range(heads):
            sl = slice(h * HEAD_DIM, (h + 1) * HEAD_DIM)
            acc = None
            for g, dil in enumerate(dils):
                o = o_refs[g][0, rows, sl].astype(F32) if dil == 1 else ot_ref[g * heads + h, rows, :]
                term = wts[g][:, h:h + 1] * o
                acc = term if acc is None else acc + term
            a_ref[rows, sl] = acc.astype(BF16)

    merged(0)
    for c in range(tm // ROW_CHUNK):
        rows = slice(c * ROW_CHUNK, (c + 1) * ROW_CHUNK)
        if c + 1 < tm // ROW_CHUNK:
            merged(c + 1)
        out_ref[rows, :] = r_ref[rows, :] + jnp.dot(a_ref[rows, :], w_ref[...], preferred_element_type=F32)


def merge_proj_residual(outs, lses, w, res, *, seq, tm):
    dils = tuple(o.shape[1] for o in outs)
    k = outs[0].shape[-1]
    m, n = res.shape
    g_n = len(outs)
    heads = k // HEAD_DIM
    tpb = seq // tm
    assert all(ROW_CHUNK % (BF16_SUBLANES * dil) == 0 for dil in dils) and tm % ROW_CHUNK == 0

    def stream_spec(dil, width):
        return pl.BlockSpec((None, dil, tm // dil, width), lambda i: (i // tpb, 0, i % tpb, 0))

    vmem = (2 * g_n * tm * k * 2 + 2 * g_n * tm * HEAD_DIM * 4 + 2 * k * n * 2 + 4 * tm * n * 4
            + tm * k * 2 + tm * n * 4 + g_n * tm * HEAD_DIM * 4 + g_n * tm * k * 4)
    return pl.pallas_call(
        functools.partial(_merge_proj_kernel, dils=dils, heads=heads),
        grid=(m // tm,),
        in_specs=([stream_spec(dil, k) for dil in dils]
                  + [stream_spec(dil, HEAD_DIM) for dil in dils]
                  + [pl.BlockSpec((k, n), lambda i: (0, 0)),
                     pl.BlockSpec((tm, n), lambda i: (i, 0))]),
        out_specs=pl.BlockSpec((tm, n), lambda i: (i, 0)),
        out_shape=jax.ShapeDtypeStruct((m, n), F32),
        scratch_shapes=[pltpu.VMEM((tm, k), BF16),
                        pltpu.VMEM((g_n, tm, HEAD_DIM), F32),
                        pltpu.VMEM((g_n * heads, tm, HEAD_DIM), F32),
                        pltpu.VMEM((ROW_CHUNK, HEAD_DIM), F32)],
        compiler_params=_params(("parallel",), vmem),
        name="dil_merge_proj_residual",
    )(*outs, *lses, w, res)


def kernel(x, na_norm, na_wqkv, na_rpb, na_wo, ffn0_norm, ffn0_w1, ffn0_w2, dil_norm, dil_wqkv, dil_wo,
           ffn1_norm, ffn1_w1, ffn1_w2, final_norm):
    batch, seq, d = x.shape
    m = batch * seq
    rows = seq // GRID_W
    na_heads = na_rpb.shape[0]
    assert rows >= NA_KEY_ROWS and rows % NA_BLOCK_ROWS == 0 and rows >= NA_WIN_ROWS
    assert NA_KEY_ROWS >= NA_BLOCK_ROWS + NA_WIN_ROWS - 1

    xf = x.reshape(m, d)
    bf = lambda w: w.astype(BF16)

    qw = na_heads * HEAD_DIM
    col_scale = jnp.concatenate([jnp.full((qw,), Q_SCALE, F32), jnp.ones((2 * qw,), F32)])
    qkv, na_wo_b, ffn0_w1_b, ffn0_w2_b = norm_matmul(
        xf, na_norm, bf(na_wqkv), col_scale, tm=min(1024, m), tn=min(1024, qw), cast=(na_wo, ffn0_w1, ffn0_w2))
    bias = _na_bias_table(na_rpb)
    attn = na_attention(qkv, bias, batch=batch, seq=seq, heads_per_step=min(16, na_heads))
    xf = proj_residual(attn, na_wo_b, xf, tm=min(512, m))
    xf, dil_wqkv_b, dil_wo_b, ffn1_w1_b, ffn1_w2_b = ffn(
        xf, ffn0_norm, ffn0_w1_b, ffn0_w2_b, final_norm, tm=min(1024, m), tf=512, final_norm=False,
        cast=(dil_wqkv, dil_wo, ffn1_w1, ffn1_w2))

    rot = _rope_tables(seq)
    dils = tuple(dil for _, dil in DIL_GROUPS)
    qkvs = norm_matmul_rope(xf, dil_norm, dil_wqkv_b, rot, batch=batch, seq=seq, tm=min(1024, seq), dils=dils)
    outs, lses = [], []
    for gi, (window, dil) in enumerate(DIL_GROUPS):
        o, lse = dil_attention(qkvs[gi], group=gi, halo=(window // 2) // dil)
        outs.append(o)
        lses.append(lse)
    xf = merge_proj_residual(outs, lses, dil_wo_b, xf, seq=seq, tm=min(512, seq))
    xf, = ffn(xf, ffn1_norm, ffn1_w1_b, ffn1_w2_b, final_norm, tm=min(1024, m), tf=512, final_norm=True)
    return xf.reshape(batch, seq, d)
```

```python
import functools
import math

import numpy as np
import jax
import jax.numpy as jnp
from jax import lax
from jax.experimental import pallas as pl
from jax.experimental.pallas import tpu as pltpu

GRID_W = 64
HEAD_DIM = 128
NA_WIN_ROWS = 8
NA_WIN_COLS = 16
DIL_GROUPS = ((128, 1), (512, 4), (2048, 16))
ROPE_THETA = 500000.0
ROPE_DIM = HEAD_DIM // 4
BLOCK_Q = 128
EPS = 1e-6
NEG = -1e30

LOG2E = math.log2(math.e)
LN2 = math.log(2.0)
Q_SCALE = HEAD_DIM ** -0.5 * LOG2E

NA_BLOCK_ROWS = 4
NA_KEY_BACK = NA_WIN_ROWS // 2
NA_KEY_ROWS = 12
NA_KEY_CHUNKS = 3

DIL_STEP_ROWS = 512
ROW_CHUNK = 256
FFN_ROW_CHUNK = 512
BF16_SUBLANES = 16
F32_SUBLANES = 8
LANES = 128

V7X_VMEM_BYTES = 64 * 1024 * 1024
VMEM_CAP_BYTES = V7X_VMEM_BYTES - 8 * 1024 * 1024

F32 = jnp.float32
BF16 = jnp.bfloat16


def _params(semantics, vmem_bytes):
    limit = min(int(vmem_bytes * 1.25) + (4 << 20), VMEM_CAP_BYTES)
    return pltpu.CompilerParams(dimension_semantics=semantics, vmem_limit_bytes=limit)


def _rms(x, g):
    return x * lax.rsqrt(jnp.mean(x * x, axis=-1, keepdims=True) + EPS) * g


def _tile_then_next(n_tiles, n_inner):
    switch = max(n_inner // 2, 1)
    return lambda i, j: (jnp.minimum(i + jnp.where(j >= switch, 1, 0), n_tiles - 1), 0)


def _cast_rows(n_rows, n_steps):
    for rows in range(BF16_SUBLANES, n_rows + 1, BF16_SUBLANES):
        if n_rows % rows == 0 and n_rows // rows <= n_steps:
            return rows
    raise ValueError((n_rows, n_steps))


def _cast_specs(weights, n_steps, n_inner):
    specs = []
    for w in weights:
        rows = _cast_rows(w.shape[0], n_steps)
        last = w.shape[0] // rows - 1
        specs.append(pl.BlockSpec((rows, w.shape[1]), lambda i, j, last=last: (jnp.minimum(i * n_inner + j, last), 0)))
    return specs


def _cast_vmem(weights, n_steps):
    return sum(2 * _cast_rows(w.shape[0], n_steps) * w.shape[1] * 6 for w in weights)


def _cast_slabs(src_refs, dst_refs):
    for src, dst in zip(src_refs, dst_refs):
        dst[...] = src[...].astype(BF16)


def _norm_matmul_kernel(x_ref, g_ref, w_ref, cs_ref, *refs, n_cast):
    cast_src = refs[:n_cast]
    o_ref = refs[n_cast]
    cast_dst = refs[n_cast + 1:2 * n_cast + 1]
    h_ref = refs[2 * n_cast + 1]

    def body(with_norm):
        _cast_slabs(cast_src, cast_dst)
        for c in range(x_ref.shape[0] // ROW_CHUNK):
            rows = slice(c * ROW_CHUNK, (c + 1) * ROW_CHUNK)
            if with_norm:
                h_ref[rows, :] = _rms(x_ref[rows, :], g_ref[...]).astype(BF16)
            acc = jnp.dot(h_ref[rows, :], w_ref[...], preferred_element_type=F32)
            o_ref[rows, :] = (acc * cs_ref[...]).astype(o_ref.dtype)

    pl.when(pl.program_id(1) == 0)(functools.partial(body, True))
    pl.when(pl.program_id(1) != 0)(functools.partial(body, False))


def norm_matmul(x, g, w, col_scale, *, tm, tn, cast=()):
    m, k = x.shape
    n = w.shape[1]
    n_steps = (m // tm) * (n // tn)
    vmem = (2 * tm * k * 4 + tm * k * 2 + 2 * k * tn * 2 + 2 * tm * tn * 2 + tm * tn * 4
            + _cast_vmem(cast, n_steps))
    cast_specs = _cast_specs(cast, n_steps, n // tn)
    return pl.pallas_call(
        functools.partial(_norm_matmul_kernel, n_cast=len(cast)),
        grid=(m // tm, n // tn),
        in_specs=[
            pl.BlockSpec((tm, k), _tile_then_next(m // tm, n // tn)),
            pl.BlockSpec((1, k), lambda i, j: (0, 0)),
            pl.BlockSpec((k, tn), lambda i, j: (0, j)),
            pl.BlockSpec((1, tn), lambda i, j: (0, j)),
        ] + cast_specs,
        out_specs=[pl.BlockSpec((tm, tn), lambda i, j: (i, j))] + cast_specs,
        out_shape=[jax.ShapeDtypeStruct((m, n), BF16)] + [jax.ShapeDtypeStruct(c.shape, BF16) for c in cast],
        scratch_shapes=[pltpu.VMEM((tm, k), BF16)],
        compiler_params=_params(("arbitrary", "arbitrary"), vmem),
        name="norm_qkv_na",
    )(x, g.reshape(1, k), w, col_scale.reshape(1, n), *cast)


def _norm_matmul_rope_kernel(x_ref, g_ref, w_ref, rot_ref, *refs, dils):
    o_refs = refs[:len(dils)]
    h_ref, s_ref, t_ref = refs[len(dils):]
    j = pl.program_id(1)
    tm = x_ref.shape[0]
    heads = s_ref.shape[0]
    half = ROPE_DIM // 2

    def body(o_ref, dil, rotate, with_norm=False):
        n = ROW_CHUNK // dil
        d1, d2 = _stride_split(dil)
        n1 = ROW_CHUNK // d1
        scale = jnp.where(j % 3 == 0, Q_SCALE, 1.0).astype(F32)
        for c in range(tm // ROW_CHUNK):
            rows = slice(c * ROW_CHUNK, (c + 1) * ROW_CHUNK)
            if with_norm:
                h_ref[rows, :] = _rms(x_ref[rows, :], g_ref[...]).astype(BF16)
            acc = jnp.dot(h_ref[rows, :], w_ref[...], preferred_element_type=F32)
            if rotate:
                cos = rot_ref[0, rows, :] * scale
                sin_hi = rot_ref[1, rows, :] * scale
                sin_lo = rot_ref[2, rows, :] * scale
            for h in range(heads):
                cols = slice(h * HEAD_DIM, (h + 1) * HEAD_DIM)
                a = acc[:, cols]
                if rotate:
                    a = (a * cos + pltpu.roll(a, half, 1) * sin_hi
                         + pltpu.roll(a, HEAD_DIM - half, 1) * sin_lo)
                if dil == 1:
                    o_ref[0, rows, cols] = a.astype(BF16)
                    continue
                s_ref[h, rows, :] = a
                if d2 == 1:
                    for r in range(dil):
                        o_ref[r, c * n:(c + 1) * n, cols] = (
                            s_ref[h, pl.ds(c * ROW_CHUNK + r, n, stride=dil), :].astype(BF16))
                    continue
                for r1 in range(d1):
                    t_ref[h, r1 * n1:(r1 + 1) * n1, :] = s_ref[h, pl.ds(c * ROW_CHUNK + r1, n1, stride=d1), :]
                for r1 in range(d1):
                    for r2 in range(d2):
                        o_ref[r1 + d1 * r2, c * n:(c + 1) * n, cols] = (
                            t_ref[h, pl.ds(r1 * n1 + r2, n, stride=d2), :].astype(BF16))

    pl.when(j == 0)(functools.partial(body, o_refs[0], dils[0], True, with_norm=True))
    for g, dil in enumerate(dils):
        pl.when((j // 3 == g) & (j % 3 != 2) & (j != 0))(functools.partial(body, o_refs[g], dil, True))
        pl.when((j // 3 == g) & (j % 3 == 2))(functools.partial(body, o_refs[g], dil, False))


def _stride_split(dil):
    if dil <= F32_SUBLANES:
        return dil, 1
    d1 = F32_SUBLANES // 2
    assert dil % d1 == 0 and dil // d1 <= F32_SUBLANES
    return d1, dil // d1


def norm_matmul_rope(x, g, w, rot, *, batch, seq, tm, dils):
    m, k = x.shape
    n = w.shape[1]
    tn = n // (3 * len(dils))
    heads = tn // HEAD_DIM
    tpb = seq // tm
    assert all(ROW_CHUNK % (BF16_SUBLANES * dil) == 0 for dil in dils) and tm % ROW_CHUNK == 0

    def out_map(g):
        return lambda i, j: (jnp.clip(j - 3 * g, 0, 2), i // tpb, 0, i % tpb, 0)

    vmem = (2 * tm * k * 4 + tm * k * 2 + 2 * k * tn * 2 + 2 * len(dils) * tm * tn * 2 + tm * tn * 4
            + 2 * ROW_CHUNK * tn * 4 + 2 * 3 * tm * HEAD_DIM * 4 + heads * ROW_CHUNK * HEAD_DIM * 4)
    return pl.pallas_call(
        functools.partial(_norm_matmul_rope_kernel, dils=dils),
        grid=(m // tm, n // tn),
        in_specs=[
            pl.BlockSpec((tm, k), _tile_then_next(m // tm, n // tn)),
            pl.BlockSpec((1, k), lambda i, j: (0, 0)),
            pl.BlockSpec((k, tn), lambda i, j: (0, j)),
            pl.BlockSpec((3, tm, HEAD_DIM), lambda i, j: (0, i % tpb, 0)),
        ],
        out_specs=[pl.BlockSpec((None, None, dil, tm // dil, tn), out_map(g)) for g, dil in enumerate(dils)],
        out_shape=[jax.ShapeDtypeStruct((3, batch, dil, seq // dil, tn), BF16) for dil in dils],
        scratch_shapes=[pltpu.VMEM((tm, k), BF16), pltpu.VMEM((heads, tm, HEAD_DIM), F32),
                        pltpu.VMEM((heads, ROW_CHUNK, HEAD_DIM), F32)],
        compiler_params=_params(("arbitrary", "arbitrary"), vmem),
        name="norm_qkv_dil_rope",
    )(x, g.reshape(1, k), w, rot)


def _na_bias_table(rpb):
    nh, n_row_off, n_col_off = rpb.shape
    assert 2 * GRID_W == LANES and NA_KEY_ROWS % 2 == 0 and n_col_off <= LANES
    rpb_lanes = jnp.pad(rpb, ((0, 0), (0, 0), (0, LANES - n_col_off)))
    return pl.pallas_call(
        _na_bias_kernel,
        grid=(nh,),
        in_specs=[pl.BlockSpec((None, n_row_off, LANES), lambda h: (h, 0, 0))],
        out_specs=pl.BlockSpec((None, NA_BLOCK_ROWS * GRID_W, NA_KEY_ROWS * GRID_W), lambda h: (h, 0, 0)),
        out_shape=jax.ShapeDtypeStruct((nh, NA_BLOCK_ROWS * GRID_W, NA_KEY_ROWS * GRID_W), F32),
        compiler_params=_params(("parallel",), 4 * NA_BLOCK_ROWS * GRID_W * NA_KEY_ROWS * GRID_W * 4),
        name="na_bias_table",
    )(rpb_lanes)


def _na_bias_kernel(rpb_ref, o_ref):
    w = GRID_W
    qc = lax.broadcasted_iota(jnp.int32, (w, LANES), 0)
    kc = lax.broadcasted_iota(jnp.int32, (w, LANES), 1)
    start = jnp.clip(qc - NA_WIN_COLS // 2, 0, w - NA_WIN_COLS)
    in_window = (kc >= start) & (kc < start + NA_WIN_COLS)
    by_col = []
    for a in range(rpb_ref.shape[0]):
        row = jnp.broadcast_to(rpb_ref[a:a + 1, :], (w, LANES))
        shifted = pltpu.roll(row, LANES - (NA_WIN_COLS - 1), 1, stride=1, stride_axis=0)
        by_col.append(jnp.where(in_window, shifted * LOG2E, NEG))
    for rr in range(NA_BLOCK_ROWS):
        for kk in range(0, NA_KEY_ROWS, 2):
            a = kk - NA_KEY_BACK - rr + NA_WIN_ROWS - 1
            assert 0 <= a and a + 1 < rpb_ref.shape[0]
            pair = jnp.where(kc < w, by_col[a], pltpu.roll(by_col[a + 1], w, 1))
            o_ref[rr * w:(rr + 1) * w, kk * w:(kk + 2) * w] = pair


def _na_attn_kernel(q_ref, *refs, heads, rows):
    k_refs = refs[:NA_KEY_CHUNKS]
    v_refs = refs[NA_KEY_CHUNKS:2 * NA_KEY_CHUNKS]
    b_ref, o_ref, bm_ref = refs[2 * NA_KEY_CHUNKS:]
    qb = q_ref.shape[0]
    ck = k_refs[0].shape[0]
    kh = min(NA_WIN_ROWS, rows)
    i = pl.program_id(2)

    def clipped(blk):
        r0 = blk * NA_BLOCK_ROWS
        return (r0 < kh // 2) | (r0 + NA_BLOCK_ROWS - 1 - kh // 2 > rows - kh)

    @pl.when((i == 0) | clipped(i) | clipped(i - 1))
    def _():
        r0 = i * NA_BLOCK_ROWS
        q_row = r0 + lax.broadcasted_iota(jnp.int32, (qb, ck), 0) // GRID_W
        lo = jnp.clip(q_row - kh // 2, 0, rows - kh)
        for c in range(NA_KEY_CHUNKS):
            k_row = (r0 - NA_KEY_BACK + c * (ck // GRID_W)
                     + lax.broadcasted_iota(jnp.int32, (qb, ck), 1) // GRID_W)
            row_mask = jnp.where((k_row >= lo) & (k_row < lo + kh), 0.0, NEG).astype(F32)
            for h in range(heads):
                bm_ref[h, :, c * ck:(c + 1) * ck] = b_ref[h, :, c * ck:(c + 1) * ck] + row_mask

    def scores(h):
        sl = slice(h * HEAD_DIM, (h + 1) * HEAD_DIM)
        q = q_ref[:, sl]
        s = []
        for c, k_ref in enumerate(k_refs):
            sc = lax.dot_general(q, k_ref[:, sl], (((1,), (1,)), ((), ())), preferred_element_type=F32)
            s.append(sc + bm_ref[h, :, c * ck:(c + 1) * ck])
        return s, jnp.max(functools.reduce(jnp.maximum, s), axis=-1, keepdims=True)

    def finish(h, s, m):
        sl = slice(h * HEAD_DIM, (h + 1) * HEAD_DIM)
        p = [jnp.exp2(sc - m) for sc in s]
        l = jnp.sum(functools.reduce(jnp.add, p), axis=-1, keepdims=True)
        o = functools.reduce(jnp.add, [
            jnp.dot(pc.astype(BF16), v_ref[:, sl], preferred_element_type=F32)
            for pc, v_ref in zip(p, v_refs)])
        o_ref[:, sl] = (o / l).astype(o_ref.dtype)

    cur = scores(0)
    for h in range(heads):
        nxt = scores(h + 1) if h + 1 < heads else None
        finish(h, *cur)
        cur = nxt


def na_attention(qkv, bias, *, batch, seq, heads_per_step):
    m = qkv.shape[0]
    nh = bias.shape[0]
    hg = heads_per_step
    wq = hg * HEAD_DIM
    qb = NA_BLOCK_ROWS * GRID_W
    kb = NA_KEY_ROWS * GRID_W // NA_KEY_CHUNKS
    assert qb % kb == 0 and (NA_KEY_BACK * GRID_W) % kb == 0
    nblk = seq // qb
    kpb = seq // kb
    ratio = qb // kb
    back = NA_KEY_BACK * GRID_W // kb
    ncol = nh // hg

    def k_map(c, sec):
        def index(g, b, i):
            return (b * kpb + jnp.clip(i * ratio - back + c, 0, kpb - 1), sec * ncol + g)
        return index

    in_specs = [pl.BlockSpec((qb, wq), lambda g, b, i: (b * nblk + i, g))]
    in_specs += [pl.BlockSpec((kb, wq), k_map(c, 1)) for c in range(NA_KEY_CHUNKS)]
    in_specs += [pl.BlockSpec((kb, wq), k_map(c, 2)) for c in range(NA_KEY_CHUNKS)]
    in_specs += [pl.BlockSpec((hg, qb, NA_KEY_CHUNKS * kb), lambda g, b, i: (g, 0, 0),
                              pipeline_mode=pl.Buffered(1))]
    vmem = (2 * 2 * qb * wq * 2 + 2 * 2 * NA_KEY_CHUNKS * kb * wq * 2
            + 2 * hg * qb * NA_KEY_CHUNKS * kb * 4 + 6 * qb * NA_KEY_CHUNKS * kb * 4)
    return pl.pallas_call(
        functools.partial(_na_attn_kernel, heads=hg, rows=seq // GRID_W),
        grid=(ncol, batch, nblk),
        in_specs=in_specs,
        out_specs=pl.BlockSpec((qb, wq), lambda g, b, i: (b * nblk + i, g)),
        out_shape=jax.ShapeDtypeStruct((m, nh * HEAD_DIM), BF16),
        scratch_shapes=[pltpu.VMEM((hg, qb, NA_KEY_CHUNKS * kb), F32)],
        compiler_params=_params(("parallel", "parallel", "arbitrary"), vmem),
        name="na_attention",
    )(qkv, *([qkv] * (2 * NA_KEY_CHUNKS)), bias)


def _proj_residual_kernel(a_ref, w_ref, r_ref, freq_ref, o_ref, rot_ref):
    _, rows, _ = rot_ref.shape
    pos = (pl.program_id(0) * rows + lax.broadcasted_iota(jnp.int32, (rows, LANES), 0)).astype(F32)
    ang = pos * freq_ref[...]
    lane = lax.broadcasted_iota(jnp.int32, (rows, LANES), 1)
    half = ROPE_DIM // 2
    sin = jnp.sin(ang)
    rot_ref[0] = jnp.cos(ang)
    rot_ref[1] = jnp.where((lane >= half) & (lane < ROPE_DIM), sin, 0.0)
    rot_ref[2] = jnp.where(lane < half, -sin, 0.0)

    o_ref[...] = r_ref[...] + jnp.dot(a_ref[...], w_ref[...], preferred_element_type=F32)


def proj_residual(a, w, res, *, tm, rope_seq):
    m, k = a.shape
    n = w.shape[1]
    n_steps = m // tm
    assert rope_seq % n_steps == 0 and (rope_seq // n_steps) % F32_SUBLANES == 0 and HEAD_DIM == LANES
    rope_rows = rope_seq // n_steps
    inv_freq = ROPE_THETA ** (-jnp.arange(0, ROPE_DIM, 2, dtype=F32) / ROPE_DIM)
    freq = jnp.concatenate([inv_freq, inv_freq, jnp.zeros((HEAD_DIM - ROPE_DIM,), F32)]).reshape(1, HEAD_DIM)
    vmem = (2 * tm * k * 2 + 2 * k * n * 2 + 4 * tm * n * 4 + tm * n * 4
            + 2 * 3 * rope_rows * HEAD_DIM * 4 * 3)
    return pl.pallas_call(
        _proj_residual_kernel,
        grid=(n_steps,),
        in_specs=[
            pl.BlockSpec((tm, k), lambda i: (i, 0)),
            pl.BlockSpec((k, n), lambda i: (0, 0)),
            pl.BlockSpec((tm, n), lambda i: (i, 0)),
            pl.BlockSpec((1, HEAD_DIM), lambda i: (0, 0)),
        ],
        out_specs=[pl.BlockSpec((tm, n), lambda i: (i, 0)),
                   pl.BlockSpec((3, rope_rows, HEAD_DIM), lambda i: (0, i, 0))],
        out_shape=[jax.ShapeDtypeStruct((m, n), F32),
                   jax.ShapeDtypeStruct((3, rope_seq, HEAD_DIM), F32)],
        compiler_params=_params(("parallel",), vmem),
        name="na_proj_residual",
    )(a, w, res, freq)


def _ffn_kernel(x_ref, g_ref, w1_ref, w2_ref, gf_ref, *refs, final_norm, n_cast):
    cast_src = refs[:n_cast]
    o_ref = refs[n_cast]
    cast_dst = refs[n_cast + 1:2 * n_cast + 1]
    h_ref = refs[2 * n_cast + 1]
    f = pl.program_id(1)

    def body(first):
        _cast_slabs(cast_src, cast_dst)
        for c in range(x_ref.shape[0] // FFN_ROW_CHUNK):
            rows = slice(c * FFN_ROW_CHUNK, (c + 1) * FFN_ROW_CHUNK)
            if first:
                h_ref[rows, :] = _rms(x_ref[rows, :], g_ref[...]).astype(BF16)
            a = jnp.dot(h_ref[rows, :], w1_ref[...], preferred_element_type=F32)
            a = jnp.square(jnp.maximum(a, 0.0)).astype(BF16)
            y = jnp.dot(a, w2_ref[...], preferred_element_type=F32)
            if first:
                o_ref[rows, :] = x_ref[rows, :] + y
            else:
                o_ref[rows, :] += y

    pl.when(f == 0)(functools.partial(body, True))
    pl.when(f != 0)(functools.partial(body, False))

    if final_norm:
        @pl.when(f == pl.num_programs(1) - 1)
        def _():
            o_ref[...] = _rms(o_ref[...], gf_ref[...])


def ffn(x, g, w1, w2, g_final, *, tm, tf, final_norm, cast=()):
    m, d = x.shape
    dff = w1.shape[1]
    assert tm % FFN_ROW_CHUNK == 0
    n_steps = (m // tm) * (dff // tf)
    vmem = (4 * tm * d * 4 + tm * d * 2 + 2 * 2 * d * tf * 2
            + FFN_ROW_CHUNK * tf * 6 + FFN_ROW_CHUNK * d * 4 + _cast_vmem(cast, n_steps))
    cast_specs = _cast_specs(cast, n_steps, dff // tf)
    return pl.pallas_call(
        functools.partial(_ffn_kernel, final_norm=final_norm, n_cast=len(cast)),
        grid=(m // tm, dff // tf),
        in_specs=[
            pl.BlockSpec((tm, d), _tile_then_next(m // tm, dff // tf)),
            pl.BlockSpec((1, d), lambda i, f: (0, 0)),
            pl.BlockSpec((d, tf), lambda i, f: (0, f)),
            pl.BlockSpec((tf, d), lambda i, f: (f, 0)),
            pl.BlockSpec((1, d), lambda i, f: (0, 0)),
        ] + cast_specs,
        out_specs=[pl.BlockSpec((tm, d), lambda i, f: (i, 0))] + cast_specs,
        out_shape=[jax.ShapeDtypeStruct((m, d), F32)] + [jax.ShapeDtypeStruct(c.shape, BF16) for c in cast],
        scratch_shapes=[pltpu.VMEM((tm, d), BF16)],
        compiler_params=_params(("arbitrary", "arbitrary"), vmem),
        name="ffn_final" if final_norm else "ffn",
    )(x, g.reshape(1, d), w1, w2, g_final.reshape(1, d), *cast)


def _dil_attn_kernel(q_ref, kp_ref, kc_ref, kn_ref, vp_ref, vc_ref, vn_ref, o_ref, lse_ref,
                     *, heads, halo, length):
    qb = q_ref.shape[0]
    sub = BLOCK_Q
    nk = sub + 2 * halo
    m0 = pl.program_id(2) * qb
    qi = lax.broadcasted_iota(jnp.int32, (sub, nk), 0)
    kj = lax.broadcasted_iota(jnp.int32, (sub, nk), 1) - halo
    band = jnp.abs(kj - qi) <= halo
    lane = lax.broadcasted_iota(jnp.int32, (sub, HEAD_DIM), 1)

    def window(prev_ref, cur_ref, next_ref, lo, hi, cols):
        parts = [prev_ref[:, cols]] if lo < 0 else []
        parts.append(cur_ref[max(lo, 0):min(hi, qb), cols])
        if hi > qb:
            parts.append(next_ref[:, cols])
        return parts[0] if len(parts) == 1 else jnp.concatenate(parts, axis=0)

    def scores(u, h):
        rows = slice(u * sub, (u + 1) * sub)
        cols = slice(h * HEAD_DIM, (h + 1) * HEAD_DIM)
        k = window(kp_ref, kc_ref, kn_ref, u * sub - halo, (u + 1) * sub + halo, cols)
        s = lax.dot_general(q_ref[rows, cols], k, (((1,), (1,)), ((), ())), preferred_element_type=F32)
        s = s + masks[u]
        return s, jnp.max(s, axis=-1, keepdims=True)

    def finish(u, h, s, m):
        rows = slice(u * sub, (u + 1) * sub)
        cols = slice(h * HEAD_DIM, (h + 1) * HEAD_DIM)
        v = window(vp_ref, vc_ref, vn_ref, u * sub - halo, (u + 1) * sub + halo, cols)
        p = jnp.exp2(s - m)
        l = jnp.sum(p, axis=-1, keepdims=True)
        o = jnp.dot(p.astype(BF16), v, preferred_element_type=F32)
        o_ref[rows, cols] = (o / l).astype(o_ref.dtype)
        return m * LN2 + jnp.log(l)

    masks = []
    for u in range(qb // sub):
        kpos = kj + (m0 + u * sub)
        masks.append(jnp.where(band & (kpos >= 0) & (kpos < length), 0.0, NEG).astype(F32))

    units = [(u, h) for u in range(qb // sub) for h in range(heads)]
    cur = scores(*units[0])
    lse_tile = None
    for n, (u, h) in enumerate(units):
        nxt = scores(*units[n + 1]) if n + 1 < len(units) else None
        lse = finish(u, h, *cur)
        lse_tile = jnp.where(lane == h, lse, jnp.zeros((sub, HEAD_DIM), F32) if h == 0 else lse_tile)
        if h == heads - 1:
            lse_ref[u * sub:(u + 1) * sub, :] = lse_tile
        cur = nxt


def dil_attention(qkv, *, group, halo):
    _, batch, dil, length, wq = qkv.shape
    heads = wq // HEAD_DIM
    qb = min(DIL_STEP_ROWS, length)
    assert BLOCK_Q % halo == 0 and qb % BLOCK_Q == 0 and length % qb == 0
    nblk = length // qb
    hpb = length // halo
    per = qb // halo

    def cur_map(sec):
        return lambda b, r, i: (sec, b, r, i, 0)

    def prev_map(sec):
        return lambda b, r, i: (sec, b, r, jnp.maximum(i * per - 1, 0), 0)

    def next_map(sec):
        return lambda b, r, i: (sec, b, r, jnp.minimum((i + 1) * per, hpb - 1), 0)

    def spec(rows, index_map):
        return pl.BlockSpec((None, None, None, rows, wq), index_map)

    in_specs = [spec(qb, cur_map(0))]
    for sec in (1, 2):
        in_specs += [spec(halo, prev_map(sec)), spec(qb, cur_map(sec)), spec(halo, next_map(sec))]
    vmem = (2 * (2 * qb * wq * 2 + 2 * (qb + 2 * halo) * wq * 2 + qb * HEAD_DIM * 4)
            + 8 * BLOCK_Q * (BLOCK_Q + 2 * halo) * 4)
    return pl.pallas_call(
        functools.partial(_dil_attn_kernel, heads=heads, halo=halo, length=length),
        grid=(batch, dil, nblk),
        in_specs=in_specs,
        out_specs=[pl.BlockSpec((None, None, qb, wq), lambda b, r, i: (b, r, i, 0)),
                   pl.BlockSpec((None, None, qb, HEAD_DIM), lambda b, r, i: (b, r, i, 0))],
        out_shape=[jax.ShapeDtypeStruct((batch, dil, length, wq), BF16),
                   jax.ShapeDtypeStruct((batch, dil, length, HEAD_DIM), F32)],
        compiler_params=_params(("parallel", "parallel", "arbitrary"), vmem),
        name=f"dil_attention_g{group}",
    )(*([qkv] * 7))


def _merge_proj_kernel(*refs, dils, heads):
    g_n = len(dils)
    o_refs = refs[:g_n]
    lse_refs = refs[g_n:2 * g_n]
    w_ref, r_ref, out_ref, a_ref, lt_ref, ot_ref, tt_ref = refs[2 * g_n:]
    tm = a_ref.shape[0]

    def merged(c):
        rows = slice(c * ROW_CHUNK, (c + 1) * ROW_CHUNK)
        lses = []
        for g, dil in enumerate(dils):
            if dil == 1:
                lses.append(lse_refs[g][0, rows, :])
                continue
            n = ROW_CHUNK // dil
            d1, d2 = _stride_split(dil)
            n1 = ROW_CHUNK // d1

            def to_token_order(dst_ref, dst_idx, src):
                if d2 == 1:
                    for r in range(dil):
                        dst_ref[dst_idx, pl.ds(c * ROW_CHUNK + r, n, stride=dil), :] = src(r)
                    return
                for r1 in range(d1):
                    for r2 in range(d2):
                        tt_ref[pl.ds(r1 * n1 + r2, n, stride=d2), :] = src(r1 + d1 * r2)
                for r1 in range(d1):
                    dst_ref[dst_idx, pl.ds(c * ROW_CHUNK + r1, n1, stride=d1), :] = tt_ref[r1 * n1:(r1 + 1) * n1, :]

            to_token_order(lt_ref, g, lambda r: lse_refs[g][r, c * n:(c + 1) * n, :])
            for h in range(heads):
                to_token_order(ot_ref, g * heads + h, lambda r: o_refs[g][
                    r, c * n:(c + 1) * n, h * HEAD_DIM:(h + 1) * HEAD_DIM].astype(F32))
            lses.append(lt_ref[g, rows, :])

        mx = functools.reduce(jnp.maximum, lses)
        es = [jnp.exp(l - mx) for l in lses]
        den = functools.reduce(jnp.add, es)
        wts = [e / den for e in es]
        for h in range(heads):
            sl = slice(h * HEAD_DIM, (h + 1) * HEAD_DIM)
            acc = None
            for g, dil in enumerate(dils):
                o = o_refs[g][0, rows, sl].astype(F32) if dil == 1 else ot_ref[g * heads + h, rows, :]
                term = wts[g][:, h:h + 1] * o
                acc = term if acc is None else acc + term
            a_ref[rows, sl] = acc.astype(BF16)

    merged(0)
    for c in range(tm // ROW_CHUNK):
        rows = slice(c * ROW_CHUNK, (c + 1) * ROW_CHUNK)
        if c + 1 < tm // ROW_CHUNK:
            merged(c + 1)
        out_ref[rows, :] = r_ref[rows, :] + jnp.dot(a_ref[rows, :], w_ref[...], preferred_element_type=F32)


def merge_proj_residual(outs, lses, w, res, *, seq, tm):
    dils = tuple(o.shape[1] for o in outs)
    k = outs[0].shape[-1]
    m, n = res.shape
    g_n = len(outs)
    heads = k // HEAD_DIM
    tpb = seq // tm
    assert all(ROW_CHUNK % (BF16_SUBLANES * dil) == 0 for dil in dils) and tm % ROW_CHUNK == 0

    def stream_spec(dil, width):
        return pl.BlockSpec((None, dil, tm // dil, width), lambda i: (i // tpb, 0, i % tpb, 0))

    vmem = (2 * g_n * tm * k * 2 + 2 * g_n * tm * HEAD_DIM * 4 + 2 * k * n * 2 + 4 * tm * n * 4
            + tm * k * 2 + tm * n * 4 + g_n * tm * HEAD_DIM * 4 + g_n * tm * k * 4)
    return pl.pallas_call(
        functools.partial(_merge_proj_kernel, dils=dils, heads=heads),
        grid=(m // tm,),
        in_specs=([stream_spec(dil, k) for dil in dils]
                  + [stream_spec(dil, HEAD_DIM) for dil in dils]
                  + [pl.BlockSpec((k, n), lambda i: (0, 0)),
                     pl.BlockSpec((tm, n), lambda i: (i, 0))]),
        out_specs=pl.BlockSpec((tm, n), lambda i: (i, 0)),
        out_shape=jax.ShapeDtypeStruct((m, n), F32),
        scratch_shapes=[pltpu.VMEM((tm, k), BF16),
                        pltpu.VMEM((g_n, tm, HEAD_DIM), F32),
                        pltpu.VMEM((g_n * heads, tm, HEAD_DIM), F32),
                        pltpu.VMEM((ROW_CHUNK, HEAD_DIM), F32)],
        compiler_params=_params(("parallel",), vmem),
        name="dil_merge_proj_residual",
    )(*outs, *lses, w, res)


def kernel(x, na_norm, na_wqkv, na_rpb, na_wo, ffn0_norm, ffn0_w1, ffn0_w2, dil_norm, dil_wqkv, dil_wo,
           ffn1_norm, ffn1_w1, ffn1_w2, final_norm):
    batch, seq, d = x.shape
    m = batch * seq
    rows = seq // GRID_W
    na_heads = na_rpb.shape[0]
    assert rows >= NA_KEY_ROWS and rows % NA_BLOCK_ROWS == 0 and rows >= NA_WIN_ROWS
    assert NA_KEY_ROWS >= NA_BLOCK_ROWS + NA_WIN_ROWS - 1

    xf = x.reshape(m, d)
    bf = lambda w: w.astype(BF16)

    qw = na_heads * HEAD_DIM
    col_scale = jnp.concatenate([jnp.full((qw,), Q_SCALE, F32), jnp.ones((2 * qw,), F32)])
    qkv, na_wo_b, ffn0_w1_b, ffn0_w2_b = norm_matmul(
        xf, na_norm, bf(na_wqkv), col_scale, tm=min(1024, m), tn=min(1024, qw), cast=(na_wo, ffn0_w1, ffn0_w2))
    bias = _na_bias_table(na_rpb)
    attn = na_attention(qkv, bias, batch=batch, seq=seq, heads_per_step=min(16, na_heads))
    xf, rot = proj_residual(attn, na_wo_b, xf, tm=min(512, m), rope_seq=seq)
    xf, dil_wqkv_b, dil_wo_b, ffn1_w1_b, ffn1_w2_b = ffn(
        xf, ffn0_norm, ffn0_w1_b, ffn0_w2_b, final_norm, tm=min(1024, m), tf=512, final_norm=False,
        cast=(dil_wqkv, dil_wo, ffn1_w1, ffn1_w2))

    dils = tuple(dil for _, dil in DIL_GROUPS)
    qkvs = norm_matmul_rope(xf, dil_norm, dil_wqkv_b, rot, batch=batch, seq=seq, tm=min(1024, seq), dils=dils)
    outs, lses = [], []
    for gi, (window, dil) in enumerate(DIL_GROUPS):
        o, lse = dil_attention(qkvs[gi], group=gi, halo=(window // 2) // dil)
        outs.append(o)
        lses.append(lse)
    xf = merge_proj_residual(outs, lses, dil_wo_b, xf, seq=seq, tm=min(512, seq))
    xf, = ffn(xf, ffn1_norm, ffn1_w1_b, ffn1_w2_b, final_norm, tm=min(1024, m), tf=512, final_norm=True)
    return xf.reshape(batch, seq, d)
```

```python
import functools
import math

import numpy as np
import jax
import jax.numpy as jnp
from jax import lax
from jax.experimental import pallas as pl
from jax.experimental.pallas import tpu as pltpu

GRID_W = 64
HEAD_DIM = 128
NA_WIN_ROWS = 8
NA_WIN_COLS = 16
DIL_GROUPS = ((128, 1), (512, 4), (2048, 16))
ROPE_THETA = 500000.0
ROPE_DIM = HEAD_DIM // 4
BLOCK_Q = 128
EPS = 1e-6
NEG = -1e30

LOG2E = math.log2(math.e)
LN2 = math.log(2.0)
Q_SCALE = HEAD_DIM ** -0.5 * LOG2E

NA_BLOCK_ROWS = 4
NA_KEY_BACK = NA_WIN_ROWS // 2
NA_KEY_ROWS = 12
NA_KEY_CHUNKS = 3

DIL_STEP_ROWS = 512
ROW_CHUNK = 256
FFN_ROW_CHUNK = 512
BF16_SUBLANES = 16
F32_SUBLANES = 8
LANES = 128

V7X_VMEM_BYTES = 64 * 1024 * 1024
VMEM_CAP_BYTES = V7X_VMEM_BYTES - 8 * 1024 * 1024

F32 = jnp.float32
BF16 = jnp.bfloat16


def _params(semantics, vmem_bytes):
    limit = min(int(vmem_bytes * 1.25) + (4 << 20), VMEM_CAP_BYTES)
    return pltpu.CompilerParams(dimension_semantics=semantics, vmem_limit_bytes=limit)


def _rms(x, g):
    return x * lax.rsqrt(jnp.mean(x * x, axis=-1, keepdims=True) + EPS) * g


def _tile_then_next(n_tiles, n_inner):
    switch = max(n_inner // 2, 1)
    return lambda i, j: (jnp.minimum(i + jnp.where(j >= switch, 1, 0), n_tiles - 1), 0)


def _cast_rows(n_rows, n_steps):
    for rows in range(BF16_SUBLANES, n_rows + 1, BF16_SUBLANES):
        if n_rows % rows == 0 and n_rows // rows <= n_steps:
            return rows
    raise ValueError((n_rows, n_steps))


def _cast_specs(weights, n_steps, n_inner):
    specs = []
    for w in weights:
        rows = _cast_rows(w.shape[0], n_steps)
        last = w.shape[0] // rows - 1
        specs.append(pl.BlockSpec((rows, w.shape[1]), lambda i, j, last=last: (jnp.minimum(i * n_inner + j, last), 0)))
    return specs


def _cast_vmem(weights, n_steps):
    return sum(2 * _cast_rows(w.shape[0], n_steps) * w.shape[1] * 6 for w in weights)


def _cast_slabs(src_refs, dst_refs):
    for src, dst in zip(src_refs, dst_refs):
        dst[...] = src[...].astype(BF16)


def _norm_matmul_kernel(x_ref, g_ref, w_ref, cs_ref, *refs, n_cast):
    cast_src = refs[:n_cast]
    o_ref = refs[n_cast]
    cast_dst = refs[n_cast + 1:2 * n_cast + 1]
    h_ref = refs[2 * n_cast + 1]

    def body(with_norm):
        _cast_slabs(cast_src, cast_dst)
        for c in range(x_ref.shape[0] // ROW_CHUNK):
            rows = slice(c * ROW_CHUNK, (c + 1) * ROW_CHUNK)
            if with_norm:
                h_ref[rows, :] = _rms(x_ref[rows, :], g_ref[...]).astype(BF16)
            acc = jnp.dot(h_ref[rows, :], w_ref[...], preferred_element_type=F32)
            o_ref[rows, :] = (acc * cs_ref[...]).astype(o_ref.dtype)

    pl.when(pl.program_id(1) == 0)(functools.partial(body, True))
    pl.when(pl.program_id(1) != 0)(functools.partial(body, False))


def norm_matmul(x, g, w, col_scale, *, tm, tn, cast=()):
    m, k = x.shape
    n = w.shape[1]
    n_steps = (m // tm) * (n // tn)
    vmem = (2 * tm * k * 4 + tm * k * 2 + 2 * k * tn * 2 + 2 * tm * tn * 2 + tm * tn * 4
            + _cast_vmem(cast, n_steps))
    cast_specs = _cast_specs(cast, n_steps, n // tn)
    return pl.pallas_call(
        functools.partial(_norm_matmul_kernel, n_cast=len(cast)),
        grid=(m // tm, n // tn),
        in_specs=[
            pl.BlockSpec((tm, k), _tile_then_next(m // tm, n // tn)),
            pl.BlockSpec((1, k), lambda i, j: (0, 0)),
            pl.BlockSpec((k, tn), lambda i, j: (0, j)),
            pl.BlockSpec((1, tn), lambda i, j: (0, j)),
        ] + cast_specs,
        out_specs=[pl.BlockSpec((tm, tn), lambda i, j: (i, j))] + cast_specs,
        out_shape=[jax.ShapeDtypeStruct((m, n), BF16)] + [jax.ShapeDtypeStruct(c.shape, BF16) for c in cast],
        scratch_shapes=[pltpu.VMEM((tm, k), BF16)],
        compiler_params=_params(("arbitrary", "arbitrary"), vmem),
        name="norm_qkv_na",
    )(x, g.reshape(1, k), w, col_scale.reshape(1, n), *cast)


def _norm_matmul_rope_kernel(x_ref, g_ref, w_ref, rot_ref, *refs, dils):
    o_refs = refs[:len(dils)]
    h_ref, s_ref, t_ref = refs[len(dils):]
    j = pl.program_id(1)
    tm = x_ref.shape[0]
    heads = s_ref.shape[0]
    half = ROPE_DIM // 2

    def body(o_ref, dil, rotate, with_norm=False):
        n = ROW_CHUNK // dil
        d1, d2 = _stride_split(dil)
        n1 = ROW_CHUNK // d1
        scale = jnp.where(j % 3 == 0, Q_SCALE, 1.0).astype(F32)
        for c in range(tm // ROW_CHUNK):
            rows = slice(c * ROW_CHUNK, (c + 1) * ROW_CHUNK)
            if with_norm:
                h_ref[rows, :] = _rms(x_ref[rows, :], g_ref[...]).astype(BF16)
            acc = jnp.dot(h_ref[rows, :], w_ref[...], preferred_element_type=F32)
            if rotate:
                cos = rot_ref[0, rows, :] * scale
                sin_hi = rot_ref[1, rows, :] * scale
                sin_lo = rot_ref[2, rows, :] * scale
            for h in range(heads):
                cols = slice(h * HEAD_DIM, (h + 1) * HEAD_DIM)
                a = acc[:, cols]
                if rotate:
                    a = (a * cos + pltpu.roll(a, half, 1) * sin_hi
                         + pltpu.roll(a, HEAD_DIM - half, 1) * sin_lo)
                if dil == 1:
                    o_ref[0, rows, cols] = a.astype(BF16)
                    continue
                s_ref[h, rows, :] = a
                if d2 == 1:
                    for r in range(dil):
                        o_ref[r, c * n:(c + 1) * n, cols] = (
                            s_ref[h, pl.ds(c * ROW_CHUNK + r, n, stride=dil), :].astype(BF16))
                    continue
                for r1 in range(d1):
                    t_ref[h, r1 * n1:(r1 + 1) * n1, :] = s_ref[h, pl.ds(c * ROW_CHUNK + r1, n1, stride=d1), :]
                for r1 in range(d1):
                    for r2 in range(d2):
                        o_ref[r1 + d1 * r2, c * n:(c + 1) * n, cols] = (
                            t_ref[h, pl.ds(r1 * n1 + r2, n, stride=d2), :].astype(BF16))

    pl.when(j == 0)(functools.partial(body, o_refs[0], dils[0], True, with_norm=True))
    for g, dil in enumerate(dils):
        pl.when((j // 3 == g) & (j % 3 != 2) & (j != 0))(functools.partial(body, o_refs[g], dil, True))
        pl.when((j // 3 == g) & (j % 3 == 2))(functools.partial(body, o_refs[g], dil, False))


def _stride_split(dil):
    if dil <= F32_SUBLANES:
        return dil, 1
    d1 = F32_SUBLANES // 2
    assert dil % d1 == 0 and dil // d1 <= F32_SUBLANES
    return d1, dil // d1


def norm_matmul_rope(x, g, w, rot, *, batch, seq, tm, dils):
    m, k = x.shape
    n = w.shape[1]
    tn = n // (3 * len(dils))
    heads = tn // HEAD_DIM
    tpb = seq // tm
    assert all(ROW_CHUNK % (BF16_SUBLANES * dil) == 0 for dil in dils) and tm % ROW_CHUNK == 0

    def out_map(g):
        return lambda i, j: (jnp.clip(j - 3 * g, 0, 2), i // tpb, 0, i % tpb, 0)

    vmem = (2 * tm * k * 4 + tm * k * 2 + 2 * k * tn * 2 + 2 * len(dils) * tm * tn * 2 + tm * tn * 4
            + 2 * ROW_CHUNK * tn * 4 + 2 * 3 * tm * HEAD_DIM * 4 + heads * ROW_CHUNK * HEAD_DIM * 4)
    return pl.pallas_call(
        functools.partial(_norm_matmul_rope_kernel, dils=dils),
        grid=(m // tm, n // tn),
        in_specs=[
            pl.BlockSpec((tm, k), _tile_then_next(m // tm, n // tn)),
            pl.BlockSpec((1, k), lambda i, j: (0, 0)),
            pl.BlockSpec((k, tn), lambda i, j: (0, j)),
            pl.BlockSpec((3, tm, HEAD_DIM), lambda i, j: (0, i % tpb, 0)),
        ],
        out_specs=[pl.BlockSpec((None, None, dil, tm // dil, tn), out_map(g)) for g, dil in enumerate(dils)],
        out_shape=[jax.ShapeDtypeStruct((3, batch, dil, seq // dil, tn), BF16) for dil in dils],
        scratch_shapes=[pltpu.VMEM((tm, k), BF16), pltpu.VMEM((heads, tm, HEAD_DIM), F32),
                        pltpu.VMEM((heads, ROW_CHUNK, HEAD_DIM), F32)],
        compiler_params=_params(("arbitrary", "arbitrary"), vmem),
        name="norm_qkv_dil_rope",
    )(x, g.reshape(1, k), w, rot)


def _na_bias_table(rpb):
    nh, n_row_off, n_col_off = rpb.shape
    assert 2 * GRID_W == LANES and NA_KEY_ROWS % 2 == 0 and n_col_off <= LANES
    rpb_lanes = jnp.pad(rpb, ((0, 0), (0, 0), (0, LANES - n_col_off)))
    return pl.pallas_call(
        _na_bias_kernel,
        grid=(nh,),
        in_specs=[pl.BlockSpec((None, n_row_off, LANES), lambda h: (h, 0, 0))],
        out_specs=pl.BlockSpec((None, NA_BLOCK_ROWS * GRID_W, NA_KEY_ROWS * GRID_W), lambda h: (h, 0, 0)),
        out_shape=jax.ShapeDtypeStruct((nh, NA_BLOCK_ROWS * GRID_W, NA_KEY_ROWS * GRID_W), F32),
        compiler_params=_params(("parallel",), 4 * NA_BLOCK_ROWS * GRID_W * NA_KEY_ROWS * GRID_W * 4),
        name="na_bias_table",
    )(rpb_lanes)


def _na_bias_kernel(rpb_ref, o_ref):
    w = GRID_W
    qc = lax.broadcasted_iota(jnp.int32, (w, LANES), 0)
    kc = lax.broadcasted_iota(jnp.int32, (w, LANES), 1)
    start = jnp.clip(qc - NA_WIN_COLS // 2, 0, w - NA_WIN_COLS)
    in_window = (kc >= start) & (kc < start + NA_WIN_COLS)
    by_col = []
    for a in range(rpb_ref.shape[0]):
        row = jnp.broadcast_to(rpb_ref[a:a + 1, :], (w, LANES))
        shifted = pltpu.roll(row, LANES - (NA_WIN_COLS - 1), 1, stride=1, stride_axis=0)
        by_col.append(jnp.where(in_window, shifted * LOG2E, NEG))
    for rr in range(NA_BLOCK_ROWS):
        for kk in range(0, NA_KEY_ROWS, 2):
            a = kk - NA_KEY_BACK - rr + NA_WIN_ROWS - 1
            assert 0 <= a and a + 1 < rpb_ref.shape[0]
            pair = jnp.where(kc < w, by_col[a], pltpu.roll(by_col[a + 1], w, 1))
            o_ref[rr * w:(rr + 1) * w, kk * w:(kk + 2) * w] = pair


def _na_attn_kernel(q_ref, *refs, heads, rows):
    k_refs = refs[:NA_KEY_CHUNKS]
    v_refs = refs[NA_KEY_CHUNKS:2 * NA_KEY_CHUNKS]
    b_ref, o_ref, bm_ref = refs[2 * NA_KEY_CHUNKS:]
    qb = q_ref.shape[0]
    ck = k_refs[0].shape[0]
    kh = min(NA_WIN_ROWS, rows)
    i = pl.program_id(2)

    def clipped(blk):
        r0 = blk * NA_BLOCK_ROWS
        return (r0 < kh // 2) | (r0 + NA_BLOCK_ROWS - 1 - kh // 2 > rows - kh)

    @pl.when((i == 0) | clipped(i) | clipped(i - 1))
    def _():
        r0 = i * NA_BLOCK_ROWS
        q_row = r0 + lax.broadcasted_iota(jnp.int32, (qb, ck), 0) // GRID_W
        lo = jnp.clip(q_row - kh // 2, 0, rows - kh)
        for c in range(NA_KEY_CHUNKS):
            k_row = (r0 - NA_KEY_BACK + c * (ck // GRID_W)
                     + lax.broadcasted_iota(jnp.int32, (qb, ck), 1) // GRID_W)
            row_mask = jnp.where((k_row >= lo) & (k_row < lo + kh), 0.0, NEG).astype(F32)
            for h in range(heads):
                bm_ref[h, :, c * ck:(c + 1) * ck] = b_ref[h, :, c * ck:(c + 1) * ck] + row_mask

    def scores(h):
        sl = slice(h * HEAD_DIM, (h + 1) * HEAD_DIM)
        q = q_ref[:, sl]
        s = []
        for c, k_ref in enumerate(k_refs):
            sc = lax.dot_general(q, k_ref[:, sl], (((1,), (1,)), ((), ())), preferred_element_type=F32)
            s.append(sc + bm_ref[h, :, c * ck:(c + 1) * ck])
        return s, jnp.max(functools.reduce(jnp.maximum, s), axis=-1, keepdims=True)

    def finish(h, s, m):
        sl = slice(h * HEAD_DIM, (h + 1) * HEAD_DIM)
        p = [jnp.exp2(sc - m) for sc in s]
        l = jnp.sum(functools.reduce(jnp.add, p), axis=-1, keepdims=True)
        o = functools.reduce(jnp.add, [
            jnp.dot(pc.astype(BF16), v_ref[:, sl], preferred_element_type=F32)
            for pc, v_ref in zip(p, v_refs)])
        o_ref[:, sl] = (o / l).astype(o_ref.dtype)

    cur = scores(0)
    for h in range(heads):
        nxt = scores(h + 1) if h + 1 < heads else None
        finish(h, *cur)
        cur = nxt


def na_attention(qkv, bias, *, batch, seq, heads_per_step):
    m = qkv.shape[0]
    nh = bias.shape[0]
    hg = heads_per_step
    wq = hg * HEAD_DIM
    qb = NA_BLOCK_ROWS * GRID_W
    kb = NA_KEY_ROWS * GRID_W // NA_KEY_CHUNKS
    assert qb % kb == 0 and (NA_KEY_BACK * GRID_W) % kb == 0
    nblk = seq // qb
    kpb = seq // kb
    ratio = qb // kb
    back = NA_KEY_BACK * GRID_W // kb
    ncol = nh // hg

    def k_map(c, sec):
        def index(g, b, i):
            return (b * kpb + jnp.clip(i * ratio - back + c, 0, kpb - 1), sec * ncol + g)
        return index

    in_specs = [pl.BlockSpec((qb, wq), lambda g, b, i: (b * nblk + i, g))]
    in_specs += [pl.BlockSpec((kb, wq), k_map(c, 1)) for c in range(NA_KEY_CHUNKS)]
    in_specs += [pl.BlockSpec((kb, wq), k_map(c, 2)) for c in range(NA_KEY_CHUNKS)]
    in_specs += [pl.BlockSpec((hg, qb, NA_KEY_CHUNKS * kb), lambda g, b, i: (g, 0, 0),
                              pipeline_mode=pl.Buffered(1))]
    vmem = (2 * 2 * qb * wq * 2 + 2 * 2 * NA_KEY_CHUNKS * kb * wq * 2
            + 2 * hg * qb * NA_KEY_CHUNKS * kb * 4 + 6 * qb * NA_KEY_CHUNKS * kb * 4)
    return pl.pallas_call(
        functools.partial(_na_attn_kernel, heads=hg, rows=seq // GRID_W),
        grid=(ncol, batch, nblk),
        in_specs=in_specs,
        out_specs=pl.BlockSpec((qb, wq), lambda g, b, i: (b * nblk + i, g)),
        out_shape=jax.ShapeDtypeStruct((m, nh * HEAD_DIM), BF16),
        scratch_shapes=[pltpu.VMEM((hg, qb, NA_KEY_CHUNKS * kb), F32)],
        compiler_params=_params(("parallel", "parallel", "arbitrary"), vmem),
        name="na_attention",
    )(qkv, *([qkv] * (2 * NA_KEY_CHUNKS)), bias)


def _proj_residual_kernel(a_ref, w_ref, r_ref, freq_ref, o_ref, rot_ref):
    _, rows, _ = rot_ref.shape
    pos = (pl.program_id(0) * rows + lax.broadcasted_iota(jnp.int32, (rows, LANES), 0)).astype(F32)
    ang = pos * freq_ref[...]
    lane = lax.broadcasted_iota(jnp.int32, (rows, LANES), 1)
    half = ROPE_DIM // 2
    sin = jnp.sin(ang)
    rot_ref[0] = jnp.cos(ang)
    rot_ref[1] = jnp.where((lane >= half) & (lane < ROPE_DIM), sin, 0.0)
    rot_ref[2] = jnp.where(lane < half, -sin, 0.0)

    o_ref[...] = r_ref[...] + jnp.dot(a_ref[...], w_ref[...], preferred_element_type=F32)


def proj_residual(a, w, res, *, tm, rope_seq):
    m, k = a.shape
    n = w.shape[1]
    n_steps = m // tm
    assert rope_seq % n_steps == 0 and (rope_seq // n_steps) % F32_SUBLANES == 0 and HEAD_DIM == LANES
    rope_rows = rope_seq // n_steps
    inv_freq = ROPE_THETA ** (-jnp.arange(0, ROPE_DIM, 2, dtype=F32) / ROPE_DIM)
    freq = jnp.concatenate([inv_freq, inv_freq, jnp.zeros((HEAD_DIM - ROPE_DIM,), F32)]).reshape(1, HEAD_DIM)
    vmem = (2 * tm * k * 2 + 2 * k * n * 2 + 4 * tm * n * 4 + tm * n * 4
            + 2 * 3 * rope_rows * HEAD_DIM * 4 * 3)
    return pl.pallas_call(
        _proj_residual_kernel,
        grid=(n_steps,),
        in_specs=[
            pl.BlockSpec((tm, k), lambda i: (i, 0)),
            pl.BlockSpec((k, n), lambda i: (0, 0)),
            pl.BlockSpec((tm, n), lambda i: (i, 0)),
            pl.BlockSpec((1, HEAD_DIM), lambda i: (0, 0)),
        ],
        out_specs=[pl.BlockSpec((tm, n), lambda i: (i, 0)),
                   pl.BlockSpec((3, rope_rows, HEAD_DIM), lambda i: (0, i, 0))],
        out_shape=[jax.ShapeDtypeStruct((m, n), F32),
                   jax.ShapeDtypeStruct((3, rope_seq, HEAD_DIM), F32)],
        compiler_params=_params(("parallel",), vmem),
        name="na_proj_residual",
    )(a, w, res, freq)


def _ffn_kernel(x_ref, g_ref, w1_ref, w2_ref, gf_ref, *refs, final_norm, n_cast, cast_steps):
    cast_src = refs[:n_cast]
    o_ref = refs[n_cast]
    cast_dst = refs[n_cast + 1:2 * n_cast + 1]
    h_ref = refs[2 * n_cast + 1]
    f = pl.program_id(1)

    def body(first, with_cast):
        if with_cast:
            _cast_slabs(cast_src, cast_dst)
        for c in range(x_ref.shape[0] // FFN_ROW_CHUNK):
            rows = slice(c * FFN_ROW_CHUNK, (c + 1) * FFN_ROW_CHUNK)
            if first:
                h_ref[rows, :] = _rms(x_ref[rows, :], g_ref[...]).astype(BF16)
            a = jnp.dot(h_ref[rows, :], w1_ref[...], preferred_element_type=F32)
            a = jnp.square(jnp.maximum(a, 0.0)).astype(BF16)
            y = jnp.dot(a, w2_ref[...], preferred_element_type=F32)
            if first:
                o_ref[rows, :] = x_ref[rows, :] + y
            else:
                o_ref[rows, :] += y

    if n_cast:
        casting = pl.program_id(0) * pl.num_programs(1) + f < cast_steps
        pl.when((f == 0) & casting)(functools.partial(body, True, True))
        pl.when((f != 0) & casting)(functools.partial(body, False, True))
        pl.when((f == 0) & ~casting)(functools.partial(body, True, False))
        pl.when((f != 0) & ~casting)(functools.partial(body, False, False))
    else:
        pl.when(f == 0)(functools.partial(body, True, False))
        pl.when(f != 0)(functools.partial(body, False, False))

    if final_norm:
        @pl.when(f == pl.num_programs(1) - 1)
        def _():
            o_ref[...] = _rms(o_ref[...], gf_ref[...])


def ffn(x, g, w1, w2, g_final, *, tm, tf, final_norm, cast=()):
    m, d = x.shape
    dff = w1.shape[1]
    assert tm % FFN_ROW_CHUNK == 0
    n_steps = (m // tm) * (dff // tf)
    vmem = (4 * tm * d * 4 + tm * d * 2 + 2 * 2 * d * tf * 2
            + FFN_ROW_CHUNK * tf * 6 + FFN_ROW_CHUNK * d * 4 + _cast_vmem(cast, n_steps))
    cast_specs = _cast_specs(cast, n_steps, dff // tf)
    return pl.pallas_call(
        functools.partial(_ffn_kernel, final_norm=final_norm, n_cast=len(cast),
                          cast_steps=max((c.shape[0] // _cast_rows(c.shape[0], n_steps) for c in cast), default=0)),
        grid=(m // tm, dff // tf),
        in_specs=[
            pl.BlockSpec((tm, d), _tile_then_next(m // tm, dff // tf)),
            pl.BlockSpec((1, d), lambda i, f: (0, 0)),
            pl.BlockSpec((d, tf), lambda i, f: (0, f)),
            pl.BlockSpec((tf, d), lambda i, f: (f, 0)),
            pl.BlockSpec((1, d), lambda i, f: (0, 0)),
        ] + cast_specs,
        out_specs=[pl.BlockSpec((tm, d), lambda i, f: (i, 0))] + cast_specs,
        out_shape=[jax.ShapeDtypeStruct((m, d), F32)] + [jax.ShapeDtypeStruct(c.shape, BF16) for c in cast],
        scratch_shapes=[pltpu.VMEM((tm, d), BF16)],
        compiler_params=_params(("arbitrary", "arbitrary"), vmem),
        name="ffn_final" if final_norm else "ffn",
    )(x, g.reshape(1, d), w1, w2, g_final.reshape(1, d), *cast)


def _dil_attn_kernel(q_ref, kp_ref, kc_ref, kn_ref, vp_ref, vc_ref, vn_ref, o_ref, lse_ref,
                     *, heads, halo, length):
    qb = q_ref.shape[0]
    sub = BLOCK_Q
    nk = sub + 2 * halo
    m0 = pl.program_id(2) * qb
    qi = lax.broadcasted_iota(jnp.int32, (sub, nk), 0)
    kj = lax.broadcasted_iota(jnp.int32, (sub, nk), 1) - halo
    band = jnp.abs(kj - qi) <= halo
    lane = lax.broadcasted_iota(jnp.int32, (sub, HEAD_DIM), 1)

    def window(prev_ref, cur_ref, next_ref, lo, hi, cols):
        parts = [prev_ref[:, cols]] if lo < 0 else []
        parts.append(cur_ref[max(lo, 0):min(hi, qb), cols])
        if hi > qb:
            parts.append(next_ref[:, cols])
        return parts[0] if len(parts) == 1 else jnp.concatenate(parts, axis=0)

    def scores(u, h):
        rows = slice(u * sub, (u + 1) * sub)
        cols = slice(h * HEAD_DIM, (h + 1) * HEAD_DIM)
        k = window(kp_ref, kc_ref, kn_ref, u * sub - halo, (u + 1) * sub + halo, cols)
        s = lax.dot_general(q_ref[rows, cols], k, (((1,), (1,)), ((), ())), preferred_element_type=F32)
        s = s + masks[u]
        return s, jnp.max(s, axis=-1, keepdims=True)

    def finish(u, h, s, m):
        rows = slice(u * sub, (u + 1) * sub)
        cols = slice(h * HEAD_DIM, (h + 1) * HEAD_DIM)
        v = window(vp_ref, vc_ref, vn_ref, u * sub - halo, (u + 1) * sub + halo, cols)
        p = jnp.exp2(s - m)
        l = jnp.sum(p, axis=-1, keepdims=True)
        o = jnp.dot(p.astype(BF16), v, preferred_element_type=F32)
        o_ref[rows, cols] = (o / l).astype(o_ref.dtype)
        return m * LN2 + jnp.log(l)

    masks = []
    for u in range(qb // sub):
        kpos = kj + (m0 + u * sub)
        masks.append(jnp.where(band & (kpos >= 0) & (kpos < length), 0.0, NEG).astype(F32))

    units = [(u, h) for u in range(qb // sub) for h in range(heads)]
    cur = scores(*units[0])
    lse_tile = None
    for n, (u, h) in enumerate(units):
        nxt = scores(*units[n + 1]) if n + 1 < len(units) else None
        lse = finish(u, h, *cur)
        lse_tile = jnp.where(lane == h, lse, jnp.zeros((sub, HEAD_DIM), F32) if h == 0 else lse_tile)
        if h == heads - 1:
            lse_ref[u * sub:(u + 1) * sub, :] = lse_tile
        cur = nxt


def dil_attention(qkv, *, group, halo):
    _, batch, dil, length, wq = qkv.shape
    heads = wq // HEAD_DIM
    qb = min(DIL_STEP_ROWS, length)
    assert BLOCK_Q % halo == 0 and qb % BLOCK_Q == 0 and length % qb == 0
    nblk = length // qb
    hpb = length // halo
    per = qb // halo

    def cur_map(sec):
        return lambda b, r, i: (sec, b, r, i, 0)

    def prev_map(sec):
        return lambda b, r, i: (sec, b, r, jnp.maximum(i * per - 1, 0), 0)

    def next_map(sec):
        return lambda b, r, i: (sec, b, r, jnp.minimum((i + 1) * per, hpb - 1), 0)

    def spec(rows, index_map):
        return pl.BlockSpec((None, None, None, rows, wq), index_map)

    in_specs = [spec(qb, cur_map(0))]
    for sec in (1, 2):
        in_specs += [spec(halo, prev_map(sec)), spec(qb, cur_map(sec)), spec(halo, next_map(sec))]
    vmem = (2 * (2 * qb * wq * 2 + 2 * (qb + 2 * halo) * wq * 2 + qb * HEAD_DIM * 4)
            + 8 * BLOCK_Q * (BLOCK_Q + 2 * halo) * 4)
    return pl.pallas_call(
        functools.partial(_dil_attn_kernel, heads=heads, halo=halo, length=length),
        grid=(batch, dil, nblk),
        in_specs=in_specs,
        out_specs=[pl.BlockSpec((None, None, qb, wq), lambda b, r, i: (b, r, i, 0)),
                   pl.BlockSpec((None, None, qb, HEAD_DIM), lambda b, r, i: (b, r, i, 0))],
        out_shape=[jax.ShapeDtypeStruct((batch, dil, length, wq), BF16),
                   jax.ShapeDtypeStruct((batch, dil, length, HEAD_DIM), F32)],
        compiler_params=_params(("parallel", "parallel", "arbitrary"), vmem),
        name=f"dil_attention_g{group}",
    )(*([qkv] * 7))


def _merge_proj_kernel(*refs, dils, heads):
    g_n = len(dils)
    o_refs = refs[:g_n]
    lse_refs = refs[g_n:2 * g_n]
    w_ref, r_ref, out_ref, a_ref, lt_ref, ot_ref, tt_ref = refs[2 * g_n:]
    tm = a_ref.shape[0]

    def merged(c):
        rows = slice(c * ROW_CHUNK, (c + 1) * ROW_CHUNK)
        lses = []
        for g, dil in enumerate(dils):
            if dil == 1:
                lses.append(lse_refs[g][0, rows, :])
                continue
            n = ROW_CHUNK // dil
            d1, d2 = _stride_split(dil)
            n1 = ROW_CHUNK // d1

            def to_token_order(dst_ref, dst_idx, src):
                if d2 == 1:
                    for r in range(dil):
                        dst_ref[dst_idx, pl.ds(c * ROW_CHUNK + r, n, stride=dil), :] = src(r)
                    return
                for r1 in range(d1):
                    for r2 in range(d2):
                        tt_ref[pl.ds(r1 * n1 + r2, n, stride=d2), :] = src(r1 + d1 * r2)
                for r1 in range(d1):
                    dst_ref[dst_idx, pl.ds(c * ROW_CHUNK + r1, n1, stride=d1), :] = tt_ref[r1 * n1:(r1 + 1) * n1, :]

            to_token_order(lt_ref, g, lambda r: lse_refs[g][r, c * n:(c + 1) * n, :])
            for h in range(heads):
                to_token_order(ot_ref, g * heads + h, lambda r: o_refs[g][
                    r, c * n:(c + 1) * n, h * HEAD_DIM:(h + 1) * HEAD_DIM].astype(F32))
            lses.append(lt_ref[g, rows, :])

        mx = functools.reduce(jnp.maximum, lses)
        es = [jnp.exp(l - mx) for l in lses]
        den = functools.reduce(jnp.add, es)
        wts = [e / den for e in es]
        for h in range(heads):
            sl = slice(h * HEAD_DIM, (h + 1) * HEAD_DIM)
            acc = None
            for g, dil in enumerate(dils):
                o = o_refs[g][0, rows, sl].astype(F32) if dil == 1 else ot_ref[g * heads + h, rows, :]
                term = wts[g][:, h:h + 1] * o
                acc = term if acc is None else acc + term
            a_ref[rows, sl] = acc.astype(BF16)

    merged(0)
    for c in range(tm // ROW_CHUNK):
        rows = slice(c * ROW_CHUNK, (c + 1) * ROW_CHUNK)
        if c + 1 < tm // ROW_CHUNK:
            merged(c + 1)
        out_ref[rows, :] = r_ref[rows, :] + jnp.dot(a_ref[rows, :], w_ref[...], preferred_element_type=F32)


def merge_proj_residual(outs, lses, w, res, *, seq, tm):
    dils = tuple(o.shape[1] for o in outs)
    k = outs[0].shape[-1]
    m, n = res.shape
    g_n = len(outs)
    heads = k // HEAD_DIM
    tpb = seq // tm
    assert all(ROW_CHUNK % (BF16_SUBLANES * dil) == 0 for dil in dils) and tm % ROW_CHUNK == 0

    def stream_spec(dil, width):
        return pl.BlockSpec((None, dil, tm // dil, width), lambda i: (i // tpb, 0, i % tpb, 0))

    vmem = (2 * g_n * tm * k * 2 + 2 * g_n * tm * HEAD_DIM * 4 + 2 * k * n * 2 + 4 * tm * n * 4
            + tm * k * 2 + tm * n * 4 + g_n * tm * HEAD_DIM * 4 + g_n * tm * k * 4)
    return pl.pallas_call(
        functools.partial(_merge_proj_kernel, dils=dils, heads=heads),
        grid=(m // tm,),
        in_specs=([stream_spec(dil, k) for dil in dils]
                  + [stream_spec(dil, HEAD_DIM) for dil in dils]
                  + [pl.BlockSpec((k, n), lambda i: (0, 0)),
                     pl.BlockSpec((tm, n), lambda i: (i, 0))]),
        out_specs=pl.BlockSpec((tm, n), lambda i: (i, 0)),
        out_shape=jax.ShapeDtypeStruct((m, n), F32),
        scratch_shapes=[pltpu.VMEM((tm, k), BF16),
                        pltpu.VMEM((g_n, tm, HEAD_DIM), F32),
                        pltpu.VMEM((g_n * heads, tm, HEAD_DIM), F32),
                        pltpu.VMEM((ROW_CHUNK, HEAD_DIM), F32)],
        compiler_params=_params(("parallel",), vmem),
        name="dil_merge_proj_residual",
    )(*outs, *lses, w, res)


def kernel(x, na_norm, na_wqkv, na_rpb, na_wo, ffn0_norm, ffn0_w1, ffn0_w2, dil_norm, dil_wqkv, dil_wo,
           ffn1_norm, ffn1_w1, ffn1_w2, final_norm):
    batch, seq, d = x.shape
    m = batch * seq
    rows = seq // GRID_W
    na_heads = na_rpb.shape[0]
    assert rows >= NA_KEY_ROWS and rows % NA_BLOCK_ROWS == 0 and rows >= NA_WIN_ROWS
    assert NA_KEY_ROWS >= NA_BLOCK_ROWS + NA_WIN_ROWS - 1

    xf = x.reshape(m, d)
    bf = lambda w: w.astype(BF16)

    qw = na_heads * HEAD_DIM
    col_scale = jnp.concatenate([jnp.full((qw,), Q_SCALE, F32), jnp.ones((2 * qw,), F32)])
    qkv, na_wo_b, ffn0_w1_b, ffn0_w2_b = norm_matmul(
        xf, na_norm, bf(na_wqkv), col_scale, tm=min(1024, m), tn=min(1024, qw), cast=(na_wo, ffn0_w1, ffn0_w2))
    bias = _na_bias_table(na_rpb)
    attn = na_attention(qkv, bias, batch=batch, seq=seq, heads_per_step=min(16, na_heads))
    xf, rot = proj_residual(attn, na_wo_b, xf, tm=min(512, m), rope_seq=seq)
    xf, dil_wqkv_b, dil_wo_b, ffn1_w1_b, ffn1_w2_b = ffn(
        xf, ffn0_norm, ffn0_w1_b, ffn0_w2_b, final_norm, tm=min(1024, m), tf=512, final_norm=False,
        cast=(dil_wqkv, dil_wo, ffn1_w1, ffn1_w2))

    dils = tuple(dil for _, dil in DIL_GROUPS)
    qkvs = norm_matmul_rope(xf, dil_norm, dil_wqkv_b, rot, batch=batch, seq=seq, tm=min(1024, seq), dils=dils)
    outs, lses = [], []
    for gi, (window, dil) in enumerate(DIL_GROUPS):
        o, lse = dil_attention(qkvs[gi], group=gi, halo=(window // 2) // dil)
        outs.append(o)
        lses.append(lse)
    xf = merge_proj_residual(outs, lses, dil_wo_b, xf, seq=seq, tm=min(512, seq))
    xf, = ffn(xf, ffn1_norm, ffn1_w1_b, ffn1_w2_b, final_norm, tm=min(1024, m), tf=512, final_norm=True)
    return xf.reshape(batch, seq, d)
```

```python
import functools
import math

import numpy as np
import jax
import jax.numpy as jnp
from jax import lax
from jax.experimental import pallas as pl
from jax.experimental.pallas import tpu as pltpu

GRID_W = 64
HEAD_DIM = 128
NA_WIN_ROWS = 8
NA_WIN_COLS = 16
DIL_GROUPS = ((128, 1), (512, 4), (2048, 16))
ROPE_THETA = 500000.0
ROPE_DIM = HEAD_DIM // 4
BLOCK_Q = 128
EPS = 1e-6
NEG = -1e30

LOG2E = math.log2(math.e)
LN2 = math.log(2.0)
Q_SCALE = HEAD_DIM ** -0.5 * LOG2E

NA_BLOCK_ROWS = 4
NA_KEY_BACK = NA_WIN_ROWS // 2
NA_KEY_ROWS = 12
NA_KEY_CHUNKS = 3

DIL_STEP_ROWS = 512
ROW_CHUNK = 256
FFN_ROW_CHUNK = 512
BF16_SUBLANES = 16
F32_SUBLANES = 8
LANES = 128

V7X_VMEM_BYTES = 64 * 1024 * 1024
VMEM_CAP_BYTES = V7X_VMEM_BYTES - 8 * 1024 * 1024

F32 = jnp.float32
BF16 = jnp.bfloat16


def _params(semantics, vmem_bytes):
    limit = min(int(vmem_bytes * 1.25) + (4 << 20), VMEM_CAP_BYTES)
    return pltpu.CompilerParams(dimension_semantics=semantics, vmem_limit_bytes=limit)


def _rms(x, g):
    return x * lax.rsqrt(jnp.mean(x * x, axis=-1, keepdims=True) + EPS) * g


def _tile_then_next(n_tiles, n_inner):
    switch = max(n_inner // 2, 1)
    return lambda i, j: (jnp.minimum(i + jnp.where(j >= switch, 1, 0), n_tiles - 1), 0)


def _cast_rows(n_rows, n_steps):
    for rows in range(BF16_SUBLANES, n_rows + 1, BF16_SUBLANES):
        if n_rows % rows == 0 and n_rows // rows <= n_steps:
            return rows
    raise ValueError((n_rows, n_steps))


def _cast_specs(weights, n_steps, n_inner):
    specs = []
    for w in weights:
        rows = _cast_rows(w.shape[0], n_steps)
        last = w.shape[0] // rows - 1
        specs.append(pl.BlockSpec((rows, w.shape[1]), lambda i, j, last=last: (jnp.minimum(i * n_inner + j, last), 0)))
    return specs


def _cast_vmem(weights, n_steps):
    return sum(2 * _cast_rows(w.shape[0], n_steps) * w.shape[1] * 6 for w in weights)


def _cast_slabs(src_refs, dst_refs):
    for src, dst in zip(src_refs, dst_refs):
        dst[...] = src[...].astype(BF16)


def _norm_matmul_kernel(x_ref, g_ref, w_ref, cs_ref, *refs, n_cast):
    cast_src = refs[:n_cast]
    o_ref = refs[n_cast]
    cast_dst = refs[n_cast + 1:2 * n_cast + 1]
    h_ref = refs[2 * n_cast + 1]

    def body(with_norm):
        _cast_slabs(cast_src, cast_dst)
        for c in range(x_ref.shape[0] // ROW_CHUNK):
            rows = slice(c * ROW_CHUNK, (c + 1) * ROW_CHUNK)
            if with_norm:
                h_ref[rows, :] = _rms(x_ref[rows, :], g_ref[...]).astype(BF16)
            acc = jnp.dot(h_ref[rows, :], w_ref[...], preferred_element_type=F32)
            o_ref[rows, :] = (acc * cs_ref[...]).astype(o_ref.dtype)

    pl.when(pl.program_id(1) == 0)(functools.partial(body, True))
    pl.when(pl.program_id(1) != 0)(functools.partial(body, False))


def norm_matmul(x, g, w, col_scale, *, tm, tn, cast=()):
    m, k = x.shape
    n = w.shape[1]
    n_steps = (m // tm) * (n // tn)
    vmem = (2 * tm * k * 4 + tm * k * 2 + 2 * k * tn * 2 + 2 * tm * tn * 2 + tm * tn * 4
            + _cast_vmem(cast, n_steps))
    cast_specs = _cast_specs(cast, n_steps, n // tn)
    return pl.pallas_call(
        functools.partial(_norm_matmul_kernel, n_cast=len(cast)),
        grid=(m // tm, n // tn),
        in_specs=[
            pl.BlockSpec((tm, k), _tile_then_next(m // tm, n // tn)),
            pl.BlockSpec((1, k), lambda i, j: (0, 0)),
            pl.BlockSpec((k, tn), lambda i, j: (0, j)),
            pl.BlockSpec((1, tn), lambda i, j: (0, j)),
        ] + cast_specs,
        out_specs=[pl.BlockSpec((tm, tn), lambda i, j: (i, j))] + cast_specs,
        out_shape=[jax.ShapeDtypeStruct((m, n), BF16)] + [jax.ShapeDtypeStruct(c.shape, BF16) for c in cast],
        scratch_shapes=[pltpu.VMEM((tm, k), BF16)],
        compiler_params=_params(("arbitrary", "arbitrary"), vmem),
        name="norm_qkv_na",
    )(x, g.reshape(1, k), w, col_scale.reshape(1, n), *cast)


def _norm_matmul_rope_kernel(x_ref, g_ref, w_ref, rot_ref, *refs, dils):
    o_refs = refs[:len(dils)]
    h_ref, s_ref, t_ref = refs[len(dils):]
    j = pl.program_id(1)
    tm = x_ref.shape[0]
    heads = s_ref.shape[0]
    half = ROPE_DIM // 2

    def body(o_ref, dil, rotate, with_norm=False):
        n = ROW_CHUNK // dil
        d1, d2 = _stride_split(dil)
        n1 = ROW_CHUNK // d1
        scale = jnp.where(j % 3 == 0, Q_SCALE, 1.0).astype(F32)
        for c in range(tm // ROW_CHUNK):
            rows = slice(c * ROW_CHUNK, (c + 1) * ROW_CHUNK)
            if with_norm:
                h_ref[rows, :] = _rms(x_ref[rows, :], g_ref[...]).astype(BF16)
            acc = jnp.dot(h_ref[rows, :], w_ref[...], preferred_element_type=F32)
            if rotate:
                cos = rot_ref[0, rows, :] * scale
                sin_hi = rot_ref[1, rows, :] * scale
                sin_lo = rot_ref[2, rows, :] * scale
            for h in range(heads):
                cols = slice(h * HEAD_DIM, (h + 1) * HEAD_DIM)
                a = acc[:, cols]
                if rotate:
                    a = (a * cos + pltpu.roll(a, half, 1) * sin_hi
                         + pltpu.roll(a, HEAD_DIM - half, 1) * sin_lo)
                if dil == 1:
                    o_ref[0, rows, cols] = a.astype(BF16)
                    continue
                s_ref[h, rows, :] = a
                if d2 == 1:
                    for r in range(dil):
                        o_ref[r, c * n:(c + 1) * n, cols] = (
                            s_ref[h, pl.ds(c * ROW_CHUNK + r, n, stride=dil), :].astype(BF16))
                    continue
                for r1 in range(d1):
                    t_ref[h, r1 * n1:(r1 + 1) * n1, :] = s_ref[h, pl.ds(c * ROW_CHUNK + r1, n1, stride=d1), :]
                for r1 in range(d1):
                    for r2 in range(d2):
                        o_ref[r1 + d1 * r2, c * n:(c + 1) * n, cols] = (
                            t_ref[h, pl.ds(r1 * n1 + r2, n, stride=d2), :].astype(BF16))

    pl.when(j == 0)(functools.partial(body, o_refs[0], dils[0], True, with_norm=True))
    for g, dil in enumerate(dils):
        pl.when((j // 3 == g) & (j % 3 != 2) & (j != 0))(functools.partial(body, o_refs[g], dil, True))
        pl.when((j // 3 == g) & (j % 3 == 2))(functools.partial(body, o_refs[g], dil, False))


def _stride_split(dil):
    if dil <= F32_SUBLANES:
        return dil, 1
    d1 = F32_SUBLANES // 2
    assert dil % d1 == 0 and dil // d1 <= F32_SUBLANES
    return d1, dil // d1


def norm_matmul_rope(x, g, w, rot, *, batch, seq, tm, dils):
    m, k = x.shape
    n = w.shape[1]
    tn = n // (3 * len(dils))
    heads = tn // HEAD_DIM
    tpb = seq // tm
    assert all(ROW_CHUNK % (BF16_SUBLANES * dil) == 0 for dil in dils) and tm % ROW_CHUNK == 0

    def out_map(g):
        return lambda i, j: (jnp.clip(j - 3 * g, 0, 2), i // tpb, 0, i % tpb, 0)

    vmem = (2 * tm * k * 4 + tm * k * 2 + 2 * k * tn * 2 + 2 * len(dils) * tm * tn * 2 + tm * tn * 4
            + 2 * ROW_CHUNK * tn * 4 + 2 * 3 * tm * HEAD_DIM * 4 + heads * ROW_CHUNK * HEAD_DIM * 4)
    return pl.pallas_call(
        functools.partial(_norm_matmul_rope_kernel, dils=dils),
        grid=(m // tm, n // tn),
        in_specs=[
            pl.BlockSpec((tm, k), _tile_then_next(m // tm, n // tn)),
            pl.BlockSpec((1, k), lambda i, j: (0, 0)),
            pl.BlockSpec((k, tn), lambda i, j: (0, j)),
            pl.BlockSpec((3, tm, HEAD_DIM), lambda i, j: (0, i % tpb, 0)),
        ],
        out_specs=[pl.BlockSpec((None, None, dil, tm // dil, tn), out_map(g)) for g, dil in enumerate(dils)],
        out_shape=[jax.ShapeDtypeStruct((3, batch, dil, seq // dil, tn), BF16) for dil in dils],
        scratch_shapes=[pltpu.VMEM((tm, k), BF16), pltpu.VMEM((heads, tm, HEAD_DIM), F32),
                        pltpu.VMEM((heads, ROW_CHUNK, HEAD_DIM), F32)],
        compiler_params=_params(("arbitrary", "arbitrary"), vmem),
        name="norm_qkv_dil_rope",
    )(x, g.reshape(1, k), w, rot)


def _na_bias_table(rpb):
    nh, n_row_off, n_col_off = rpb.shape
    assert 2 * GRID_W == LANES and NA_KEY_ROWS % 2 == 0 and n_col_off <= LANES
    rpb_lanes = jnp.pad(rpb, ((0, 0), (0, 0), (0, LANES - n_col_off)))
    return pl.pallas_call(
        _na_bias_kernel,
        grid=(nh,),
        in_specs=[pl.BlockSpec((None, n_row_off, LANES), lambda h: (h, 0, 0))],
        out_specs=pl.BlockSpec((None, NA_BLOCK_ROWS * GRID_W, NA_KEY_ROWS * GRID_W), lambda h: (h, 0, 0)),
        out_shape=jax.ShapeDtypeStruct((nh, NA_BLOCK_ROWS * GRID_W, NA_KEY_ROWS * GRID_W), F32),
        compiler_params=_params(("parallel",), 4 * NA_BLOCK_ROWS * GRID_W * NA_KEY_ROWS * GRID_W * 4),
        name="na_bias_table",
    )(rpb_lanes)


def _na_bias_kernel(rpb_ref, o_ref):
    w = GRID_W
    qc = lax.broadcasted_iota(jnp.int32, (w, LANES), 0)
    kc = lax.broadcasted_iota(jnp.int32, (w, LANES), 1)
    start = jnp.clip(qc - NA_WIN_COLS // 2, 0, w - NA_WIN_COLS)
    in_window = (kc >= start) & (kc < start + NA_WIN_COLS)
    by_col = []
    for a in range(rpb_ref.shape[0]):
        row = jnp.broadcast_to(rpb_ref[a:a + 1, :], (w, LANES))
        shifted = pltpu.roll(row, LANES - (NA_WIN_COLS - 1), 1, stride=1, stride_axis=0)
        by_col.append(jnp.where(in_window, shifted * LOG2E, NEG))
    for rr in range(NA_BLOCK_ROWS):
        for kk in range(0, NA_KEY_ROWS, 2):
            a = kk - NA_KEY_BACK - rr + NA_WIN_ROWS - 1
            assert 0 <= a and a + 1 < rpb_ref.shape[0]
            pair = jnp.where(kc < w, by_col[a], pltpu.roll(by_col[a + 1], w, 1))
            o_ref[rr * w:(rr + 1) * w, kk * w:(kk + 2) * w] = pair


def _na_attn_kernel(q_ref, *refs, heads, rows):
    k_refs = refs[:NA_KEY_CHUNKS]
    v_refs = refs[NA_KEY_CHUNKS:2 * NA_KEY_CHUNKS]
    b_ref, o_ref, bm_ref = refs[2 * NA_KEY_CHUNKS:]
    qb = q_ref.shape[0]
    ck = k_refs[0].shape[0]
    kh = min(NA_WIN_ROWS, rows)
    i = pl.program_id(2)

    def clipped(blk):
        r0 = blk * NA_BLOCK_ROWS
        return (r0 < kh // 2) | (r0 + NA_BLOCK_ROWS - 1 - kh // 2 > rows - kh)

    @pl.when((i == 0) | clipped(i) | clipped(i - 1))
    def _():
        r0 = i * NA_BLOCK_ROWS
        q_row = r0 + lax.broadcasted_iota(jnp.int32, (qb, ck), 0) // GRID_W
        lo = jnp.clip(q_row - kh // 2, 0, rows - kh)
        for c in range(NA_KEY_CHUNKS):
            k_row = (r0 - NA_KEY_BACK + c * (ck // GRID_W)
                     + lax.broadcasted_iota(jnp.int32, (qb, ck), 1) // GRID_W)
            row_mask = jnp.where((k_row >= lo) & (k_row < lo + kh), 0.0, NEG).astype(F32)
            for h in range(heads):
                bm_ref[h, :, c * ck:(c + 1) * ck] = b_ref[h, :, c * ck:(c + 1) * ck] + row_mask

    def scores(h):
        sl = slice(h * HEAD_DIM, (h + 1) * HEAD_DIM)
        q = q_ref[:, sl]
        s = []
        for c, k_ref in enumerate(k_refs):
            sc = lax.dot_general(q, k_ref[:, sl], (((1,), (1,)), ((), ())), preferred_element_type=F32)
            s.append(sc + bm_ref[h, :, c * ck:(c + 1) * ck])
        return s, jnp.max(functools.reduce(jnp.maximum, s), axis=-1, keepdims=True)

    def finish(h, s, m):
        sl = slice(h * HEAD_DIM, (h + 1) * HEAD_DIM)
        p = [jnp.exp2(sc - m) for sc in s]
        l = jnp.sum(functools.reduce(jnp.add, p), axis=-1, keepdims=True)
        o = functools.reduce(jnp.add, [
            jnp.dot(pc.astype(BF16), v_ref[:, sl], preferred_element_type=F32)
            for pc, v_ref in zip(p, v_refs)])
        o_ref[:, sl] = (o / l).astype(o_ref.dtype)

    cur = scores(0)
    for h in range(heads):
        nxt = scores(h + 1) if h + 1 < heads else None
        finish(h, *cur)
        cur = nxt


def na_attention(qkv, bias, *, batch, seq, heads_per_step):
    m = qkv.shape[0]
    nh = bias.shape[0]
    hg = heads_per_step
    wq = hg * HEAD_DIM
    qb = NA_BLOCK_ROWS * GRID_W
    kb = NA_KEY_ROWS * GRID_W // NA_KEY_CHUNKS
    assert qb % kb == 0 and (NA_KEY_BACK * GRID_W) % kb == 0
    nblk = seq // qb
    kpb = seq // kb
    ratio = qb // kb
    back = NA_KEY_BACK * GRID_W // kb
    ncol = nh // hg

    def k_map(c, sec):
        def index(g, b, i):
            return (b * kpb + jnp.clip(i * ratio - back + c, 0, kpb - 1), sec * ncol + g)
        return index

    in_specs = [pl.BlockSpec((qb, wq), lambda g, b, i: (b * nblk + i, g))]
    in_specs += [pl.BlockSpec((kb, wq), k_map(c, 1)) for c in range(NA_KEY_CHUNKS)]
    in_specs += [pl.BlockSpec((kb, wq), k_map(c, 2)) for c in range(NA_KEY_CHUNKS)]
    in_specs += [pl.BlockSpec((hg, qb, NA_KEY_CHUNKS * kb), lambda g, b, i: (g, 0, 0),
                              pipeline_mode=pl.Buffered(1))]
    vmem = (2 * 2 * qb * wq * 2 + 2 * 2 * NA_KEY_CHUNKS * kb * wq * 2
            + 2 * hg * qb * NA_KEY_CHUNKS * kb * 4 + 6 * qb * NA_KEY_CHUNKS * kb * 4)
    return pl.pallas_call(
        functools.partial(_na_attn_kernel, heads=hg, rows=seq // GRID_W),
        grid=(ncol, batch, nblk),
        in_specs=in_specs,
        out_specs=pl.BlockSpec((qb, wq), lambda g, b, i: (b * nblk + i, g)),
        out_shape=jax.ShapeDtypeStruct((m, nh * HEAD_DIM), BF16),
        scratch_shapes=[pltpu.VMEM((hg, qb, NA_KEY_CHUNKS * kb), F32)],
        compiler_params=_params(("parallel", "parallel", "arbitrary"), vmem),
        name="na_attention",
    )(qkv, *([qkv] * (2 * NA_KEY_CHUNKS)), bias)


NA_PROJ_HEADS = 4


def _na_block_variant(blk, nblk):
    return jnp.where(blk == 0, 0, jnp.where(blk == nblk - 1, 2, 1))


def _na_bias_variants(rpb, rows):
    nh, n_row_off, n_col_off = rpb.shape
    kh = min(NA_WIN_ROWS, rows)
    assert 2 * GRID_W == LANES and NA_KEY_ROWS % 2 == 0 and n_col_off <= LANES
    row_ok = np.zeros((3, NA_BLOCK_ROWS, NA_KEY_ROWS), bool)
    for v, r0 in enumerate((0, NA_BLOCK_ROWS, rows - NA_BLOCK_ROWS)):
        for rr in range(NA_BLOCK_ROWS):
            lo = int(np.clip(r0 + rr - kh // 2, 0, rows - kh))
            for kk in range(NA_KEY_ROWS):
                row_ok[v, rr, kk] = lo <= r0 - NA_KEY_BACK + kk < lo + kh
    rpb_lanes = jnp.pad(rpb, ((0, 0), (0, 0), (0, LANES - n_col_off)))
    return pl.pallas_call(
        functools.partial(_na_bias_variants_kernel, row_ok=row_ok),
        grid=(3, nh),
        in_specs=[pl.BlockSpec((None, n_row_off, LANES), lambda v, h: (h, 0, 0))],
        out_specs=pl.BlockSpec((None, None, NA_BLOCK_ROWS * GRID_W, NA_KEY_ROWS * GRID_W), lambda v, h: (v, h, 0, 0)),
        out_shape=jax.ShapeDtypeStruct((3, nh, NA_BLOCK_ROWS * GRID_W, NA_KEY_ROWS * GRID_W), F32),
        compiler_params=_params(("arbitrary", "arbitrary"), 4 * NA_BLOCK_ROWS * GRID_W * NA_KEY_ROWS * GRID_W * 4),
        name="na_bias_variants",
    )(rpb_lanes)


def _na_bias_variants_kernel(rpb_ref, o_ref, *, row_ok):
    w = GRID_W
    qc = lax.broadcasted_iota(jnp.int32, (w, LANES), 0)
    kc = lax.broadcasted_iota(jnp.int32, (w, LANES), 1)
    start = jnp.clip(qc - NA_WIN_COLS // 2, 0, w - NA_WIN_COLS)
    in_window = (kc >= start) & (kc < start + NA_WIN_COLS)
    by_col = []
    for a in range(rpb_ref.shape[0]):
        row = jnp.broadcast_to(rpb_ref[a:a + 1, :], (w, LANES))
        shifted = pltpu.roll(row, LANES - (NA_WIN_COLS - 1), 1, stride=1, stride_axis=0)
        by_col.append(jnp.where(in_window, shifted * LOG2E, NEG))
    for v in range(3):
        @pl.when(pl.program_id(0) == v)
        def _(v=v):
            for rr in range(NA_BLOCK_ROWS):
                for kk in range(0, NA_KEY_ROWS, 2):
                    a = kk - NA_KEY_BACK - rr + NA_WIN_ROWS - 1
                    lo_half = by_col[a] if row_ok[v, rr, kk] else jnp.full((w, LANES), NEG, F32)
                    hi_half = (pltpu.roll(by_col[a + 1], w, 1) if row_ok[v, rr, kk + 1]
                               else jnp.full((w, LANES), NEG, F32))
                    o_ref[rr * w:(rr + 1) * w, kk * w:(kk + 2) * w] = jnp.where(kc < w, lo_half, hi_half)


def _na_attn_proj_kernel(q_ref, *refs, heads):
    k_refs = refs[:NA_KEY_CHUNKS]
    v_refs = refs[NA_KEY_CHUNKS:2 * NA_KEY_CHUNKS]
    b_ref, w_ref, r_ref, o_ref, a_ref = refs[2 * NA_KEY_CHUNKS:]
    ck = k_refs[0].shape[0]

    def scores(h):
        sl = slice(h * HEAD_DIM, (h + 1) * HEAD_DIM)
        q = q_ref[:, sl]
        s = []
        for c, k_ref in enumerate(k_refs):
            sc = lax.dot_general(q, k_ref[:, sl], (((1,), (1,)), ((), ())), preferred_element_type=F32)
            s.append(sc + b_ref[h, :, c * ck:(c + 1) * ck])
        return s, jnp.max(functools.reduce(jnp.maximum, s), axis=-1, keepdims=True)

    def finish(h, s, m):
        sl = slice(h * HEAD_DIM, (h + 1) * HEAD_DIM)
        p = [jnp.exp2(sc - m) for sc in s]
        l = jnp.sum(functools.reduce(jnp.add, p), axis=-1, keepdims=True)
        o = functools.reduce(jnp.add, [
            jnp.dot(pc.astype(BF16), v_ref[:, sl], preferred_element_type=F32)
            for pc, v_ref in zip(p, v_refs)])
        a_ref[:, sl] = (o / l).astype(BF16)

    grp = math.gcd(NA_PROJ_HEADS, heads)
    cur = scores(0)
    for h in range(heads):
        nxt = scores(h + 1) if h + 1 < heads else None
        finish(h, *cur)
        cur = nxt
        if (h + 1) % grp == 0:
            cols = slice((h + 1 - grp) * HEAD_DIM, (h + 1) * HEAD_DIM)
            y = jnp.dot(a_ref[:, cols], w_ref[cols, :], preferred_element_type=F32)
            if h + 1 == grp:
                o_ref[...] = r_ref[...] + y
            else:
                o_ref[...] += y


def na_attention_proj(qkv, bias3, w, res, *, batch, seq):
    m = qkv.shape[0]
    nh = bias3.shape[1]
    wq = nh * HEAD_DIM
    n = w.shape[1]
    rows = seq // GRID_W
    kh = min(NA_WIN_ROWS, rows)
    qb = NA_BLOCK_ROWS * GRID_W
    kb = NA_KEY_ROWS * GRID_W // NA_KEY_CHUNKS
    assert qb % kb == 0 and (NA_KEY_BACK * GRID_W) % kb == 0
    assert NA_BLOCK_ROWS >= kh // 2 and rows - 2 * NA_BLOCK_ROWS + NA_BLOCK_ROWS - 1 - kh // 2 <= rows - kh
    nblk = seq // qb
    kpb = seq // kb
    ratio = qb // kb
    back = NA_KEY_BACK * GRID_W // kb

    def k_map(c, sec):
        return lambda b, i: (b * kpb + jnp.clip(i * ratio - back + c, 0, kpb - 1), sec)

    in_specs = [pl.BlockSpec((qb, wq), lambda b, i: (b * nblk + i, 0))]
    in_specs += [pl.BlockSpec((kb, wq), k_map(c, 1)) for c in range(NA_KEY_CHUNKS)]
    in_specs += [pl.BlockSpec((kb, wq), k_map(c, 2)) for c in range(NA_KEY_CHUNKS)]
    in_specs += [pl.BlockSpec((None, nh, qb, NA_KEY_CHUNKS * kb), lambda b, i: (_na_block_variant(i, nblk), 0, 0, 0),
                              pipeline_mode=pl.Buffered(1)),
                 pl.BlockSpec((wq, n), lambda b, i: (0, 0), pipeline_mode=pl.Buffered(1)),
                 pl.BlockSpec((qb, n), lambda b, i: (b * nblk + i, 0))]
    vmem = (2 * qb * wq * 2 + 2 * 2 * NA_KEY_CHUNKS * kb * wq * 2 + nh * qb * NA_KEY_CHUNKS * kb * 4
            + wq * n * 2 + 4 * qb * n * 4 + qb * wq * 2 + 6 * qb * NA_KEY_CHUNKS * kb * 4 + qb * n * 4)
    return pl.pallas_call(
        functools.partial(_na_attn_proj_kernel, heads=nh),
        grid=(batch, nblk),
        in_specs=in_specs,
        out_specs=pl.BlockSpec((qb, n), lambda b, i: (b * nblk + i, 0)),
        out_shape=jax.ShapeDtypeStruct((m, n), F32),
        scratch_shapes=[pltpu.VMEM((qb, wq), BF16)],
        compiler_params=_params(("parallel", "arbitrary"), vmem),
        name="na_attention_proj",
    )(qkv, *([qkv] * (2 * NA_KEY_CHUNKS)), bias3, w, res)


def _rope_tables(t):
    half = ROPE_DIM // 2
    pos = jnp.arange(t, dtype=F32)
    inv_freq = ROPE_THETA ** (-jnp.arange(0, ROPE_DIM, 2, dtype=F32) / ROPE_DIM)
    ang = pos[:, None] * inv_freq[None, :]
    cos = jnp.cos(ang)
    sin = jnp.sin(ang)
    rest = HEAD_DIM - ROPE_DIM
    one = jnp.ones((t, rest), F32)
    zero = jnp.zeros((t, rest), F32)
    zh = jnp.zeros((t, half), F32)
    return jnp.stack([jnp.concatenate([cos, cos, one], axis=-1),
                      jnp.concatenate([zh, sin, zero], axis=-1),
                      jnp.concatenate([-sin, zh, zero], axis=-1)], axis=0)


def _proj_residual_kernel(a_ref, w_ref, r_ref, freq_ref, o_ref, rot_ref):
    _, rows, _ = rot_ref.shape
    pos = (pl.program_id(0) * rows + lax.broadcasted_iota(jnp.int32, (rows, LANES), 0)).astype(F32)
    ang = pos * freq_ref[...]
    lane = lax.broadcasted_iota(jnp.int32, (rows, LANES), 1)
    half = ROPE_DIM // 2
    sin = jnp.sin(ang)
    rot_ref[0] = jnp.cos(ang)
    rot_ref[1] = jnp.where((lane >= half) & (lane < ROPE_DIM), sin, 0.0)
    rot_ref[2] = jnp.where(lane < half, -sin, 0.0)

    o_ref[...] = r_ref[...] + jnp.dot(a_ref[...], w_ref[...], preferred_element_type=F32)


def proj_residual(a, w, res, *, tm, rope_seq):
    m, k = a.shape
    n = w.shape[1]
    n_steps = m // tm
    assert rope_seq % n_steps == 0 and (rope_seq // n_steps) % F32_SUBLANES == 0 and HEAD_DIM == LANES
    rope_rows = rope_seq // n_steps
    inv_freq = ROPE_THETA ** (-jnp.arange(0, ROPE_DIM, 2, dtype=F32) / ROPE_DIM)
    freq = jnp.concatenate([inv_freq, inv_freq, jnp.zeros((HEAD_DIM - ROPE_DIM,), F32)]).reshape(1, HEAD_DIM)
    vmem = (2 * tm * k * 2 + 2 * k * n * 2 + 4 * tm * n * 4 + tm * n * 4
            + 2 * 3 * rope_rows * HEAD_DIM * 4 * 3)
    return pl.pallas_call(
        _proj_residual_kernel,
        grid=(n_steps,),
        in_specs=[
            pl.BlockSpec((tm, k), lambda i: (i, 0)),
            pl.BlockSpec((k, n), lambda i: (0, 0)),
            pl.BlockSpec((tm, n), lambda i: (i, 0)),
            pl.BlockSpec((1, HEAD_DIM), lambda i: (0, 0)),
        ],
        out_specs=[pl.BlockSpec((tm, n), lambda i: (i, 0)),
                   pl.BlockSpec((3, rope_rows, HEAD_DIM), lambda i: (0, i, 0))],
        out_shape=[jax.ShapeDtypeStruct((m, n), F32),
                   jax.ShapeDtypeStruct((3, rope_seq, HEAD_DIM), F32)],
        compiler_params=_params(("parallel",), vmem),
        name="na_proj_residual",
    )(a, w, res, freq)


def _ffn_kernel(x_ref, g_ref, w1_ref, w2_ref, gf_ref, *refs, final_norm, n_cast):
    cast_src = refs[:n_cast]
    o_ref = refs[n_cast]
    cast_dst = refs[n_cast + 1:2 * n_cast + 1]
    h_ref = refs[2 * n_cast + 1]
    f = pl.program_id(1)

    def body(first):
        _cast_slabs(cast_src, cast_dst)
        for c in range(x_ref.shape[0] // FFN_ROW_CHUNK):
            rows = slice(c * FFN_ROW_CHUNK, (c + 1) * FFN_ROW_CHUNK)
            if first:
                h_ref[rows, :] = _rms(x_ref[rows, :], g_ref[...]).astype(BF16)
            a = jnp.dot(h_ref[rows, :], w1_ref[...], preferred_element_type=F32)
            a = jnp.square(jnp.maximum(a, 0.0)).astype(BF16)
            y = jnp.dot(a, w2_ref[...], preferred_element_type=F32)
            if first:
                o_ref[rows, :] = x_ref[rows, :] + y
            else:
                o_ref[rows, :] += y

    pl.when(f == 0)(functools.partial(body, True))
    pl.when(f != 0)(functools.partial(body, False))

    if final_norm:
        @pl.when(f == pl.num_programs(1) - 1)
        def _():
            o_ref[...] = _rms(o_ref[...], gf_ref[...])


def ffn(x, g, w1, w2, g_final, *, tm, tf, final_norm, cast=()):
    m, d = x.shape
    dff = w1.shape[1]
    assert tm % FFN_ROW_CHUNK == 0
    n_steps = (m // tm) * (dff // tf)
    vmem = (4 * tm * d * 4 + tm * d * 2 + 2 * 2 * d * tf * 2
            + FFN_ROW_CHUNK * tf * 6 + FFN_ROW_CHUNK * d * 4 + _cast_vmem(cast, n_steps))
    cast_specs = _cast_specs(cast, n_steps, dff // tf)
    return pl.pallas_call(
        functools.partial(_ffn_kernel, final_norm=final_norm, n_cast=len(cast)),
        grid=(m // tm, dff // tf),
        in_specs=[
            pl.BlockSpec((tm, d), _tile_then_next(m // tm, dff // tf)),
            pl.BlockSpec((1, d), lambda i, f: (0, 0)),
            pl.BlockSpec((d, tf), lambda i, f: (0, f)),
            pl.BlockSpec((tf, d), lambda i, f: (f, 0)),
            pl.BlockSpec((1, d), lambda i, f: (0, 0)),
        ] + cast_specs,
        out_specs=[pl.BlockSpec((tm, d), lambda i, f: (i, 0))] + cast_specs,
        out_shape=[jax.ShapeDtypeStruct((m, d), F32)] + [jax.ShapeDtypeStruct(c.shape, BF16) for c in cast],
        scratch_shapes=[pltpu.VMEM((tm, d), BF16)],
        compiler_params=_params(("arbitrary", "arbitrary"), vmem),
        name="ffn_final" if final_norm else "ffn",
    )(x, g.reshape(1, d), w1, w2, g_final.reshape(1, d), *cast)


def _dil_attn_kernel(q_ref, kp_ref, kc_ref, kn_ref, vp_ref, vc_ref, vn_ref, o_ref, lse_ref,
                     *, heads, halo, length):
    qb = q_ref.shape[0]
    sub = BLOCK_Q
    nk = sub + 2 * halo
    m0 = pl.program_id(2) * qb
    qi = lax.broadcasted_iota(jnp.int32, (sub, nk), 0)
    kj = lax.broadcasted_iota(jnp.int32, (sub, nk), 1) - halo
    band = jnp.abs(kj - qi) <= halo
    lane = lax.broadcasted_iota(jnp.int32, (sub, HEAD_DIM), 1)

    def window(prev_ref, cur_ref, next_ref, lo, hi, cols):
        parts = [prev_ref[:, cols]] if lo < 0 else []
        parts.append(cur_ref[max(lo, 0):min(hi, qb), cols])
        if hi > qb:
            parts.append(next_ref[:, cols])
        return parts[0] if len(parts) == 1 else jnp.concatenate(parts, axis=0)

    def scores(u, h):
        rows = slice(u * sub, (u + 1) * sub)
        cols = slice(h * HEAD_DIM, (h + 1) * HEAD_DIM)
        k = window(kp_ref, kc_ref, kn_ref, u * sub - halo, (u + 1) * sub + halo, cols)
        s = lax.dot_general(q_ref[rows, cols], k, (((1,), (1,)), ((), ())), preferred_element_type=F32)
        s = s + masks[u]
        return s, jnp.max(s, axis=-1, keepdims=True)

    def finish(u, h, s, m):
        rows = slice(u * sub, (u + 1) * sub)
        cols = slice(h * HEAD_DIM, (h + 1) * HEAD_DIM)
        v = window(vp_ref, vc_ref, vn_ref, u * sub - halo, (u + 1) * sub + halo, cols)
        p = jnp.exp2(s - m)
        l = jnp.sum(p, axis=-1, keepdims=True)
        o = jnp.dot(p.astype(BF16), v, preferred_element_type=F32)
        o_ref[rows, cols] = (o / l).astype(o_ref.dtype)
        return m * LN2 + jnp.log(l)

    masks = []
    for u in range(qb // sub):
        kpos = kj + (m0 + u * sub)
        masks.append(jnp.where(band & (kpos >= 0) & (kpos < length), 0.0, NEG).astype(F32))

    units = [(u, h) for u in range(qb // sub) for h in range(heads)]
    cur = scores(*units[0])
    lse_tile = None
    for n, (u, h) in enumerate(units):
        nxt = scores(*units[n + 1]) if n + 1 < len(units) else None
        lse = finish(u, h, *cur)
        lse_tile = jnp.where(lane == h, lse, jnp.zeros((sub, HEAD_DIM), F32) if h == 0 else lse_tile)
        if h == heads - 1:
            lse_ref[u * sub:(u + 1) * sub, :] = lse_tile
        cur = nxt


def dil_attention(qkv, *, group, halo):
    _, batch, dil, length, wq = qkv.shape
    heads = wq // HEAD_DIM
    qb = min(DIL_STEP_ROWS, length)
    assert BLOCK_Q % halo == 0 and qb % BLOCK_Q == 0 and length % qb == 0
    nblk = length // qb
    hpb = length // halo
    per = qb // halo

    def cur_map(sec):
        return lambda b, r, i: (sec, b, r, i, 0)

    def prev_map(sec):
        return lambda b, r, i: (sec, b, r, jnp.maximum(i * per - 1, 0), 0)

    def next_map(sec):
        return lambda b, r, i: (sec, b, r, jnp.minimum((i + 1) * per, hpb - 1), 0)

    def spec(rows, index_map):
        return pl.BlockSpec((None, None, None, rows, wq), index_map)

    in_specs = [spec(qb, cur_map(0))]
    for sec in (1, 2):
        in_specs += [spec(halo, prev_map(sec)), spec(qb, cur_map(sec)), spec(halo, next_map(sec))]
    vmem = (2 * (2 * qb * wq * 2 + 2 * (qb + 2 * halo) * wq * 2 + qb * HEAD_DIM * 4)
            + 8 * BLOCK_Q * (BLOCK_Q + 2 * halo) * 4)
    return pl.pallas_call(
        functools.partial(_dil_attn_kernel, heads=heads, halo=halo, length=length),
        grid=(batch, dil, nblk),
        in_specs=in_specs,
        out_specs=[pl.BlockSpec((None, None, qb, wq), lambda b, r, i: (b, r, i, 0)),
                   pl.BlockSpec((None, None, qb, HEAD_DIM), lambda b, r, i: (b, r, i, 0))],
        out_shape=[jax.ShapeDtypeStruct((batch, dil, length, wq), BF16),
                   jax.ShapeDtypeStruct((batch, dil, length, HEAD_DIM), F32)],
        compiler_params=_params(("parallel", "parallel", "arbitrary"), vmem),
        name=f"dil_attention_g{group}",
    )(*([qkv] * 7))


def _merge_proj_kernel(*refs, dils, heads):
    g_n = len(dils)
    o_refs = refs[:g_n]
    lse_refs = refs[g_n:2 * g_n]
    w_ref, r_ref, out_ref, a_ref, lt_ref, ot_ref, tt_ref = refs[2 * g_n:]
    tm = a_ref.shape[0]

    def merged(c):
        rows = slice(c * ROW_CHUNK, (c + 1) * ROW_CHUNK)
        lses = []
        for g, dil in enumerate(dils):
            if dil == 1:
                lses.append(lse_refs[g][0, rows, :])
                continue
            n = ROW_CHUNK // dil
            d1, d2 = _stride_split(dil)
            n1 = ROW_CHUNK // d1

            def to_token_order(dst_ref, dst_idx, src):
                if d2 == 1:
                    for r in range(dil):
                        dst_ref[dst_idx, pl.ds(c * ROW_CHUNK + r, n, stride=dil), :] = src(r)
                    return
                for r1 in range(d1):
                    for r2 in range(d2):
                        tt_ref[pl.ds(r1 * n1 + r2, n, stride=d2), :] = src(r1 + d1 * r2)
                for r1 in range(d1):
                    dst_ref[dst_idx, pl.ds(c * ROW_CHUNK + r1, n1, stride=d1), :] = tt_ref[r1 * n1:(r1 + 1) * n1, :]

            to_token_order(lt_ref, g, lambda r: lse_refs[g][r, c * n:(c + 1) * n, :])
            for h in range(heads):
                to_token_order(ot_ref, g * heads + h, lambda r: o_refs[g][
                    r, c * n:(c + 1) * n, h * HEAD_DIM:(h + 1) * HEAD_DIM].astype(F32))
            lses.append(lt_ref[g, rows, :])

        mx = functools.reduce(jnp.maximum, lses)
        es = [jnp.exp(l - mx) for l in lses]
        den = functools.reduce(jnp.add, es)
        wts = [e / den for e in es]
        for h in range(heads):
            sl = slice(h * HEAD_DIM, (h + 1) * HEAD_DIM)
            acc = None
            for g, dil in enumerate(dils):
                o = o_refs[g][0, rows, sl].astype(F32) if dil == 1 else ot_ref[g * heads + h, rows, :]
                term = wts[g][:, h:h + 1] * o
                acc = term if acc is None else acc + term
            a_ref[rows, sl] = acc.astype(BF16)

    merged(0)
    for c in range(tm // ROW_CHUNK):
        rows = slice(c * ROW_CHUNK, (c + 1) * ROW_CHUNK)
        if c + 1 < tm // ROW_CHUNK:
            merged(c + 1)
        out_ref[rows, :] = r_ref[rows, :] + jnp.dot(a_ref[rows, :], w_ref[...], preferred_element_type=F32)


def merge_proj_residual(outs, lses, w, res, *, seq, tm):
    dils = tuple(o.shape[1] for o in outs)
    k = outs[0].shape[-1]
    m, n = res.shape
    g_n = len(outs)
    heads = k // HEAD_DIM
    tpb = seq // tm
    assert all(ROW_CHUNK % (BF16_SUBLANES * dil) == 0 for dil in dils) and tm % ROW_CHUNK == 0

    def stream_spec(dil, width):
        return pl.BlockSpec((None, dil, tm // dil, width), lambda i: (i // tpb, 0, i % tpb, 0))

    vmem = (2 * g_n * tm * k * 2 + 2 * g_n * tm * HEAD_DIM * 4 + 2 * k * n * 2 + 4 * tm * n * 4
            + tm * k * 2 + tm * n * 4 + g_n * tm * HEAD_DIM * 4 + g_n * tm * k * 4)
    return pl.pallas_call(
        functools.partial(_merge_proj_kernel, dils=dils, heads=heads),
        grid=(m // tm,),
        in_specs=([stream_spec(dil, k) for dil in dils]
                  + [stream_spec(dil, HEAD_DIM) for dil in dils]
                  + [pl.BlockSpec((k, n), lambda i: (0, 0)),
                     pl.BlockSpec((tm, n), lambda i: (i, 0))]),
        out_specs=pl.BlockSpec((tm, n), lambda i: (i, 0)),
        out_shape=jax.ShapeDtypeStruct((m, n), F32),
        scratch_shapes=[pltpu.VMEM((tm, k), BF16),
                        pltpu.VMEM((g_n, tm, HEAD_DIM), F32),
                        pltpu.VMEM((g_n * heads, tm, HEAD_DIM), F32),
                        pltpu.VMEM((ROW_CHUNK, HEAD_DIM), F32)],
        compiler_params=_params(("parallel",), vmem),
        name="dil_merge_proj_residual",
    )(*outs, *lses, w, res)


def kernel(x, na_norm, na_wqkv, na_rpb, na_wo, ffn0_norm, ffn0_w1, ffn0_w2, dil_norm, dil_wqkv, dil_wo,
           ffn1_norm, ffn1_w1, ffn1_w2, final_norm):
    batch, seq, d = x.shape
    m = batch * seq
    rows = seq // GRID_W
    na_heads = na_rpb.shape[0]
    assert rows >= NA_KEY_ROWS and rows % NA_BLOCK_ROWS == 0 and rows >= NA_WIN_ROWS
    assert NA_KEY_ROWS >= NA_BLOCK_ROWS + NA_WIN_ROWS - 1

    xf = x.reshape(m, d)
    bf = lambda w: w.astype(BF16)

    qw = na_heads * HEAD_DIM
    col_scale = jnp.concatenate([jnp.full((qw,), Q_SCALE, F32), jnp.ones((2 * qw,), F32)])
    qkv, na_wo_b, ffn0_w1_b, ffn0_w2_b = norm_matmul(
        xf, na_norm, bf(na_wqkv), col_scale, tm=min(1024, m), tn=min(1024, qw), cast=(na_wo, ffn0_w1, ffn0_w2))
    xf = na_attention_proj(qkv, _na_bias_variants(na_rpb, rows), na_wo_b, xf, batch=batch, seq=seq)
    rot = _rope_tables(seq)
    xf, dil_wqkv_b, dil_wo_b, ffn1_w1_b, ffn1_w2_b = ffn(
        xf, ffn0_norm, ffn0_w1_b, ffn0_w2_b, final_norm, tm=min(1024, m), tf=512, final_norm=False,
        cast=(dil_wqkv, dil_wo, ffn1_w1, ffn1_w2))

    dils = tuple(dil for _, dil in DIL_GROUPS)
    qkvs = norm_matmul_rope(xf, dil_norm, dil_wqkv_b, rot, batch=batch, seq=seq, tm=min(1024, seq), dils=dils)
    outs, lses = [], []
    for gi, (window, dil) in enumerate(DIL_GROUPS):
        o, lse = dil_attention(qkvs[gi], group=gi, halo=(window // 2) // dil)
        outs.append(o)
        lses.append(lse)
    xf = merge_proj_residual(outs, lses, dil_wo_b, xf, seq=seq, tm=min(512, seq))
    xf, = ffn(xf, ffn1_norm, ffn1_w1_b, ffn1_w2_b, final_norm, tm=min(1024, m), tf=512, final_norm=True)
    return xf.reshape(batch, seq, d)
```

```python
import functools
import math

import numpy as np
import jax
import jax.numpy as jnp
from jax import lax
from jax.experimental import pallas as pl
from jax.experimental.pallas import tpu as pltpu

GRID_W = 64
HEAD_DIM = 128
NA_WIN_ROWS = 8
NA_WIN_COLS = 16
DIL_GROUPS = ((128, 1), (512, 4), (2048, 16))
ROPE_THETA = 500000.0
ROPE_DIM = HEAD_DIM // 4
BLOCK_Q = 128
EPS = 1e-6
NEG = -1e30

LOG2E = math.log2(math.e)
LN2 = math.log(2.0)
Q_SCALE = HEAD_DIM ** -0.5 * LOG2E

NA_BLOCK_ROWS = 4
NA_KEY_BACK = NA_WIN_ROWS // 2
NA_KEY_ROWS = 12
NA_KEY_CHUNKS = 3

DIL_STEP_ROWS = 512
ROW_CHUNK = 256
FFN_ROW_CHUNK = 512
BF16_SUBLANES = 16
F32_SUBLANES = 8
LANES = 128

V7X_VMEM_BYTES = 64 * 1024 * 1024
VMEM_CAP_BYTES = V7X_VMEM_BYTES - 8 * 1024 * 1024

F32 = jnp.float32
BF16 = jnp.bfloat16


def _params(semantics, vmem_bytes):
    limit = min(int(vmem_bytes * 1.25) + (4 << 20), VMEM_CAP_BYTES)
    return pltpu.CompilerParams(dimension_semantics=semantics, vmem_limit_bytes=limit)


def _rms(x, g):
    return x * lax.rsqrt(jnp.mean(x * x, axis=-1, keepdims=True) + EPS) * g


def _tile_then_next(n_tiles, n_inner):
    switch = max(n_inner // 2, 1)
    return lambda i, j: (jnp.minimum(i + jnp.where(j >= switch, 1, 0), n_tiles - 1), 0)


def _cast_rows(n_rows, n_steps):
    for rows in range(BF16_SUBLANES, n_rows + 1, BF16_SUBLANES):
        if n_rows % rows == 0 and n_rows // rows <= n_steps:
            return rows
    raise ValueError((n_rows, n_steps))


def _cast_specs(weights, n_steps, n_inner):
    specs = []
    for w in weights:
        rows = _cast_rows(w.shape[0], n_steps)
        last = w.shape[0] // rows - 1
        specs.append(pl.BlockSpec((rows, w.shape[1]), lambda i, j, last=last: (jnp.minimum(i * n_inner + j, last), 0)))
    return specs


def _cast_vmem(weights, n_steps):
    return sum(2 * _cast_rows(w.shape[0], n_steps) * w.shape[1] * 6 for w in weights)


def _cast_slabs(src_refs, dst_refs):
    for src, dst in zip(src_refs, dst_refs):
        dst[...] = src[...].astype(BF16)


def _norm_matmul_kernel(x_ref, g_ref, w_ref, cs_ref, *refs, n_cast):
    cast_src = refs[:n_cast]
    o_ref = refs[n_cast]
    cast_dst = refs[n_cast + 1:2 * n_cast + 1]
    h_ref = refs[2 * n_cast + 1]

    def body(with_norm):
        _cast_slabs(cast_src, cast_dst)
        for c in range(x_ref.shape[0] // ROW_CHUNK):
            rows = slice(c * ROW_CHUNK, (c + 1) * ROW_CHUNK)
            if with_norm:
                h_ref[rows, :] = _rms(x_ref[rows, :], g_ref[...]).astype(BF16)
            acc = jnp.dot(h_ref[rows, :], w_ref[...], preferred_element_type=F32)
            o_ref[rows, :] = (acc * cs_ref[...]).astype(o_ref.dtype)

    pl.when(pl.program_id(1) == 0)(functools.partial(body, True))
    pl.when(pl.program_id(1) != 0)(functools.partial(body, False))


def norm_matmul(x, g, w, col_scale, *, tm, tn, cast=()):
    m, k = x.shape
    n = w.shape[1]
    n_steps = (m // tm) * (n // tn)
    vmem = (2 * tm * k * 4 + tm * k * 2 + 2 * k * tn * 2 + 2 * tm * tn * 2 + tm * tn * 4
            + _cast_vmem(cast, n_steps))
    cast_specs = _cast_specs(cast, n_steps, n // tn)
    return pl.pallas_call(
        functools.partial(_norm_matmul_kernel, n_cast=len(cast)),
        grid=(m // tm, n // tn),
        in_specs=[
            pl.BlockSpec((tm, k), _tile_then_next(m // tm, n // tn)),
            pl.BlockSpec((1, k), lambda i, j: (0, 0)),
            pl.BlockSpec((k, tn), lambda i, j: (0, j)),
            pl.BlockSpec((1, tn), lambda i, j: (0, j)),
        ] + cast_specs,
        out_specs=[pl.BlockSpec((tm, tn), lambda i, j: (i, j))] + cast_specs,
        out_shape=[jax.ShapeDtypeStruct((m, n), BF16)] + [jax.ShapeDtypeStruct(c.shape, BF16) for c in cast],
        scratch_shapes=[pltpu.VMEM((tm, k), BF16)],
        compiler_params=_params(("arbitrary", "arbitrary"), vmem),
        name="norm_qkv_na",
    )(x, g.reshape(1, k), w, col_scale.reshape(1, n), *cast)


def _norm_matmul_rope_kernel(x_ref, g_ref, w_ref, rot_ref, *refs, dils):
    o_refs = refs[:len(dils)]
    h_ref, s_ref, t_ref = refs[len(dils):]
    j = pl.program_id(1)
    tm = x_ref.shape[0]
    heads = s_ref.shape[0]
    half = ROPE_DIM // 2

    def body(o_ref, dil, rotate, with_norm=False):
        d1, d2 = _stride_split(dil)
        n1 = ROW_CHUNK // d1
        scale = jnp.where(j % 3 == 0, Q_SCALE, 1.0).astype(F32)
        sizes = [ROW_CHUNK] * (tm // ROW_CHUNK)
        if d2 == 1 and (ROW_CHUNK // 2) % (BF16_SUBLANES * dil) == 0:
            sizes = sizes[:-1] + [ROW_CHUNK // 2] * 2
        for start, size in zip(np.cumsum([0] + sizes[:-1]).tolist(), sizes):
            rows = slice(start, start + size)
            n = size // dil
            if with_norm:
                h_ref[rows, :] = _rms(x_ref[rows, :], g_ref[...]).astype(BF16)
            acc = jnp.dot(h_ref[rows, :], w_ref[...], preferred_element_type=F32)
            if rotate:
                cos = rot_ref[0, rows, :] * scale
                sin_hi = rot_ref[1, rows, :] * scale
                sin_lo = rot_ref[2, rows, :] * scale
            for h in range(heads):
                cols = slice(h * HEAD_DIM, (h + 1) * HEAD_DIM)
                a = acc[:, cols]
                if rotate:
                    a = (a * cos + pltpu.roll(a, half, 1) * sin_hi
                         + pltpu.roll(a, HEAD_DIM - half, 1) * sin_lo)
                if dil == 1:
                    o_ref[0, rows, cols] = a.astype(BF16)
                    continue
                s_ref[h, rows, :] = a
                p0 = start // dil
                if d2 == 1:
                    for r in range(dil):
                        o_ref[r, p0:p0 + n, cols] = s_ref[h, pl.ds(start + r, n, stride=dil), :].astype(BF16)
                    continue
                for r1 in range(d1):
                    t_ref[h, r1 * n1:(r1 + 1) * n1, :] = s_ref[h, pl.ds(start + r1, n1, stride=d1), :]
                for r1 in range(d1):
                    for r2 in range(d2):
                        o_ref[r1 + d1 * r2, p0:p0 + n, cols] = (
                            t_ref[h, pl.ds(r1 * n1 + r2, n, stride=d2), :].astype(BF16))

    pl.when(j == 0)(functools.partial(body, o_refs[0], dils[0], True, with_norm=True))
    for g, dil in enumerate(dils):
        pl.when((j // 3 == g) & (j % 3 != 2) & (j != 0))(functools.partial(body, o_refs[g], dil, True))
        pl.when((j // 3 == g) & (j % 3 == 2))(functools.partial(body, o_refs[g], dil, False))


def _stride_split(dil):
    if dil <= F32_SUBLANES:
        return dil, 1
    d1 = F32_SUBLANES // 2
    assert dil % d1 == 0 and dil // d1 <= F32_SUBLANES
    return d1, dil // d1


def norm_matmul_rope(x, g, w, rot, *, batch, seq, tm, dils):
    m, k = x.shape
    n = w.shape[1]
    tn = n // (3 * len(dils))
    heads = tn // HEAD_DIM
    tpb = seq // tm
    assert all(ROW_CHUNK % (BF16_SUBLANES * dil) == 0 for dil in dils) and tm % ROW_CHUNK == 0

    def out_map(g):
        return lambda i, j: (jnp.clip(j - 3 * g, 0, 2), i // tpb, 0, i % tpb, 0)

    vmem = (2 * tm * k * 4 + tm * k * 2 + 2 * k * tn * 2 + 2 * len(dils) * tm * tn * 2 + tm * tn * 4
            + 2 * ROW_CHUNK * tn * 4 + 2 * 3 * tm * HEAD_DIM * 4 + heads * ROW_CHUNK * HEAD_DIM * 4)
    return pl.pallas_call(
        functools.partial(_norm_matmul_rope_kernel, dils=dils),
        grid=(m // tm, n // tn),
        in_specs=[
            pl.BlockSpec((tm, k), _tile_then_next(m // tm, n // tn)),
            pl.BlockSpec((1, k), lambda i, j: (0, 0)),
            pl.BlockSpec((k, tn), lambda i, j: (0, j)),
            pl.BlockSpec((3, tm, HEAD_DIM), lambda i, j: (0, i % tpb, 0)),
        ],
        out_specs=[pl.BlockSpec((None, None, dil, tm // dil, tn), out_map(g)) for g, dil in enumerate(dils)],
        out_shape=[jax.ShapeDtypeStruct((3, batch, dil, seq // dil, tn), BF16) for dil in dils],
        scratch_shapes=[pltpu.VMEM((tm, k), BF16), pltpu.VMEM((heads, tm, HEAD_DIM), F32),
                        pltpu.VMEM((heads, ROW_CHUNK, HEAD_DIM), F32)],
        compiler_params=_params(("arbitrary", "arbitrary"), vmem),
        name="norm_qkv_dil_rope",
    )(x, g.reshape(1, k), w, rot)


def _na_bias_table(rpb):
    nh, n_row_off, n_col_off = rpb.shape
    assert 2 * GRID_W == LANES and NA_KEY_ROWS % 2 == 0 and n_col_off <= LANES
    rpb_lanes = jnp.pad(rpb, ((0, 0), (0, 0), (0, LANES - n_col_off)))
    return pl.pallas_call(
        _na_bias_kernel,
        grid=(nh,),
        in_specs=[pl.BlockSpec((None, n_row_off, LANES), lambda h: (h, 0, 0))],
        out_specs=pl.BlockSpec((None, NA_BLOCK_ROWS * GRID_W, NA_KEY_ROWS * GRID_W), lambda h: (h, 0, 0)),
        out_shape=jax.ShapeDtypeStruct((nh, NA_BLOCK_ROWS * GRID_W, NA_KEY_ROWS * GRID_W), F32),
        compiler_params=_params(("parallel",), 4 * NA_BLOCK_ROWS * GRID_W * NA_KEY_ROWS * GRID_W * 4),
        name="na_bias_table",
    )(rpb_lanes)


def _na_bias_kernel(rpb_ref, o_ref):
    w = GRID_W
    qc = lax.broadcasted_iota(jnp.int32, (w, LANES), 0)
    kc = lax.broadcasted_iota(jnp.int32, (w, LANES), 1)
    start = jnp.clip(qc - NA_WIN_COLS // 2, 0, w - NA_WIN_COLS)
    in_window = (kc >= start) & (kc < start + NA_WIN_COLS)
    by_col = []
    for a in range(rpb_ref.shape[0]):
        row = jnp.broadcast_to(rpb_ref[a:a + 1, :], (w, LANES))
        shifted = pltpu.roll(row, LANES - (NA_WIN_COLS - 1), 1, stride=1, stride_axis=0)
        by_col.append(jnp.where(in_window, shifted * LOG2E, NEG))
    for rr in range(NA_BLOCK_ROWS):
        for kk in range(0, NA_KEY_ROWS, 2):
            a = kk - NA_KEY_BACK - rr + NA_WIN_ROWS - 1
            assert 0 <= a and a + 1 < rpb_ref.shape[0]
            pair = jnp.where(kc < w, by_col[a], pltpu.roll(by_col[a + 1], w, 1))
            o_ref[rr * w:(rr + 1) * w, kk * w:(kk + 2) * w] = pair


def _na_attn_kernel(q_ref, *refs, heads, rows):
    k_refs = refs[:NA_KEY_CHUNKS]
    v_refs = refs[NA_KEY_CHUNKS:2 * NA_KEY_CHUNKS]
    b_ref, o_ref, bm_ref = refs[2 * NA_KEY_CHUNKS:]
    qb = q_ref.shape[0]
    ck = k_refs[0].shape[0]
    kh = min(NA_WIN_ROWS, rows)
    i = pl.program_id(2)

    def clipped(blk):
        r0 = blk * NA_BLOCK_ROWS
        return (r0 < kh // 2) | (r0 + NA_BLOCK_ROWS - 1 - kh // 2 > rows - kh)

    @pl.when((i == 0) | clipped(i) | clipped(i - 1))
    def _():
        r0 = i * NA_BLOCK_ROWS
        q_row = r0 + lax.broadcasted_iota(jnp.int32, (qb, ck), 0) // GRID_W
        lo = jnp.clip(q_row - kh // 2, 0, rows - kh)
        for c in range(NA_KEY_CHUNKS):
            k_row = (r0 - NA_KEY_BACK + c * (ck // GRID_W)
                     + lax.broadcasted_iota(jnp.int32, (qb, ck), 1) // GRID_W)
            row_mask = jnp.where((k_row >= lo) & (k_row < lo + kh), 0.0, NEG).astype(F32)
            for h in range(heads):
                bm_ref[h, :, c * ck:(c + 1) * ck] = b_ref[h, :, c * ck:(c + 1) * ck] + row_mask

    def scores(h):
        sl = slice(h * HEAD_DIM, (h + 1) * HEAD_DIM)
        q = q_ref[:, sl]
        s = []
        for c, k_ref in enumerate(k_refs):
            sc = lax.dot_general(q, k_ref[:, sl], (((1,), (1,)), ((), ())), preferred_element_type=F32)
            s.append(sc + bm_ref[h, :, c * ck:(c + 1) * ck])
        return s, jnp.max(functools.reduce(jnp.maximum, s), axis=-1, keepdims=True)

    def finish(h, s, m):
        sl = slice(h * HEAD_DIM, (h + 1) * HEAD_DIM)
        p = [jnp.exp2(sc - m) for sc in s]
        l = jnp.sum(functools.reduce(jnp.add, p), axis=-1, keepdims=True)
        o = functools.reduce(jnp.add, [
            jnp.dot(pc.astype(BF16), v_ref[:, sl], preferred_element_type=F32)
            for pc, v_ref in zip(p, v_refs)])
        o_ref[:, sl] = (o / l).astype(o_ref.dtype)

    cur = scores(0)
    for h in range(heads):
        nxt = scores(h + 1) if h + 1 < heads else None
        finish(h, *cur)
        cur = nxt


def na_attention(qkv, bias, *, batch, seq, heads_per_step):
    m = qkv.shape[0]
    nh = bias.shape[0]
    hg = heads_per_step
    wq = hg * HEAD_DIM
    qb = NA_BLOCK_ROWS * GRID_W
    kb = NA_KEY_ROWS * GRID_W // NA_KEY_CHUNKS
    assert qb % kb == 0 and (NA_KEY_BACK * GRID_W) % kb == 0
    nblk = seq // qb
    kpb = seq // kb
    ratio = qb // kb
    back = NA_KEY_BACK * GRID_W // kb
    ncol = nh // hg

    def k_map(c, sec):
        def index(g, b, i):
            return (b * kpb + jnp.clip(i * ratio - back + c, 0, kpb - 1), sec * ncol + g)
        return index

    in_specs = [pl.BlockSpec((qb, wq), lambda g, b, i: (b * nblk + i, g))]
    in_specs += [pl.BlockSpec((kb, wq), k_map(c, 1)) for c in range(NA_KEY_CHUNKS)]
    in_specs += [pl.BlockSpec((kb, wq), k_map(c, 2)) for c in range(NA_KEY_CHUNKS)]
    in_specs += [pl.BlockSpec((hg, qb, NA_KEY_CHUNKS * kb), lambda g, b, i: (g, 0, 0),
                              pipeline_mode=pl.Buffered(1))]
    vmem = (2 * 2 * qb * wq * 2 + 2 * 2 * NA_KEY_CHUNKS * kb * wq * 2
            + 2 * hg * qb * NA_KEY_CHUNKS * kb * 4 + 6 * qb * NA_KEY_CHUNKS * kb * 4)
    return pl.pallas_call(
        functools.partial(_na_attn_kernel, heads=hg, rows=seq // GRID_W),
        grid=(ncol, batch, nblk),
        in_specs=in_specs,
        out_specs=pl.BlockSpec((qb, wq), lambda g, b, i: (b * nblk + i, g)),
        out_shape=jax.ShapeDtypeStruct((m, nh * HEAD_DIM), BF16),
        scratch_shapes=[pltpu.VMEM((hg, qb, NA_KEY_CHUNKS * kb), F32)],
        compiler_params=_params(("parallel", "parallel", "arbitrary"), vmem),
        name="na_attention",
    )(qkv, *([qkv] * (2 * NA_KEY_CHUNKS)), bias)


def _proj_residual_kernel(a_ref, w_ref, r_ref, freq_ref, o_ref, rot_ref):
    _, rows, _ = rot_ref.shape
    pos = (pl.program_id(0) * rows + lax.broadcasted_iota(jnp.int32, (rows, LANES), 0)).astype(F32)
    ang = pos * freq_ref[...]
    lane = lax.broadcasted_iota(jnp.int32, (rows, LANES), 1)
    half = ROPE_DIM // 2
    sin = jnp.sin(ang)
    rot_ref[0] = jnp.cos(ang)
    rot_ref[1] = jnp.where((lane >= half) & (lane < ROPE_DIM), sin, 0.0)
    rot_ref[2] = jnp.where(lane < half, -sin, 0.0)

    o_ref[...] = r_ref[...] + jnp.dot(a_ref[...], w_ref[...], preferred_element_type=F32)


def proj_residual(a, w, res, *, tm, rope_seq):
    m, k = a.shape
    n = w.shape[1]
    n_steps = m // tm
    assert rope_seq % n_steps == 0 and (rope_seq // n_steps) % F32_SUBLANES == 0 and HEAD_DIM == LANES
    rope_rows = rope_seq // n_steps
    inv_freq = ROPE_THETA ** (-jnp.arange(0, ROPE_DIM, 2, dtype=F32) / ROPE_DIM)
    freq = jnp.concatenate([inv_freq, inv_freq, jnp.zeros((HEAD_DIM - ROPE_DIM,), F32)]).reshape(1, HEAD_DIM)
    vmem = (2 * tm * k * 2 + 2 * k * n * 2 + 4 * tm * n * 4 + tm * n * 4
            + 2 * 3 * rope_rows * HEAD_DIM * 4 * 3)
    return pl.pallas_call(
        _proj_residual_kernel,
        grid=(n_steps,),
        in_specs=[
            pl.BlockSpec((tm, k), lambda i: (i, 0)),
            pl.BlockSpec((k, n), lambda i: (0, 0)),
            pl.BlockSpec((tm, n), lambda i: (i, 0)),
            pl.BlockSpec((1, HEAD_DIM), lambda i: (0, 0)),
        ],
        out_specs=[pl.BlockSpec((tm, n), lambda i: (i, 0)),
                   pl.BlockSpec((3, rope_rows, HEAD_DIM), lambda i: (0, i, 0))],
        out_shape=[jax.ShapeDtypeStruct((m, n), F32),
                   jax.ShapeDtypeStruct((3, rope_seq, HEAD_DIM), F32)],
        compiler_params=_params(("parallel",), vmem),
        name="na_proj_residual",
    )(a, w, res, freq)


def _ffn_kernel(x_ref, g_ref, w1_ref, w2_ref, gf_ref, *refs, final_norm, n_cast):
    cast_src = refs[:n_cast]
    o_ref = refs[n_cast]
    cast_dst = refs[n_cast + 1:2 * n_cast + 1]
    h_ref = refs[2 * n_cast + 1]
    f = pl.program_id(1)

    def body(first):
        _cast_slabs(cast_src, cast_dst)
        for c in range(x_ref.shape[0] // FFN_ROW_CHUNK):
            rows = slice(c * FFN_ROW_CHUNK, (c + 1) * FFN_ROW_CHUNK)
            if first:
                h_ref[rows, :] = _rms(x_ref[rows, :], g_ref[...]).astype(BF16)
            a = jnp.dot(h_ref[rows, :], w1_ref[...], preferred_element_type=F32)
            a = jnp.square(jnp.maximum(a, 0.0)).astype(BF16)
            y = jnp.dot(a, w2_ref[...], preferred_element_type=F32)
            if first:
                o_ref[rows, :] = x_ref[rows, :] + y
            else:
                o_ref[rows, :] += y

    pl.when(f == 0)(functools.partial(body, True))
    pl.when(f != 0)(functools.partial(body, False))

    if final_norm:
        @pl.when(f == pl.num_programs(1) - 1)
        def _():
            o_ref[...] = _rms(o_ref[...], gf_ref[...])


def ffn(x, g, w1, w2, g_final, *, tm, tf, final_norm, cast=()):
    m, d = x.shape
    dff = w1.shape[1]
    assert tm % FFN_ROW_CHUNK == 0
    n_steps = (m // tm) * (dff // tf)
    vmem = (4 * tm * d * 4 + tm * d * 2 + 2 * 2 * d * tf * 2
            + FFN_ROW_CHUNK * tf * 6 + FFN_ROW_CHUNK * d * 4 + _cast_vmem(cast, n_steps))
    cast_specs = _cast_specs(cast, n_steps, dff // tf)
    return pl.pallas_call(
        functools.partial(_ffn_kernel, final_norm=final_norm, n_cast=len(cast)),
        grid=(m // tm, dff // tf),
        in_specs=[
            pl.BlockSpec((tm, d), _tile_then_next(m // tm, dff // tf)),
            pl.BlockSpec((1, d), lambda i, f: (0, 0)),
            pl.BlockSpec((d, tf), lambda i, f: (0, f)),
            pl.BlockSpec((tf, d), lambda i, f: (f, 0)),
            pl.BlockSpec((1, d), lambda i, f: (0, 0)),
        ] + cast_specs,
        out_specs=[pl.BlockSpec((tm, d), lambda i, f: (i, 0))] + cast_specs,
        out_shape=[jax.ShapeDtypeStruct((m, d), F32)] + [jax.ShapeDtypeStruct(c.shape, BF16) for c in cast],
        scratch_shapes=[pltpu.VMEM((tm, d), BF16)],
        compiler_params=_params(("arbitrary", "arbitrary"), vmem),
        name="ffn_final" if final_norm else "ffn",
    )(x, g.reshape(1, d), w1, w2, g_final.reshape(1, d), *cast)


def _dil_attn_kernel(q_ref, kp_ref, kc_ref, kn_ref, vp_ref, vc_ref, vn_ref, o_ref, lse_ref,
                     *, heads, halo, length):
    qb = q_ref.shape[0]
    sub = BLOCK_Q
    nk = sub + 2 * halo
    m0 = pl.program_id(2) * qb
    qi = lax.broadcasted_iota(jnp.int32, (sub, nk), 0)
    kj = lax.broadcasted_iota(jnp.int32, (sub, nk), 1) - halo
    band = jnp.abs(kj - qi) <= halo
    lane = lax.broadcasted_iota(jnp.int32, (sub, HEAD_DIM), 1)

    def window(prev_ref, cur_ref, next_ref, lo, hi, cols):
        parts = [prev_ref[:, cols]] if lo < 0 else []
        parts.append(cur_ref[max(lo, 0):min(hi, qb), cols])
        if hi > qb:
            parts.append(next_ref[:, cols])
        return parts[0] if len(parts) == 1 else jnp.concatenate(parts, axis=0)

    def scores(u, h):
        rows = slice(u * sub, (u + 1) * sub)
        cols = slice(h * HEAD_DIM, (h + 1) * HEAD_DIM)
        k = window(kp_ref, kc_ref, kn_ref, u * sub - halo, (u + 1) * sub + halo, cols)
        s = lax.dot_general(q_ref[rows, cols], k, (((1,), (1,)), ((), ())), preferred_element_type=F32)
        s = s + masks[u]
        return s, jnp.max(s, axis=-1, keepdims=True)

    def finish(u, h, s, m):
        rows = slice(u * sub, (u + 1) * sub)
        cols = slice(h * HEAD_DIM, (h + 1) * HEAD_DIM)
        v = window(vp_ref, vc_ref, vn_ref, u * sub - halo, (u + 1) * sub + halo, cols)
        p = jnp.exp2(s - m)
        l = jnp.sum(p, axis=-1, keepdims=True)
        o = jnp.dot(p.astype(BF16), v, preferred_element_type=F32)
        o_ref[rows, cols] = (o / l).astype(o_ref.dtype)
        return m * LN2 + jnp.log(l)

    masks = []
    for u in range(qb // sub):
        kpos = kj + (m0 + u * sub)
        masks.append(jnp.where(band & (kpos >= 0) & (kpos < length), 0.0, NEG).astype(F32))

    units = [(u, h) for u in range(qb // sub) for h in range(heads)]
    cur = scores(*units[0])
    lse_tile = None
    for n, (u, h) in enumerate(units):
        nxt = scores(*units[n + 1]) if n + 1 < len(units) else None
        lse = finish(u, h, *cur)
        lse_tile = jnp.where(lane == h, lse, jnp.zeros((sub, HEAD_DIM), F32) if h == 0 else lse_tile)
        if h == heads - 1:
            lse_ref[u * sub:(u + 1) * sub, :] = lse_tile
        cur = nxt


def dil_attention(qkv, *, group, halo):
    _, batch, dil, length, wq = qkv.shape
    heads = wq // HEAD_DIM
    qb = min(DIL_STEP_ROWS, length)
    assert BLOCK_Q % halo == 0 and qb % BLOCK_Q == 0 and length % qb == 0
    nblk = length // qb
    hpb = length // halo
    per = qb // halo

    def cur_map(sec):
        return lambda b, r, i: (sec, b, r, i, 0)

    def prev_map(sec):
        return lambda b, r, i: (sec, b, r, jnp.maximum(i * per - 1, 0), 0)

    def next_map(sec):
        return lambda b, r, i: (sec, b, r, jnp.minimum((i + 1) * per, hpb - 1), 0)

    def spec(rows, index_map):
        return pl.BlockSpec((None, None, None, rows, wq), index_map)

    in_specs = [spec(qb, cur_map(0))]
    for sec in (1, 2):
        in_specs += [spec(halo, prev_map(sec)), spec(qb, cur_map(sec)), spec(halo, next_map(sec))]
    vmem = (2 * (2 * qb * wq * 2 + 2 * (qb + 2 * halo) * wq * 2 + qb * HEAD_DIM * 4)
            + 8 * BLOCK_Q * (BLOCK_Q + 2 * halo) * 4)
    return pl.pallas_call(
        functools.partial(_dil_attn_kernel, heads=heads, halo=halo, length=length),
        grid=(batch, dil, nblk),
        in_specs=in_specs,
        out_specs=[pl.BlockSpec((None, None, qb, wq), lambda b, r, i: (b, r, i, 0)),
                   pl.BlockSpec((None, None, qb, HEAD_DIM), lambda b, r, i: (b, r, i, 0))],
        out_shape=[jax.ShapeDtypeStruct((batch, dil, length, wq), BF16),
                   jax.ShapeDtypeStruct((batch, dil, length, HEAD_DIM), F32)],
        compiler_params=_params(("parallel", "parallel", "arbitrary"), vmem),
        name=f"dil_attention_g{group}",
    )(*([qkv] * 7))


def _merge_proj_kernel(*refs, dils, heads):
    g_n = len(dils)
    o_refs = refs[:g_n]
    lse_refs = refs[g_n:2 * g_n]
    w_ref, r_ref, out_ref, a_ref, lt_ref, ot_ref, tt_ref = refs[2 * g_n:]
    tm = a_ref.shape[0]

    def merged(c):
        rows = slice(c * ROW_CHUNK, (c + 1) * ROW_CHUNK)
        lses = []
        for g, dil in enumerate(dils):
            if dil == 1:
                lses.append(lse_refs[g][0, rows, :])
                continue
            n = ROW_CHUNK // dil
            d1, d2 = _stride_split(dil)
            n1 = ROW_CHUNK // d1

            def to_token_order(dst_ref, dst_idx, src):
                if d2 == 1:
                    for r in range(dil):
                        dst_ref[dst_idx, pl.ds(c * ROW_CHUNK + r, n, stride=dil), :] = src(r)
                    return
                for r1 in range(d1):
                    for r2 in range(d2):
                        tt_ref[pl.ds(r1 * n1 + r2, n, stride=d2), :] = src(r1 + d1 * r2)
                for r1 in range(d1):
                    dst_ref[dst_idx, pl.ds(c * ROW_CHUNK + r1, n1, stride=d1), :] = tt_ref[r1 * n1:(r1 + 1) * n1, :]

            to_token_order(lt_ref, g, lambda r: lse_refs[g][r, c * n:(c + 1) * n, :])
            for h in range(heads):
                to_token_order(ot_ref, g * heads + h, lambda r: o_refs[g][
                    r, c * n:(c + 1) * n, h * HEAD_DIM:(h + 1) * HEAD_DIM].astype(F32))
            lses.append(lt_ref[g, rows, :])

        mx = functools.reduce(jnp.maximum, lses)
        es = [jnp.exp(l - mx) for l in lses]
        den = functools.reduce(jnp.add, es)
        wts = [e / den for e in es]
        for h in range(heads):
            sl = slice(h * HEAD_DIM, (h + 1) * HEAD_DIM)
            acc = None
            for g, dil in enumerate(dils):
                o = o_refs[g][0, rows, sl].astype(F32) if dil == 1 else ot_ref[g * heads + h, rows, :]
                term = wts[g][:, h:h + 1] * o
                acc = term if acc is None else acc + term
            a_ref[rows, sl] = acc.astype(BF16)

    merged(0)
    for c in range(tm // ROW_CHUNK):
        rows = slice(c * ROW_CHUNK, (c + 1) * ROW_CHUNK)
        if c + 1 < tm // ROW_CHUNK:
            merged(c + 1)
        out_ref[rows, :] = r_ref[rows, :] + jnp.dot(a_ref[rows, :], w_ref[...], preferred_element_type=F32)


def merge_proj_residual(outs, lses, w, res, *, seq, tm):
    dils = tuple(o.shape[1] for o in outs)
    k = outs[0].shape[-1]
    m, n = res.shape
    g_n = len(outs)
    heads = k // HEAD_DIM
    tpb = seq // tm
    assert all(ROW_CHUNK % (BF16_SUBLANES * dil) == 0 for dil in dils) and tm % ROW_CHUNK == 0

    def stream_spec(dil, width):
        return pl.BlockSpec((None, dil, tm // dil, width), lambda i: (i // tpb, 0, i % tpb, 0))

    vmem = (2 * g_n * tm * k * 2 + 2 * g_n * tm * HEAD_DIM * 4 + 2 * k * n * 2 + 4 * tm * n * 4
            + tm * k * 2 + tm * n * 4 + g_n * tm * HEAD_DIM * 4 + g_n * tm * k * 4)
    return pl.pallas_call(
        functools.partial(_merge_proj_kernel, dils=dils, heads=heads),
        grid=(m // tm,),
        in_specs=([stream_spec(dil, k) for dil in dils]
                  + [stream_spec(dil, HEAD_DIM) for dil in dils]
                  + [pl.BlockSpec((k, n), lambda i: (0, 0)),
                     pl.BlockSpec((tm, n), lambda i: (i, 0))]),
        out_specs=pl.BlockSpec((tm, n), lambda i: (i, 0)),
        out_shape=jax.ShapeDtypeStruct((m, n), F32),
        scratch_shapes=[pltpu.VMEM((tm, k), BF16),
                        pltpu.VMEM((g_n, tm, HEAD_DIM), F32),
                        pltpu.VMEM((g_n * heads, tm, HEAD_DIM), F32),
                        pltpu.VMEM((ROW_CHUNK, HEAD_DIM), F32)],
        compiler_params=_params(("parallel",), vmem),
        name="dil_merge_proj_residual",
    )(*outs, *lses, w, res)


def kernel(x, na_norm, na_wqkv, na_rpb, na_wo, ffn0_norm, ffn0_w1, ffn0_w2, dil_norm, dil_wqkv, dil_wo,
           ffn1_norm, ffn1_w1, ffn1_w2, final_norm):
    batch, seq, d = x.shape
    m = batch * seq
    rows = seq // GRID_W
    na_heads = na_rpb.shape[0]
    assert rows >= NA_KEY_ROWS and rows % NA_BLOCK_ROWS == 0 and rows >= NA_WIN_ROWS
    assert NA_KEY_ROWS >= NA_BLOCK_ROWS + NA_WIN_ROWS - 1

    xf = x.reshape(m, d)
    bf = lambda w: w.astype(BF16)

    qw = na_heads * HEAD_DIM
    col_scale = jnp.concatenate([jnp.full((qw,), Q_SCALE, F32), jnp.ones((2 * qw,), F32)])
    qkv, na_wo_b, ffn0_w1_b, ffn0_w2_b = norm_matmul(
        xf, na_norm, bf(na_wqkv), col_scale, tm=min(1024, m), tn=min(1024, qw), cast=(na_wo, ffn0_w1, ffn0_w2))
    bias = _na_bias_table(na_rpb)
    attn = na_attention(qkv, bias, batch=batch, seq=seq, heads_per_step=min(16, na_heads))
    xf, rot = proj_residual(attn, na_wo_b, xf, tm=min(512, m), rope_seq=seq)
    xf, dil_wqkv_b, dil_wo_b, ffn1_w1_b, ffn1_w2_b = ffn(
        xf, ffn0_norm, ffn0_w1_b, ffn0_w2_b, final_norm, tm=min(1024, m), tf=512, final_norm=False,
        cast=(dil_wqkv, dil_wo, ffn1_w1, ffn1_w2))

    dils = tuple(dil for _, dil in DIL_GROUPS)
    qkvs = norm_matmul_rope(xf, dil_norm, dil_wqkv_b, rot, batch=batch, seq=seq, tm=min(1024, seq), dils=dils)
    outs, lses = [], []
    for gi, (window, dil) in enumerate(DIL_GROUPS):
        o, lse = dil_attention(qkvs[gi], group=gi, halo=(window // 2) // dil)
        outs.append(o)
        lses.append(lse)
    xf = merge_proj_residual(outs, lses, dil_wo_b, xf, seq=seq, tm=min(512, seq))
    xf, = ffn(xf, ffn1_norm, ffn1_w1_b, ffn1_w2_b, final_norm, tm=min(1024, m), tf=512, final_norm=True)
    return xf.reshape(batch, seq, d)
```

```python
import functools
import math

import numpy as np
import jax
import jax.numpy as jnp
from jax import lax
from jax.experimental import pallas as pl
from jax.experimental.pallas import tpu as pltpu

GRID_W = 64
HEAD_DIM = 128
NA_WIN_ROWS = 8
NA_WIN_COLS = 16
DIL_GROUPS = ((128, 1), (512, 4), (2048, 16))
ROPE_THETA = 500000.0
ROPE_DIM = HEAD_DIM // 4
BLOCK_Q = 128
EPS = 1e-6
NEG = -1e30

LOG2E = math.log2(math.e)
LN2 = math.log(2.0)
Q_SCALE = HEAD_DIM ** -0.5 * LOG2E

NA_BLOCK_ROWS = 4
NA_KEY_BACK = NA_WIN_ROWS // 2
NA_KEY_ROWS = 12
NA_KEY_CHUNKS = 3

DIL_STEP_ROWS = 512
ROW_CHUNK = 256
FFN_ROW_CHUNK = 256
BF16_SUBLANES = 16
F32_SUBLANES = 8
LANES = 128

V7X_VMEM_BYTES = 64 * 1024 * 1024
VMEM_CAP_BYTES = V7X_VMEM_BYTES - 8 * 1024 * 1024

F32 = jnp.float32
BF16 = jnp.bfloat16


def _params(semantics, vmem_bytes):
    limit = min(int(vmem_bytes * 1.25) + (4 << 20), VMEM_CAP_BYTES)
    return pltpu.CompilerParams(dimension_semantics=semantics, vmem_limit_bytes=limit)


def _rms(x, g):
    return x * lax.rsqrt(jnp.mean(x * x, axis=-1, keepdims=True) + EPS) * g


def _tile_then_next(n_tiles, n_inner):
    switch = max(n_inner // 2, 1)
    return lambda i, j: (jnp.minimum(i + jnp.where(j >= switch, 1, 0), n_tiles - 1), 0)


def _cast_rows(n_rows, n_steps):
    for rows in range(BF16_SUBLANES, n_rows + 1, BF16_SUBLANES):
        if n_rows % rows == 0 and n_rows // rows <= n_steps:
            return rows
    raise ValueError((n_rows, n_steps))


def _cast_specs(weights, n_steps, n_inner):
    specs = []
    for w in weights:
        rows = _cast_rows(w.shape[0], n_steps)
        last = w.shape[0] // rows - 1
        specs.append(pl.BlockSpec((rows, w.shape[1]), lambda i, j, last=last: (jnp.minimum(i * n_inner + j, last), 0)))
    return specs


def _cast_vmem(weights, n_steps):
    return sum(2 * _cast_rows(w.shape[0], n_steps) * w.shape[1] * 6 for w in weights)


def _cast_slabs(src_refs, dst_refs):
    for src, dst in zip(src_refs, dst_refs):
        dst[...] = src[...].astype(BF16)


def _norm_matmul_kernel(x_ref, g_ref, w_ref, cs_ref, *refs, n_cast):
    cast_src = refs[:n_cast]
    o_ref = refs[n_cast]
    cast_dst = refs[n_cast + 1:2 * n_cast + 1]
    h_ref = refs[2 * n_cast + 1]

    def body(with_norm):
        _cast_slabs(cast_src, cast_dst)
        for c in range(x_ref.shape[0] // ROW_CHUNK):
            rows = slice(c * ROW_CHUNK, (c + 1) * ROW_CHUNK)
            if with_norm:
                h_ref[rows, :] = _rms(x_ref[rows, :], g_ref[...]).astype(BF16)
            acc = jnp.dot(h_ref[rows, :], w_ref[...], preferred_element_type=F32)
            o_ref[rows, :] = (acc * cs_ref[...]).astype(o_ref.dtype)

    pl.when(pl.program_id(1) == 0)(functools.partial(body, True))
    pl.when(pl.program_id(1) != 0)(functools.partial(body, False))


def norm_matmul(x, g, w, col_scale, *, tm, tn, cast=()):
    m, k = x.shape
    n = w.shape[1]
    n_steps = (m // tm) * (n // tn)
    vmem = (2 * tm * k * 4 + tm * k * 2 + 2 * k * tn * 2 + 2 * tm * tn * 2 + tm * tn * 4
            + _cast_vmem(cast, n_steps))
    cast_specs = _cast_specs(cast, n_steps, n // tn)
    return pl.pallas_call(
        functools.partial(_norm_matmul_kernel, n_cast=len(cast)),
        grid=(m // tm, n // tn),
        in_specs=[
            pl.BlockSpec((tm, k), _tile_then_next(m // tm, n // tn)),
            pl.BlockSpec((1, k), lambda i, j: (0, 0)),
            pl.BlockSpec((k, tn), lambda i, j: (0, j)),
            pl.BlockSpec((1, tn), lambda i, j: (0, j)),
        ] + cast_specs,
        out_specs=[pl.BlockSpec((tm, tn), lambda i, j: (i, j))] + cast_specs,
        out_shape=[jax.ShapeDtypeStruct((m, n), BF16)] + [jax.ShapeDtypeStruct(c.shape, BF16) for c in cast],
        scratch_shapes=[pltpu.VMEM((tm, k), BF16)],
        compiler_params=_params(("arbitrary", "arbitrary"), vmem),
        name="norm_qkv_na",
    )(x, g.reshape(1, k), w, col_scale.reshape(1, n), *cast)


def _norm_matmul_rope_kernel(x_ref, g_ref, w_ref, rot_ref, *refs, dils):
    o_refs = refs[:len(dils)]
    h_ref, s_ref, t_ref = refs[len(dils):]
    j = pl.program_id(1)
    tm = x_ref.shape[0]
    heads = s_ref.shape[0]
    half = ROPE_DIM // 2

    def body(o_ref, dil, rotate, with_norm=False):
        n = ROW_CHUNK // dil
        d1, d2 = _stride_split(dil)
        n1 = ROW_CHUNK // d1
        scale = jnp.where(j % 3 == 0, Q_SCALE, 1.0).astype(F32)
        for c in range(tm // ROW_CHUNK):
            rows = slice(c * ROW_CHUNK, (c + 1) * ROW_CHUNK)
            if with_norm:
                h_ref[rows, :] = _rms(x_ref[rows, :], g_ref[...]).astype(BF16)
            acc = jnp.dot(h_ref[rows, :], w_ref[...], preferred_element_type=F32)
            if rotate:
                cos = rot_ref[0, rows, :] * scale
                sin_hi = rot_ref[1, rows, :] * scale
                sin_lo = rot_ref[2, rows, :] * scale
            for h in range(heads):
                cols = slice(h * HEAD_DIM, (h + 1) * HEAD_DIM)
                a = acc[:, cols]
                if rotate:
                    a = (a * cos + pltpu.roll(a, half, 1) * sin_hi
                         + pltpu.roll(a, HEAD_DIM - half, 1) * sin_lo)
                if dil == 1:
                    o_ref[0, rows, cols] = a.astype(BF16)
                    continue
                s_ref[h, rows, :] = a
                if d2 == 1:
                    for r in range(dil):
                        o_ref[r, c * n:(c + 1) * n, cols] = (
                            s_ref[h, pl.ds(c * ROW_CHUNK + r, n, stride=dil), :].astype(BF16))
                    continue
                for r1 in range(d1):
                    t_ref[h, r1 * n1:(r1 + 1) * n1, :] = s_ref[h, pl.ds(c * ROW_CHUNK + r1, n1, stride=d1), :]
                for r1 in range(d1):
                    for r2 in range(d2):
                        o_ref[r1 + d1 * r2, c * n:(c + 1) * n, cols] = (
                            t_ref[h, pl.ds(r1 * n1 + r2, n, stride=d2), :].astype(BF16))

    pl.when(j == 0)(functools.partial(body, o_refs[0], dils[0], True, with_norm=True))
    for g, dil in enumerate(dils):
        pl.when((j // 3 == g) & (j % 3 != 2) & (j != 0))(functools.partial(body, o_refs[g], dil, True))
        pl.when((j // 3 == g) & (j % 3 == 2))(functools.partial(body, o_refs[g], dil, False))


def _stride_split(dil):
    if dil <= F32_SUBLANES:
        return dil, 1
    d1 = F32_SUBLANES // 2
    assert dil % d1 == 0 and dil // d1 <= F32_SUBLANES
    return d1, dil // d1


def norm_matmul_rope(x, g, w, rot, *, batch, seq, tm, dils):
    m, k = x.shape
    n = w.shape[1]
    tn = n // (3 * len(dils))
    heads = tn // HEAD_DIM
    tpb = seq // tm
    assert all(ROW_CHUNK % (BF16_SUBLANES * dil) == 0 for dil in dils) and tm % ROW_CHUNK == 0

    def out_map(g):
        return lambda i, j: (jnp.clip(j - 3 * g, 0, 2), i // tpb, 0, i % tpb, 0)

    vmem = (2 * tm * k * 4 + tm * k * 2 + 2 * k * tn * 2 + 2 * len(dils) * tm * tn * 2 + tm * tn * 4
            + 2 * ROW_CHUNK * tn * 4 + 2 * 3 * tm * HEAD_DIM * 4 + heads * ROW_CHUNK * HEAD_DIM * 4)
    return pl.pallas_call(
        functools.partial(_norm_matmul_rope_kernel, dils=dils),
        grid=(m // tm, n // tn),
        in_specs=[
            pl.BlockSpec((tm, k), _tile_then_next(m // tm, n // tn)),
            pl.BlockSpec((1, k), lambda i, j: (0, 0)),
            pl.BlockSpec((k, tn), lambda i, j: (0, j)),
            pl.BlockSpec((3, tm, HEAD_DIM), lambda i, j: (0, i % tpb, 0)),
        ],
        out_specs=[pl.BlockSpec((None, None, dil, tm // dil, tn), out_map(g)) for g, dil in enumerate(dils)],
        out_shape=[jax.ShapeDtypeStruct((3, batch, dil, seq // dil, tn), BF16) for dil in dils],
        scratch_shapes=[pltpu.VMEM((tm, k), BF16), pltpu.VMEM((heads, tm, HEAD_DIM), F32),
                        pltpu.VMEM((heads, ROW_CHUNK, HEAD_DIM), F32)],
        compiler_params=_params(("arbitrary", "arbitrary"), vmem),
        name="norm_qkv_dil_rope",
    )(x, g.reshape(1, k), w, rot)


def _na_bias_table(rpb):
    nh, n_row_off, n_col_off = rpb.shape
    assert 2 * GRID_W == LANES and NA_KEY_ROWS % 2 == 0 and n_col_off <= LANES
    rpb_lanes = jnp.pad(rpb, ((0, 0), (0, 0), (0, LANES - n_col_off)))
    return pl.pallas_call(
        _na_bias_kernel,
        grid=(nh,),
        in_specs=[pl.BlockSpec((None, n_row_off, LANES), lambda h: (h, 0, 0))],
        out_specs=pl.BlockSpec((None, NA_BLOCK_ROWS * GRID_W, NA_KEY_ROWS * GRID_W), lambda h: (h, 0, 0)),
        out_shape=jax.ShapeDtypeStruct((nh, NA_BLOCK_ROWS * GRID_W, NA_KEY_ROWS * GRID_W), F32),
        compiler_params=_params(("parallel",), 4 * NA_BLOCK_ROWS * GRID_W * NA_KEY_ROWS * GRID_W * 4),
        name="na_bias_table",
    )(rpb_lanes)


def _na_bias_kernel(rpb_ref, o_ref):
    w = GRID_W
    qc = lax.broadcasted_iota(jnp.int32, (w, LANES), 0)
    kc = lax.broadcasted_iota(jnp.int32, (w, LANES), 1)
    start = jnp.clip(qc - NA_WIN_COLS // 2, 0, w - NA_WIN_COLS)
    in_window = (kc >= start) & (kc < start + NA_WIN_COLS)
    by_col = []
    for a in range(rpb_ref.shape[0]):
        row = jnp.broadcast_to(rpb_ref[a:a + 1, :], (w, LANES))
        shifted = pltpu.roll(row, LANES - (NA_WIN_COLS - 1), 1, stride=1, stride_axis=0)
        by_col.append(jnp.where(in_window, shifted * LOG2E, NEG))
    for rr in range(NA_BLOCK_ROWS):
        for kk in range(0, NA_KEY_ROWS, 2):
            a = kk - NA_KEY_BACK - rr + NA_WIN_ROWS - 1
            assert 0 <= a and a + 1 < rpb_ref.shape[0]
            pair = jnp.where(kc < w, by_col[a], pltpu.roll(by_col[a + 1], w, 1))
            o_ref[rr * w:(rr + 1) * w, kk * w:(kk + 2) * w] = pair


def _na_attn_kernel(q_ref, *refs, heads, rows):
    k_refs = refs[:NA_KEY_CHUNKS]
    v_refs = refs[NA_KEY_CHUNKS:2 * NA_KEY_CHUNKS]
    b_ref, o_ref, bm_ref = refs[2 * NA_KEY_CHUNKS:]
    qb = q_ref.shape[0]
    ck = k_refs[0].shape[0]
    kh = min(NA_WIN_ROWS, rows)
    i = pl.program_id(2)

    def clipped(blk):
        r0 = blk * NA_BLOCK_ROWS
        return (r0 < kh // 2) | (r0 + NA_BLOCK_ROWS - 1 - kh // 2 > rows - kh)

    @pl.when((i == 0) | clipped(i) | clipped(i - 1))
    def _():
        r0 = i * NA_BLOCK_ROWS
        q_row = r0 + lax.broadcasted_iota(jnp.int32, (qb, ck), 0) // GRID_W
        lo = jnp.clip(q_row - kh // 2, 0, rows - kh)
        for c in range(NA_KEY_CHUNKS):
            k_row = (r0 - NA_KEY_BACK + c * (ck // GRID_W)
                     + lax.broadcasted_iota(jnp.int32, (qb, ck), 1) // GRID_W)
            row_mask = jnp.where((k_row >= lo) & (k_row < lo + kh), 0.0, NEG).astype(F32)
            for h in range(heads):
                bm_ref[h, :, c * ck:(c + 1) * ck] = b_ref[h, :, c * ck:(c + 1) * ck] + row_mask

    def scores(h):
        sl = slice(h * HEAD_DIM, (h + 1) * HEAD_DIM)
        q = q_ref[:, sl]
        s = []
        for c, k_ref in enumerate(k_refs):
            sc = lax.dot_general(q, k_ref[:, sl], (((1,), (1,)), ((), ())), preferred_element_type=F32)
            s.append(sc + bm_ref[h, :, c * ck:(c + 1) * ck])
        return s, jnp.max(functools.reduce(jnp.maximum, s), axis=-1, keepdims=True)

    def finish(h, s, m):
        sl = slice(h * HEAD_DIM, (h + 1) * HEAD_DIM)
        p = [jnp.exp2(sc - m) for sc in s]
        l = jnp.sum(functools.reduce(jnp.add, p), axis=-1, keepdims=True)
        o = functools.reduce(jnp.add, [
            jnp.dot(pc.astype(BF16), v_ref[:, sl], preferred_element_type=F32)
            for pc, v_ref in zip(p, v_refs)])
        o_ref[:, sl] = (o / l).astype(o_ref.dtype)

    cur = scores(0)
    for h in range(heads):
        nxt = scores(h + 1) if h + 1 < heads else None
        finish(h, *cur)
        cur = nxt


def na_attention(qkv, bias, *, batch, seq, heads_per_step):
    m = qkv.shape[0]
    nh = bias.shape[0]
    hg = heads_per_step
    wq = hg * HEAD_DIM
    qb = NA_BLOCK_ROWS * GRID_W
    kb = NA_KEY_ROWS * GRID_W // NA_KEY_CHUNKS
    assert qb % kb == 0 and (NA_KEY_BACK * GRID_W) % kb == 0
    nblk = seq // qb
    kpb = seq // kb
    ratio = qb // kb
    back = NA_KEY_BACK * GRID_W // kb
    ncol = nh // hg

    def k_map(c, sec):
        def index(g, b, i):
            return (b * kpb + jnp.clip(i * ratio - back + c, 0, kpb - 1), sec * ncol + g)
        return index

    in_specs = [pl.BlockSpec((qb, wq), lambda g, b, i: (b * nblk + i, g))]
    in_specs += [pl.BlockSpec((kb, wq), k_map(c, 1)) for c in range(NA_KEY_CHUNKS)]
    in_specs += [pl.BlockSpec((kb, wq), k_map(c, 2)) for c in range(NA_KEY_CHUNKS)]
    in_specs += [pl.BlockSpec((hg, qb, NA_KEY_CHUNKS * kb), lambda g, b, i: (g, 0, 0),
                              pipeline_mode=pl.Buffered(1))]
    vmem = (2 * 2 * qb * wq * 2 + 2 * 2 * NA_KEY_CHUNKS * kb * wq * 2
            + 2 * hg * qb * NA_KEY_CHUNKS * kb * 4 + 6 * qb * NA_KEY_CHUNKS * kb * 4)
    return pl.pallas_call(
        functools.partial(_na_attn_kernel, heads=hg, rows=seq // GRID_W),
        grid=(ncol, batch, nblk),
        in_specs=in_specs,
        out_specs=pl.BlockSpec((qb, wq), lambda g, b, i: (b * nblk + i, g)),
        out_shape=jax.ShapeDtypeStruct((m, nh * HEAD_DIM), BF16),
        scratch_shapes=[pltpu.VMEM((hg, qb, NA_KEY_CHUNKS * kb), F32)],
        compiler_params=_params(("parallel", "parallel", "arbitrary"), vmem),
        name="na_attention",
    )(qkv, *([qkv] * (2 * NA_KEY_CHUNKS)), bias)


def _proj_residual_kernel(a_ref, w_ref, r_ref, freq_ref, o_ref, rot_ref):
    _, rows, _ = rot_ref.shape
    pos = (pl.program_id(0) * rows + lax.broadcasted_iota(jnp.int32, (rows, LANES), 0)).astype(F32)
    ang = pos * freq_ref[...]
    lane = lax.broadcasted_iota(jnp.int32, (rows, LANES), 1)
    half = ROPE_DIM // 2
    sin = jnp.sin(ang)
    rot_ref[0] = jnp.cos(ang)
    rot_ref[1] = jnp.where((lane >= half) & (lane < ROPE_DIM), sin, 0.0)
    rot_ref[2] = jnp.where(lane < half, -sin, 0.0)

    o_ref[...] = r_ref[...] + jnp.dot(a_ref[...], w_ref[...], preferred_element_type=F32)


def proj_residual(a, w, res, *, tm, rope_seq):
    m, k = a.shape
    n = w.shape[1]
    n_steps = m // tm
    assert rope_seq % n_steps == 0 and (rope_seq // n_steps) % F32_SUBLANES == 0 and HEAD_DIM == LANES
    rope_rows = rope_seq // n_steps
    inv_freq = ROPE_THETA ** (-jnp.arange(0, ROPE_DIM, 2, dtype=F32) / ROPE_DIM)
    freq = jnp.concatenate([inv_freq, inv_freq, jnp.zeros((HEAD_DIM - ROPE_DIM,), F32)]).reshape(1, HEAD_DIM)
    vmem = (2 * tm * k * 2 + 2 * k * n * 2 + 4 * tm * n * 4 + tm * n * 4
            + 2 * 3 * rope_rows * HEAD_DIM * 4 * 3)
    return pl.pallas_call(
        _proj_residual_kernel,
        grid=(n_steps,),
        in_specs=[
            pl.BlockSpec((tm, k), lambda i: (i, 0)),
            pl.BlockSpec((k, n), lambda i: (0, 0)),
            pl.BlockSpec((tm, n), lambda i: (i, 0)),
            pl.BlockSpec((1, HEAD_DIM), lambda i: (0, 0)),
        ],
        out_specs=[pl.BlockSpec((tm, n), lambda i: (i, 0)),
                   pl.BlockSpec((3, rope_rows, HEAD_DIM), lambda i: (0, i, 0))],
        out_shape=[jax.ShapeDtypeStruct((m, n), F32),
                   jax.ShapeDtypeStruct((3, rope_seq, HEAD_DIM), F32)],
        compiler_params=_params(("parallel",), vmem),
        name="na_proj_residual",
    )(a, w, res, freq)


def _ffn_kernel(x_ref, g_ref, w1_ref, w2_ref, gf_ref, *refs, final_norm, n_cast):
    cast_src = refs[:n_cast]
    o_ref = refs[n_cast]
    cast_dst = refs[n_cast + 1:2 * n_cast + 1]
    h_ref = refs[2 * n_cast + 1]
    f = pl.program_id(1)

    def body(first):
        _cast_slabs(cast_src, cast_dst)
        for c in range(x_ref.shape[0] // FFN_ROW_CHUNK):
            rows = slice(c * FFN_ROW_CHUNK, (c + 1) * FFN_ROW_CHUNK)
            if first:
                h_ref[rows, :] = _rms(x_ref[rows, :], g_ref[...]).astype(BF16)
            a = jnp.dot(h_ref[rows, :], w1_ref[...], preferred_element_type=F32)
            a = jnp.square(jnp.maximum(a, 0.0)).astype(BF16)
            y = jnp.dot(a, w2_ref[...], preferred_element_type=F32)
            if first:
                o_ref[rows, :] = x_ref[rows, :] + y
            else:
                o_ref[rows, :] += y

    pl.when(f == 0)(functools.partial(body, True))
    pl.when(f != 0)(functools.partial(body, False))

    if final_norm:
        @pl.when(f == pl.num_programs(1) - 1)
        def _():
            o_ref[...] = _rms(o_ref[...], gf_ref[...])


def ffn(x, g, w1, w2, g_final, *, tm, tf, final_norm, cast=()):
    m, d = x.shape
    dff = w1.shape[1]
    assert tm % FFN_ROW_CHUNK == 0
    n_steps = (m // tm) * (dff // tf)
    vmem = (4 * tm * d * 4 + tm * d * 2 + 2 * 2 * d * tf * 2
            + FFN_ROW_CHUNK * tf * 6 + FFN_ROW_CHUNK * d * 4 + _cast_vmem(cast, n_steps))
    cast_specs = _cast_specs(cast, n_steps, dff // tf)
    return pl.pallas_call(
        functools.partial(_ffn_kernel, final_norm=final_norm, n_cast=len(cast)),
        grid=(m // tm, dff // tf),
        in_specs=[
            pl.BlockSpec((tm, d), _tile_then_next(m // tm, dff // tf)),
            pl.BlockSpec((1, d), lambda i, f: (0, 0)),
            pl.BlockSpec((d, tf), lambda i, f: (0, f)),
            pl.BlockSpec((tf, d), lambda i, f: (f, 0)),
            pl.BlockSpec((1, d), lambda i, f: (0, 0)),
        ] + cast_specs,
        out_specs=[pl.BlockSpec((tm, d), lambda i, f: (i, 0))] + cast_specs,
        out_shape=[jax.ShapeDtypeStruct((m, d), F32)] + [jax.ShapeDtypeStruct(c.shape, BF16) for c in cast],
        scratch_shapes=[pltpu.VMEM((tm, d), BF16)],
        compiler_params=_params(("arbitrary", "arbitrary"), vmem),
        name="ffn_final" if final_norm else "ffn",
    )(x, g.reshape(1, d), w1, w2, g_final.reshape(1, d), *cast)


def _dil_attn_kernel(q_ref, kp_ref, kc_ref, kn_ref, vp_ref, vc_ref, vn_ref, o_ref, lse_ref,
                     *, heads, halo, length):
    qb = q_ref.shape[0]
    sub = BLOCK_Q
    nk = sub + 2 * halo
    m0 = pl.program_id(2) * qb
    qi = lax.broadcasted_iota(jnp.int32, (sub, nk), 0)
    kj = lax.broadcasted_iota(jnp.int32, (sub, nk), 1) - halo
    band = jnp.abs(kj - qi) <= halo
    lane = lax.broadcasted_iota(jnp.int32, (sub, HEAD_DIM), 1)

    def window(prev_ref, cur_ref, next_ref, lo, hi, cols):
        parts = [prev_ref[:, cols]] if lo < 0 else []
        parts.append(cur_ref[max(lo, 0):min(hi, qb), cols])
        if hi > qb:
            parts.append(next_ref[:, cols])
        return parts[0] if len(parts) == 1 else jnp.concatenate(parts, axis=0)

    def scores(u, h):
        rows = slice(u * sub, (u + 1) * sub)
        cols = slice(h * HEAD_DIM, (h + 1) * HEAD_DIM)
        k = window(kp_ref, kc_ref, kn_ref, u * sub - halo, (u + 1) * sub + halo, cols)
        s = lax.dot_general(q_ref[rows, cols], k, (((1,), (1,)), ((), ())), preferred_element_type=F32)
        s = s + masks[u]
        return s, jnp.max(s, axis=-1, keepdims=True)

    def finish(u, h, s, m):
        rows = slice(u * sub, (u + 1) * sub)
        cols = slice(h * HEAD_DIM, (h + 1) * HEAD_DIM)
        v = window(vp_ref, vc_ref, vn_ref, u * sub - halo, (u + 1) * sub + halo, cols)
        p = jnp.exp2(s - m)
        l = jnp.sum(p, axis=-1, keepdims=True)
        o = jnp.dot(p.astype(BF16), v, preferred_element_type=F32)
        o_ref[rows, cols] = (o / l).astype(o_ref.dtype)
        return m * LN2 + jnp.log(l)

    masks = []
    for u in range(qb // sub):
        kpos = kj + (m0 + u * sub)
        masks.append(jnp.where(band & (kpos >= 0) & (kpos < length), 0.0, NEG).astype(F32))

    units = [(u, h) for u in range(qb // sub) for h in range(heads)]
    cur = scores(*units[0])
    lse_tile = None
    for n, (u, h) in enumerate(units):
        nxt = scores(*units[n + 1]) if n + 1 < len(units) else None
        lse = finish(u, h, *cur)
        lse_tile = jnp.where(lane == h, lse, jnp.zeros((sub, HEAD_DIM), F32) if h == 0 else lse_tile)
        if h == heads - 1:
            lse_ref[u * sub:(u + 1) * sub, :] = lse_tile
        cur = nxt


def dil_attention(qkv, *, group, halo):
    _, batch, dil, length, wq = qkv.shape
    heads = wq // HEAD_DIM
    qb = min(DIL_STEP_ROWS, length)
    assert BLOCK_Q % halo == 0 and qb % BLOCK_Q == 0 and length % qb == 0
    nblk = length // qb
    hpb = length // halo
    per = qb // halo

    def cur_map(sec):
        return lambda b, r, i: (sec, b, r, i, 0)

    def prev_map(sec):
        return lambda b, r, i: (sec, b, r, jnp.maximum(i * per - 1, 0), 0)

    def next_map(sec):
        return lambda b, r, i: (sec, b, r, jnp.minimum((i + 1) * per, hpb - 1), 0)

    def spec(rows, index_map):
        return pl.BlockSpec((None, None, None, rows, wq), index_map)

    in_specs = [spec(qb, cur_map(0))]
    for sec in (1, 2):
        in_specs += [spec(halo, prev_map(sec)), spec(qb, cur_map(sec)), spec(halo, next_map(sec))]
    vmem = (2 * (2 * qb * wq * 2 + 2 * (qb + 2 * halo) * wq * 2 + qb * HEAD_DIM * 4)
            + 8 * BLOCK_Q * (BLOCK_Q + 2 * halo) * 4)
    return pl.pallas_call(
        functools.partial(_dil_attn_kernel, heads=heads, halo=halo, length=length),
        grid=(batch, dil, nblk),
        in_specs=in_specs,
        out_specs=[pl.BlockSpec((None, None, qb, wq), lambda b, r, i: (b, r, i, 0)),
                   pl.BlockSpec((None, None, qb, HEAD_DIM), lambda b, r, i: (b, r, i, 0))],
        out_shape=[jax.ShapeDtypeStruct((batch, dil, length, wq), BF16),
                   jax.ShapeDtypeStruct((batch, dil, length, HEAD_DIM), F32)],
        compiler_params=_params(("parallel", "parallel", "arbitrary"), vmem),
        name=f"dil_attention_g{group}",
    )(*([qkv] * 7))


def _merge_proj_kernel(*refs, dils, heads):
    g_n = len(dils)
    o_refs = refs[:g_n]
    lse_refs = refs[g_n:2 * g_n]
    w_ref, r_ref, out_ref, a_ref, lt_ref, ot_ref, tt_ref = refs[2 * g_n:]
    tm = a_ref.shape[0]

    def merged(c):
        rows = slice(c * ROW_CHUNK, (c + 1) * ROW_CHUNK)
        lses = []
        for g, dil in enumerate(dils):
            if dil == 1:
                lses.append(lse_refs[g][0, rows, :])
                continue
            n = ROW_CHUNK // dil
            d1, d2 = _stride_split(dil)
            n1 = ROW_CHUNK // d1

            def to_token_order(dst_ref, dst_idx, src):
                if d2 == 1:
                    for r in range(dil):
                        dst_ref[dst_idx, pl.ds(c * ROW_CHUNK + r, n, stride=dil), :] = src(r)
                    return
                for r1 in range(d1):
                    for r2 in range(d2):
                        tt_ref[pl.ds(r1 * n1 + r2, n, stride=d2), :] = src(r1 + d1 * r2)
                for r1 in range(d1):
                    dst_ref[dst_idx, pl.ds(c * ROW_CHUNK + r1, n1, stride=d1), :] = tt_ref[r1 * n1:(r1 + 1) * n1, :]

            to_token_order(lt_ref, g, lambda r: lse_refs[g][r, c * n:(c + 1) * n, :])
            for h in range(heads):
                to_token_order(ot_ref, g * heads + h, lambda r: o_refs[g][
                    r, c * n:(c + 1) * n, h * HEAD_DIM:(h + 1) * HEAD_DIM].astype(F32))
            lses.append(lt_ref[g, rows, :])

        mx = functools.reduce(jnp.maximum, lses)
        es = [jnp.exp(l - mx) for l in lses]
        den = functools.reduce(jnp.add, es)
        wts = [e / den for e in es]
        for h in range(heads):
            sl = slice(h * HEAD_DIM, (h + 1) * HEAD_DIM)
            acc = None
            for g, dil in enumerate(dils):
                o = o_refs[g][0, rows, sl].astype(F32) if dil == 1 else ot_ref[g * heads + h, rows, :]
                term = wts[g][:, h:h + 1] * o
                acc = term if acc is None else acc + term
            a_ref[rows, sl] = acc.astype(BF16)

    merged(0)
    for c in range(tm // ROW_CHUNK):
        rows = slice(c * ROW_CHUNK, (c + 1) * ROW_CHUNK)
        if c + 1 < tm // ROW_CHUNK:
            merged(c + 1)
        out_ref[rows, :] = r_ref[rows, :] + jnp.dot(a_ref[rows, :], w_ref[...], preferred_element_type=F32)


def merge_proj_residual(outs, lses, w, res, *, seq, tm):
    dils = tuple(o.shape[1] for o in outs)
    k = outs[0].shape[-1]
    m, n = res.shape
    g_n = len(outs)
    heads = k // HEAD_DIM
    tpb = seq // tm
    assert all(ROW_CHUNK % (BF16_SUBLANES * dil) == 0 for dil in dils) and tm % ROW_CHUNK == 0

    def stream_spec(dil, width):
        return pl.BlockSpec((None, dil, tm // dil, width), lambda i: (i // tpb, 0, i % tpb, 0))

    vmem = (2 * g_n * tm * k * 2 + 2 * g_n * tm * HEAD_DIM * 4 + 2 * k * n * 2 + 4 * tm * n * 4
            + tm * k * 2 + tm * n * 4 + g_n * tm * HEAD_DIM * 4 + g_n * tm * k * 4)
    return pl.pallas_call(
        functools.partial(_merge_proj_kernel, dils=dils, heads=heads),
        grid=(m // tm,),
        in_specs=([stream_spec(dil, k) for dil in dils]
                  + [stream_spec(dil, HEAD_DIM) for dil in dils]
                  + [pl.BlockSpec((k, n), lambda i: (0, 0)),
                     pl.BlockSpec((tm, n), lambda i: (i, 0))]),
        out_specs=pl.BlockSpec((tm, n), lambda i: (i, 0)),
        out_shape=jax.ShapeDtypeStruct((m, n), F32),
        scratch_shapes=[pltpu.VMEM((tm, k), BF16),
                        pltpu.VMEM((g_n, tm, HEAD_DIM), F32),
                        pltpu.VMEM((g_n * heads, tm, HEAD_DIM), F32),
                        pltpu.VMEM((ROW_CHUNK, HEAD_DIM), F32)],
        compiler_params=_params(("parallel",), vmem),
        name="dil_merge_proj_residual",
    )(*outs, *lses, w, res)


def kernel(x, na_norm, na_wqkv, na_rpb, na_wo, ffn0_norm, ffn0_w1, ffn0_w2, dil_norm, dil_wqkv, dil_wo,
           ffn1_norm, ffn1_w1, ffn1_w2, final_norm):
    batch, seq, d = x.shape
    m = batch * seq
    rows = seq // GRID_W
    na_heads = na_rpb.shape[0]
    assert rows >= NA_KEY_ROWS and rows % NA_BLOCK_ROWS == 0 and rows >= NA_WIN_ROWS
    assert NA_KEY_ROWS >= NA_BLOCK_ROWS + NA_WIN_ROWS - 1

    xf = x.reshape(m, d)
    bf = lambda w: w.astype(BF16)

    qw = na_heads * HEAD_DIM
    col_scale = jnp.concatenate([jnp.full((qw,), Q_SCALE, F32), jnp.ones((2 * qw,), F32)])
    qkv, na_wo_b, ffn0_w1_b, ffn0_w2_b = norm_matmul(
        xf, na_norm, bf(na_wqkv), col_scale, tm=min(1024, m), tn=min(1024, qw), cast=(na_wo, ffn0_w1, ffn0_w2))
    bias = _na_bias_table(na_rpb)
    attn = na_attention(qkv, bias, batch=batch, seq=seq, heads_per_step=min(16, na_heads))
    xf, rot = proj_residual(attn, na_wo_b, xf, tm=min(512, m), rope_seq=seq)
    xf, dil_wqkv_b, dil_wo_b, ffn1_w1_b, ffn1_w2_b = ffn(
        xf, ffn0_norm, ffn0_w1_b, ffn0_w2_b, final_norm, tm=min(1024, m), tf=512, final_norm=False,
        cast=(dil_wqkv, dil_wo, ffn1_w1, ffn1_w2))

    dils = tuple(dil for _, dil in DIL_GROUPS)
    qkvs = norm_matmul_rope(xf, dil_norm, dil_wqkv_b, rot, batch=batch, seq=seq, tm=min(1024, seq), dils=dils)
    outs, lses = [], []
    for gi, (window, dil) in enumerate(DIL_GROUPS):
        o, lse = dil_attention(qkvs[gi], group=gi, halo=(window // 2) // dil)
        outs.append(o)
        lses.append(lse)
    xf = merge_proj_residual(outs, lses, dil_wo_b, xf, seq=seq, tm=min(512, seq))
    xf, = ffn(xf, ffn1_norm, ffn1_w1_b, ffn1_w2_b, final_norm, tm=min(1024, m), tf=512, final_norm=True)
    return xf.reshape(batch, seq, d)
```

```python
import functools
import math

import numpy as np
import jax
import jax.numpy as jnp
from jax import lax
from jax.experimental import pallas as pl
from jax.experimental.pallas import tpu as pltpu

GRID_W = 64
HEAD_DIM = 128
NA_WIN_ROWS = 8
NA_WIN_COLS = 16
DIL_GROUPS = ((128, 1), (512, 4), (2048, 16))
ROPE_THETA = 500000.0
ROPE_DIM = HEAD_DIM // 4
BLOCK_Q = 128
EPS = 1e-6
NEG = -1e30

LOG2E = math.log2(math.e)
LN2 = math.log(2.0)
Q_SCALE = HEAD_DIM ** -0.5 * LOG2E

NA_BLOCK_ROWS = 4
NA_KEY_BACK = NA_WIN_ROWS // 2
NA_KEY_ROWS = 12
NA_KEY_CHUNKS = 3

DIL_STEP_ROWS = 512
ROW_CHUNK = 256
FFN_ROW_CHUNK = 1024
BF16_SUBLANES = 16
F32_SUBLANES = 8
LANES = 128

V7X_VMEM_BYTES = 64 * 1024 * 1024
VMEM_CAP_BYTES = V7X_VMEM_BYTES - 8 * 1024 * 1024

F32 = jnp.float32
BF16 = jnp.bfloat16


def _params(semantics, vmem_bytes):
    limit = min(int(vmem_bytes * 1.25) + (4 << 20), VMEM_CAP_BYTES)
    return pltpu.CompilerParams(dimension_semantics=semantics, vmem_limit_bytes=limit)


def _rms(x, g):
    return x * lax.rsqrt(jnp.mean(x * x, axis=-1, keepdims=True) + EPS) * g


def _tile_then_next(n_tiles, n_inner):
    switch = max(n_inner // 2, 1)
    return lambda i, j: (jnp.minimum(i + jnp.where(j >= switch, 1, 0), n_tiles - 1), 0)


def _cast_rows(n_rows, n_steps):
    for rows in range(BF16_SUBLANES, n_rows + 1, BF16_SUBLANES):
        if n_rows % rows == 0 and n_rows // rows <= n_steps:
            return rows
    raise ValueError((n_rows, n_steps))


def _cast_specs(weights, n_steps, n_inner):
    specs = []
    for w in weights:
        rows = _cast_rows(w.shape[0], n_steps)
        last = w.shape[0] // rows - 1
        specs.append(pl.BlockSpec((rows, w.shape[1]), lambda i, j, last=last: (jnp.minimum(i * n_inner + j, last), 0)))
    return specs


def _cast_vmem(weights, n_steps):
    return sum(2 * _cast_rows(w.shape[0], n_steps) * w.shape[1] * 6 for w in weights)


def _cast_slabs(src_refs, dst_refs):
    for src, dst in zip(src_refs, dst_refs):
        dst[...] = src[...].astype(BF16)


def _norm_matmul_kernel(x_ref, g_ref, w_ref, cs_ref, *refs, n_cast):
    cast_src = refs[:n_cast]
    o_ref = refs[n_cast]
    cast_dst = refs[n_cast + 1:2 * n_cast + 1]
    h_ref = refs[2 * n_cast + 1]

    def body(with_norm):
        _cast_slabs(cast_src, cast_dst)
        for c in range(x_ref.shape[0] // ROW_CHUNK):
            rows = slice(c * ROW_CHUNK, (c + 1) * ROW_CHUNK)
            if with_norm:
                h_ref[rows, :] = _rms(x_ref[rows, :], g_ref[...]).astype(BF16)
            acc = jnp.dot(h_ref[rows, :], w_ref[...], preferred_element_type=F32)
            o_ref[rows, :] = (acc * cs_ref[...]).astype(o_ref.dtype)

    pl.when(pl.program_id(1) == 0)(functools.partial(body, True))
    pl.when(pl.program_id(1) != 0)(functools.partial(body, False))


def norm_matmul(x, g, w, col_scale, *, tm, tn, cast=()):
    m, k = x.shape
    n = w.shape[1]
    n_steps = (m // tm) * (n // tn)
    vmem = (2 * tm * k * 4 + tm * k * 2 + 2 * k * tn * 2 + 2 * tm * tn * 2 + tm * tn * 4
            + _cast_vmem(cast, n_steps))
    cast_specs = _cast_specs(cast, n_steps, n // tn)
    return pl.pallas_call(
        functools.partial(_norm_matmul_kernel, n_cast=len(cast)),
        grid=(m // tm, n // tn),
        in_specs=[
            pl.BlockSpec((tm, k), _tile_then_next(m // tm, n // tn)),
            pl.BlockSpec((1, k), lambda i, j: (0, 0)),
            pl.BlockSpec((k, tn), lambda i, j: (0, j)),
            pl.BlockSpec((1, tn), lambda i, j: (0, j)),
        ] + cast_specs,
        out_specs=[pl.BlockSpec((tm, tn), lambda i, j: (i, j))] + cast_specs,
        out_shape=[jax.ShapeDtypeStruct((m, n), BF16)] + [jax.ShapeDtypeStruct(c.shape, BF16) for c in cast],
        scratch_shapes=[pltpu.VMEM((tm, k), BF16)],
        compiler_params=_params(("arbitrary", "arbitrary"), vmem),
        name="norm_qkv_na",
    )(x, g.reshape(1, k), w, col_scale.reshape(1, n), *cast)


def _norm_matmul_rope_kernel(x_ref, g_ref, w_ref, rot_ref, *refs, dils):
    o_refs = refs[:len(dils)]
    h_ref, s_ref, t_ref = refs[len(dils):]
    j = pl.program_id(1)
    tm = x_ref.shape[0]
    heads = s_ref.shape[0]
    half = ROPE_DIM // 2

    def body(o_ref, dil, rotate, with_norm=False):
        n = ROW_CHUNK // dil
        d1, d2 = _stride_split(dil)
        n1 = ROW_CHUNK // d1
        scale = jnp.where(j % 3 == 0, Q_SCALE, 1.0).astype(F32)
        for c in range(tm // ROW_CHUNK):
            rows = slice(c * ROW_CHUNK, (c + 1) * ROW_CHUNK)
            if with_norm:
                h_ref[rows, :] = _rms(x_ref[rows, :], g_ref[...]).astype(BF16)
            acc = jnp.dot(h_ref[rows, :], w_ref[...], preferred_element_type=F32)
            if rotate:
                cos = rot_ref[0, rows, :] * scale
                sin_hi = rot_ref[1, rows, :] * scale
                sin_lo = rot_ref[2, rows, :] * scale
            for h in range(heads):
                cols = slice(h * HEAD_DIM, (h + 1) * HEAD_DIM)
                a = acc[:, cols]
                if rotate:
                    a = (a * cos + pltpu.roll(a, half, 1) * sin_hi
                         + pltpu.roll(a, HEAD_DIM - half, 1) * sin_lo)
                if dil == 1:
                    o_ref[0, rows, cols] = a.astype(BF16)
                    continue
                s_ref[h, rows, :] = a
                if d2 == 1:
                    for r in range(dil):
                        o_ref[r, c * n:(c + 1) * n, cols] = (
                            s_ref[h, pl.ds(c * ROW_CHUNK + r, n, stride=dil), :].astype(BF16))
                    continue
                for r1 in range(d1):
                    t_ref[h, r1 * n1:(r1 + 1) * n1, :] = s_ref[h, pl.ds(c * ROW_CHUNK + r1, n1, stride=d1), :]
                for r1 in range(d1):
                    for r2 in range(d2):
                        o_ref[r1 + d1 * r2, c * n:(c + 1) * n, cols] = (
                            t_ref[h, pl.ds(r1 * n1 + r2, n, stride=d2), :].astype(BF16))

    pl.when(j == 0)(functools.partial(body, o_refs[0], dils[0], True, with_norm=True))
    for g, dil in enumerate(dils):
        pl.when((j // 3 == g) & (j % 3 != 2) & (j != 0))(functools.partial(body, o_refs[g], dil, True))
        pl.when((j // 3 == g) & (j % 3 == 2))(functools.partial(body, o_refs[g], dil, False))


def _stride_split(dil):
    if dil <= F32_SUBLANES:
        return dil, 1
    d1 = F32_SUBLANES // 2
    assert dil % d1 == 0 and dil // d1 <= F32_SUBLANES
    return d1, dil // d1


def norm_matmul_rope(x, g, w, rot, *, batch, seq, tm, dils):
    m, k = x.shape
    n = w.shape[1]
    tn = n // (3 * len(dils))
    heads = tn // HEAD_DIM
    tpb = seq // tm
    assert all(ROW_CHUNK % (BF16_SUBLANES * dil) == 0 for dil in dils) and tm % ROW_CHUNK == 0

    def out_map(g):
        return lambda i, j: (jnp.clip(j - 3 * g, 0, 2), i // tpb, 0, i % tpb, 0)

    vmem = (2 * tm * k * 4 + tm * k * 2 + 2 * k * tn * 2 + 2 * len(dils) * tm * tn * 2 + tm * tn * 4
            + 2 * ROW_CHUNK * tn * 4 + 2 * 3 * tm * HEAD_DIM * 4 + heads * ROW_CHUNK * HEAD_DIM * 4)
    return pl.pallas_call(
        functools.partial(_norm_matmul_rope_kernel, dils=dils),
        grid=(m // tm, n // tn),
        in_specs=[
            pl.BlockSpec((tm, k), _tile_then_next(m // tm, n // tn)),
            pl.BlockSpec((1, k), lambda i, j: (0, 0)),
            pl.BlockSpec((k, tn), lambda i, j: (0, j)),
            pl.BlockSpec((3, tm, HEAD_DIM), lambda i, j: (0, i % tpb, 0)),
        ],
        out_specs=[pl.BlockSpec((None, None, dil, tm // dil, tn), out_map(g)) for g, dil in enumerate(dils)],
        out_shape=[jax.ShapeDtypeStruct((3, batch, dil, seq // dil, tn), BF16) for dil in dils],
        scratch_shapes=[pltpu.VMEM((tm, k), BF16), pltpu.VMEM((heads, tm, HEAD_DIM), F32),
                        pltpu.VMEM((heads, ROW_CHUNK, HEAD_DIM), F32)],
        compiler_params=_params(("arbitrary", "arbitrary"), vmem),
        name="norm_qkv_dil_rope",
    )(x, g.reshape(1, k), w, rot)


def _na_bias_table(rpb):
    nh, n_row_off, n_col_off = rpb.shape
    assert 2 * GRID_W == LANES and NA_KEY_ROWS % 2 == 0 and n_col_off <= LANES
    rpb_lanes = jnp.pad(rpb, ((0, 0), (0, 0), (0, LANES - n_col_off)))
    return pl.pallas_call(
        _na_bias_kernel,
        grid=(nh,),
        in_specs=[pl.BlockSpec((None, n_row_off, LANES), lambda h: (h, 0, 0))],
        out_specs=pl.BlockSpec((None, NA_BLOCK_ROWS * GRID_W, NA_KEY_ROWS * GRID_W), lambda h: (h, 0, 0)),
        out_shape=jax.ShapeDtypeStruct((nh, NA_BLOCK_ROWS * GRID_W, NA_KEY_ROWS * GRID_W), F32),
        compiler_params=_params(("parallel",), 4 * NA_BLOCK_ROWS * GRID_W * NA_KEY_ROWS * GRID_W * 4),
        name="na_bias_table",
    )(rpb_lanes)


def _na_bias_kernel(rpb_ref, o_ref):
    w = GRID_W
    qc = lax.broadcasted_iota(jnp.int32, (w, LANES), 0)
    kc = lax.broadcasted_iota(jnp.int32, (w, LANES), 1)
    start = jnp.clip(qc - NA_WIN_COLS // 2, 0, w - NA_WIN_COLS)
    in_window = (kc >= start) & (kc < start + NA_WIN_COLS)
    by_col = []
    for a in range(rpb_ref.shape[0]):
        row = jnp.broadcast_to(rpb_ref[a:a + 1, :], (w, LANES))
        shifted = pltpu.roll(row, LANES - (NA_WIN_COLS - 1), 1, stride=1, stride_axis=0)
        by_col.append(jnp.where(in_window, shifted * LOG2E, NEG))
    for rr in range(NA_BLOCK_ROWS):
        for kk in range(0, NA_KEY_ROWS, 2):
            a = kk - NA_KEY_BACK - rr + NA_WIN_ROWS - 1
            assert 0 <= a and a + 1 < rpb_ref.shape[0]
            pair = jnp.where(kc < w, by_col[a], pltpu.roll(by_col[a + 1], w, 1))
            o_ref[rr * w:(rr + 1) * w, kk * w:(kk + 2) * w] = pair


def _na_attn_kernel(q_ref, *refs, heads, rows):
    k_refs = refs[:NA_KEY_CHUNKS]
    v_refs = refs[NA_KEY_CHUNKS:2 * NA_KEY_CHUNKS]
    b_ref, o_ref, bm_ref = refs[2 * NA_KEY_CHUNKS:]
    qb = q_ref.shape[0]
    ck = k_refs[0].shape[0]
    kh = min(NA_WIN_ROWS, rows)
    i = pl.program_id(2)

    def clipped(blk):
        r0 = blk * NA_BLOCK_ROWS
        return (r0 < kh // 2) | (r0 + NA_BLOCK_ROWS - 1 - kh // 2 > rows - kh)

    @pl.when((i == 0) | clipped(i) | clipped(i - 1))
    def _():
        r0 = i * NA_BLOCK_ROWS
        q_row = r0 + lax.broadcasted_iota(jnp.int32, (qb, ck), 0) // GRID_W
        lo = jnp.clip(q_row - kh // 2, 0, rows - kh)
        for c in range(NA_KEY_CHUNKS):
            k_row = (r0 - NA_KEY_BACK + c * (ck // GRID_W)
                     + lax.broadcasted_iota(jnp.int32, (qb, ck), 1) // GRID_W)
            row_mask = jnp.where((k_row >= lo) & (k_row < lo + kh), 0.0, NEG).astype(F32)
            for h in range(heads):
                bm_ref[h, :, c * ck:(c + 1) * ck] = b_ref[h, :, c * ck:(c + 1) * ck] + row_mask

    def scores(h):
        sl = slice(h * HEAD_DIM, (h + 1) * HEAD_DIM)
        q = q_ref[:, sl]
        s = []
        for c, k_ref in enumerate(k_refs):
            sc = lax.dot_general(q, k_ref[:, sl], (((1,), (1,)), ((), ())), preferred_element_type=F32)
            s.append(sc + bm_ref[h, :, c * ck:(c + 1) * ck])
        return s, jnp.max(functools.reduce(jnp.maximum, s), axis=-1, keepdims=True)

    def finish(h, s, m):
        sl = slice(h * HEAD_DIM, (h + 1) * HEAD_DIM)
        p = [jnp.exp2(sc - m) for sc in s]
        l = jnp.sum(functools.reduce(jnp.add, p), axis=-1, keepdims=True)
        o = functools.reduce(jnp.add, [
            jnp.dot(pc.astype(BF16), v_ref[:, sl], preferred_element_type=F32)
            for pc, v_ref in zip(p, v_refs)])
        o_ref[:, sl] = (o / l).astype(o_ref.dtype)

    cur = scores(0)
    for h in range(heads):
        nxt = scores(h + 1) if h + 1 < heads else None
        finish(h, *cur)
        cur = nxt


def na_attention(qkv, bias, *, batch, seq, heads_per_step):
    m = qkv.shape[0]
    nh = bias.shape[0]
    hg = heads_per_step
    wq = hg * HEAD_DIM
    qb = NA_BLOCK_ROWS * GRID_W
    kb = NA_KEY_ROWS * GRID_W // NA_KEY_CHUNKS
    assert qb % kb == 0 and (NA_KEY_BACK * GRID_W) % kb == 0
    nblk = seq // qb
    kpb = seq // kb
    ratio = qb // kb
    back = NA_KEY_BACK * GRID_W // kb
    ncol = nh // hg

    def k_map(c, sec):
        def index(g, b, i):
            return (b * kpb + jnp.clip(i * ratio - back + c, 0, kpb - 1), sec * ncol + g)
        return index

    in_specs = [pl.BlockSpec((qb, wq), lambda g, b, i: (b * nblk + i, g))]
    in_specs += [pl.BlockSpec((kb, wq), k_map(c, 1)) for c in range(NA_KEY_CHUNKS)]
    in_specs += [pl.BlockSpec((kb, wq), k_map(c, 2)) for c in range(NA_KEY_CHUNKS)]
    in_specs += [pl.BlockSpec((hg, qb, NA_KEY_CHUNKS * kb), lambda g, b, i: (g, 0, 0),
                              pipeline_mode=pl.Buffered(1))]
    vmem = (2 * 2 * qb * wq * 2 + 2 * 2 * NA_KEY_CHUNKS * kb * wq * 2
            + 2 * hg * qb * NA_KEY_CHUNKS * kb * 4 + 6 * qb * NA_KEY_CHUNKS * kb * 4)
    return pl.pallas_call(
        functools.partial(_na_attn_kernel, heads=hg, rows=seq // GRID_W),
        grid=(ncol, batch, nblk),
        in_specs=in_specs,
        out_specs=pl.BlockSpec((qb, wq), lambda g, b, i: (b * nblk + i, g)),
        out_shape=jax.ShapeDtypeStruct((m, nh * HEAD_DIM), BF16),
        scratch_shapes=[pltpu.VMEM((hg, qb, NA_KEY_CHUNKS * kb), F32)],
        compiler_params=_params(("parallel", "parallel", "arbitrary"), vmem),
        name="na_attention",
    )(qkv, *([qkv] * (2 * NA_KEY_CHUNKS)), bias)


def _proj_residual_kernel(a_ref, w_ref, r_ref, freq_ref, o_ref, rot_ref):
    _, rows, _ = rot_ref.shape
    pos = (pl.program_id(0) * rows + lax.broadcasted_iota(jnp.int32, (rows, LANES), 0)).astype(F32)
    ang = pos * freq_ref[...]
    lane = lax.broadcasted_iota(jnp.int32, (rows, LANES), 1)
    half = ROPE_DIM // 2
    sin = jnp.sin(ang)
    rot_ref[0] = jnp.cos(ang)
    rot_ref[1] = jnp.where((lane >= half) & (lane < ROPE_DIM), sin, 0.0)
    rot_ref[2] = jnp.where(lane < half, -sin, 0.0)

    o_ref[...] = r_ref[...] + jnp.dot(a_ref[...], w_ref[...], preferred_element_type=F32)


def proj_residual(a, w, res, *, tm, rope_seq):
    m, k = a.shape
    n = w.shape[1]
    n_steps = m // tm
    assert rope_seq % n_steps == 0 and (rope_seq // n_steps) % F32_SUBLANES == 0 and HEAD_DIM == LANES
    rope_rows = rope_seq // n_steps
    inv_freq = ROPE_THETA ** (-jnp.arange(0, ROPE_DIM, 2, dtype=F32) / ROPE_DIM)
    freq = jnp.concatenate([inv_freq, inv_freq, jnp.zeros((HEAD_DIM - ROPE_DIM,), F32)]).reshape(1, HEAD_DIM)
    vmem = (2 * tm * k * 2 + 2 * k * n * 2 + 4 * tm * n * 4 + tm * n * 4
            + 2 * 3 * rope_rows * HEAD_DIM * 4 * 3)
    return pl.pallas_call(
        _proj_residual_kernel,
        grid=(n_steps,),
        in_specs=[
            pl.BlockSpec((tm, k), lambda i: (i, 0)),
            pl.BlockSpec((k, n), lambda i: (0, 0)),
            pl.BlockSpec((tm, n), lambda i: (i, 0)),
            pl.BlockSpec((1, HEAD_DIM), lambda i: (0, 0)),
        ],
        out_specs=[pl.BlockSpec((tm, n), lambda i: (i, 0)),
                   pl.BlockSpec((3, rope_rows, HEAD_DIM), lambda i: (0, i, 0))],
        out_shape=[jax.ShapeDtypeStruct((m, n), F32),
                   jax.ShapeDtypeStruct((3, rope_seq, HEAD_DIM), F32)],
        compiler_params=_params(("parallel",), vmem),
        name="na_proj_residual",
    )(a, w, res, freq)


def _ffn_kernel(x_ref, g_ref, w1_ref, w2_ref, gf_ref, *refs, final_norm, n_cast):
    cast_src = refs[:n_cast]
    o_ref = refs[n_cast]
    cast_dst = refs[n_cast + 1:2 * n_cast + 1]
    h_ref = refs[2 * n_cast + 1]
    f = pl.program_id(1)

    def body(first):
        _cast_slabs(cast_src, cast_dst)
        for c in range(x_ref.shape[0] // FFN_ROW_CHUNK):
            rows = slice(c * FFN_ROW_CHUNK, (c + 1) * FFN_ROW_CHUNK)
            if first:
                h_ref[rows, :] = _rms(x_ref[rows, :], g_ref[...]).astype(BF16)
            a = jnp.dot(h_ref[rows, :], w1_ref[...], preferred_element_type=F32)
            a = jnp.square(jnp.maximum(a, 0.0)).astype(BF16)
            y = jnp.dot(a, w2_ref[...], preferred_element_type=F32)
            if first:
                o_ref[rows, :] = x_ref[rows, :] + y
            else:
                o_ref[rows, :] += y

    pl.when(f == 0)(functools.partial(body, True))
    pl.when(f != 0)(functools.partial(body, False))

    if final_norm:
        @pl.when(f == pl.num_programs(1) - 1)
        def _():
            o_ref[...] = _rms(o_ref[...], gf_ref[...])


def ffn(x, g, w1, w2, g_final, *, tm, tf, final_norm, cast=()):
    m, d = x.shape
    dff = w1.shape[1]
    assert tm % FFN_ROW_CHUNK == 0
    n_steps = (m // tm) * (dff // tf)
    vmem = (4 * tm * d * 4 + tm * d * 2 + 2 * 2 * d * tf * 2
            + FFN_ROW_CHUNK * tf * 6 + FFN_ROW_CHUNK * d * 4 + _cast_vmem(cast, n_steps))
    cast_specs = _cast_specs(cast, n_steps, dff // tf)
    return pl.pallas_call(
        functools.partial(_ffn_kernel, final_norm=final_norm, n_cast=len(cast)),
        grid=(m // tm, dff // tf),
        in_specs=[
            pl.BlockSpec((tm, d), _tile_then_next(m // tm, dff // tf)),
            pl.BlockSpec((1, d), lambda i, f: (0, 0)),
            pl.BlockSpec((d, tf), lambda i, f: (0, f)),
            pl.BlockSpec((tf, d), lambda i, f: (f, 0)),
            pl.BlockSpec((1, d), lambda i, f: (0, 0)),
        ] + cast_specs,
        out_specs=[pl.BlockSpec((tm, d), lambda i, f: (i, 0))] + cast_specs,
        out_shape=[jax.ShapeDtypeStruct((m, d), F32)] + [jax.ShapeDtypeStruct(c.shape, BF16) for c in cast],
        scratch_shapes=[pltpu.VMEM((tm, d), BF16)],
        compiler_params=_params(("arbitrary", "arbitrary"), vmem),
        name="ffn_final" if final_norm else "ffn",
    )(x, g.reshape(1, d), w1, w2, g_final.reshape(1, d), *cast)


def _dil_attn_kernel(q_ref, kp_ref, kc_ref, kn_ref, vp_ref, vc_ref, vn_ref, o_ref, lse_ref,
                     *, heads, halo, length):
    qb = q_ref.shape[0]
    sub = BLOCK_Q
    nk = sub + 2 * halo
    m0 = pl.program_id(2) * qb
    qi = lax.broadcasted_iota(jnp.int32, (sub, nk), 0)
    kj = lax.broadcasted_iota(jnp.int32, (sub, nk), 1) - halo
    band = jnp.abs(kj - qi) <= halo
    lane = lax.broadcasted_iota(jnp.int32, (sub, HEAD_DIM), 1)

    def window(prev_ref, cur_ref, next_ref, lo, hi, cols):
        parts = [prev_ref[:, cols]] if lo < 0 else []
        parts.append(cur_ref[max(lo, 0):min(hi, qb), cols])
        if hi > qb:
            parts.append(next_ref[:, cols])
        return parts[0] if len(parts) == 1 else jnp.concatenate(parts, axis=0)

    def scores(u, h):
        rows = slice(u * sub, (u + 1) * sub)
        cols = slice(h * HEAD_DIM, (h + 1) * HEAD_DIM)
        k = window(kp_ref, kc_ref, kn_ref, u * sub - halo, (u + 1) * sub + halo, cols)
        s = lax.dot_general(q_ref[rows, cols], k, (((1,), (1,)), ((), ())), preferred_element_type=F32)
        s = s + masks[u]
        return s, jnp.max(s, axis=-1, keepdims=True)

    def finish(u, h, s, m):
        rows = slice(u * sub, (u + 1) * sub)
        cols = slice(h * HEAD_DIM, (h + 1) * HEAD_DIM)
        v = window(vp_ref, vc_ref, vn_ref, u * sub - halo, (u + 1) * sub + halo, cols)
        p = jnp.exp2(s - m)
        l = jnp.sum(p, axis=-1, keepdims=True)
        o = jnp.dot(p.astype(BF16), v, preferred_element_type=F32)
        o_ref[rows, cols] = (o / l).astype(o_ref.dtype)
        return m * LN2 + jnp.log(l)

    masks = []
    for u in range(qb // sub):
        kpos = kj + (m0 + u * sub)
        masks.append(jnp.where(band & (kpos >= 0) & (kpos < length), 0.0, NEG).astype(F32))

    units = [(u, h) for u in range(qb // sub) for h in range(heads)]
    cur = scores(*units[0])
    lse_tile = None
    for n, (u, h) in enumerate(units):
        nxt = scores(*units[n + 1]) if n + 1 < len(units) else None
        lse = finish(u, h, *cur)
        lse_tile = jnp.where(lane == h, lse, jnp.zeros((sub, HEAD_DIM), F32) if h == 0 else lse_tile)
        if h == heads - 1:
            lse_ref[u * sub:(u + 1) * sub, :] = lse_tile
        cur = nxt


def dil_attention(qkv, *, group, halo):
    _, batch, dil, length, wq = qkv.shape
    heads = wq // HEAD_DIM
    qb = min(DIL_STEP_ROWS, length)
    assert BLOCK_Q % halo == 0 and qb % BLOCK_Q == 0 and length % qb == 0
    nblk = length // qb
    hpb = length // halo
    per = qb // halo

    def cur_map(sec):
        return lambda b, r, i: (sec, b, r, i, 0)

    def prev_map(sec):
        return lambda b, r, i: (sec, b, r, jnp.maximum(i * per - 1, 0), 0)

    def next_map(sec):
        return lambda b, r, i: (sec, b, r, jnp.minimum((i + 1) * per, hpb - 1), 0)

    def spec(rows, index_map):
        return pl.BlockSpec((None, None, None, rows, wq), index_map)

    in_specs = [spec(qb, cur_map(0))]
    for sec in (1, 2):
        in_specs += [spec(halo, prev_map(sec)), spec(qb, cur_map(sec)), spec(halo, next_map(sec))]
    vmem = (2 * (2 * qb * wq * 2 + 2 * (qb + 2 * halo) * wq * 2 + qb * HEAD_DIM * 4)
            + 8 * BLOCK_Q * (BLOCK_Q + 2 * halo) * 4)
    return pl.pallas_call(
        functools.partial(_dil_attn_kernel, heads=heads, halo=halo, length=length),
        grid=(batch, dil, nblk),
        in_specs=in_specs,
        out_specs=[pl.BlockSpec((None, None, qb, wq), lambda b, r, i: (b, r, i, 0)),
                   pl.BlockSpec((None, None, qb, HEAD_DIM), lambda b, r, i: (b, r, i, 0))],
        out_shape=[jax.ShapeDtypeStruct((batch, dil, length, wq), BF16),
                   jax.ShapeDtypeStruct((batch, dil, length, HEAD_DIM), F32)],
        compiler_params=_params(("parallel", "parallel", "arbitrary"), vmem),
        name=f"dil_attention_g{group}",
    )(*([qkv] * 7))


def _merge_proj_kernel(*refs, dils, heads):
    g_n = len(dils)
    o_refs = refs[:g_n]
    lse_refs = refs[g_n:2 * g_n]
    w_ref, r_ref, out_ref, a_ref, lt_ref, ot_ref, tt_ref = refs[2 * g_n:]
    tm = a_ref.shape[0]

    def merged(c):
        rows = slice(c * ROW_CHUNK, (c + 1) * ROW_CHUNK)
        lses = []
        for g, dil in enumerate(dils):
            if dil == 1:
                lses.append(lse_refs[g][0, rows, :])
                continue
            n = ROW_CHUNK // dil
            d1, d2 = _stride_split(dil)
            n1 = ROW_CHUNK // d1

            def to_token_order(dst_ref, dst_idx, src):
                if d2 == 1:
                    for r in range(dil):
                        dst_ref[dst_idx, pl.ds(c * ROW_CHUNK + r, n, stride=dil), :] = src(r)
                    return
                for r1 in range(d1):
                    for r2 in range(d2):
                        tt_ref[pl.ds(r1 * n1 + r2, n, stride=d2), :] = src(r1 + d1 * r2)
                for r1 in range(d1):
                    dst_ref[dst_idx, pl.ds(c * ROW_CHUNK + r1, n1, stride=d1), :] = tt_ref[r1 * n1:(r1 + 1) * n1, :]

            to_token_order(lt_ref, g, lambda r: lse_refs[g][r, c * n:(c + 1) * n, :])
            for h in range(heads):
                to_token_order(ot_ref, g * heads + h, lambda r: o_refs[g][
                    r, c * n:(c + 1) * n, h * HEAD_DIM:(h + 1) * HEAD_DIM].astype(F32))
            lses.append(lt_ref[g, rows, :])

        mx = functools.reduce(jnp.maximum, lses)
        es = [jnp.exp(l - mx) for l in lses]
        den = functools.reduce(jnp.add, es)
        wts = [e / den for e in es]
        for h in range(heads):
            sl = slice(h * HEAD_DIM, (h + 1) * HEAD_DIM)
            acc = None
            for g, dil in enumerate(dils):
                o = o_refs[g][0, rows, sl].astype(F32) if dil == 1 else ot_ref[g * heads + h, rows, :]
                term = wts[g][:, h:h + 1] * o
                acc = term if acc is None else acc + term
            a_ref[rows, sl] = acc.astype(BF16)

    merged(0)
    for c in range(tm // ROW_CHUNK):
        rows = slice(c * ROW_CHUNK, (c + 1) * ROW_CHUNK)
        if c + 1 < tm // ROW_CHUNK:
            merged(c + 1)
        out_ref[rows, :] = r_ref[rows, :] + jnp.dot(a_ref[rows, :], w_ref[...], preferred_element_type=F32)


def merge_proj_residual(outs, lses, w, res, *, seq, tm):
    dils = tuple(o.shape[1] for o in outs)
    k = outs[0].shape[-1]
    m, n = res.shape
    g_n = len(outs)
    heads = k // HEAD_DIM
    tpb = seq // tm
    assert all(ROW_CHUNK % (BF16_SUBLANES * dil) == 0 for dil in dils) and tm % ROW_CHUNK == 0

    def stream_spec(dil, width):
        return pl.BlockSpec((None, dil, tm // dil, width), lambda i: (i // tpb, 0, i % tpb, 0))

    vmem = (2 * g_n * tm * k * 2 + 2 * g_n * tm * HEAD_DIM * 4 + 2 * k * n * 2 + 4 * tm * n * 4
            + tm * k * 2 + tm * n * 4 + g_n * tm * HEAD_DIM * 4 + g_n * tm * k * 4)
    return pl.pallas_call(
        functools.partial(_merge_proj_kernel, dils=dils, heads=heads),
        grid=(m // tm,),
        in_specs=([stream_spec(dil, k) for dil in dils]
                  + [stream_spec(dil, HEAD_DIM) for dil in dils]
                  + [pl.BlockSpec((k, n), lambda i: (0, 0)),
                     pl.BlockSpec((tm, n), lambda i: (i, 0))]),
        out_specs=pl.BlockSpec((tm, n), lambda i: (i, 0)),
        out_shape=jax.ShapeDtypeStruct((m, n), F32),
        scratch_shapes=[pltpu.VMEM((tm, k), BF16),
                        pltpu.VMEM((g_n, tm, HEAD_DIM), F32),
                        pltpu.VMEM((g_n * heads, tm, HEAD_DIM), F32),
                        pltpu.VMEM((ROW_CHUNK, HEAD_DIM), F32)],
        compiler_params=_params(("parallel",), vmem),
        name="dil_merge_proj_residual",
    )(*outs, *lses, w, res)


def kernel(x, na_norm, na_wqkv, na_rpb, na_wo, ffn0_norm, ffn0_w1, ffn0_w2, dil_norm, dil_wqkv, dil_wo,
           ffn1_norm, ffn1_w1, ffn1_w2, final_norm):
    batch, seq, d = x.shape
    m = batch * seq
    rows = seq // GRID_W
    na_heads = na_rpb.shape[0]
    assert rows >= NA_KEY_ROWS and rows % NA_BLOCK_ROWS == 0 and rows >= NA_WIN_ROWS
    assert NA_KEY_ROWS >= NA_BLOCK_ROWS + NA_WIN_ROWS - 1

    xf = x.reshape(m, d)
    bf = lambda w: w.astype(BF16)

    qw = na_heads * HEAD_DIM
    col_scale = jnp.concatenate([jnp.full((qw,), Q_SCALE, F32), jnp.ones((2 * qw,), F32)])
    qkv, na_wo_b, ffn0_w1_b, ffn0_w2_b = norm_matmul(
        xf, na_norm, bf(na_wqkv), col_scale, tm=min(1024, m), tn=min(1024, qw), cast=(na_wo, ffn0_w1, ffn0_w2))
    bias = _na_bias_table(na_rpb)
    attn = na_attention(qkv, bias, batch=batch, seq=seq, heads_per_step=min(16, na_heads))
    xf, rot = proj_residual(attn, na_wo_b, xf, tm=min(512, m), rope_seq=seq)
    xf, dil_wqkv_b, dil_wo_b, ffn1_w1_b, ffn1_w2_b = ffn(
        xf, ffn0_norm, ffn0_w1_b, ffn0_w2_b, final_norm, tm=min(1024, m), tf=512, final_norm=False,
        cast=(dil_wqkv, dil_wo, ffn1_w1, ffn1_w2))

    dils = tuple(dil for _, dil in DIL_GROUPS)
    qkvs = norm_matmul_rope(xf, dil_norm, dil_wqkv_b, rot, batch=batch, seq=seq, tm=min(1024, seq), dils=dils)
    outs, lses = [], []
    for gi, (window, dil) in enumerate(DIL_GROUPS):
        o, lse = dil_attention(qkvs[gi], group=gi, halo=(window // 2) // dil)
        outs.append(o)
        lses.append(lse)
    xf = merge_proj_residual(outs, lses, dil_wo_b, xf, seq=seq, tm=min(512, seq))
    xf, = ffn(xf, ffn1_norm, ffn1_w1_b, ffn1_w2_b, final_norm, tm=min(1024, m), tf=512, final_norm=True)
    return xf.reshape(batch, seq, d)
```
